```python
import math
import jax, jax.numpy as jnp
from jax import lax
import numpy as np


D_MODEL = 2048
BATCH = 32
SEQ = 256
DEPTH = 2
DEC_BATCH = 4
DEC_SEQ = 2048
PAST_LEN = 256

GRID_W = 64
N_EVEN = (DEPTH + 1) // 2
N_ODD = DEPTH // 2
EPS = 1e-6

LRU_WIDTH = D_MODEL // 2
LRU_HEADS = 8
LRU_BLOCK = LRU_WIDTH // LRU_HEADS
CONV_W = 4
CONV_PAD = (2, 1)
LRU_C = 8.0
GMLP_WIDTH = D_MODEL // 2
GMLP_GROUPS = 8
GMLP_GROUP_DIM = GMLP_WIDTH // GMLP_GROUPS
CHUNK = 128
IN0_WIDTH = 2 * LRU_WIDTH + 2 * GMLP_WIDTH
MIX0_WIDTH = LRU_WIDTH + GMLP_WIDTH

MLA_HEADS = 16
Q_LORA = 512
KV_LORA = 512
NOPE_DIM = 128
ROPE_DIM = 64
V_DIM = 128
QK_DIM = NOPE_DIM + ROPE_DIM
ROPE_BASE = 10000.0
Q_BLOCK = 128

N_EXPERTS = 16
N_EXPERT_GROUPS = 4
EXPERTS_PER_GROUP = N_EXPERTS // N_EXPERT_GROUPS
TOP_K = 2
D_EXPERT = 512
ROUTED_SCALE = 1.0

kernel_name = 'hybrid_diffusion_rglru_gmlp_mla_moe'


def rms_norm(x, gain):
    xf = x.astype(jnp.float32)
    y = xf * lax.rsqrt(jnp.mean(xf * xf, axis=-1, keepdims=True) + EPS)
    return (y * gain.astype(jnp.float32)).astype(x.dtype)


def modulation(cond, w_mod, b_mod):
    m = jnp.einsum('bd,de->be', jax.nn.silu(cond), w_mod) + b_mod
    return jnp.split(m[:, None, :], 6, axis=-1)


def modulate(x, gain, shift, scale):
    return rms_norm(x, gain) * (1.0 + scale) + shift


def depthwise_conv_centred(x, w, b):
    s = x.shape[1]
    xp = jnp.pad(x, ((0, 0), CONV_PAD, (0, 0)))
    return sum(xp[:, k:k + s, :] * w[k] for k in range(CONV_W)) + b


def block_diag_linear(x, w, b):
    bsz, s, _ = x.shape
    xh = x.reshape(bsz, s, LRU_HEADS, LRU_BLOCK)
    return jnp.einsum('bshi,hij->bshj', xh, w).reshape(bsz, s, LRU_WIDTH) + b


def _scan_combine(left, right):
    a_l, b_l = left
    a_r, b_r = right
    return a_l * a_r, a_r * b_l + b_r


def rglru_direction(xc, w_r, b_r, w_i, b_i, lam, h0, reverse):
    r = jax.nn.sigmoid(block_diag_linear(xc, w_r, b_r).astype(jnp.float32))
    gi = jax.nn.sigmoid(block_diag_linear(xc, w_i, b_i).astype(jnp.float32))
    log_a = -LRU_C * r * jax.nn.softplus(-lam.astype(jnp.float32))
    a = jnp.exp(log_a)
    bx = jnp.sqrt(-jnp.expm1(2.0 * log_a)) * gi * xc.astype(jnp.float32)
    if reverse:
        a, bx = jnp.flip(a, axis=1), jnp.flip(bx, axis=1)
    bx = bx.at[:, 0].add(a[:, 0] * h0.astype(jnp.float32))
    _, h = lax.associative_scan(_scan_combine, (a, bx), axis=1)
    final = h[:, -1]
    if reverse:
        h = jnp.flip(h, axis=1)
    return h, final


def chunk_gmlp(u, v, v_gain, w_s, b_s):
    bsz, s, _ = u.shape
    u = jax.nn.gelu(u)
    v = rms_norm(jax.nn.gelu(v), v_gain)
    vc = v.reshape(bsz, s // CHUNK, CHUNK, GMLP_GROUPS, GMLP_GROUP_DIM)
    mixed = jnp.einsum('gpq,bcqgd->bcpgd', w_s, vc) + b_s.T[:, :, None]
    return u * mixed.reshape(bsz, s, GMLP_WIDTH)


def even_mixer(h, w_in, w_out, conv_w, conv_b, w_r, b_r, w_i, b_i, lam, v_gain, w_s, b_s, h0_f, h0_b):
    z = jnp.einsum('bsd,de->bse', h, w_in)
    xb, gb, u, v = jnp.split(z, [LRU_WIDTH, 2 * LRU_WIDTH, 2 * LRU_WIDTH + GMLP_WIDTH], axis=-1)
    xc = depthwise_conv_centred(xb, conv_w, conv_b)
    h_f, fin_f = rglru_direction(xc, w_r[0], b_r[0], w_i[0], b_i[0], lam[0], h0_f, False)
    h_b, fin_b = rglru_direction(xc, w_r[1], b_r[1], w_i[1], b_i[1], lam[1], h0_b, True)
    y_a = (h_f + h_b).astype(h.dtype) * jax.nn.gelu(gb)
    y_b = chunk_gmlp(u, v, v_gain, w_s, b_s)
    y = jnp.einsum('bse,ed->bsd', jnp.concatenate([y_a, y_b], axis=-1), w_out)
    return y, fin_f, fin_b


def axial_rope_tables(n):
    rows = n // GRID_W
    pos = jnp.arange(rows * GRID_W)
    row_pos = (pos // GRID_W).astype(jnp.float32)
    col_pos = (pos % GRID_W).astype(jnp.float32)
    n_freq = ROPE_DIM // 4
    inv_freq = ROPE_BASE ** (-jnp.arange(n_freq, dtype=jnp.float32) / n_freq)
    ang = jnp.stack([row_pos[:, None] * inv_freq, col_pos[:, None] * inv_freq], axis=1)
    return jnp.cos(ang), jnp.sin(ang)


def apply_axial_rope(x, cos, sin):
    shp = x.shape
    xr = x.reshape(shp[:-1] + (2, 2, ROPE_DIM // 4)).astype(jnp.float32)
    x1, x2 = xr[..., 0, :], xr[..., 1, :]
    c, s = cos[:, None], sin[:, None]
    out = jnp.stack([x1 * c - x2 * s, x2 * c + x1 * s], axis=-2)
    return out.reshape(shp).astype(x.dtype)


def rope_tail(x, cos, sin):
    return jnp.concatenate([x[..., :NOPE_DIM], apply_axial_rope(x[..., NOPE_DIM:], cos, sin)], axis=-1)


def blocked_attention(q, k, v):
    bsz, sq, nh, dq = q.shape
    scale = dq ** -0.5
    qb = q.reshape(bsz, sq // Q_BLOCK, Q_BLOCK, nh, dq).transpose(1, 0, 2, 3, 4)

    def one_block(qblk):
        s = jnp.einsum('bqhd,bkhd->bhqk', qblk, k, preferred_element_type=jnp.float32) * scale
        p = jax.nn.softmax(s, axis=-1)
        return jnp.einsum('bhqk,bkhd->bqhd', p.astype(v.dtype), v)

    o = lax.map(one_block, qb)
    return o.transpose(1, 0, 2, 3, 4).reshape(bsz, sq, nh, v.shape[-1])


def mla_project(h, w_down, q_a_norm, kv_a_norm, w_uq, q_norm):
    bsz, s, _ = h.shape
    z = jnp.einsum('bsd,de->bse', h, w_down)
    c_q, c_kv, k_rope = jnp.split(z, [Q_LORA, Q_LORA + KV_LORA], axis=-1)
    c_q = rms_norm(c_q, q_a_norm)
    c_kv = rms_norm(c_kv, kv_a_norm)
    q = jnp.einsum('bsc,ce->bse', c_q, w_uq).reshape(bsz, s, MLA_HEADS, QK_DIM)
    return rms_norm(q, q_norm), c_kv, k_rope


def mla_expand_kv(c_kv, k_rope, w_ukv, k_norm):
    bsz, s, _ = c_kv.shape
    kv = jnp.einsum('bsc,ce->bse', c_kv, w_ukv).reshape(bsz, s, MLA_HEADS, NOPE_DIM + V_DIM)
    k_nope, v = jnp.split(kv, [NOPE_DIM], axis=-1)
    k_r = jnp.broadcast_to(k_rope[:, :, None, :], (bsz, s, MLA_HEADS, ROPE_DIM)).astype(k_nope.dtype)
    k = rms_norm(jnp.concatenate([k_nope, k_r], axis=-1), k_norm)
    return k, v


def mla_context(h, w_down, q_a_norm, kv_a_norm, w_uq, w_ukv, q_norm, k_norm, w_o):
    bsz, s, _ = h.shape
    q, c_kv, k_rope = mla_project(h, w_down, q_a_norm, kv_a_norm, w_uq, q_norm)
    k, v = mla_expand_kv(c_kv, k_rope, w_ukv, k_norm)
    o = blocked_attention(q, k, v).reshape(bsz, s, MLA_HEADS * V_DIM)
    return jnp.einsum('bse,ed->bsd', o, w_o), c_kv, k_rope


def mla_latent(h, ctx_ckv, ctx_krope, cos, sin, w_down, q_a_norm, kv_a_norm, w_uq, w_ukv, q_norm, k_norm, w_o):
    bsz, s, _ = h.shape
    q, c_kv, k_rope = mla_project(h, w_down, q_a_norm, kv_a_norm, w_uq, q_norm)
    k_lat, v_lat = mla_expand_kv(c_kv, k_rope, w_ukv, k_norm)
    q = rope_tail(q, cos, sin)
    k_lat = rope_tail(k_lat, cos, sin)
    k_ctx, v_ctx = mla_expand_kv(ctx_ckv, ctx_krope, w_ukv, k_norm)
    k = jnp.concatenate([k_ctx.astype(k_lat.dtype), k_lat], axis=1)
    v = jnp.concatenate([v_ctx.astype(v_lat.dtype), v_lat], axis=1)
    o = blocked_attention(q, k, v).reshape(bsz, s, MLA_HEADS * V_DIM)
    return jnp.einsum('bse,ed->bsd', o, w_o)


def moe_ffn(h, router_w, router_bias, w_gate, w_up, w_down):
    bsz, s, d = h.shape
    t = h.reshape(bsz * s, d)
    logits = jnp.einsum('td,de->te', t, router_w, preferred_element_type=jnp.float32)
    scores = jax.nn.sigmoid(logits)
    biased = scores + router_bias.astype(jnp.float32)
    grp_top, _ = lax.top_k(biased.reshape(-1, N_EXPERT_GROUPS, EXPERTS_PER_GROUP), TOP_K)
    sel_group = jnp.argmax(grp_top.sum(axis=-1), axis=-1)
    expert_mask = jnp.repeat(jax.nn.one_hot(sel_group, N_EXPERT_GROUPS, dtype=jnp.float32), EXPERTS_PER_GROUP, axis=-1)
    masked = jnp.where(expert_mask > 0, biased, -jnp.inf)
    _, idx = lax.top_k(masked, TOP_K)
    sel = jnp.take_along_axis(scores, idx, axis=-1)
    wts = sel / jnp.sum(sel, axis=-1, keepdims=True) * ROUTED_SCALE
    combine = jnp.sum(jax.nn.one_hot(idx, N_EXPERTS, dtype=jnp.float32) * wts[..., None], axis=1)
    out = jnp.zeros((bsz * s, d), jnp.float32)
    for e in range(N_EXPERTS):
        g = jnp.einsum('td,df->tf', t, w_gate[e])
        u = jnp.einsum('td,df->tf', t, w_up[e])
        y = jnp.einsum('tf,fd->td', jax.nn.silu(g) * u, w_down[e])
        out = out + combine[:, e:e + 1] * y
    return out.astype(h.dtype).reshape(bsz, s, d)


def setup_inputs(seed: int = 0) -> dict:
    key = jax.random.key(seed)
    ks = iter(jax.random.split(key, 64))

    def nrm(shape, scale):
        return jax.random.normal(next(ks), shape, jnp.float32) * scale

    def gain(shape):
        return 1.0 + nrm(shape, 0.05)

    a_c = jax.random.uniform(next(ks), (N_EVEN, 2, LRU_WIDTH), jnp.float32, 0.9, 0.999)
    a_base = a_c ** (1.0 / LRU_C)
    lru_lambda = jnp.log(a_base) - jnp.log1p(-a_base)
    return {
        'x_prompt': nrm((BATCH, SEQ, D_MODEL), 1.0),
        'x_sample': nrm((DEC_BATCH, DEC_SEQ, D_MODEL), 1.0),
        'state_lru_fwd': nrm((DEC_BATCH, N_EVEN, LRU_WIDTH), 0.5),
        'state_lru_bwd': nrm((DEC_BATCH, N_EVEN, LRU_WIDTH), 0.5),
        'cache_mla_ckv': nrm((DEC_BATCH, N_ODD, PAST_LEN, KV_LORA), 1.0),
        'cache_mla_krope': nrm((DEC_BATCH, N_ODD, PAST_LEN, ROPE_DIM), 1.0),
        'c': nrm((DEC_BATCH, D_MODEL), 1.0),
        'c_ctx': nrm((D_MODEL,), 1.0),
        'ada_w': nrm((DEPTH, D_MODEL, 6 * D_MODEL), 0.5 * D_MODEL ** -0.5),
        'ada_b': nrm((DEPTH, 6 * D_MODEL), 0.01),
        'norm_mix': gain((DEPTH, D_MODEL)),
        'norm_ffn': gain((DEPTH, D_MODEL)),
        'mix0_w_in': nrm((N_EVEN, D_MODEL, IN0_WIDTH), D_MODEL ** -0.5),
        'mix0_w_out': nrm((N_EVEN, MIX0_WIDTH, D_MODEL), MIX0_WIDTH ** -0.5),
        'lru_conv_w': nrm((N_EVEN, CONV_W, LRU_WIDTH), CONV_W ** -0.5),
        'lru_conv_b': nrm((N_EVEN, LRU_WIDTH), 0.01),
        'lru_w_r': nrm((N_EVEN, 2, LRU_HEADS, LRU_BLOCK, LRU_BLOCK), LRU_BLOCK ** -0.5),
        'lru_b_r': nrm((N_EVEN, 2, LRU_WIDTH), 0.01),
        'lru_w_i': nrm((N_EVEN, 2, LRU_HEADS, LRU_BLOCK, LRU_BLOCK), LRU_BLOCK ** -0.5),
        'lru_b_i': nrm((N_EVEN, 2, LRU_WIDTH), 0.01),
        'lru_lambda': lru_lambda,
        'gmlp_v_norm': gain((N_EVEN, GMLP_WIDTH)),
        'gmlp_w_s': nrm((N_EVEN, GMLP_GROUPS, CHUNK, CHUNK), CHUNK ** -0.5),
        'gmlp_b_s': 1.0 + nrm((N_EVEN, GMLP_GROUPS, CHUNK), 0.02),
        'mla_w_down': nrm((N_ODD, D_MODEL, Q_LORA + KV_LORA + ROPE_DIM), D_MODEL ** -0.5),
        'mla_q_a_norm': gain((N_ODD, Q_LORA)),
        'mla_kv_a_norm': gain((N_ODD, KV_LORA)),
        'mla_w_uq': nrm((N_ODD, Q_LORA, MLA_HEADS * QK_DIM), Q_LORA ** -0.5),
        'mla_w_ukv': nrm((N_ODD, KV_LORA, MLA_HEADS * (NOPE_DIM + V_DIM)), KV_LORA ** -0.5),
        'mla_q_norm': gain((N_ODD, QK_DIM)),
        'mla_k_norm': gain((N_ODD, QK_DIM)),
        'mla_w_o': nrm((N_ODD, MLA_HEADS * V_DIM, D_MODEL), (MLA_HEADS * V_DIM) ** -0.5),
        'router_w': nrm((D_MODEL, N_EXPERTS), D_MODEL ** -0.5),
        'router_bias': nrm((N_EXPERTS,), 0.01),
        'moe_w_gate': nrm((DEPTH, N_EXPERTS, D_MODEL, D_EXPERT), D_MODEL ** -0.5),
        'moe_w_up': nrm((DEPTH, N_EXPERTS, D_MODEL, D_EXPERT), D_MODEL ** -0.5),
        'moe_w_down': nrm((DEPTH, N_EXPERTS, D_EXPERT, D_MODEL), D_EXPERT ** -0.5),
    }


def reference(x_prompt, x_sample, state_lru_fwd, state_lru_bwd, cache_mla_ckv, cache_mla_krope, c, c_ctx,
              ada_w, ada_b, norm_mix, norm_ffn, mix0_w_in, mix0_w_out, lru_conv_w, lru_conv_b,
              lru_w_r, lru_b_r, lru_w_i, lru_b_i, lru_lambda, gmlp_v_norm, gmlp_w_s, gmlp_b_s,
              mla_w_down, mla_q_a_norm, mla_kv_a_norm, mla_w_uq, mla_w_ukv, mla_q_norm, mla_k_norm, mla_w_o,
              router_w, router_bias, moe_w_gate, moe_w_up, moe_w_down):
    cos, sin = axial_rope_tables(x_sample.shape[1])
    zero_state = jnp.zeros((x_prompt.shape[0], LRU_WIDTH), jnp.float32)
    xp, xs = x_prompt, x_sample
    fwd_states, bwd_states, ckv_caches, krope_caches = [], [], [], []
    for layer in range(DEPTH):
        j = layer // 2
        mp = modulation(c_ctx[None, :], ada_w[layer], ada_b[layer])
        ms = modulation(c, ada_w[layer], ada_b[layer])
        hp = modulate(xp, norm_mix[layer], mp[0], mp[1])
        hs = modulate(xs, norm_mix[layer], ms[0], ms[1])
        if layer % 2 == 0:
            even_p = (mix0_w_in[j], mix0_w_out[j], lru_conv_w[j], lru_conv_b[j], lru_w_r[j], lru_b_r[j],
                      lru_w_i[j], lru_b_i[j], lru_lambda[j], gmlp_v_norm[j], gmlp_w_s[j], gmlp_b_s[j])
            yp, fin_f, fin_b = even_mixer(hp, *even_p, zero_state, zero_state)
            ys, _, _ = even_mixer(hs, *even_p, state_lru_fwd[:, j], state_lru_bwd[:, j])
            fwd_states.append(fin_f)
            bwd_states.append(fin_b)
        else:
            mla_p = (mla_w_down[j], mla_q_a_norm[j], mla_kv_a_norm[j], mla_w_uq[j], mla_w_ukv[j],
                     mla_q_norm[j], mla_k_norm[j], mla_w_o[j])
            yp, ckv, krope = mla_context(hp, *mla_p)
            ys = mla_latent(hs, cache_mla_ckv[:, j], cache_mla_krope[:, j], cos, sin, *mla_p)
            ckv_caches.append(ckv)
            krope_caches.append(krope)
        xp = xp + mp[2] * yp
        xs = xs + ms[2] * ys
        hp = modulate(xp, norm_ffn[layer], mp[3], mp[4])
        hs = modulate(xs, norm_ffn[layer], ms[3], ms[4])
        xp = xp + mp[5] * moe_ffn(hp, router_w, router_bias, moe_w_gate[layer], moe_w_up[layer], moe_w_down[layer])
        xs = xs + ms[5] * moe_ffn(hs, router_w, router_bias, moe_w_gate[layer], moe_w_up[layer], moe_w_down[layer])
    new_state_lru_fwd = jnp.stack(fwd_states, axis=1).astype(xp.dtype)
    new_state_lru_bwd = jnp.stack(bwd_states, axis=1).astype(xp.dtype)
    new_cache_mla_ckv = jnp.stack(ckv_caches, axis=1)
    new_cache_mla_krope = jnp.stack(krope_caches, axis=1)
    return (xp, xs, new_state_lru_fwd, new_state_lru_bwd, new_cache_mla_ckv, new_cache_mla_krope)
```

```python
import functools

import jax
import jax.numpy as jnp
from jax import lax
from jax.experimental import pallas as pl
from jax.experimental.pallas import tpu as pltpu

F32 = jnp.float32
BF16 = jnp.bfloat16
U32 = jnp.uint32
I32 = jnp.int32

EPS = 1e-6
LRU_C = 8.0
GRID_W = 64
ROPE_BASE = 10000.0
ROPE_QUARTER = 16
N_EXPERT_GROUPS = 4
TOP_K = 2

LANES = 128
SUBLANES = 8
VMEM_BYTES_V7X = 64 * 1024 * 1024

TOKEN_TILE = 256
EXPERT_TILE = 256
MOD_COL_TILE = 1024
ATTN_Q_TILE = 512
LRU_GATE_CHUNK = 256


def _cparams(semantics, vmem_mb):
    return pltpu.CompilerParams(dimension_semantics=semantics, vmem_limit_bytes=vmem_mb * 1024 * 1024)


def _resident(shape):
    nd = len(shape)
    return pl.BlockSpec(shape, lambda *_: (0,) * nd, pipeline_mode=pl.Buffered(1))


def _mod_spec(layer, chunk, tile, n_prompt, sample_seq, d):
    def index(i, *_):
        t = i * tile
        row = jnp.where(t < n_prompt, 0, 1 + (t - n_prompt) // sample_seq)
        return (layer, row, chunk, 0, 0)

    return pl.BlockSpec((None, None, None, 1, d), index)


def _modulated(x, gain, shift, scale):
    y = x * lax.rsqrt(jnp.mean(x * x, axis=-1, keepdims=True) + EPS)
    return (y * gain) * (1.0 + scale) + shift


def _mod_body(c_ref, w_ref, b_ref, o_ref):
    c = c_ref[...]
    s = (c * jax.nn.sigmoid(c)).astype(BF16)
    o_ref[...] = jnp.dot(s, w_ref[...].astype(BF16), preferred_element_type=F32) + b_ref[...]


def _modulation(cond, ada_w, ada_b):
    depth, d, n = ada_w.shape
    tn = MOD_COL_TILE
    return pl.pallas_call(
        _mod_body,
        grid=(depth, n // tn),
        in_specs=[
            pl.BlockSpec((SUBLANES, d), lambda l, j: (0, 0)),
            pl.BlockSpec((None, d, tn), lambda l, j: (l, 0, j)),
            pl.BlockSpec((None, 1, tn), lambda l, j: (l, 0, j)),
        ],
        out_specs=pl.BlockSpec((None, SUBLANES, tn), lambda l, j: (l, 0, j)),
        out_shape=jax.ShapeDtypeStruct((depth, SUBLANES, n), F32),
        compiler_params=_cparams(("parallel", "parallel"), 40),
        name="adaln_projection",
    )(cond, ada_w, ada_b.reshape(depth, 1, n))


def _in0_body(x_ref, g_ref, sh_ref, sc_ref, w_ref, xb_ref, gg_ref, gu_ref, gv_ref):
    h = _modulated(x_ref[...], g_ref[...], sh_ref[...], sc_ref[...])
    z = jnp.dot(h.astype(BF16), w_ref[...], preferred_element_type=F32)
    w = xb_ref.shape[1]
    xb_ref[...] = z[:, :w]
    gg_ref[...] = jax.nn.gelu(z[:, w:2 * w]).astype(BF16)
    gu_ref[...] = jax.nn.gelu(z[:, 2 * w:3 * w]).astype(BF16)
    gv_ref[...] = jax.nn.gelu(z[:, 3 * w:]).astype(BF16)


def _even_in_proj(x, mod, layer, gain, w_in, n_prompt, sample_seq):
    t, d = x.shape
    tm = TOKEN_TILE
    w4 = w_in.shape[1]
    w = w4 // 4
    row = lambda i: (i, 0)
    ms = functools.partial(_mod_spec, layer, tile=tm, n_prompt=n_prompt, sample_seq=sample_seq, d=d)
    return pl.pallas_call(
        _in0_body,
        grid=(t // tm,),
        in_specs=[
            pl.BlockSpec((tm, d), row),
            _resident((1, d)),
            ms(chunk=0),
            ms(chunk=1),
            _resident((d, w4)),
        ],
        out_specs=[pl.BlockSpec((tm, w), row)] * 4,
        out_shape=[
            jax.ShapeDtypeStruct((t, w), F32),
            jax.ShapeDtypeStruct((t, w), BF16),
            jax.ShapeDtypeStruct((t, w), BF16),
            jax.ShapeDtypeStruct((t, w), BF16),
        ],
        compiler_params=_cparams(("parallel",), 48),
        name="even_in_proj",
    )(x, gain.reshape(1, d), mod, mod, w_in.astype(BF16))


def _tile_scan(a, b, row, reverse):
    for d in (1, 2, 4):
        if reverse:
            keep = row < SUBLANES - d
            a_s = jnp.where(keep, pltpu.roll(a, SUBLANES - d, 0), 1.0)
            b_s = jnp.where(keep, pltpu.roll(b, SUBLANES - d, 0), 0.0)
        else:
            keep = row >= d
            a_s = jnp.where(keep, pltpu.roll(a, d, 0), 1.0)
            b_s = jnp.where(keep, pltpu.roll(b, d, 0), 0.0)
        b = b + a * b_s
        a = a * a_s
    return a, b


def _lru_body(xb_ref, gg_ref, h0f_ref, h0b_ref, cw_ref, cb_ref, w_ref, bias_ref, lam_ref,
              ya_ref, ff_ref, fb_ref,
              xp_ref, af_ref, bf_ref, ab_ref, bb_ref, hf_ref, hb_ref, *, seq, chunk):
    hw = LANES
    pad = SUBLANES
    xp_ref[pl.ds(0, pad), :] = jnp.zeros((pad, hw), F32)
    xp_ref[pl.ds(pad + seq, pad), :] = jnp.zeros((pad, hw), F32)
    xp_ref[pl.ds(pad, seq), :] = xb_ref[...]

    cw = cw_ref[...]
    cb = cb_ref[...]
    lam = lam_ref[...]
    neg = -lam
    softplus = jnp.maximum(neg, 0.0) + jnp.log1p(jnp.exp(-jnp.abs(neg)))
    nsp = -LRU_C * softplus
    w = w_ref[...]
    bias = bias_ref[...]

    for c in range(seq // chunk):
        base = c * chunk
        xc = (cw[0:1] * xp_ref[pl.ds(base + pad - 2, chunk), :]
              + cw[1:2] * xp_ref[pl.ds(base + pad - 1, chunk), :]
              + cw[2:3] * xp_ref[pl.ds(base + pad, chunk), :]
              + cw[3:4] * xp_ref[pl.ds(base + pad + 1, chunk), :]) + cb
        g = jnp.dot(xc.astype(BF16), w, preferred_element_type=F32) + bias
        for direction, (a_ref, b_ref) in enumerate(((af_ref, bf_ref), (ab_ref, bb_ref))):
            r = jax.nn.sigmoid(g[:, (2 * direction) * hw:(2 * direction + 1) * hw])
            gi = jax.nn.sigmoid(g[:, (2 * direction + 1) * hw:(2 * direction + 2) * hw])
            log_a = r * nsp[direction:direction + 1]
            a = jnp.exp(log_a)
            a_ref[pl.ds(base, chunk), :] = a
            b_ref[pl.ds(base, chunk), :] = jnp.sqrt(1.0 - a * a) * gi * xc

    n_tiles = seq // SUBLANES
    row = lax.broadcasted_iota(I32, (SUBLANES, hw), 0)

    def step(i, carry):
        hf, hb = carry
        rf = pl.multiple_of(i * SUBLANES, SUBLANES)
        rb = pl.multiple_of((n_tiles - 1 - i) * SUBLANES, SUBLANES)
        a, b = _tile_scan(af_ref[pl.ds(rf, SUBLANES), :], bf_ref[pl.ds(rf, SUBLANES), :], row, False)
        h = b + a * hf
        hf_ref[pl.ds(rf, SUBLANES), :] = h
        hf = jnp.broadcast_to(h[SUBLANES - 1:SUBLANES, :], (SUBLANES, hw))
        a, b = _tile_scan(ab_ref[pl.ds(rb, SUBLANES), :], bb_ref[pl.ds(rb, SUBLANES), :], row, True)
        h = b + a * hb
        hb_ref[pl.ds(rb, SUBLANES), :] = h
        hb = jnp.broadcast_to(h[0:1, :], (SUBLANES, hw))
        return hf, hb

    hf0 = jnp.broadcast_to(h0f_ref[...], (SUBLANES, hw))
    hb0 = jnp.broadcast_to(h0b_ref[...], (SUBLANES, hw))
    hf, hb = lax.fori_loop(0, n_tiles, step, (hf0, hb0), unroll=2)
    ff_ref[...] = hf[0:1, :]
    fb_ref[...] = hb[0:1, :]
    ya_ref[...] = ((hf_ref[...] + hb_ref[...]) * gg_ref[...].astype(F32)).astype(BF16)


def _lru_mixer(xb, gg, h0f, h0b, conv_w, conv_b, wcat, bcat, lam, *, row0, batch, seq):
    _, width = xb.shape
    heads = width // LANES
    blk0 = row0 // seq
    tok = lambda b, h: (blk0 + b, h)
    state = lambda b, h: (b, 0, h)
    seq_buf = pltpu.VMEM((seq, LANES), F32)
    return pl.pallas_call(
        functools.partial(_lru_body, seq=seq, chunk=min(seq, LRU_GATE_CHUNK)),
        grid=(batch, heads),
        in_specs=[
            pl.BlockSpec((seq, LANES), tok),
            pl.BlockSpec((seq, LANES), tok),
            pl.BlockSpec((None, 1, LANES), state),
            pl.BlockSpec((None, 1, LANES), state),
            pl.BlockSpec((conv_w.shape[0], LANES), lambda b, h: (0, h)),
            pl.BlockSpec((1, LANES), lambda b, h: (0, h)),
            pl.BlockSpec((None, LANES, 4 * LANES), lambda b, h: (h, 0, 0)),
            pl.BlockSpec((None, 1, 4 * LANES), lambda b, h: (h, 0, 0)),
            pl.BlockSpec((2, LANES), lambda b, h: (0, h)),
        ],
        out_specs=[
            pl.BlockSpec((seq, LANES), lambda b, h: (b, h)),
            pl.BlockSpec((None, 1, LANES), state),
            pl.BlockSpec((None, 1, LANES), state),
        ],
        out_shape=[
            jax.ShapeDtypeStruct((batch * seq, width), BF16),
            jax.ShapeDtypeStruct((batch, 1, width), F32),
            jax.ShapeDtypeStruct((batch, 1, width), F32),
        ],
        scratch_shapes=[pltpu.VMEM((seq + 2 * SUBLANES, LANES), F32)] + [seq_buf] * 6,
        compiler_params=_cparams(("parallel", "parallel"), 40),
        name=f"rglru_seq{seq}",
    )(xb, gg, h0f, h0b, conv_w, conv_b, wcat, bcat, lam)


def _group_select(i, prompt_tiles, p_ref, s_ref):
    return jnp.where(i < prompt_tiles, p_ref[...], s_ref[...])


def _group_specs(block, prompt_tiles):
    return [pl.BlockSpec(block, lambda i: (jnp.minimum(i, prompt_tiles - 1), 0)),
            pl.BlockSpec(block, lambda i: (jnp.maximum(i - prompt_tiles, 0), 0))]


def _out0_body(x_ref, gate_ref, yap_ref, yas_ref, gu_ref, gv_ref, vg_ref, ws_ref, bs_ref, wa_ref, wb_ref,
               o_ref, yb_ref, *, chunk, prompt_tiles):
    tm = x_ref.shape[0]
    ya = _group_select(pl.program_id(0), prompt_tiles, yap_ref, yas_ref)
    v = gv_ref[...].astype(F32)
    vn = (v * lax.rsqrt(jnp.mean(v * v, axis=-1, keepdims=True) + EPS) * vg_ref[...]).astype(BF16)
    groups = ws_ref.shape[0]
    gd = vn.shape[1] // groups
    for c in range(tm // chunk):
        rows = slice(c * chunk, (c + 1) * chunk)
        for g in range(groups):
            cols = slice(g * gd, (g + 1) * gd)
            mixed = jnp.dot(ws_ref[g], vn[rows, cols], preferred_element_type=F32) + bs_ref[:, cols]
            yb_ref[rows, cols] = (gu_ref[rows, cols].astype(F32) * mixed).astype(BF16)
    y = (jnp.dot(ya, wa_ref[...], preferred_element_type=F32)
         + jnp.dot(yb_ref[...], wb_ref[...], preferred_element_type=F32))
    o_ref[...] = x_ref[...] + gate_ref[...] * y


def _even_out_proj(x, mod, layer, ya_p, ya_s, gu, gv, v_gain, w_s, b_full, w_out, n_prompt, sample_seq):
    t, d = x.shape
    tm = TOKEN_TILE
    w = ya_p.shape[1]
    row = lambda i: (i, 0)
    chunk = w_s.shape[1]
    prompt_tiles = n_prompt // tm
    return pl.pallas_call(
        functools.partial(_out0_body, chunk=chunk, prompt_tiles=prompt_tiles),
        grid=(t // tm,),
        in_specs=[
            pl.BlockSpec((tm, d), row),
            _mod_spec(layer, 2, tm, n_prompt, sample_seq, d),
            *_group_specs((tm, w), prompt_tiles),
            pl.BlockSpec((tm, w), row),
            pl.BlockSpec((tm, w), row),
            _resident((1, w)),
            _resident(w_s.shape),
            _resident(b_full.shape),
            _resident((w, d)),
            _resident((w, d)),
        ],
        out_specs=pl.BlockSpec((tm, d), row),
        out_shape=jax.ShapeDtypeStruct((t, d), F32),
        scratch_shapes=[pltpu.VMEM((tm, w), BF16)],
        compiler_params=_cparams(("parallel",), 40),
        name="even_out_proj",
    )(x, mod, ya_p, ya_s, gu, gv, v_gain.reshape(1, w), w_s.astype(BF16), b_full,
      w_out[:w].astype(BF16), w_out[w:].astype(BF16))


def _swap_halves(x, lane):
    quarter = ROPE_QUARTER
    up = pltpu.roll(x, LANES - quarter, 1)
    down = pltpu.roll(x, quarter, 1)
    return jnp.where((lane % (2 * quarter)) < quarter, up, down)


def _mla_in_body(x_ref, g_ref, sh_ref, sc_ref, wd_ref, qan_ref, kvan_ref, wuq_ref, qn_ref, cos_ref, sin_ref,
                 q_ref, ckv_ref, kr_ref, *, heads, q_lora, kv_lora, qk_dim, sm_scale):
    h = _modulated(x_ref[...], g_ref[...], sh_ref[...], sc_ref[...])
    z = jnp.dot(h.astype(BF16), wd_ref[...], preferred_element_type=F32)
    cq = z[:, :q_lora]
    cq = cq * lax.rsqrt(jnp.mean(cq * cq, axis=-1, keepdims=True) + EPS) * qan_ref[...]
    ckv = z[:, q_lora:q_lora + kv_lora]
    ckv_ref[...] = ckv * lax.rsqrt(jnp.mean(ckv * ckv, axis=-1, keepdims=True) + EPS) * kvan_ref[...]
    kr_ref[...] = z[:, q_lora + kv_lora:]
    q = jnp.dot(cq.astype(BF16), wuq_ref[...], preferred_element_type=F32)
    tm = q.shape[0]
    lane = lax.broadcasted_iota(I32, (tm, LANES), 1)
    gn = qn_ref[0:1, :]
    gr = qn_ref[1:2, :]
    grs = qn_ref[2:3, :]
    cos = cos_ref[...]
    sin = sin_ref[...]
    for hd in range(heads):
        qn = q[:, 2 * hd * LANES:(2 * hd + 1) * LANES]
        qr = q[:, (2 * hd + 1) * LANES:(2 * hd + 2) * LANES]
        ss = jnp.sum(qn * qn, axis=-1, keepdims=True) + jnp.sum(qr * qr, axis=-1, keepdims=True)
        rinv = lax.rsqrt(ss * (1.0 / qk_dim) + EPS) * sm_scale
        q_ref[:, 2 * hd * LANES:(2 * hd + 1) * LANES] = (qn * gn * rinv).astype(BF16)
        rot = (qr * gr) * cos + (_swap_halves(qr, lane) * grs) * sin
        q_ref[:, (2 * hd + 1) * LANES:(2 * hd + 2) * LANES] = (rot * rinv).astype(BF16)


def _rope_spec(tile, n_prompt, sample_seq):
    def index(i):
        t = i * tile
        return (jnp.where(t < n_prompt, 0, 1 + ((t - n_prompt) % sample_seq) // tile), 0)

    return pl.BlockSpec((tile, LANES), index)


def _mla_in_proj(x, mod, layer, gain, wd_ext, q_a_norm, kv_a_norm, wuq_pad, qn_rows, cos_tab, sin_tab,
                 n_prompt, sample_seq, *, heads, q_lora, kv_lora, qk_dim):
    t, d = x.shape
    tm = TOKEN_TILE
    row = lambda i: (i, 0)
    ms = functools.partial(_mod_spec, layer, tile=tm, n_prompt=n_prompt, sample_seq=sample_seq, d=d)
    body = functools.partial(_mla_in_body, heads=heads, q_lora=q_lora, kv_lora=kv_lora, qk_dim=qk_dim,
                             sm_scale=float(qk_dim) ** -0.5)
    return pl.pallas_call(
        body,
        grid=(t // tm,),
        in_specs=[
            pl.BlockSpec((tm, d), row),
            _resident((1, d)),
            ms(chunk=0),
            ms(chunk=1),
            _resident(wd_ext.shape),
            _resident((1, q_lora)),
            _resident((1, kv_lora)),
            _resident(wuq_pad.shape),
            _resident(qn_rows.shape),
            _rope_spec(tm, n_prompt, sample_seq),
            _rope_spec(tm, n_prompt, sample_seq),
        ],
        out_specs=[
            pl.BlockSpec((tm, heads * 2 * LANES), row),
            pl.BlockSpec((tm, kv_lora), row),
            pl.BlockSpec((tm, LANES), row),
        ],
        out_shape=[
            jax.ShapeDtypeStruct((t, heads * 2 * LANES), BF16),
            jax.ShapeDtypeStruct((t, kv_lora), F32),
            jax.ShapeDtypeStruct((t, LANES), F32),
        ],
        compiler_params=_cparams(("parallel",), 48),
        name="mla_in_proj",
    )(x, gain.reshape(1, d), mod, mod, wd_ext, q_a_norm.reshape(1, q_lora), kv_a_norm.reshape(1, kv_lora),
      wuq_pad, qn_rows, cos_tab, sin_tab)


def _kv_body(ckv_ref, kr_ref, w_ref, kn_ref, cos_ref, sin_ref, k_ref, v_ref, *, heads, qk_dim):
    kv = jnp.dot(ckv_ref[...].astype(BF16), w_ref[...], preferred_element_type=F32)
    tm = kv.shape[0]
    lane = lax.broadcasted_iota(I32, (tm, LANES), 1)
    kr = kr_ref[...]
    gn = kn_ref[0:1, :]
    gr = kn_ref[1:2, :]
    grs = kn_ref[2:3, :]
    ssr = jnp.sum(kr * kr, axis=-1, keepdims=True)
    rot = (kr * gr) * cos_ref[...] + (_swap_halves(kr, lane) * grs) * sin_ref[...]
    for hd in range(heads):
        kn = kv[:, hd * LANES:(hd + 1) * LANES]
        rinv = lax.rsqrt((jnp.sum(kn * kn, axis=-1, keepdims=True) + ssr) * (1.0 / qk_dim) + EPS)
        k_ref[:, 2 * hd * LANES:(2 * hd + 1) * LANES] = (kn * gn * rinv).astype(BF16)
        k_ref[:, (2 * hd + 1) * LANES:(2 * hd + 2) * LANES] = (rot * rinv).astype(BF16)
    v_ref[...] = kv[:, heads * LANES:].astype(BF16)


def _kv_expand(ckv, kr, w_ukv_re, kn_rows, cos_tab, sin_tab, rope_index, *, heads, qk_dim):
    rows, kv_lora = ckv.shape
    tm = TOKEN_TILE
    row = lambda i: (i, 0)
    return pl.pallas_call(
        functools.partial(_kv_body, heads=heads, qk_dim=qk_dim),
        grid=(rows // tm,),
        in_specs=[
            pl.BlockSpec((tm, kv_lora), row),
            pl.BlockSpec((tm, LANES), row),
            _resident(w_ukv_re.shape),
            _resident(kn_rows.shape),
            pl.BlockSpec((tm, LANES), rope_index),
            pl.BlockSpec((tm, LANES), rope_index),
        ],
        out_specs=[pl.BlockSpec((tm, heads * 2 * LANES), row), pl.BlockSpec((tm, heads * LANES), row)],
        out_shape=[
            jax.ShapeDtypeStruct((rows, heads * 2 * LANES), BF16),
            jax.ShapeDtypeStruct((rows, heads * LANES), BF16),
        ],
        compiler_params=_cparams(("parallel",), 40),
        name=f"mla_kv_expand_{rows}",
    )(ckv, kr, w_ukv_re, kn_rows, cos_tab, sin_tab)


def _attn_body(q_ref, k_ref, v_ref, o_ref):
    s = lax.dot_general(q_ref[...], k_ref[...], (((1,), (1,)), ((), ())), preferred_element_type=F32)
    m = jnp.max(s, axis=-1, keepdims=True)
    p = jnp.exp(s - m)
    l = jnp.sum(p, axis=-1, keepdims=True)
    o = jnp.dot(p.astype(BF16), v_ref[...], preferred_element_type=F32)
    o_ref[...] = (o / l).astype(BF16)


def _attention(q, k, v, *, row0, batch, seq, kv_len, heads, tq):
    nq = seq // tq
    qblk0 = row0 // tq
    return pl.pallas_call(
        _attn_body,
        grid=(batch, heads, nq),
        in_specs=[
            pl.BlockSpec((tq, 2 * LANES), lambda b, h, i: (qblk0 + b * nq + i, h)),
            pl.BlockSpec((kv_len, 2 * LANES), lambda b, h, i: (b, h)),
            pl.BlockSpec((kv_len, LANES), lambda b, h, i: (b, h)),
        ],
        out_specs=pl.BlockSpec((tq, LANES), lambda b, h, i: (b * nq + i, h)),
        out_shape=jax.ShapeDtypeStruct((batch * seq, heads * LANES), BF16),
        compiler_params=_cparams(("parallel", "parallel", "parallel"), 40),
        name=f"mla_attention_kv{kv_len}",
    )(q, k, v)


def _oproj_body(x_ref, gate_ref, op_ref, os_ref, w_ref, out_ref, *, prompt_tiles):
    o = _group_select(pl.program_id(0), prompt_tiles, op_ref, os_ref)
    y = jnp.dot(o, w_ref[...], preferred_element_type=F32)
    out_ref[...] = x_ref[...] + gate_ref[...] * y


def _mla_out_proj(x, mod, layer, o_p, o_s, w_o, n_prompt, sample_seq):
    t, d = x.shape
    tm = TOKEN_TILE
    row = lambda i: (i, 0)
    prompt_tiles = n_prompt // tm
    return pl.pallas_call(
        functools.partial(_oproj_body, prompt_tiles=prompt_tiles),
        grid=(t // tm,),
        in_specs=[
            pl.BlockSpec((tm, d), row),
            _mod_spec(layer, 2, tm, n_prompt, sample_seq, d),
            *_group_specs((tm, o_p.shape[1]), prompt_tiles),
            _resident(w_o.shape),
        ],
        out_specs=pl.BlockSpec((tm, d), row),
        out_shape=jax.ShapeDtypeStruct((t, d), F32),
        compiler_params=_cparams(("parallel",), 40),
        name="mla_out_proj",
    )(x, mod, o_p, o_s, w_o.astype(BF16))


def _pack_bf16_pair(lo_half, hi_half):
    a = pltpu.bitcast(lo_half.astype(BF16).astype(F32), U32)
    b = pltpu.bitcast(hi_half.astype(BF16).astype(F32), U32)
    return a | (b >> 16)


def _unpack_bf16_pair(p):
    a = pltpu.bitcast(p & jnp.uint32(0xFFFF0000), F32)
    b = pltpu.bitcast(p << 16, F32)
    return a, b


def _router_body(x_ref, g_ref, sh_ref, sc_ref, whi_ref, wlo_ref, rb_ref,
                 hp_ref, ex_ref, wt_ref, rk_ref, cnt_ref, carry_ref, *, n_experts):
    i = pl.program_id(0)

    @pl.when(i == 0)
    def _():
        carry_ref[...] = jnp.zeros(carry_ref.shape, F32)

    h = _modulated(x_ref[...], g_ref[...], sh_ref[...], sc_ref[...])
    tm, d = h.shape
    hp_ref[...] = _pack_bf16_pair(h[:, :d // 2], h[:, d // 2:])

    h_hi = h.astype(BF16)
    h_lo = (h - h_hi.astype(F32)).astype(BF16)
    nt = (((1,), (1,)), ((), ()))
    logits = (lax.dot_general(whi_ref[...], h_hi, nt, preferred_element_type=F32)
              + lax.dot_general(whi_ref[...], h_lo, nt, preferred_element_type=F32)
              + lax.dot_general(wlo_ref[...], h_hi, nt, preferred_element_type=F32))
    scores = jax.nn.sigmoid(logits)
    biased = scores + rb_ref[...]
    per_group = n_experts // N_EXPERT_GROUPS
    assert per_group == 4 and TOP_K == 2
    b_rows = [biased[e:e + 1, :] for e in range(n_experts)]
    s_rows = [scores[e:e + 1, :] for e in range(n_experts)]

    best = None
    sel = jnp.zeros((1, tm), I32)
    for g in range(N_EXPERT_GROUPS):
        b0, b1, b2, b3 = b_rows[g * 4:(g + 1) * 4]
        m1, n1 = jnp.maximum(b0, b1), jnp.minimum(b0, b1)
        m2, n2 = jnp.maximum(b2, b3), jnp.minimum(b2, b3)
        top1 = jnp.maximum(m1, m2)
        top2 = jnp.maximum(jnp.minimum(m1, m2), jnp.maximum(n1, n2))
        gsum = top1 + top2
        if best is None:
            best = gsum
        else:
            better = gsum > best
            sel = jnp.where(better, g, sel)
            best = jnp.where(better, gsum, best)

    def pick(rows, j):
        out = rows[j]
        for g in range(1, N_EXPERT_GROUPS):
            out = jnp.where(sel == g, rows[g * 4 + j], out)
        return out

    cand_b = [pick(b_rows, j) for j in range(4)]
    cand_s = [pick(s_rows, j) for j in range(4)]

    def argmax4(vals):
        bv, bi = vals[0], jnp.zeros((1, tm), I32)
        for j in range(1, 4):
            gt = vals[j] > bv
            bi = jnp.where(gt, j, bi)
            bv = jnp.where(gt, vals[j], bv)
        return bi

    i1 = argmax4(cand_b)
    i2 = argmax4([jnp.where(i1 == j, -jnp.inf, cand_b[j]) for j in range(4)])

    def take(vals, idx):
        out = vals[0]
        for j in range(1, 4):
            out = jnp.where(idx == j, vals[j], out)
        return out

    s1 = take(cand_s, i1)
    s2 = take(cand_s, i2)
    tot = s1 + s2
    e1 = sel * 4 + i1
    e2 = sel * 4 + i2
    ex_ref[0:1, :] = e1
    ex_ref[1:2, :] = e2
    wt_ref[0:1, :] = s1 / tot
    wt_ref[1:2, :] = s2 / tot

    eid = lax.broadcasted_iota(I32, (n_experts, tm), 0)
    is1 = eid == e1
    is2 = eid == e2
    chosen = jnp.where(is1 | is2, 1.0, 0.0)
    before = (lax.broadcasted_iota(I32, (tm, tm), 0) < lax.broadcasted_iota(I32, (tm, tm), 1))
    rank = jnp.dot(chosen.astype(BF16), jnp.where(before, 1.0, 0.0).astype(BF16),
                   preferred_element_type=F32) + carry_ref[:, 0:1]
    rk_ref[0:1, :] = jnp.sum(jnp.where(is1, rank, 0.0), axis=0, keepdims=True).astype(I32)
    rk_ref[1:2, :] = jnp.sum(jnp.where(is2, rank, 0.0), axis=0, keepdims=True).astype(I32)
    carry_ref[...] = carry_ref[...] + jnp.sum(chosen, axis=1, keepdims=True)
    cnt_ref[...] = carry_ref[...]


def _router(x, mod, layer, gain, w_hi, w_lo, r_bias, n_prompt, sample_seq):
    t, d = x.shape
    tm = TOKEN_TILE
    n_experts = w_hi.shape[0]
    ms = functools.partial(_mod_spec, layer, tile=tm, n_prompt=n_prompt, sample_seq=sample_seq, d=d)
    col = lambda i: (0, i)
    return pl.pallas_call(
        functools.partial(_router_body, n_experts=n_experts),
        grid=(t // tm,),
        in_specs=[
            pl.BlockSpec((tm, d), lambda i: (i, 0)),
            _resident((1, d)),
            ms(chunk=3),
            ms(chunk=4),
            _resident(w_hi.shape),
            _resident(w_lo.shape),
            _resident((n_experts, 1)),
        ],
        out_specs=[
            pl.BlockSpec((tm, d // 2), lambda i: (i, 0)),
            pl.BlockSpec((TOP_K, tm), col),
            pl.BlockSpec((TOP_K, tm), col),
            pl.BlockSpec((TOP_K, tm), col),
            pl.BlockSpec((n_experts, LANES), lambda i: (0, 0)),
        ],
        out_shape=[
            jax.ShapeDtypeStruct((t, d // 2), U32),
            jax.ShapeDtypeStruct((TOP_K, t), I32),
            jax.ShapeDtypeStruct((TOP_K, t), F32),
            jax.ShapeDtypeStruct((TOP_K, t), I32),
            jax.ShapeDtypeStruct((n_experts, LANES), F32),
        ],
        scratch_shapes=[pltpu.VMEM((n_experts, LANES), F32)],
        compiler_params=_cparams(("arbitrary",), 40),
        name="moe_router",
    )(x, gain.reshape(1, d), mod, mod, w_hi, w_lo, r_bias.reshape(n_experts, 1))


def _row_copy(src_ref, src_row, dst_ref, dst_row, sem):
    return pltpu.make_async_copy(src_ref.at[pl.ds(src_row, 1)], dst_ref.at[pl.ds(dst_row, 1)], sem)


def _dispatch_body(pos_ref, h_ref, init_ref, o_ref, sem):
    del init_ref
    tm = h_ref.shape[0]

    def issue(r, c):
        for k in range(TOP_K):
            _row_copy(h_ref, r, o_ref, pos_ref[k, r], sem).start()
        return c

    lax.fori_loop(0, tm, issue, 0)

    def drain(r, c):
        for k in range(TOP_K):
            _row_copy(h_ref, 0, o_ref, 0, sem).wait()
        return c

    lax.fori_loop(0, tm, drain, 0)


def _dispatch(hp, pos, sorted_rows):
    t, w = hp.shape
    tm = TOKEN_TILE
    init = jnp.zeros((sorted_rows, w), hp.dtype)
    return pl.pallas_call(
        _dispatch_body,
        grid=(t // tm,),
        in_specs=[
            pl.BlockSpec((TOP_K, tm), lambda i: (0, i), memory_space=pltpu.SMEM),
            pl.BlockSpec((tm, w), lambda i: (i, 0)),
            pl.BlockSpec(memory_space=pl.ANY),
        ],
        out_specs=pl.BlockSpec(memory_space=pl.ANY),
        out_shape=jax.ShapeDtypeStruct((sorted_rows, w), hp.dtype),
        scratch_shapes=[pltpu.SemaphoreType.DMA],
        input_output_aliases={2: 0},
        compiler_params=_cparams(("arbitrary",), 32),
        name="moe_dispatch",
    )(pos, hp, init)


def _expert_body(te_ref, nv_ref, x_ref, wg_ref, wu_ref, wd_ref, y_ref):
    i = pl.program_id(0)

    @pl.when(i < nv_ref[0])
    def _():
        a, b = _unpack_bf16_pair(x_ref[...])
        a = a.astype(BF16)
        b = b.astype(BF16)
        half = a.shape[1]
        g = (jnp.dot(a, wg_ref[:half, :], preferred_element_type=F32)
             + jnp.dot(b, wg_ref[half:, :], preferred_element_type=F32))
        u = (jnp.dot(a, wu_ref[:half, :], preferred_element_type=F32)
             + jnp.dot(b, wu_ref[half:, :], preferred_element_type=F32))
        act = (g * jax.nn.sigmoid(g) * u).astype(BF16)
        y = jnp.dot(act, wd_ref[...], preferred_element_type=F32)
        d = y.shape[1]
        y_ref[...] = _pack_bf16_pair(y[:, :d // 2], y[:, d // 2:])

    @pl.when(i >= nv_ref[0])
    def _():
        y_ref[...] = jnp.zeros(y_ref.shape, U32)


def _experts(xs, tile_expert, n_valid, w_gate, w_up, w_down, layer):
    rows, half = xs.shape
    tm = EXPERT_TILE
    _, n_experts, d, f = w_gate.shape

    def xrow(i, te, nv):
        return (jnp.minimum(i, nv[0] - 1), 0)

    grid_spec = pltpu.PrefetchScalarGridSpec(
        num_scalar_prefetch=2,
        grid=(rows // tm,),
        in_specs=[
            pl.BlockSpec((tm, half), xrow),
            pl.BlockSpec((None, None, d, f), lambda i, te, nv: (layer, te[i], 0, 0)),
            pl.BlockSpec((None, None, d, f), lambda i, te, nv: (layer, te[i], 0, 0)),
            pl.BlockSpec((None, None, f, d), lambda i, te, nv: (layer, te[i], 0, 0)),
        ],
        out_specs=pl.BlockSpec((tm, half), lambda i, te, nv: (i, 0)),
    )
    return pl.pallas_call(
        _expert_body,
        grid_spec=grid_spec,
        out_shape=jax.ShapeDtypeStruct((rows, half), U32),
        compiler_params=_cparams(("arbitrary",), 40),
        name="moe_experts",
    )(tile_expert, n_valid, xs, w_gate, w_up, w_down)


def _combine_body(pos_ref, x_ref, gate_ref, wt_ref, ys_ref, o_ref, buf_ref, sem):
    tm = x_ref.shape[0]

    def issue(r, c):
        for k in range(TOP_K):
            _row_copy(ys_ref, pos_ref[k, r], buf_ref.at[k], r, sem).start()
        return c

    lax.fori_loop(0, tm, issue, 0)

    def drain(r, c):
        for k in range(TOP_K):
            _row_copy(ys_ref, 0, buf_ref.at[k], 0, sem).wait()
        return c

    lax.fori_loop(0, tm, drain, 0)
    wt = wt_ref[...]
    a0, b0 = _unpack_bf16_pair(buf_ref[0])
    a1, b1 = _unpack_bf16_pair(buf_ref[1])
    w0 = wt[:, 0:1]
    w1 = wt[:, 1:2]
    half = a0.shape[1]
    gate = gate_ref[...]
    o_ref[:, :half] = x_ref[:, :half] + gate[:, :half] * (w0 * a0 + w1 * a1)
    o_ref[:, half:] = x_ref[:, half:] + gate[:, half:] * (w0 * b0 + w1 * b1)


def _combine(x, mod, layer, ys, pos, wts_t, n_prompt, sample_seq):
    t, d = x.shape
    tm = TOKEN_TILE
    return pl.pallas_call(
        _combine_body,
        grid=(t // tm,),
        in_specs=[
            pl.BlockSpec((TOP_K, tm), lambda i: (0, i), memory_space=pltpu.SMEM),
            pl.BlockSpec((tm, d), lambda i: (i, 0)),
            _mod_spec(layer, 5, tm, n_prompt, sample_seq, d),
            pl.BlockSpec((tm, TOP_K), lambda i: (i, 0)),
            pl.BlockSpec(memory_space=pl.ANY),
        ],
        out_specs=pl.BlockSpec((tm, d), lambda i: (i, 0)),
        out_shape=jax.ShapeDtypeStruct((t, d), F32),
        scratch_shapes=[pltpu.VMEM((TOP_K, tm, d // 2), U32), pltpu.SemaphoreType.DMA],
        compiler_params=_cparams(("arbitrary",), 32),
        name="moe_combine",
    )(pos, x, mod, wts_t, ys)


def _moe(x, mod, layer, gain, w_hi, w_lo, r_bias, w_gate, w_up, w_down, n_prompt, sample_seq):
    t, d = x.shape
    n_experts = w_hi.shape[0]
    hp, ex, wt, rk, cnt = _router(x, mod, layer, gain, w_hi, w_lo, r_bias, n_prompt, sample_seq)
    counts = cnt[:, 0].astype(I32)
    tiles = (counts + EXPERT_TILE - 1) // EXPERT_TILE
    tile_end = jnp.cumsum(tiles)
    seg_start = (tile_end - tiles) * EXPERT_TILE
    pos = seg_start[ex] + rk
    sorted_rows = TOP_K * t + n_experts * EXPERT_TILE
    n_tiles = sorted_rows // EXPERT_TILE
    n_valid = tile_end[-1:]
    tile_expert = jnp.minimum(jnp.searchsorted(tile_end, jnp.arange(n_tiles, dtype=I32), side="right"),
                              n_experts - 1).astype(I32)
    tile_expert = jnp.where(jnp.arange(n_tiles) < n_valid[0], tile_expert, tile_expert[n_valid[0] - 1])
    xs = _dispatch(hp, pos, sorted_rows)
    ys = _experts(xs, tile_expert, n_valid.astype(I32), w_gate, w_up, w_down, layer)
    return _combine(x, mod, layer, ys, pos, wt.T, n_prompt, sample_seq)


def _rope_tables(sample_seq, rope_dim, tile):
    n_freq = rope_dim // 4
    pos = jnp.arange(sample_seq)
    row_pos = (pos // GRID_W).astype(F32)
    col_pos = (pos % GRID_W).astype(F32)
    inv_freq = ROPE_BASE ** (-jnp.arange(n_freq, dtype=F32) / n_freq)
    ar = row_pos[:, None] * inv_freq
    ac = col_pos[:, None] * inv_freq
    zeros = jnp.zeros((sample_seq, LANES - rope_dim), F32)
    cos = jnp.concatenate([jnp.cos(ar), jnp.cos(ar), jnp.cos(ac), jnp.cos(ac), zeros], axis=1)
    sin = jnp.concatenate([-jnp.sin(ar), jnp.sin(ar), -jnp.sin(ac), jnp.sin(ac), zeros], axis=1)
    ident_c = jnp.concatenate([jnp.ones((tile, rope_dim), F32), jnp.zeros((tile, LANES - rope_dim), F32)], axis=1)
    return (jnp.concatenate([ident_c, cos], axis=0),
            jnp.concatenate([jnp.zeros((tile, LANES), F32), sin], axis=0))


def _norm_rows(gain, nope, rope_dim):
    quarter = rope_dim // 4
    gr = gain[nope:]
    grs = jnp.concatenate([gr[quarter:2 * quarter], gr[:quarter], gr[3 * quarter:], gr[2 * quarter:3 * quarter]])
    zpad = jnp.zeros((LANES - rope_dim,), F32)
    return jnp.stack([gain[:nope], jnp.concatenate([gr, zpad]), jnp.concatenate([grs, zpad])])


def kernel(x_prompt, x_sample, state_lru_fwd, state_lru_bwd, cache_mla_ckv, cache_mla_krope, c, c_ctx,
           ada_w, ada_b, norm_mix, norm_ffn, mix0_w_in, mix0_w_out, lru_conv_w, lru_conv_b,
           lru_w_r, lru_b_r, lru_w_i, lru_b_i, lru_lambda, gmlp_v_norm, gmlp_w_s, gmlp_b_s,
           mla_w_down, mla_q_a_norm, mla_kv_a_norm, mla_w_uq, mla_w_ukv, mla_q_norm, mla_k_norm, mla_w_o,
           router_w, router_bias, moe_w_gate, moe_w_up, moe_w_down):
    batch, seq, d = x_prompt.shape
    dec_batch, dec_seq, _ = x_sample.shape
    depth = ada_w.shape[0]
    n_prompt = batch * seq
    n_sample = dec_batch * dec_seq
    assert n_prompt % dec_seq == 0 and seq % TOKEN_TILE == 0 and dec_seq % TOKEN_TILE == 0
    assert 1 + dec_batch <= SUBLANES

    x = jnp.concatenate([x_prompt.reshape(n_prompt, d), x_sample.reshape(n_sample, d)], axis=0)

    cond = jnp.concatenate([c_ctx[None, :], c, jnp.zeros((SUBLANES - 1 - dec_batch, d), F32)], axis=0)
    mod = _modulation(cond, ada_w, ada_b).reshape(depth, SUBLANES, 6, 1, d)

    n_experts = router_w.shape[1]
    rw_t = router_w.T
    rw_hi = rw_t.astype(BF16)
    rw_lo = (rw_t - rw_hi.astype(F32)).astype(BF16)
    wg_bf = moe_w_gate.astype(BF16)
    wu_bf = moe_w_up.astype(BF16)
    wdn_bf = moe_w_down.astype(BF16)

    fwd_states, bwd_states, ckv_caches, krope_caches = [], [], [], []
    for layer in range(depth):
        j = layer // 2
        if layer % 2 == 0:
            width = lru_conv_w.shape[2]
            heads = lru_w_r.shape[2]
            xb, gg, gu, gv = _even_in_proj(x, mod, layer, norm_mix[layer], mix0_w_in[j], n_prompt, dec_seq)
            wcat = jnp.concatenate([lru_w_r[j, 0], lru_w_i[j, 0], lru_w_r[j, 1], lru_w_i[j, 1]], axis=-1).astype(BF16)
            hb = lambda v: v.reshape(heads, 1, width // heads)
            bcat = jnp.concatenate([hb(lru_b_r[j, 0]), hb(lru_b_i[j, 0]), hb(lru_b_r[j, 1]), hb(lru_b_i[j, 1])], axis=-1)
            lru_args = (lru_conv_w[j], lru_conv_b[j].reshape(1, width), wcat, bcat, lru_lambda[j])
            zero_state = jnp.zeros((batch, 1, width), F32)
            ya_p, fin_f, fin_b = _lru_mixer(xb, gg, zero_state, zero_state, *lru_args,
                                            row0=0, batch=batch, seq=seq)
            ya_s, _, _ = _lru_mixer(xb, gg, state_lru_fwd[:, j][:, None, :], state_lru_bwd[:, j][:, None, :],
                                    *lru_args, row0=n_prompt, batch=dec_batch, seq=dec_seq)
            fwd_states.append(fin_f[:, 0, :])
            bwd_states.append(fin_b[:, 0, :])
            groups, chunk, _ = gmlp_w_s[j].shape
            gd = width // groups
            b_full = jnp.repeat(gmlp_b_s[j].T, gd, axis=1)
            x = _even_out_proj(x, mod, layer, ya_p, ya_s, gu, gv, gmlp_v_norm[j], gmlp_w_s[j], b_full, mix0_w_out[j],
                               n_prompt, dec_seq)
        else:
            q_lora = mla_q_a_norm.shape[1]
            kv_lora = mla_kv_a_norm.shape[1]
            qk_dim = mla_q_norm.shape[1]
            rope_dim = cache_mla_krope.shape[-1]
            nope = qk_dim - rope_dim
            heads = mla_w_uq.shape[2] // qk_dim
            v_dim = mla_w_ukv.shape[2] // heads - nope
            past = cache_mla_ckv.shape[2]
            assert nope == LANES and v_dim == LANES and rope_dim == 4 * ROPE_QUARTER
            wd = mla_w_down[j]
            wd_ext = jnp.concatenate([wd, jnp.zeros((d, LANES - rope_dim), F32)], axis=1).astype(BF16)
            wuq = mla_w_uq[j].reshape(q_lora, heads, qk_dim)
            wuq_pad = jnp.concatenate([wuq, jnp.zeros((q_lora, heads, 2 * LANES - qk_dim), F32)], axis=-1)
            wuq_pad = wuq_pad.reshape(q_lora, heads * 2 * LANES).astype(BF16)
            wukv = mla_w_ukv[j].reshape(kv_lora, heads, nope + v_dim)
            w_ukv_re = jnp.concatenate([wukv[:, :, :nope].reshape(kv_lora, heads * nope),
                                        wukv[:, :, nope:].reshape(kv_lora, heads * v_dim)], axis=1).astype(BF16)
            qn_rows = _norm_rows(mla_q_norm[j], nope, rope_dim)
            kn_rows = _norm_rows(mla_k_norm[j], nope, rope_dim)
            cos_tab, sin_tab = _rope_tables(dec_seq, rope_dim, TOKEN_TILE)
            q, ckv, kr = _mla_in_proj(x, mod, layer, norm_mix[layer], wd_ext, mla_q_a_norm[j], mla_kv_a_norm[j],
                                      wuq_pad, qn_rows, cos_tab, sin_tab, n_prompt, dec_seq,
                                      heads=heads, q_lora=q_lora, kv_lora=kv_lora, qk_dim=qk_dim)
            ckv_caches.append(ckv[:n_prompt].reshape(batch, seq, kv_lora))
            krope_caches.append(kr[:n_prompt, :rope_dim].reshape(batch, seq, rope_dim))
            expand = functools.partial(_kv_expand, w_ukv_re=w_ukv_re, kn_rows=kn_rows, cos_tab=cos_tab,
                                       sin_tab=sin_tab, heads=heads, qk_dim=qk_dim)
            k_p, v_p = expand(ckv[:n_prompt], kr[:n_prompt], rope_index=lambda i: (0, 0))
            kv_len = past + dec_seq
            ctx_kr = jnp.pad(cache_mla_krope[:, j], ((0, 0), (0, 0), (0, LANES - rope_dim)))
            ckv_s = jnp.concatenate([cache_mla_ckv[:, j], ckv[n_prompt:].reshape(dec_batch, dec_seq, kv_lora)], axis=1)
            kr_s = jnp.concatenate([ctx_kr, kr[n_prompt:].reshape(dec_batch, dec_seq, LANES)], axis=1)
            tiles_per_req = kv_len // TOKEN_TILE
            ctx_tiles = past // TOKEN_TILE
            assert past % TOKEN_TILE == 0

            def latent_rope(i):
                r = i % tiles_per_req
                return (jnp.where(r < ctx_tiles, 0, 1 + r - ctx_tiles), 0)

            k_s, v_s = expand(ckv_s.reshape(dec_batch * kv_len, kv_lora), kr_s.reshape(dec_batch * kv_len, LANES),
                              rope_index=latent_rope)
            o_p = _attention(q, k_p, v_p, row0=0, batch=batch, seq=seq, kv_len=seq, heads=heads, tq=seq)
            o_s = _attention(q, k_s, v_s, row0=n_prompt, batch=dec_batch, seq=dec_seq, kv_len=kv_len,
                             heads=heads, tq=ATTN_Q_TILE)
            x = _mla_out_proj(x, mod, layer, o_p, o_s, mla_w_o[j], n_prompt, dec_seq)
        x = _moe(x, mod, layer, norm_ffn[layer], rw_hi, rw_lo, router_bias, wg_bf, wu_bf, wdn_bf, n_prompt, dec_seq)

    xp = x[:n_prompt].reshape(batch, seq, d)
    xs = x[n_prompt:].reshape(dec_batch, dec_seq, d)
    return (xp, xs,
            jnp.stack(fwd_states, axis=1), jnp.stack(bwd_states, axis=1),
            jnp.stack(ckv_caches, axis=1), jnp.stack(krope_caches, axis=1))
```

```python
import functools

import jax
import jax.numpy as jnp
from jax import lax
from jax.experimental import pallas as pl
from jax.experimental.pallas import tpu as pltpu

F32 = jnp.float32
BF16 = jnp.bfloat16
U32 = jnp.uint32
I32 = jnp.int32

EPS = 1e-6
LRU_C = 8.0
GRID_W = 64
ROPE_BASE = 10000.0
ROPE_QUARTER = 16
N_EXPERT_GROUPS = 4
TOP_K = 2

LANES = 128
SUBLANES = 8
VMEM_BYTES_V7X = 64 * 1024 * 1024

TOKEN_TILE = 256
EXPERT_TILE = 256
MOD_COL_TILE = 1024
ATTN_Q_TILE = 512
ATTN_LATENT_HEADS_PER_STEP = 2
LRU_GATE_CHUNK = 256
DMA_ISSUE_UNROLL = 8


def _cparams(semantics, vmem_mb):
    return pltpu.CompilerParams(dimension_semantics=semantics, vmem_limit_bytes=vmem_mb * 1024 * 1024)


def _resident(shape):
    nd = len(shape)
    return pl.BlockSpec(shape, lambda *_: (0,) * nd, pipeline_mode=pl.Buffered(1))


def _mod_spec(layer, chunk, tile, n_prompt, sample_seq, d):
    def index(i, *_):
        t = i * tile
        row = jnp.where(t < n_prompt, 0, 1 + (t - n_prompt) // sample_seq)
        return (layer, row, chunk, 0, 0)

    return pl.BlockSpec((None, None, None, 1, d), index)


def _modulated(x, gain, shift, scale):
    y = x * lax.rsqrt(jnp.mean(x * x, axis=-1, keepdims=True) + EPS)
    return (y * gain) * (1.0 + scale) + shift


def _mod_body(c_ref, w_ref, b_ref, o_ref):
    c = c_ref[...]
    s = (c * jax.nn.sigmoid(c)).astype(BF16)
    o_ref[...] = jnp.dot(s, w_ref[...].astype(BF16), preferred_element_type=F32) + b_ref[...]


def _modulation(cond, ada_w, ada_b):
    depth, d, n = ada_w.shape
    tn = MOD_COL_TILE
    return pl.pallas_call(
        _mod_body,
        grid=(depth, n // tn),
        in_specs=[
            pl.BlockSpec((SUBLANES, d), lambda l, j: (0, 0)),
            pl.BlockSpec((None, d, tn), lambda l, j: (l, 0, j)),
            pl.BlockSpec((None, 1, tn), lambda l, j: (l, 0, j)),
        ],
        out_specs=pl.BlockSpec((None, SUBLANES, tn), lambda l, j: (l, 0, j)),
        out_shape=jax.ShapeDtypeStruct((depth, SUBLANES, n), F32),
        compiler_params=_cparams(("parallel", "parallel"), 40),
        name="adaln_projection",
    )(cond, ada_w, ada_b.reshape(depth, 1, n))


def _in0_body(x_ref, g_ref, sh_ref, sc_ref, w_ref, xb_ref, gg_ref, gu_ref, gv_ref):
    h = _modulated(x_ref[...], g_ref[...], sh_ref[...], sc_ref[...])
    z = jnp.dot(h.astype(BF16), w_ref[...], preferred_element_type=F32)
    w = xb_ref.shape[1]
    xb_ref[...] = z[:, :w]
    gg_ref[...] = jax.nn.gelu(z[:, w:2 * w]).astype(BF16)
    gu_ref[...] = jax.nn.gelu(z[:, 2 * w:3 * w]).astype(BF16)
    gv_ref[...] = jax.nn.gelu(z[:, 3 * w:]).astype(BF16)


def _even_in_proj(x, mod, layer, gain, w_in, n_prompt, sample_seq):
    t, d = x.shape
    tm = TOKEN_TILE
    w4 = w_in.shape[1]
    w = w4 // 4
    row = lambda i: (i, 0)
    ms = functools.partial(_mod_spec, layer, tile=tm, n_prompt=n_prompt, sample_seq=sample_seq, d=d)
    return pl.pallas_call(
        _in0_body,
        grid=(t // tm,),
        in_specs=[
            pl.BlockSpec((tm, d), row),
            _resident((1, d)),
            ms(chunk=0),
            ms(chunk=1),
            _resident((d, w4)),
        ],
        out_specs=[pl.BlockSpec((tm, w), row)] * 4,
        out_shape=[
            jax.ShapeDtypeStruct((t, w), F32),
            jax.ShapeDtypeStruct((t, w), BF16),
            jax.ShapeDtypeStruct((t, w), BF16),
            jax.ShapeDtypeStruct((t, w), BF16),
        ],
        compiler_params=_cparams(("parallel",), 48),
        name="even_in_proj",
    )(x, gain.reshape(1, d), mod, mod, w_in.astype(BF16))


def _tile_scan(a, b, row, reverse):
    for d in (1, 2, 4):
        if reverse:
            keep = row < SUBLANES - d
            a_s = jnp.where(keep, pltpu.roll(a, SUBLANES - d, 0), 1.0)
            b_s = jnp.where(keep, pltpu.roll(b, SUBLANES - d, 0), 0.0)
        else:
            keep = row >= d
            a_s = jnp.where(keep, pltpu.roll(a, d, 0), 1.0)
            b_s = jnp.where(keep, pltpu.roll(b, d, 0), 0.0)
        b = b + a * b_s
        a = a * a_s
    return a, b


def _lru_body(xb_ref, gg_ref, h0f_ref, h0b_ref, cw_ref, cb_ref, w_ref, bias_ref, lam_ref,
              ya_ref, ff_ref, fb_ref,
              xp_ref, af_ref, bf_ref, ab_ref, bb_ref, hf_ref, hb_ref, *, seq, chunk):
    hw = LANES
    pad = SUBLANES
    xp_ref[pl.ds(0, pad), :] = jnp.zeros((pad, hw), F32)
    xp_ref[pl.ds(pad + seq, pad), :] = jnp.zeros((pad, hw), F32)
    xp_ref[pl.ds(pad, seq), :] = xb_ref[...]

    cw = cw_ref[...]
    cb = cb_ref[...]
    lam = lam_ref[...]
    neg = -lam
    softplus = jnp.maximum(neg, 0.0) + jnp.log1p(jnp.exp(-jnp.abs(neg)))
    nsp = -LRU_C * softplus
    w = w_ref[...]
    bias = bias_ref[...]

    for c in range(seq // chunk):
        base = c * chunk
        xc = (cw[0:1] * xp_ref[pl.ds(base + pad - 2, chunk), :]
              + cw[1:2] * xp_ref[pl.ds(base + pad - 1, chunk), :]
              + cw[2:3] * xp_ref[pl.ds(base + pad, chunk), :]
              + cw[3:4] * xp_ref[pl.ds(base + pad + 1, chunk), :]) + cb
        g = jnp.dot(xc.astype(BF16), w, preferred_element_type=F32) + bias
        for direction, (a_ref, b_ref) in enumerate(((af_ref, bf_ref), (ab_ref, bb_ref))):
            r = jax.nn.sigmoid(g[:, (2 * direction) * hw:(2 * direction + 1) * hw])
            gi = jax.nn.sigmoid(g[:, (2 * direction + 1) * hw:(2 * direction + 2) * hw])
            log_a = r * nsp[direction:direction + 1]
            a = jnp.exp(log_a)
            a_ref[pl.ds(base, chunk), :] = a
            b_ref[pl.ds(base, chunk), :] = jnp.sqrt(1.0 - a * a) * gi * xc

    n_tiles = seq // SUBLANES
    row = lax.broadcasted_iota(I32, (SUBLANES, hw), 0)

    def step(i, carry):
        hf, hb = carry
        rf = pl.multiple_of(i * SUBLANES, SUBLANES)
        rb = pl.multiple_of((n_tiles - 1 - i) * SUBLANES, SUBLANES)
        a, b = _tile_scan(af_ref[pl.ds(rf, SUBLANES), :], bf_ref[pl.ds(rf, SUBLANES), :], row, False)
        h = b + a * hf
        hf_ref[pl.ds(rf, SUBLANES), :] = h
        hf = jnp.broadcast_to(h[SUBLANES - 1:SUBLANES, :], (SUBLANES, hw))
        a, b = _tile_scan(ab_ref[pl.ds(rb, SUBLANES), :], bb_ref[pl.ds(rb, SUBLANES), :], row, True)
        h = b + a * hb
        hb_ref[pl.ds(rb, SUBLANES), :] = h
        hb = jnp.broadcast_to(h[0:1, :], (SUBLANES, hw))
        return hf, hb

    hf0 = jnp.broadcast_to(h0f_ref[...], (SUBLANES, hw))
    hb0 = jnp.broadcast_to(h0b_ref[...], (SUBLANES, hw))
    hf, hb = lax.fori_loop(0, n_tiles, step, (hf0, hb0), unroll=2)
    ff_ref[...] = hf[0:1, :]
    fb_ref[...] = hb[0:1, :]
    ya_ref[...] = ((hf_ref[...] + hb_ref[...]) * gg_ref[...].astype(F32)).astype(BF16)


def _lru_mixer(xb, gg, h0f, h0b, conv_w, conv_b, wcat, bcat, lam, *, row0, batch, seq):
    _, width = xb.shape
    heads = width // LANES
    blk0 = row0 // seq
    tok = lambda b, h: (blk0 + b, h)
    state = lambda b, h: (b, 0, h)
    seq_buf = pltpu.VMEM((seq, LANES), F32)
    return pl.pallas_call(
        functools.partial(_lru_body, seq=seq, chunk=min(seq, LRU_GATE_CHUNK)),
        grid=(batch, heads),
        in_specs=[
            pl.BlockSpec((seq, LANES), tok),
            pl.BlockSpec((seq, LANES), tok),
            pl.BlockSpec((None, 1, LANES), state),
            pl.BlockSpec((None, 1, LANES), state),
            pl.BlockSpec((conv_w.shape[0], LANES), lambda b, h: (0, h)),
            pl.BlockSpec((1, LANES), lambda b, h: (0, h)),
            pl.BlockSpec((None, LANES, 4 * LANES), lambda b, h: (h, 0, 0)),
            pl.BlockSpec((None, 1, 4 * LANES), lambda b, h: (h, 0, 0)),
            pl.BlockSpec((2, LANES), lambda b, h: (0, h)),
        ],
        out_specs=[
            pl.BlockSpec((seq, LANES), lambda b, h: (b, h)),
            pl.BlockSpec((None, 1, LANES), state),
            pl.BlockSpec((None, 1, LANES), state),
        ],
        out_shape=[
            jax.ShapeDtypeStruct((batch * seq, width), BF16),
            jax.ShapeDtypeStruct((batch, 1, width), F32),
            jax.ShapeDtypeStruct((batch, 1, width), F32),
        ],
        scratch_shapes=[pltpu.VMEM((seq + 2 * SUBLANES, LANES), F32)] + [seq_buf] * 6,
        compiler_params=_cparams(("parallel", "parallel"), 40),
        name=f"rglru_seq{seq}",
    )(xb, gg, h0f, h0b, conv_w, conv_b, wcat, bcat, lam)


def _group_select(i, prompt_tiles, p_ref, s_ref):
    return jnp.where(i < prompt_tiles, p_ref[...], s_ref[...])


def _group_specs(block, prompt_tiles):
    return [pl.BlockSpec(block, lambda i: (jnp.minimum(i, prompt_tiles - 1), 0)),
            pl.BlockSpec(block, lambda i: (jnp.maximum(i - prompt_tiles, 0), 0))]


def _out0_body(x_ref, gate_ref, yap_ref, yas_ref, gu_ref, gv_ref, vg_ref, ws_ref, bs_ref, wa_ref, wb_ref,
               o_ref, yb_ref, *, chunk, prompt_tiles):
    tm = x_ref.shape[0]
    ya = _group_select(pl.program_id(0), prompt_tiles, yap_ref, yas_ref)
    v = gv_ref[...].astype(F32)
    vn = (v * lax.rsqrt(jnp.mean(v * v, axis=-1, keepdims=True) + EPS) * vg_ref[...]).astype(BF16)
    groups = ws_ref.shape[0]
    gd = vn.shape[1] // groups
    for c in range(tm // chunk):
        rows = slice(c * chunk, (c + 1) * chunk)
        for g in range(groups):
            cols = slice(g * gd, (g + 1) * gd)
            mixed = jnp.dot(ws_ref[g], vn[rows, cols], preferred_element_type=F32) + bs_ref[:, cols]
            yb_ref[rows, cols] = (gu_ref[rows, cols].astype(F32) * mixed).astype(BF16)
    y = (jnp.dot(ya, wa_ref[...], preferred_element_type=F32)
         + jnp.dot(yb_ref[...], wb_ref[...], preferred_element_type=F32))
    o_ref[...] = x_ref[...] + gate_ref[...] * y


def _even_out_proj(x, mod, layer, ya_p, ya_s, gu, gv, v_gain, w_s, b_full, w_out, n_prompt, sample_seq):
    t, d = x.shape
    tm = TOKEN_TILE
    w = ya_p.shape[1]
    row = lambda i: (i, 0)
    chunk = w_s.shape[1]
    prompt_tiles = n_prompt // tm
    return pl.pallas_call(
        functools.partial(_out0_body, chunk=chunk, prompt_tiles=prompt_tiles),
        grid=(t // tm,),
        in_specs=[
            pl.BlockSpec((tm, d), row),
            _mod_spec(layer, 2, tm, n_prompt, sample_seq, d),
            *_group_specs((tm, w), prompt_tiles),
            pl.BlockSpec((tm, w), row),
            pl.BlockSpec((tm, w), row),
            _resident((1, w)),
            _resident(w_s.shape),
            _resident(b_full.shape),
            _resident((w, d)),
            _resident((w, d)),
        ],
        out_specs=pl.BlockSpec((tm, d), row),
        out_shape=jax.ShapeDtypeStruct((t, d), F32),
        scratch_shapes=[pltpu.VMEM((tm, w), BF16)],
        compiler_params=_cparams(("parallel",), 40),
        name="even_out_proj",
    )(x, mod, ya_p, ya_s, gu, gv, v_gain.reshape(1, w), w_s.astype(BF16), b_full,
      w_out[:w].astype(BF16), w_out[w:].astype(BF16))


def _swap_halves(x, lane):
    quarter = ROPE_QUARTER
    up = pltpu.roll(x, LANES - quarter, 1)
    down = pltpu.roll(x, quarter, 1)
    return jnp.where((lane % (2 * quarter)) < quarter, up, down)


def _mla_in_body(x_ref, g_ref, sh_ref, sc_ref, wd_ref, qan_ref, kvan_ref, wuq_ref, qn_ref, cos_ref, sin_ref,
                 q_ref, ckv_ref, kr_ref, *, heads, q_lora, kv_lora, qk_dim, sm_scale):
    h = _modulated(x_ref[...], g_ref[...], sh_ref[...], sc_ref[...])
    z = jnp.dot(h.astype(BF16), wd_ref[...], preferred_element_type=F32)
    cq = z[:, :q_lora]
    cq = cq * lax.rsqrt(jnp.mean(cq * cq, axis=-1, keepdims=True) + EPS) * qan_ref[...]
    ckv = z[:, q_lora:q_lora + kv_lora]
    ckv_ref[...] = ckv * lax.rsqrt(jnp.mean(ckv * ckv, axis=-1, keepdims=True) + EPS) * kvan_ref[...]
    kr_ref[...] = z[:, q_lora + kv_lora:]
    q = jnp.dot(cq.astype(BF16), wuq_ref[...], preferred_element_type=F32)
    tm = q.shape[0]
    lane = lax.broadcasted_iota(I32, (tm, LANES), 1)
    gn = qn_ref[0:1, :]
    gr = qn_ref[1:2, :]
    grs = qn_ref[2:3, :]
    cos = cos_ref[...]
    sin = sin_ref[...]
    for hd in range(heads):
        qn = q[:, 2 * hd * LANES:(2 * hd + 1) * LANES]
        qr = q[:, (2 * hd + 1) * LANES:(2 * hd + 2) * LANES]
        ss = jnp.sum(qn * qn, axis=-1, keepdims=True) + jnp.sum(qr * qr, axis=-1, keepdims=True)
        rinv = lax.rsqrt(ss * (1.0 / qk_dim) + EPS) * sm_scale
        q_ref[:, 2 * hd * LANES:(2 * hd + 1) * LANES] = (qn * gn * rinv).astype(BF16)
        rot = (qr * gr) * cos + (_swap_halves(qr, lane) * grs) * sin
        q_ref[:, (2 * hd + 1) * LANES:(2 * hd + 2) * LANES] = (rot * rinv).astype(BF16)


def _rope_spec(tile, n_prompt, sample_seq):
    def index(i):
        t = i * tile
        return (jnp.where(t < n_prompt, 0, 1 + ((t - n_prompt) % sample_seq) // tile), 0)

    return pl.BlockSpec((tile, LANES), index)


def _mla_in_proj(x, mod, layer, gain, wd_ext, q_a_norm, kv_a_norm, wuq_pad, qn_rows, cos_tab, sin_tab,
                 n_prompt, sample_seq, *, heads, q_lora, kv_lora, qk_dim):
    t, d = x.shape
    tm = TOKEN_TILE
    row = lambda i: (i, 0)
    ms = functools.partial(_mod_spec, layer, tile=tm, n_prompt=n_prompt, sample_seq=sample_seq, d=d)
    body = functools.partial(_mla_in_body, heads=heads, q_lora=q_lora, kv_lora=kv_lora, qk_dim=qk_dim,
                             sm_scale=float(qk_dim) ** -0.5)
    return pl.pallas_call(
        body,
        grid=(t // tm,),
        in_specs=[
            pl.BlockSpec((tm, d), row),
            _resident((1, d)),
            ms(chunk=0),
            ms(chunk=1),
            _resident(wd_ext.shape),
            _resident((1, q_lora)),
            _resident((1, kv_lora)),
            _resident(wuq_pad.shape),
            _resident(qn_rows.shape),
            _rope_spec(tm, n_prompt, sample_seq),
            _rope_spec(tm, n_prompt, sample_seq),
        ],
        out_specs=[
            pl.BlockSpec((tm, heads * 2 * LANES), row),
            pl.BlockSpec((tm, kv_lora), row),
            pl.BlockSpec((tm, LANES), row),
        ],
        out_shape=[
            jax.ShapeDtypeStruct((t, heads * 2 * LANES), BF16),
            jax.ShapeDtypeStruct((t, kv_lora), F32),
            jax.ShapeDtypeStruct((t, LANES), F32),
        ],
        compiler_params=_cparams(("parallel",), 48),
        name="mla_in_proj",
    )(x, gain.reshape(1, d), mod, mod, wd_ext, q_a_norm.reshape(1, q_lora), kv_a_norm.reshape(1, kv_lora),
      wuq_pad, qn_rows, cos_tab, sin_tab)


def _kv_body(ckv_ref, kr_ref, w_ref, kn_ref, cos_ref, sin_ref, k_ref, v_ref, *, heads, qk_dim):
    kv = jnp.dot(ckv_ref[...].astype(BF16), w_ref[...], preferred_element_type=F32)
    tm = kv.shape[0]
    lane = lax.broadcasted_iota(I32, (tm, LANES), 1)
    kr = kr_ref[...]
    gn = kn_ref[0:1, :]
    gr = kn_ref[1:2, :]
    grs = kn_ref[2:3, :]
    ssr = jnp.sum(kr * kr, axis=-1, keepdims=True)
    rot = (kr * gr) * cos_ref[...] + (_swap_halves(kr, lane) * grs) * sin_ref[...]
    for hd in range(heads):
        kn = kv[:, hd * LANES:(hd + 1) * LANES]
        rinv = lax.rsqrt((jnp.sum(kn * kn, axis=-1, keepdims=True) + ssr) * (1.0 / qk_dim) + EPS)
        k_ref[:, 2 * hd * LANES:(2 * hd + 1) * LANES] = (kn * gn * rinv).astype(BF16)
        k_ref[:, (2 * hd + 1) * LANES:(2 * hd + 2) * LANES] = (rot * rinv).astype(BF16)
    v_ref[...] = kv[:, heads * LANES:].astype(BF16)


def _kv_expand(ckv, kr, w_ukv_re, kn_rows, cos_tab, sin_tab, rope_index, *, heads, qk_dim):
    rows, kv_lora = ckv.shape
    tm = TOKEN_TILE
    row = lambda i: (i, 0)
    return pl.pallas_call(
        functools.partial(_kv_body, heads=heads, qk_dim=qk_dim),
        grid=(rows // tm,),
        in_specs=[
            pl.BlockSpec((tm, kv_lora), row),
            pl.BlockSpec((tm, LANES), row),
            _resident(w_ukv_re.shape),
            _resident(kn_rows.shape),
            pl.BlockSpec((tm, LANES), rope_index),
            pl.BlockSpec((tm, LANES), rope_index),
        ],
        out_specs=[pl.BlockSpec((tm, heads * 2 * LANES), row), pl.BlockSpec((tm, heads * LANES), row)],
        out_shape=[
            jax.ShapeDtypeStruct((rows, heads * 2 * LANES), BF16),
            jax.ShapeDtypeStruct((rows, heads * LANES), BF16),
        ],
        compiler_params=_cparams(("parallel",), 40),
        name=f"mla_kv_expand_{rows}",
    )(ckv, kr, w_ukv_re, kn_rows, cos_tab, sin_tab)


def _attn_body(q_ref, k_ref, v_ref, o_ref, *, heads_per_step):
    for hd in range(heads_per_step):
        qk = slice(2 * hd * LANES, 2 * (hd + 1) * LANES)
        vo = slice(hd * LANES, (hd + 1) * LANES)
        s = lax.dot_general(q_ref[:, qk], k_ref[:, qk], (((1,), (1,)), ((), ())), preferred_element_type=F32)
        m = jnp.max(s, axis=-1, keepdims=True)
        p = jnp.exp(s - m)
        l = jnp.sum(p, axis=-1, keepdims=True)
        o = jnp.dot(p.astype(BF16), v_ref[:, vo], preferred_element_type=F32)
        o_ref[:, vo] = (o / l).astype(BF16)


def _attention(q, k, v, *, row0, batch, seq, kv_len, heads, tq, heads_per_step):
    nq = seq // tq
    qblk0 = row0 // tq
    hps = heads_per_step
    return pl.pallas_call(
        functools.partial(_attn_body, heads_per_step=hps),
        grid=(batch, heads // hps, nq),
        in_specs=[
            pl.BlockSpec((tq, hps * 2 * LANES), lambda b, h, i: (qblk0 + b * nq + i, h)),
            pl.BlockSpec((kv_len, hps * 2 * LANES), lambda b, h, i: (b, h)),
            pl.BlockSpec((kv_len, hps * LANES), lambda b, h, i: (b, h)),
        ],
        out_specs=pl.BlockSpec((tq, hps * LANES), lambda b, h, i: (b * nq + i, h)),
        out_shape=jax.ShapeDtypeStruct((batch * seq, heads * LANES), BF16),
        compiler_params=_cparams(("parallel", "parallel", "parallel"), 48),
        name=f"mla_attention_kv{kv_len}",
    )(q, k, v)


def _oproj_body(x_ref, gate_ref, op_ref, os_ref, w_ref, out_ref, *, prompt_tiles):
    o = _group_select(pl.program_id(0), prompt_tiles, op_ref, os_ref)
    y = jnp.dot(o, w_ref[...], preferred_element_type=F32)
    out_ref[...] = x_ref[...] + gate_ref[...] * y


def _mla_out_proj(x, mod, layer, o_p, o_s, w_o, n_prompt, sample_seq):
    t, d = x.shape
    tm = TOKEN_TILE
    row = lambda i: (i, 0)
    prompt_tiles = n_prompt // tm
    return pl.pallas_call(
        functools.partial(_oproj_body, prompt_tiles=prompt_tiles),
        grid=(t // tm,),
        in_specs=[
            pl.BlockSpec((tm, d), row),
            _mod_spec(layer, 2, tm, n_prompt, sample_seq, d),
            *_group_specs((tm, o_p.shape[1]), prompt_tiles),
            _resident(w_o.shape),
        ],
        out_specs=pl.BlockSpec((tm, d), row),
        out_shape=jax.ShapeDtypeStruct((t, d), F32),
        compiler_params=_cparams(("parallel",), 40),
        name="mla_out_proj",
    )(x, mod, o_p, o_s, w_o.astype(BF16))


def _pack_bf16_pair(lo_half, hi_half):
    a = pltpu.bitcast(lo_half.astype(BF16).astype(F32), U32)
    b = pltpu.bitcast(hi_half.astype(BF16).astype(F32), U32)
    return a | (b >> 16)


def _unpack_bf16_pair(p):
    a = pltpu.bitcast(p & jnp.uint32(0xFFFF0000), F32)
    b = pltpu.bitcast(p << 16, F32)
    return a, b


def _router_body(x_ref, g_ref, sh_ref, sc_ref, whi_ref, wlo_ref, rb_ref,
                 hp_ref, ex_ref, wt_ref, rk_ref, cnt_ref, carry_ref, *, n_experts):
    i = pl.program_id(0)

    @pl.when(i == 0)
    def _():
        carry_ref[...] = jnp.zeros(carry_ref.shape, F32)

    h = _modulated(x_ref[...], g_ref[...], sh_ref[...], sc_ref[...])
    tm, d = h.shape
    hp_ref[...] = _pack_bf16_pair(h[:, :d // 2], h[:, d // 2:])

    h_hi = h.astype(BF16)
    h_lo = (h - h_hi.astype(F32)).astype(BF16)
    nt = (((1,), (1,)), ((), ()))
    logits = (lax.dot_general(whi_ref[...], h_hi, nt, preferred_element_type=F32)
              + lax.dot_general(whi_ref[...], h_lo, nt, preferred_element_type=F32)
              + lax.dot_general(wlo_ref[...], h_hi, nt, preferred_element_type=F32))
    scores = jax.nn.sigmoid(logits)
    biased = scores + rb_ref[...]
    per_group = n_experts // N_EXPERT_GROUPS
    assert per_group == 4 and TOP_K == 2
    b_rows = [biased[e:e + 1, :] for e in range(n_experts)]
    s_rows = [scores[e:e + 1, :] for e in range(n_experts)]

    best = None
    sel = jnp.zeros((1, tm), I32)
    for g in range(N_EXPERT_GROUPS):
        b0, b1, b2, b3 = b_rows[g * 4:(g + 1) * 4]
        m1, n1 = jnp.maximum(b0, b1), jnp.minimum(b0, b1)
        m2, n2 = jnp.maximum(b2, b3), jnp.minimum(b2, b3)
        top1 = jnp.maximum(m1, m2)
        top2 = jnp.maximum(jnp.minimum(m1, m2), jnp.maximum(n1, n2))
        gsum = top1 + top2
        if best is None:
            best = gsum
        else:
            better = gsum > best
            sel = jnp.where(better, g, sel)
            best = jnp.where(better, gsum, best)

    def pick(rows, j):
        out = rows[j]
        for g in range(1, N_EXPERT_GROUPS):
            out = jnp.where(sel == g, rows[g * 4 + j], out)
        return out

    cand_b = [pick(b_rows, j) for j in range(4)]
    cand_s = [pick(s_rows, j) for j in range(4)]

    def argmax4(vals):
        bv, bi = vals[0], jnp.zeros((1, tm), I32)
        for j in range(1, 4):
            gt = vals[j] > bv
            bi = jnp.where(gt, j, bi)
            bv = jnp.where(gt, vals[j], bv)
        return bi

    i1 = argmax4(cand_b)
    i2 = argmax4([jnp.where(i1 == j, -jnp.inf, cand_b[j]) for j in range(4)])

    def take(vals, idx):
        out = vals[0]
        for j in range(1, 4):
            out = jnp.where(idx == j, vals[j], out)
        return out

    s1 = take(cand_s, i1)
    s2 = take(cand_s, i2)
    tot = s1 + s2
    e1 = sel * 4 + i1
    e2 = sel * 4 + i2
    ex_ref[0:1, :] = e1
    ex_ref[1:2, :] = e2
    wt_ref[0:1, :] = s1 / tot
    wt_ref[1:2, :] = s2 / tot

    eid = lax.broadcasted_iota(I32, (n_experts, tm), 0)
    is1 = eid == e1
    is2 = eid == e2
    chosen = jnp.where(is1 | is2, 1.0, 0.0)
    before = (lax.broadcasted_iota(I32, (tm, tm), 0) < lax.broadcasted_iota(I32, (tm, tm), 1))
    rank = jnp.dot(chosen.astype(BF16), jnp.where(before, 1.0, 0.0).astype(BF16),
                   preferred_element_type=F32) + carry_ref[:, 0:1]
    rk_ref[0:1, :] = jnp.sum(jnp.where(is1, rank, 0.0), axis=0, keepdims=True).astype(I32)
    rk_ref[1:2, :] = jnp.sum(jnp.where(is2, rank, 0.0), axis=0, keepdims=True).astype(I32)
    carry_ref[...] = carry_ref[...] + jnp.sum(chosen, axis=1, keepdims=True)
    cnt_ref[...] = carry_ref[...]


def _router(x, mod, layer, gain, w_hi, w_lo, r_bias, n_prompt, sample_seq):
    t, d = x.shape
    tm = TOKEN_TILE
    n_experts = w_hi.shape[0]
    ms = functools.partial(_mod_spec, layer, tile=tm, n_prompt=n_prompt, sample_seq=sample_seq, d=d)
    col = lambda i: (0, i)
    return pl.pallas_call(
        functools.partial(_router_body, n_experts=n_experts),
        grid=(t // tm,),
        in_specs=[
            pl.BlockSpec((tm, d), lambda i: (i, 0)),
            _resident((1, d)),
            ms(chunk=3),
            ms(chunk=4),
            _resident(w_hi.shape),
            _resident(w_lo.shape),
            _resident((n_experts, 1)),
        ],
        out_specs=[
            pl.BlockSpec((tm, d // 2), lambda i: (i, 0)),
            pl.BlockSpec((TOP_K, tm), col),
            pl.BlockSpec((TOP_K, tm), col),
            pl.BlockSpec((TOP_K, tm), col),
            pl.BlockSpec((n_experts, LANES), lambda i: (0, 0)),
        ],
        out_shape=[
            jax.ShapeDtypeStruct((t, d // 2), U32),
            jax.ShapeDtypeStruct((TOP_K, t), I32),
            jax.ShapeDtypeStruct((TOP_K, t), F32),
            jax.ShapeDtypeStruct((TOP_K, t), I32),
            jax.ShapeDtypeStruct((n_experts, LANES), F32),
        ],
        scratch_shapes=[pltpu.VMEM((n_experts, LANES), F32)],
        compiler_params=_cparams(("arbitrary",), 40),
        name="moe_router",
    )(x, gain.reshape(1, d), mod, mod, w_hi, w_lo, r_bias.reshape(n_experts, 1))


def _row_copy(src_ref, src_row, dst_ref, dst_row, sem):
    return pltpu.make_async_copy(src_ref.at[pl.ds(src_row, 1)], dst_ref.at[pl.ds(dst_row, 1)], sem)


def _dispatch_body(pos_ref, h_ref, init_ref, o_ref, sem):
    del init_ref
    tm = h_ref.shape[0]

    def issue(r, c):
        for k in range(TOP_K):
            _row_copy(h_ref, r, o_ref, pos_ref[k, r], sem).start()
        return c

    lax.fori_loop(0, tm, issue, 0, unroll=DMA_ISSUE_UNROLL)

    def drain(r, c):
        for k in range(TOP_K):
            _row_copy(h_ref, 0, o_ref, 0, sem).wait()
        return c

    lax.fori_loop(0, tm, drain, 0, unroll=DMA_ISSUE_UNROLL)


def _dispatch(hp, pos, sorted_rows):
    t, w = hp.shape
    tm = TOKEN_TILE
    init = jnp.zeros((sorted_rows, w), hp.dtype)
    return pl.pallas_call(
        _dispatch_body,
        grid=(t // tm,),
        in_specs=[
            pl.BlockSpec((TOP_K, tm), lambda i: (0, i), memory_space=pltpu.SMEM),
            pl.BlockSpec((tm, w), lambda i: (i, 0)),
            pl.BlockSpec(memory_space=pl.ANY),
        ],
        out_specs=pl.BlockSpec(memory_space=pl.ANY),
        out_shape=jax.ShapeDtypeStruct((sorted_rows, w), hp.dtype),
        scratch_shapes=[pltpu.SemaphoreType.DMA],
        input_output_aliases={2: 0},
        compiler_params=_cparams(("arbitrary",), 32),
        name="moe_dispatch",
    )(pos, hp, init)


def _expert_body(te_ref, nv_ref, x_ref, wg_ref, wu_ref, wd_ref, y_ref, wgb_ref, wub_ref, wdb_ref):
    i = pl.program_id(0)
    valid = i < nv_ref[0]
    new_expert = (i == 0) | (te_ref[i] != te_ref[jnp.maximum(i - 1, 0)])

    @pl.when(valid & new_expert)
    def _():
        wgb_ref[...] = wg_ref[...].astype(BF16)
        wub_ref[...] = wu_ref[...].astype(BF16)
        wdb_ref[...] = wd_ref[...].astype(BF16)

    @pl.when(valid)
    def _():
        a, b = _unpack_bf16_pair(x_ref[...])
        a = a.astype(BF16)
        b = b.astype(BF16)
        half = a.shape[1]
        g = (jnp.dot(a, wgb_ref[:half, :], preferred_element_type=F32)
             + jnp.dot(b, wgb_ref[half:, :], preferred_element_type=F32))
        u = (jnp.dot(a, wub_ref[:half, :], preferred_element_type=F32)
             + jnp.dot(b, wub_ref[half:, :], preferred_element_type=F32))
        act = (g * jax.nn.sigmoid(g) * u).astype(BF16)
        y = jnp.dot(act, wdb_ref[...], preferred_element_type=F32)
        d = y.shape[1]
        y_ref[...] = _pack_bf16_pair(y[:, :d // 2], y[:, d // 2:])

    @pl.when(i >= nv_ref[0])
    def _():
        y_ref[...] = jnp.zeros(y_ref.shape, U32)


def _experts(xs, tile_expert, n_valid, w_gate, w_up, w_down, layer):
    rows, half = xs.shape
    tm = EXPERT_TILE
    _, n_experts, d, f = w_gate.shape

    def xrow(i, te, nv):
        return (jnp.minimum(i, nv[0] - 1), 0)

    grid_spec = pltpu.PrefetchScalarGridSpec(
        num_scalar_prefetch=2,
        grid=(rows // tm,),
        in_specs=[
            pl.BlockSpec((tm, half), xrow),
            pl.BlockSpec((None, None, d, f), lambda i, te, nv: (layer, te[i], 0, 0)),
            pl.BlockSpec((None, None, d, f), lambda i, te, nv: (layer, te[i], 0, 0)),
            pl.BlockSpec((None, None, f, d), lambda i, te, nv: (layer, te[i], 0, 0)),
        ],
        out_specs=pl.BlockSpec((tm, half), lambda i, te, nv: (i, 0)),
        scratch_shapes=[pltpu.VMEM((d, f), BF16), pltpu.VMEM((d, f), BF16), pltpu.VMEM((f, d), BF16)],
    )
    return pl.pallas_call(
        _expert_body,
        grid_spec=grid_spec,
        out_shape=jax.ShapeDtypeStruct((rows, half), U32),
        compiler_params=_cparams(("arbitrary",), 56),
        name="moe_experts",
    )(tile_expert, n_valid, xs, w_gate, w_up, w_down)


def _combine_body(pos_ref, x_ref, gate_ref, wt_ref, ys_ref, o_ref, buf_ref, sem):
    tm = x_ref.shape[0]

    def issue(r, c):
        for k in range(TOP_K):
            _row_copy(ys_ref, pos_ref[k, r], buf_ref.at[k], r, sem).start()
        return c

    lax.fori_loop(0, tm, issue, 0, unroll=DMA_ISSUE_UNROLL)

    def drain(r, c):
        for k in range(TOP_K):
            _row_copy(ys_ref, 0, buf_ref.at[k], 0, sem).wait()
        return c

    lax.fori_loop(0, tm, drain, 0, unroll=DMA_ISSUE_UNROLL)
    wt = wt_ref[...]
    a0, b0 = _unpack_bf16_pair(buf_ref[0])
    a1, b1 = _unpack_bf16_pair(buf_ref[1])
    w0 = wt[:, 0:1]
    w1 = wt[:, 1:2]
    half = a0.shape[1]
    gate = gate_ref[...]
    o_ref[:, :half] = x_ref[:, :half] + gate[:, :half] * (w0 * a0 + w1 * a1)
    o_ref[:, half:] = x_ref[:, half:] + gate[:, half:] * (w0 * b0 + w1 * b1)


def _combine(x, mod, layer, ys, pos, wts_t, n_prompt, sample_seq):
    t, d = x.shape
    tm = TOKEN_TILE
    return pl.pallas_call(
        _combine_body,
        grid=(t // tm,),
        in_specs=[
            pl.BlockSpec((TOP_K, tm), lambda i: (0, i), memory_space=pltpu.SMEM),
            pl.BlockSpec((tm, d), lambda i: (i, 0)),
            _mod_spec(layer, 5, tm, n_prompt, sample_seq, d),
            pl.BlockSpec((tm, TOP_K), lambda i: (i, 0)),
            pl.BlockSpec(memory_space=pl.ANY),
        ],
        out_specs=pl.BlockSpec((tm, d), lambda i: (i, 0)),
        out_shape=jax.ShapeDtypeStruct((t, d), F32),
        scratch_shapes=[pltpu.VMEM((TOP_K, tm, d // 2), U32), pltpu.SemaphoreType.DMA],
        compiler_params=_cparams(("arbitrary",), 32),
        name="moe_combine",
    )(pos, x, mod, wts_t, ys)


def _moe(x, mod, layer, gain, w_hi, w_lo, r_bias, w_gate, w_up, w_down, n_prompt, sample_seq):
    t, d = x.shape
    n_experts = w_hi.shape[0]
    hp, ex, wt, rk, cnt = _router(x, mod, layer, gain, w_hi, w_lo, r_bias, n_prompt, sample_seq)
    counts = cnt[:, 0].astype(I32)
    tiles = (counts + EXPERT_TILE - 1) // EXPERT_TILE
    tile_end = jnp.cumsum(tiles)
    seg_start = (tile_end - tiles) * EXPERT_TILE
    e_ids = jnp.arange(n_experts, dtype=I32)
    pos = rk + jnp.sum(jnp.where(ex[..., None] == e_ids, seg_start, 0), axis=-1)
    sorted_rows = TOP_K * t + n_experts * EXPERT_TILE
    n_tiles = sorted_rows // EXPERT_TILE
    n_valid = tile_end[-1:]
    tile_ids = jnp.arange(n_tiles, dtype=I32)
    ends_before = lambda i: jnp.sum((tile_end[None, :] <= i[:, None]).astype(I32), axis=1)
    tile_expert = jnp.where(tile_ids < n_valid[0], ends_before(tile_ids), ends_before(n_valid - 1))
    tile_expert = jnp.minimum(tile_expert, n_experts - 1).astype(I32)
    xs = _dispatch(hp, pos, sorted_rows)
    ys = _experts(xs, tile_expert, n_valid.astype(I32), w_gate, w_up, w_down, layer)
    return _combine(x, mod, layer, ys, pos, wt.T, n_prompt, sample_seq)


def _rope_tables(sample_seq, rope_dim, tile):
    n_freq = rope_dim // 4
    pos = jnp.arange(sample_seq)
    row_pos = (pos // GRID_W).astype(F32)
    col_pos = (pos % GRID_W).astype(F32)
    inv_freq = ROPE_BASE ** (-jnp.arange(n_freq, dtype=F32) / n_freq)
    ar = row_pos[:, None] * inv_freq
    ac = col_pos[:, None] * inv_freq
    zeros = jnp.zeros((sample_seq, LANES - rope_dim), F32)
    cos = jnp.concatenate([jnp.cos(ar), jnp.cos(ar), jnp.cos(ac), jnp.cos(ac), zeros], axis=1)
    sin = jnp.concatenate([-jnp.sin(ar), jnp.sin(ar), -jnp.sin(ac), jnp.sin(ac), zeros], axis=1)
    ident_c = jnp.concatenate([jnp.ones((tile, rope_dim), F32), jnp.zeros((tile, LANES - rope_dim), F32)], axis=1)
    return (jnp.concatenate([ident_c, cos], axis=0),
            jnp.concatenate([jnp.zeros((tile, LANES), F32), sin], axis=0))


def _norm_rows(gain, nope, rope_dim):
    quarter = rope_dim // 4
    gr = gain[nope:]
    grs = jnp.concatenate([gr[quarter:2 * quarter], gr[:quarter], gr[3 * quarter:], gr[2 * quarter:3 * quarter]])
    zpad = jnp.zeros((LANES - rope_dim,), F32)
    return jnp.stack([gain[:nope], jnp.concatenate([gr, zpad]), jnp.concatenate([grs, zpad])])


def kernel(x_prompt, x_sample, state_lru_fwd, state_lru_bwd, cache_mla_ckv, cache_mla_krope, c, c_ctx,
           ada_w, ada_b, norm_mix, norm_ffn, mix0_w_in, mix0_w_out, lru_conv_w, lru_conv_b,
           lru_w_r, lru_b_r, lru_w_i, lru_b_i, lru_lambda, gmlp_v_norm, gmlp_w_s, gmlp_b_s,
           mla_w_down, mla_q_a_norm, mla_kv_a_norm, mla_w_uq, mla_w_ukv, mla_q_norm, mla_k_norm, mla_w_o,
           router_w, router_bias, moe_w_gate, moe_w_up, moe_w_down):
    batch, seq, d = x_prompt.shape
    dec_batch, dec_seq, _ = x_sample.shape
    depth = ada_w.shape[0]
    n_prompt = batch * seq
    n_sample = dec_batch * dec_seq
    assert n_prompt % dec_seq == 0 and seq % TOKEN_TILE == 0 and dec_seq % TOKEN_TILE == 0
    assert 1 + dec_batch <= SUBLANES

    x = jnp.concatenate([x_prompt.reshape(n_prompt, d), x_sample.reshape(n_sample, d)], axis=0)

    cond = jnp.concatenate([c_ctx[None, :], c, jnp.zeros((SUBLANES - 1 - dec_batch, d), F32)], axis=0)
    mod = _modulation(cond, ada_w, ada_b).reshape(depth, SUBLANES, 6, 1, d)

    n_experts = router_w.shape[1]
    rw_t = router_w.T
    rw_hi = rw_t.astype(BF16)
    rw_lo = (rw_t - rw_hi.astype(F32)).astype(BF16)

    fwd_states, bwd_states, ckv_caches, krope_caches = [], [], [], []
    for layer in range(depth):
        j = layer // 2
        if layer % 2 == 0:
            width = lru_conv_w.shape[2]
            heads = lru_w_r.shape[2]
            xb, gg, gu, gv = _even_in_proj(x, mod, layer, norm_mix[layer], mix0_w_in[j], n_prompt, dec_seq)
            wcat = jnp.concatenate([lru_w_r[j, 0], lru_w_i[j, 0], lru_w_r[j, 1], lru_w_i[j, 1]], axis=-1).astype(BF16)
            hb = lambda v: v.reshape(heads, 1, width // heads)
            bcat = jnp.concatenate([hb(lru_b_r[j, 0]), hb(lru_b_i[j, 0]), hb(lru_b_r[j, 1]), hb(lru_b_i[j, 1])], axis=-1)
            lru_args = (lru_conv_w[j], lru_conv_b[j].reshape(1, width), wcat, bcat, lru_lambda[j])
            zero_state = jnp.zeros((batch, 1, width), F32)
            ya_p, fin_f, fin_b = _lru_mixer(xb, gg, zero_state, zero_state, *lru_args,
                                            row0=0, batch=batch, seq=seq)
            ya_s, _, _ = _lru_mixer(xb, gg, state_lru_fwd[:, j][:, None, :], state_lru_bwd[:, j][:, None, :],
                                    *lru_args, row0=n_prompt, batch=dec_batch, seq=dec_seq)
            fwd_states.append(fin_f[:, 0, :])
            bwd_states.append(fin_b[:, 0, :])
            groups, chunk, _ = gmlp_w_s[j].shape
            gd = width // groups
            b_full = jnp.repeat(gmlp_b_s[j].T, gd, axis=1)
            x = _even_out_proj(x, mod, layer, ya_p, ya_s, gu, gv, gmlp_v_norm[j], gmlp_w_s[j], b_full, mix0_w_out[j],
                               n_prompt, dec_seq)
        else:
            q_lora = mla_q_a_norm.shape[1]
            kv_lora = mla_kv_a_norm.shape[1]
            qk_dim = mla_q_norm.shape[1]
            rope_dim = cache_mla_krope.shape[-1]
            nope = qk_dim - rope_dim
            heads = mla_w_uq.shape[2] // qk_dim
            v_dim = mla_w_ukv.shape[2] // heads - nope
            past = cache_mla_ckv.shape[2]
            assert nope == LANES and v_dim == LANES and rope_dim == 4 * ROPE_QUARTER
            wd = mla_w_down[j]
            wd_ext = jnp.concatenate([wd, jnp.zeros((d, LANES - rope_dim), F32)], axis=1).astype(BF16)
            wuq = mla_w_uq[j].reshape(q_lora, heads, qk_dim)
            wuq_pad = jnp.concatenate([wuq, jnp.zeros((q_lora, heads, 2 * LANES - qk_dim), F32)], axis=-1)
            wuq_pad = wuq_pad.reshape(q_lora, heads * 2 * LANES).astype(BF16)
            wukv = mla_w_ukv[j].reshape(kv_lora, heads, nope + v_dim)
            w_ukv_re = jnp.concatenate([wukv[:, :, :nope].reshape(kv_lora, heads * nope),
                                        wukv[:, :, nope:].reshape(kv_lora, heads * v_dim)], axis=1).astype(BF16)
            qn_rows = _norm_rows(mla_q_norm[j], nope, rope_dim)
            kn_rows = _norm_rows(mla_k_norm[j], nope, rope_dim)
            cos_tab, sin_tab = _rope_tables(dec_seq, rope_dim, TOKEN_TILE)
            q, ckv, kr = _mla_in_proj(x, mod, layer, norm_mix[layer], wd_ext, mla_q_a_norm[j], mla_kv_a_norm[j],
                                      wuq_pad, qn_rows, cos_tab, sin_tab, n_prompt, dec_seq,
                                      heads=heads, q_lora=q_lora, kv_lora=kv_lora, qk_dim=qk_dim)
            ckv_caches.append(ckv[:n_prompt].reshape(batch, seq, kv_lora))
            krope_caches.append(kr[:n_prompt, :rope_dim].reshape(batch, seq, rope_dim))
            expand = functools.partial(_kv_expand, w_ukv_re=w_ukv_re, kn_rows=kn_rows, cos_tab=cos_tab,
                                       sin_tab=sin_tab, heads=heads, qk_dim=qk_dim)
            k_p, v_p = expand(ckv[:n_prompt], kr[:n_prompt], rope_index=lambda i: (0, 0))
            kv_len = past + dec_seq
            ctx_kr = jnp.pad(cache_mla_krope[:, j], ((0, 0), (0, 0), (0, LANES - rope_dim)))
            ckv_s = jnp.concatenate([cache_mla_ckv[:, j], ckv[n_prompt:].reshape(dec_batch, dec_seq, kv_lora)], axis=1)
            kr_s = jnp.concatenate([ctx_kr, kr[n_prompt:].reshape(dec_batch, dec_seq, LANES)], axis=1)
            tiles_per_req = kv_len // TOKEN_TILE
            ctx_tiles = past // TOKEN_TILE
            assert past % TOKEN_TILE == 0

            def latent_rope(i):
                r = i % tiles_per_req
                return (jnp.where(r < ctx_tiles, 0, 1 + r - ctx_tiles), 0)

            k_s, v_s = expand(ckv_s.reshape(dec_batch * kv_len, kv_lora), kr_s.reshape(dec_batch * kv_len, LANES),
                              rope_index=latent_rope)
            o_p = _attention(q, k_p, v_p, row0=0, batch=batch, seq=seq, kv_len=seq, heads=heads, tq=seq,
                             heads_per_step=heads)
            o_s = _attention(q, k_s, v_s, row0=n_prompt, batch=dec_batch, seq=dec_seq, kv_len=kv_len,
                             heads=heads, tq=ATTN_Q_TILE, heads_per_step=ATTN_LATENT_HEADS_PER_STEP)
            x = _mla_out_proj(x, mod, layer, o_p, o_s, mla_w_o[j], n_prompt, dec_seq)
        x = _moe(x, mod, layer, norm_ffn[layer], rw_hi, rw_lo, router_bias, moe_w_gate, moe_w_up, moe_w_down,
                 n_prompt, dec_seq)

    xp = x[:n_prompt].reshape(batch, seq, d)
    xs = x[n_prompt:].reshape(dec_batch, dec_seq, d)
    return (xp, xs,
            jnp.stack(fwd_states, axis=1), jnp.stack(bwd_states, axis=1),
            jnp.stack(ckv_caches, axis=1), jnp.stack(krope_caches, axis=1))
```

```python
import functools

import jax
import jax.numpy as jnp
from jax import lax
from jax.experimental import pallas as pl
from jax.experimental.pallas import tpu as pltpu

F32 = jnp.float32
BF16 = jnp.bfloat16
U32 = jnp.uint32
I32 = jnp.int32

EPS = 1e-6
LRU_C = 8.0
GRID_W = 64
ROPE_BASE = 10000.0
ROPE_QUARTER = 16
N_EXPERT_GROUPS = 4
TOP_K = 2

LANES = 128
SUBLANES = 8
VMEM_BYTES_V7X = 64 * 1024 * 1024

TOKEN_TILE = 256
MLA_TOKEN_TILE = 512
EXPERT_TILE = 256
MOD_COL_TILE = 1024
ATTN_Q_TILE = 512
ATTN_LATENT_HEADS_PER_STEP = 2
LRU_GATE_CHUNK = 256
DMA_ISSUE_UNROLL = 8
LRU_PROMPT_SEQS_PER_STEP = 8


def _cparams(semantics, vmem_mb):
    return pltpu.CompilerParams(dimension_semantics=semantics, vmem_limit_bytes=vmem_mb * 1024 * 1024)


def _resident(shape):
    nd = len(shape)
    return pl.BlockSpec(shape, lambda *_: (0,) * nd, pipeline_mode=pl.Buffered(1))


def _mod_spec(layer, chunk, tile, n_prompt, sample_seq, d):
    def index(i, *_):
        t = i * tile
        row = jnp.where(t < n_prompt, 0, 1 + (t - n_prompt) // sample_seq)
        return (layer, row, chunk, 0, 0)

    return pl.BlockSpec((None, None, None, 1, d), index)


def _modulated(x, gain, shift, scale):
    y = x * lax.rsqrt(jnp.mean(x * x, axis=-1, keepdims=True) + EPS)
    return (y * gain) * (1.0 + scale) + shift


def _mod_body(c_ref, w_ref, b_ref, o_ref):
    c = c_ref[...]
    s = (c * jax.nn.sigmoid(c)).astype(BF16)
    o_ref[...] = jnp.dot(s, w_ref[...].astype(BF16), preferred_element_type=F32) + b_ref[...]


def _modulation(cond, ada_w, ada_b):
    depth, d, n = ada_w.shape
    tn = MOD_COL_TILE
    return pl.pallas_call(
        _mod_body,
        grid=(depth, n // tn),
        in_specs=[
            pl.BlockSpec((SUBLANES, d), lambda l, j: (0, 0)),
            pl.BlockSpec((None, d, tn), lambda l, j: (l, 0, j)),
            pl.BlockSpec((None, 1, tn), lambda l, j: (l, 0, j)),
        ],
        out_specs=pl.BlockSpec((None, SUBLANES, tn), lambda l, j: (l, 0, j)),
        out_shape=jax.ShapeDtypeStruct((depth, SUBLANES, n), F32),
        compiler_params=_cparams(("parallel", "parallel"), 40),
        name="adaln_projection",
    )(cond, ada_w, ada_b.reshape(depth, 1, n))


def _group_select(i, prompt_tiles, p_ref, s_ref):
    return jnp.where(i < prompt_tiles, p_ref[...], s_ref[...])


def _group_specs(block, prompt_tiles):
    return [pl.BlockSpec(block, lambda i, *_: (jnp.minimum(i, prompt_tiles - 1), 0)),
            pl.BlockSpec(block, lambda i, *_: (jnp.maximum(i - prompt_tiles, 0), 0))]


def _token_specs(x_parts, tile, prompt_tiles):
    d = x_parts[0].shape[1]
    if len(x_parts) == 1:
        return [pl.BlockSpec((tile, d), lambda i, *_: (i, 0))]
    return _group_specs((tile, d), prompt_tiles)


def _token_tile(i, prompt_tiles, x_refs):
    return x_refs[0][...] if len(x_refs) == 1 else _group_select(i, prompt_tiles, *x_refs)


def _in0_body(*refs, n_x, prompt_tiles):
    x_refs = refs[:n_x]
    g_ref, sh_ref, sc_ref, w_ref, xb_ref, gg_ref, gu_ref, gv_ref = refs[n_x:]
    x = _token_tile(pl.program_id(0), prompt_tiles, x_refs)
    h = _modulated(x, g_ref[...], sh_ref[...], sc_ref[...])
    z = jnp.dot(h.astype(BF16), w_ref[...], preferred_element_type=F32)
    w = xb_ref.shape[1]
    xb_ref[...] = z[:, :w]
    gg_ref[...] = jax.nn.gelu(z[:, w:2 * w]).astype(BF16)
    gu_ref[...] = jax.nn.gelu(z[:, 2 * w:3 * w]).astype(BF16)
    gv_ref[...] = jax.nn.gelu(z[:, 3 * w:]).astype(BF16)


def _even_in_proj(x_parts, mod, layer, gain, w_in, n_prompt, sample_seq):
    t = sum(p.shape[0] for p in x_parts)
    d = x_parts[0].shape[1]
    tm = TOKEN_TILE
    w4 = w_in.shape[1]
    w = w4 // 4
    row = lambda i: (i, 0)
    ms = functools.partial(_mod_spec, layer, tile=tm, n_prompt=n_prompt, sample_seq=sample_seq, d=d)
    return pl.pallas_call(
        functools.partial(_in0_body, n_x=len(x_parts), prompt_tiles=n_prompt // tm),
        grid=(t // tm,),
        in_specs=[
            *_token_specs(x_parts, tm, n_prompt // tm),
            _resident((1, d)),
            ms(chunk=0),
            ms(chunk=1),
            _resident((d, w4)),
        ],
        out_specs=[pl.BlockSpec((tm, w), row)] * 4,
        out_shape=[
            jax.ShapeDtypeStruct((t, w), F32),
            jax.ShapeDtypeStruct((t, w), BF16),
            jax.ShapeDtypeStruct((t, w), BF16),
            jax.ShapeDtypeStruct((t, w), BF16),
        ],
        compiler_params=_cparams(("parallel",), 48),
        name="even_in_proj",
    )(*x_parts, gain.reshape(1, d), mod, mod, w_in.astype(BF16))


def _tile_scan(a, b, row, reverse):
    for d in (1, 2, 4):
        if reverse:
            keep = row < SUBLANES - d
            a_s = jnp.where(keep, pltpu.roll(a, SUBLANES - d, 0), 1.0)
            b_s = jnp.where(keep, pltpu.roll(b, SUBLANES - d, 0), 0.0)
        else:
            keep = row >= d
            a_s = jnp.where(keep, pltpu.roll(a, d, 0), 1.0)
            b_s = jnp.where(keep, pltpu.roll(b, d, 0), 0.0)
        b = b + a * b_s
        a = a * a_s
    return a, b


def _sigmoid(x):
    return 0.5 * jnp.tanh(0.5 * x) + 0.5


def _lru_body(xb_ref, gg_ref, h0f_ref, h0b_ref, cw_ref, cb_ref, w_ref, bias_ref, lam_ref,
              ya_ref, ff_ref, fb_ref,
              xp_ref, af_ref, bf_ref, ab_ref, bb_ref, hf_ref, hb_ref, *, seq, chunk, nseq):
    hw = LANES
    pad = SUBLANES
    pitch = seq + 2 * pad
    for s in range(nseq):
        xp_ref[pl.ds(s * pitch, pad), :] = jnp.zeros((pad, hw), F32)
        xp_ref[pl.ds(s * pitch + pad + seq, pad), :] = jnp.zeros((pad, hw), F32)
        xp_ref[pl.ds(s * pitch + pad, seq), :] = xb_ref[pl.ds(s * seq, seq), :]

    cw = cw_ref[...]
    cb = cb_ref[...]
    lam = lam_ref[...]
    neg = -lam
    softplus = jnp.maximum(neg, 0.0) + jnp.log1p(jnp.exp(-jnp.abs(neg)))
    nsp = -LRU_C * softplus
    w = w_ref[...]
    bias = bias_ref[...]

    for s in range(nseq):
        for c in range(seq // chunk):
            src = s * pitch + pad + c * chunk
            dst = s * seq + c * chunk
            xc = (cw[0:1] * xp_ref[pl.ds(src - 2, chunk), :]
                  + cw[1:2] * xp_ref[pl.ds(src - 1, chunk), :]
                  + cw[2:3] * xp_ref[pl.ds(src, chunk), :]
                  + cw[3:4] * xp_ref[pl.ds(src + 1, chunk), :]) + cb
            g = jnp.dot(xc.astype(BF16), w, preferred_element_type=F32) + bias
            for direction, (a_ref, b_ref) in enumerate(((af_ref, bf_ref), (ab_ref, bb_ref))):
                r = _sigmoid(g[:, (2 * direction) * hw:(2 * direction + 1) * hw])
                gi = _sigmoid(g[:, (2 * direction + 1) * hw:(2 * direction + 2) * hw])
                log_a = r * nsp[direction:direction + 1]
                a = jnp.exp(log_a)
                a_ref[pl.ds(dst, chunk), :] = a
                b_ref[pl.ds(dst, chunk), :] = jnp.sqrt(1.0 - a * a) * gi * xc

    n_tiles = seq // SUBLANES
    row = lax.broadcasted_iota(I32, (SUBLANES, hw), 0)

    def step(i, carry):
        new = []
        for s in range(nseq):
            hf, hb = carry[2 * s], carry[2 * s + 1]
            rf = pl.multiple_of(s * seq + i * SUBLANES, SUBLANES)
            rb = pl.multiple_of(s * seq + (n_tiles - 1 - i) * SUBLANES, SUBLANES)
            a, b = _tile_scan(af_ref[pl.ds(rf, SUBLANES), :], bf_ref[pl.ds(rf, SUBLANES), :], row, False)
            h = b + a * hf
            hf_ref[pl.ds(rf, SUBLANES), :] = h
            new.append(jnp.broadcast_to(h[SUBLANES - 1:SUBLANES, :], (SUBLANES, hw)))
            a, b = _tile_scan(ab_ref[pl.ds(rb, SUBLANES), :], bb_ref[pl.ds(rb, SUBLANES), :], row, True)
            h = b + a * hb
            hb_ref[pl.ds(rb, SUBLANES), :] = h
            new.append(jnp.broadcast_to(h[0:1, :], (SUBLANES, hw)))
        return tuple(new)

    init = []
    for s in range(nseq):
        init.append(jnp.broadcast_to(h0f_ref[s], (SUBLANES, hw)))
        init.append(jnp.broadcast_to(h0b_ref[s], (SUBLANES, hw)))
    final = lax.fori_loop(0, n_tiles, step, tuple(init), unroll=2 if nseq == 1 else 1)
    for s in range(nseq):
        ff_ref[s] = final[2 * s][0:1, :]
        fb_ref[s] = final[2 * s + 1][0:1, :]
    ya_ref[...] = ((hf_ref[...] + hb_ref[...]) * gg_ref[...].astype(F32)).astype(BF16)


def _lru_mixer(xb, gg, h0f, h0b, conv_w, conv_b, wcat, bcat, lam, *, row0, batch, seq, nseq):
    _, width = xb.shape
    heads = width // LANES
    rows = nseq * seq
    assert batch % nseq == 0 and row0 % rows == 0
    blk0 = row0 // rows
    tok = lambda b, h: (blk0 + b, h)
    state = lambda b, h: (b, 0, h)
    seq_buf = pltpu.VMEM((rows, LANES), F32)
    return pl.pallas_call(
        functools.partial(_lru_body, seq=seq, chunk=min(seq, LRU_GATE_CHUNK), nseq=nseq),
        grid=(batch // nseq, heads),
        in_specs=[
            pl.BlockSpec((rows, LANES), tok),
            pl.BlockSpec((rows, LANES), tok),
            pl.BlockSpec((nseq, 1, LANES), state),
            pl.BlockSpec((nseq, 1, LANES), state),
            pl.BlockSpec((conv_w.shape[0], LANES), lambda b, h: (0, h)),
            pl.BlockSpec((1, LANES), lambda b, h: (0, h)),
            pl.BlockSpec((None, LANES, 4 * LANES), lambda b, h: (h, 0, 0)),
            pl.BlockSpec((None, 1, 4 * LANES), lambda b, h: (h, 0, 0)),
            pl.BlockSpec((2, LANES), lambda b, h: (0, h)),
        ],
        out_specs=[
            pl.BlockSpec((rows, LANES), lambda b, h: (b, h)),
            pl.BlockSpec((nseq, 1, LANES), state),
            pl.BlockSpec((nseq, 1, LANES), state),
        ],
        out_shape=[
            jax.ShapeDtypeStruct((batch * seq, width), BF16),
            jax.ShapeDtypeStruct((batch, 1, width), F32),
            jax.ShapeDtypeStruct((batch, 1, width), F32),
        ],
        scratch_shapes=[pltpu.VMEM((nseq * (seq + 2 * SUBLANES), LANES), F32)] + [seq_buf] * 6,
        compiler_params=_cparams(("parallel", "parallel"), 40),
        name=f"rglru_seq{seq}",
    )(xb, gg, h0f, h0b, conv_w, conv_b, wcat, bcat, lam)


def _out0_body(*refs, n_x, chunk, prompt_tiles):
    x_refs = refs[:n_x]
    (gate_ref, yap_ref, yas_ref, gu_ref, gv_ref, vg_ref, ws_ref, bs_ref, wa_ref, wb_ref,
     o_ref, yb_ref) = refs[n_x:]
    tm = o_ref.shape[0]
    x = _token_tile(pl.program_id(0), prompt_tiles, x_refs)
    ya = _group_select(pl.program_id(0), prompt_tiles, yap_ref, yas_ref)
    v = gv_ref[...].astype(F32)
    vn = (v * lax.rsqrt(jnp.mean(v * v, axis=-1, keepdims=True) + EPS) * vg_ref[...]).astype(BF16)
    groups = ws_ref.shape[0]
    gd = vn.shape[1] // groups
    for c in range(tm // chunk):
        rows = slice(c * chunk, (c + 1) * chunk)
        for g in range(groups):
            cols = slice(g * gd, (g + 1) * gd)
            mixed = jnp.dot(ws_ref[g], vn[rows, cols], preferred_element_type=F32) + bs_ref[:, cols]
            yb_ref[rows, cols] = (gu_ref[rows, cols].astype(F32) * mixed).astype(BF16)
    y = (jnp.dot(ya, wa_ref[...], preferred_element_type=F32)
         + jnp.dot(yb_ref[...], wb_ref[...], preferred_element_type=F32))
    o_ref[...] = x + gate_ref[...] * y


def _even_out_proj(x_parts, mod, layer, ya_p, ya_s, gu, gv, v_gain, w_s, b_full, w_out, n_prompt, sample_seq):
    t = sum(p.shape[0] for p in x_parts)
    d = x_parts[0].shape[1]
    tm = TOKEN_TILE
    w = ya_p.shape[1]
    row = lambda i: (i, 0)
    chunk = w_s.shape[1]
    prompt_tiles = n_prompt // tm
    return pl.pallas_call(
        functools.partial(_out0_body, n_x=len(x_parts), chunk=chunk, prompt_tiles=prompt_tiles),
        grid=(t // tm,),
        in_specs=[
            *_token_specs(x_parts, tm, prompt_tiles),
            _mod_spec(layer, 2, tm, n_prompt, sample_seq, d),
            *_group_specs((tm, w), prompt_tiles),
            pl.BlockSpec((tm, w), row),
            pl.BlockSpec((tm, w), row),
            _resident((1, w)),
            _resident(w_s.shape),
            _resident(b_full.shape),
            _resident((w, d)),
            _resident((w, d)),
        ],
        out_specs=pl.BlockSpec((tm, d), row),
        out_shape=jax.ShapeDtypeStruct((t, d), F32),
        scratch_shapes=[pltpu.VMEM((tm, w), BF16)],
        compiler_params=_cparams(("parallel",), 40),
        name="even_out_proj",
    )(*x_parts, mod, ya_p, ya_s, gu, gv, v_gain.reshape(1, w), w_s.astype(BF16), b_full,
      w_out[:w].astype(BF16), w_out[w:].astype(BF16))


def _swap_halves(x, lane):
    quarter = ROPE_QUARTER
    up = pltpu.roll(x, LANES - quarter, 1)
    down = pltpu.roll(x, quarter, 1)
    return jnp.where((lane % (2 * quarter)) < quarter, up, down)


def _mla_in_body(x_ref, g_ref, sh_ref, sc_ref, wd_ref, qan_ref, kvan_ref, wuq_ref, qn_ref, cos_ref, sin_ref,
                 q_ref, ckv_ref, kr_ref, *, heads, q_lora, kv_lora, qk_dim, sm_scale, sub):
    gn = qn_ref[0:1, :]
    gr = qn_ref[1:2, :]
    grs = qn_ref[2:3, :]
    lane = lax.broadcasted_iota(I32, (sub, LANES), 1)
    for r0 in range(0, x_ref.shape[0], sub):
        rows = pl.ds(r0, sub)
        h = _modulated(x_ref[rows, :], g_ref[...], sh_ref[...], sc_ref[...])
        z = jnp.dot(h.astype(BF16), wd_ref[...], preferred_element_type=F32)
        cq = z[:, :q_lora]
        cq = cq * lax.rsqrt(jnp.mean(cq * cq, axis=-1, keepdims=True) + EPS) * qan_ref[...]
        ckv = z[:, q_lora:q_lora + kv_lora]
        ckv_ref[rows, :] = ckv * lax.rsqrt(jnp.mean(ckv * ckv, axis=-1, keepdims=True) + EPS) * kvan_ref[...]
        kr_ref[rows, :] = z[:, q_lora + kv_lora:]
        q = jnp.dot(cq.astype(BF16), wuq_ref[...], preferred_element_type=F32)
        cos = cos_ref[rows, :]
        sin = sin_ref[rows, :]
        for hd in range(heads):
            qn = q[:, 2 * hd * LANES:(2 * hd + 1) * LANES]
            qr = q[:, (2 * hd + 1) * LANES:(2 * hd + 2) * LANES]
            ss = jnp.sum(qn * qn, axis=-1, keepdims=True) + jnp.sum(qr * qr, axis=-1, keepdims=True)
            rinv = lax.rsqrt(ss * (1.0 / qk_dim) + EPS) * sm_scale
            q_ref[rows, 2 * hd * LANES:(2 * hd + 1) * LANES] = (qn * gn * rinv).astype(BF16)
            rot = (qr * gr) * cos + (_swap_halves(qr, lane) * grs) * sin
            q_ref[rows, (2 * hd + 1) * LANES:(2 * hd + 2) * LANES] = (rot * rinv).astype(BF16)


def _rope_spec(tile, n_prompt, sample_seq):
    def index(i):
        t = i * tile
        return (jnp.where(t < n_prompt, 0, 1 + ((t - n_prompt) % sample_seq) // tile), 0)

    return pl.BlockSpec((tile, LANES), index)


def _mla_in_proj(x, mod, layer, gain, wd_ext, q_a_norm, kv_a_norm, wuq_pad, qn_rows, cos_tab, sin_tab,
                 n_prompt, sample_seq, *, heads, q_lora, kv_lora, qk_dim):
    t, d = x.shape
    tm = MLA_TOKEN_TILE
    row = lambda i: (i, 0)
    ms = functools.partial(_mod_spec, layer, tile=tm, n_prompt=n_prompt, sample_seq=sample_seq, d=d)
    body = functools.partial(_mla_in_body, heads=heads, q_lora=q_lora, kv_lora=kv_lora, qk_dim=qk_dim,
                             sm_scale=float(qk_dim) ** -0.5, sub=TOKEN_TILE)
    return pl.pallas_call(
        body,
        grid=(t // tm,),
        in_specs=[
            pl.BlockSpec((tm, d), row),
            _resident((1, d)),
            ms(chunk=0),
            ms(chunk=1),
            _resident(wd_ext.shape),
            _resident((1, q_lora)),
            _resident((1, kv_lora)),
            _resident(wuq_pad.shape),
            _resident(qn_rows.shape),
            _rope_spec(tm, n_prompt, sample_seq),
            _rope_spec(tm, n_prompt, sample_seq),
        ],
        out_specs=[
            pl.BlockSpec((tm, heads * 2 * LANES), row),
            pl.BlockSpec((tm, kv_lora), row),
            pl.BlockSpec((tm, LANES), row),
        ],
        out_shape=[
            jax.ShapeDtypeStruct((t, heads * 2 * LANES), BF16),
            jax.ShapeDtypeStruct((t, kv_lora), F32),
            jax.ShapeDtypeStruct((t, LANES), F32),
        ],
        compiler_params=_cparams(("parallel",), 56),
        name="mla_in_proj",
    )(x, gain.reshape(1, d), mod, mod, wd_ext, q_a_norm.reshape(1, q_lora), kv_a_norm.reshape(1, kv_lora),
      wuq_pad, qn_rows, cos_tab, sin_tab)


def _kv_body(ckv_ref, kr_ref, w_ref, kn_ref, cos_ref, sin_ref, k_ref, v_ref, *, heads, qk_dim):
    kv = jnp.dot(ckv_ref[...].astype(BF16), w_ref[...], preferred_element_type=F32)
    tm = kv.shape[0]
    lane = lax.broadcasted_iota(I32, (tm, LANES), 1)
    kr = kr_ref[...]
    gn = kn_ref[0:1, :]
    gr = kn_ref[1:2, :]
    grs = kn_ref[2:3, :]
    ssr = jnp.sum(kr * kr, axis=-1, keepdims=True)
    rot = (kr * gr) * cos_ref[...] + (_swap_halves(kr, lane) * grs) * sin_ref[...]
    for hd in range(heads):
        kn = kv[:, hd * LANES:(hd + 1) * LANES]
        rinv = lax.rsqrt((jnp.sum(kn * kn, axis=-1, keepdims=True) + ssr) * (1.0 / qk_dim) + EPS)
        k_ref[:, 2 * hd * LANES:(2 * hd + 1) * LANES] = (kn * gn * rinv).astype(BF16)
        k_ref[:, (2 * hd + 1) * LANES:(2 * hd + 2) * LANES] = (rot * rinv).astype(BF16)
    v_ref[...] = kv[:, heads * LANES:].astype(BF16)


def _kv_expand(ckv, kr, w_ukv_re, kn_rows, cos_tab, sin_tab, rope_index, *, heads, qk_dim):
    rows, kv_lora = ckv.shape
    tm = TOKEN_TILE
    row = lambda i: (i, 0)
    return pl.pallas_call(
        functools.partial(_kv_body, heads=heads, qk_dim=qk_dim),
        grid=(rows // tm,),
        in_specs=[
            pl.BlockSpec((tm, kv_lora), row),
            pl.BlockSpec((tm, LANES), row),
            _resident(w_ukv_re.shape),
            _resident(kn_rows.shape),
            pl.BlockSpec((tm, LANES), rope_index),
            pl.BlockSpec((tm, LANES), rope_index),
        ],
        out_specs=[pl.BlockSpec((tm, heads * 2 * LANES), row), pl.BlockSpec((tm, heads * LANES), row)],
        out_shape=[
            jax.ShapeDtypeStruct((rows, heads * 2 * LANES), BF16),
            jax.ShapeDtypeStruct((rows, heads * LANES), BF16),
        ],
        compiler_params=_cparams(("parallel",), 40),
        name=f"mla_kv_expand_{rows}",
    )(ckv, kr, w_ukv_re, kn_rows, cos_tab, sin_tab)


def _attn_body(q_ref, k_ref, v_ref, o_ref, *, heads_per_step):
    for hd in range(heads_per_step):
        qk = slice(2 * hd * LANES, 2 * (hd + 1) * LANES)
        vo = slice(hd * LANES, (hd + 1) * LANES)
        s = lax.dot_general(q_ref[:, qk], k_ref[:, qk], (((1,), (1,)), ((), ())), preferred_element_type=F32)
        m = jnp.max(s, axis=-1, keepdims=True)
        p = jnp.exp(s - m)
        l = jnp.sum(p, axis=-1, keepdims=True)
        o = jnp.dot(p.astype(BF16), v_ref[:, vo], preferred_element_type=F32)
        o_ref[:, vo] = (o / l).astype(BF16)


def _attention(q, k, v, *, row0, batch, seq, kv_len, heads, tq, heads_per_step):
    nq = seq // tq
    qblk0 = row0 // tq
    hps = heads_per_step
    return pl.pallas_call(
        functools.partial(_attn_body, heads_per_step=hps),
        grid=(batch, heads // hps, nq),
        in_specs=[
            pl.BlockSpec((tq, hps * 2 * LANES), lambda b, h, i: (qblk0 + b * nq + i, h)),
            pl.BlockSpec((kv_len, hps * 2 * LANES), lambda b, h, i: (b, h)),
            pl.BlockSpec((kv_len, hps * LANES), lambda b, h, i: (b, h)),
        ],
        out_specs=pl.BlockSpec((tq, hps * LANES), lambda b, h, i: (b * nq + i, h)),
        out_shape=jax.ShapeDtypeStruct((batch * seq, heads * LANES), BF16),
        compiler_params=_cparams(("parallel", "parallel", "parallel"), 48),
        name=f"mla_attention_kv{kv_len}",
    )(q, k, v)


def _oproj_body(x_ref, gate_ref, op_ref, os_ref, w_ref, out_ref, *, prompt_tiles):
    o = _group_select(pl.program_id(0), prompt_tiles, op_ref, os_ref)
    y = jnp.dot(o, w_ref[...], preferred_element_type=F32)
    out_ref[...] = x_ref[...] + gate_ref[...] * y


def _mla_out_proj(x, mod, layer, o_p, o_s, w_o, n_prompt, sample_seq):
    t, d = x.shape
    tm = TOKEN_TILE
    row = lambda i: (i, 0)
    prompt_tiles = n_prompt // tm
    return pl.pallas_call(
        functools.partial(_oproj_body, prompt_tiles=prompt_tiles),
        grid=(t // tm,),
        in_specs=[
            pl.BlockSpec((tm, d), row),
            _mod_spec(layer, 2, tm, n_prompt, sample_seq, d),
            *_group_specs((tm, o_p.shape[1]), prompt_tiles),
            _resident(w_o.shape),
        ],
        out_specs=pl.BlockSpec((tm, d), row),
        out_shape=jax.ShapeDtypeStruct((t, d), F32),
        compiler_params=_cparams(("parallel",), 40),
        name="mla_out_proj",
    )(x, mod, o_p, o_s, w_o.astype(BF16))


def _pack_bf16_pair(lo_half, hi_half):
    a = pltpu.bitcast(lo_half.astype(BF16).astype(F32), U32)
    b = pltpu.bitcast(hi_half.astype(BF16).astype(F32), U32)
    return a | (b >> 16)


def _unpack_bf16_pair(p):
    a = pltpu.bitcast(p & jnp.uint32(0xFFFF0000), F32)
    b = pltpu.bitcast(p << 16, F32)
    return a, b


def _router_body(x_ref, g_ref, sh_ref, sc_ref, whi_ref, wlo_ref, rb_ref,
                 hp_ref, ex_ref, wt_ref, rk_ref, cnt_ref, carry_ref, *, n_experts):
    i = pl.program_id(0)

    @pl.when(i == 0)
    def _():
        carry_ref[...] = jnp.zeros(carry_ref.shape, F32)

    h = _modulated(x_ref[...], g_ref[...], sh_ref[...], sc_ref[...])
    tm, d = h.shape
    hp_ref[...] = _pack_bf16_pair(h[:, :d // 2], h[:, d // 2:])

    h_hi = h.astype(BF16)
    h_lo = (h - h_hi.astype(F32)).astype(BF16)
    nt = (((1,), (1,)), ((), ()))
    logits = (lax.dot_general(whi_ref[...], h_hi, nt, preferred_element_type=F32)
              + lax.dot_general(whi_ref[...], h_lo, nt, preferred_element_type=F32)
              + lax.dot_general(wlo_ref[...], h_hi, nt, preferred_element_type=F32))
    scores = jax.nn.sigmoid(logits)
    biased = scores + rb_ref[...]
    per_group = n_experts // N_EXPERT_GROUPS
    assert per_group == 4 and TOP_K == 2
    b_rows = [biased[e:e + 1, :] for e in range(n_experts)]
    s_rows = [scores[e:e + 1, :] for e in range(n_experts)]

    best = None
    sel = jnp.zeros((1, tm), I32)
    for g in range(N_EXPERT_GROUPS):
        b0, b1, b2, b3 = b_rows[g * 4:(g + 1) * 4]
        m1, n1 = jnp.maximum(b0, b1), jnp.minimum(b0, b1)
        m2, n2 = jnp.maximum(b2, b3), jnp.minimum(b2, b3)
        top1 = jnp.maximum(m1, m2)
        top2 = jnp.maximum(jnp.minimum(m1, m2), jnp.maximum(n1, n2))
        gsum = top1 + top2
        if best is None:
            best = gsum
        else:
            better = gsum > best
            sel = jnp.where(better, g, sel)
            best = jnp.where(better, gsum, best)

    def pick(rows, j):
        out = rows[j]
        for g in range(1, N_EXPERT_GROUPS):
            out = jnp.where(sel == g, rows[g * 4 + j], out)
        return out

    cand_b = [pick(b_rows, j) for j in range(4)]
    cand_s = [pick(s_rows, j) for j in range(4)]

    def argmax4(vals):
        bv, bi = vals[0], jnp.zeros((1, tm), I32)
        for j in range(1, 4):
            gt = vals[j] > bv
            bi = jnp.where(gt, j, bi)
            bv = jnp.where(gt, vals[j], bv)
        return bi

    i1 = argmax4(cand_b)
    i2 = argmax4([jnp.where(i1 == j, -jnp.inf, cand_b[j]) for j in range(4)])

    def take(vals, idx):
        out = vals[0]
        for j in range(1, 4):
            out = jnp.where(idx == j, vals[j], out)
        return out

    s1 = take(cand_s, i1)
    s2 = take(cand_s, i2)
    tot = s1 + s2
    e1 = sel * 4 + i1
    e2 = sel * 4 + i2
    ex_ref[0:1, :] = e1
    ex_ref[1:2, :] = e2
    wt_ref[0:1, :] = s1 / tot
    wt_ref[1:2, :] = s2 / tot

    eid = lax.broadcasted_iota(I32, (n_experts, tm), 0)
    is1 = eid == e1
    is2 = eid == e2
    chosen = jnp.where(is1 | is2, 1.0, 0.0)
    before = (lax.broadcasted_iota(I32, (tm, tm), 0) < lax.broadcasted_iota(I32, (tm, tm), 1))
    rank = jnp.dot(chosen.astype(BF16), jnp.where(before, 1.0, 0.0).astype(BF16),
                   preferred_element_type=F32) + carry_ref[:, 0:1]
    rk_ref[0:1, :] = jnp.sum(jnp.where(is1, rank, 0.0), axis=0, keepdims=True).astype(I32)
    rk_ref[1:2, :] = jnp.sum(jnp.where(is2, rank, 0.0), axis=0, keepdims=True).astype(I32)
    carry_ref[...] = carry_ref[...] + jnp.sum(chosen, axis=1, keepdims=True)
    cnt_ref[...] = carry_ref[...]


def _router(x, mod, layer, gain, w_hi, w_lo, r_bias, n_prompt, sample_seq):
    t, d = x.shape
    tm = TOKEN_TILE
    n_experts = w_hi.shape[0]
    ms = functools.partial(_mod_spec, layer, tile=tm, n_prompt=n_prompt, sample_seq=sample_seq, d=d)
    col = lambda i: (0, i)
    return pl.pallas_call(
        functools.partial(_router_body, n_experts=n_experts),
        grid=(t // tm,),
        in_specs=[
            pl.BlockSpec((tm, d), lambda i: (i, 0)),
            _resident((1, d)),
            ms(chunk=3),
            ms(chunk=4),
            _resident(w_hi.shape),
            _resident(w_lo.shape),
            _resident((n_experts, 1)),
        ],
        out_specs=[
            pl.BlockSpec((tm, d // 2), lambda i: (i, 0)),
            pl.BlockSpec((TOP_K, tm), col),
            pl.BlockSpec((TOP_K, tm), col),
            pl.BlockSpec((TOP_K, tm), col),
            pl.BlockSpec((n_experts, LANES), lambda i: (0, 0)),
        ],
        out_shape=[
            jax.ShapeDtypeStruct((t, d // 2), U32),
            jax.ShapeDtypeStruct((TOP_K, t), I32),
            jax.ShapeDtypeStruct((TOP_K, t), F32),
            jax.ShapeDtypeStruct((TOP_K, t), I32),
            jax.ShapeDtypeStruct((n_experts, LANES), F32),
        ],
        scratch_shapes=[pltpu.VMEM((n_experts, LANES), F32)],
        compiler_params=_cparams(("arbitrary",), 40),
        name="moe_router",
    )(x, gain.reshape(1, d), mod, mod, w_hi, w_lo, r_bias.reshape(n_experts, 1))


def _row_copy(src_ref, src_row, dst_ref, dst_row, sem):
    return pltpu.make_async_copy(src_ref.at[pl.ds(src_row, 1)], dst_ref.at[pl.ds(dst_row, 1)], sem)


def _dispatch_body(last_ref, nv_ref, pos_ref, h_ref, o_ref, zero_ref, sem, zsem):
    tm = h_ref.shape[0]
    zt = zero_ref.shape[0]
    n_tiles = o_ref.shape[0] // zt

    @pl.when(pl.program_id(0) == 0)
    def _():
        zero_ref[...] = jnp.zeros(zero_ref.shape, zero_ref.dtype)

        def zero_tile(row):
            return pltpu.make_async_copy(zero_ref, o_ref.at[pl.ds(pl.multiple_of(row, zt), zt)], zsem)

        for e in range(last_ref.shape[0]):
            @pl.when(last_ref[e] >= 0)
            def _():
                zero_tile(last_ref[e]).start()

        def start_tail(j, c):
            zero_tile(j * zt).start()
            return c

        lax.fori_loop(nv_ref[0], n_tiles, start_tail, 0)
        for e in range(last_ref.shape[0]):
            @pl.when(last_ref[e] >= 0)
            def _():
                zero_tile(last_ref[e]).wait()

        def wait_tail(j, c):
            zero_tile(j * zt).wait()
            return c

        lax.fori_loop(nv_ref[0], n_tiles, wait_tail, 0)

    def issue(r, c):
        for k in range(TOP_K):
            _row_copy(h_ref, r, o_ref, pos_ref[k, r], sem).start()
        return c

    lax.fori_loop(0, tm, issue, 0, unroll=DMA_ISSUE_UNROLL)

    def drain(r, c):
        for k in range(TOP_K):
            _row_copy(h_ref, 0, o_ref, 0, sem).wait()
        return c

    lax.fori_loop(0, tm, drain, 0, unroll=DMA_ISSUE_UNROLL)


def _dispatch(hp, pos, last_tile_row, n_valid, sorted_rows):
    t, w = hp.shape
    tm = TOKEN_TILE
    grid_spec = pltpu.PrefetchScalarGridSpec(
        num_scalar_prefetch=2,
        grid=(t // tm,),
        in_specs=[
            pl.BlockSpec((TOP_K, tm), lambda i, *_: (0, i), memory_space=pltpu.SMEM),
            pl.BlockSpec((tm, w), lambda i, *_: (i, 0)),
        ],
        out_specs=pl.BlockSpec(memory_space=pl.ANY),
        scratch_shapes=[pltpu.VMEM((EXPERT_TILE, w), hp.dtype), pltpu.SemaphoreType.DMA, pltpu.SemaphoreType.DMA],
    )
    return pl.pallas_call(
        _dispatch_body,
        grid_spec=grid_spec,
        out_shape=jax.ShapeDtypeStruct((sorted_rows, w), hp.dtype),
        compiler_params=_cparams(("arbitrary",), 32),
        name="moe_dispatch",
    )(last_tile_row, n_valid, pos, hp)


def _expert_body(te_ref, nv_ref, x_ref, wg_ref, wu_ref, wd_ref, y_ref, wgb_ref, wub_ref, wdb_ref):
    i = pl.program_id(0)
    valid = i < nv_ref[0]
    new_expert = (i == 0) | (te_ref[i] != te_ref[jnp.maximum(i - 1, 0)])

    @pl.when(valid & new_expert)
    def _():
        wgb_ref[...] = wg_ref[...].astype(BF16)
        wub_ref[...] = wu_ref[...].astype(BF16)
        wdb_ref[...] = wd_ref[...].astype(BF16)

    @pl.when(valid)
    def _():
        a, b = _unpack_bf16_pair(x_ref[...])
        a = a.astype(BF16)
        b = b.astype(BF16)
        half = a.shape[1]
        g = (jnp.dot(a, wgb_ref[:half, :], preferred_element_type=F32)
             + jnp.dot(b, wgb_ref[half:, :], preferred_element_type=F32))
        u = (jnp.dot(a, wub_ref[:half, :], preferred_element_type=F32)
             + jnp.dot(b, wub_ref[half:, :], preferred_element_type=F32))
        act = (g * jax.nn.sigmoid(g) * u).astype(BF16)
        y = jnp.dot(act, wdb_ref[...], preferred_element_type=F32)
        d = y.shape[1]
        y_ref[...] = _pack_bf16_pair(y[:, :d // 2], y[:, d // 2:])

    @pl.when(i >= nv_ref[0])
    def _():
        y_ref[...] = jnp.zeros(y_ref.shape, U32)


def _experts(xs, tile_expert, n_valid, w_gate, w_up, w_down, layer):
    rows, half = xs.shape
    tm = EXPERT_TILE
    _, n_experts, d, f = w_gate.shape

    def xrow(i, te, nv):
        return (jnp.minimum(i, nv[0] - 1), 0)

    grid_spec = pltpu.PrefetchScalarGridSpec(
        num_scalar_prefetch=2,
        grid=(rows // tm,),
        in_specs=[
            pl.BlockSpec((tm, half), xrow),
            pl.BlockSpec((None, None, d, f), lambda i, te, nv: (layer, te[i], 0, 0)),
            pl.BlockSpec((None, None, d, f), lambda i, te, nv: (layer, te[i], 0, 0)),
            pl.BlockSpec((None, None, f, d), lambda i, te, nv: (layer, te[i], 0, 0)),
        ],
        out_specs=pl.BlockSpec((tm, half), lambda i, te, nv: (i, 0)),
        scratch_shapes=[pltpu.VMEM((d, f), BF16), pltpu.VMEM((d, f), BF16), pltpu.VMEM((f, d), BF16)],
    )
    return pl.pallas_call(
        _expert_body,
        grid_spec=grid_spec,
        out_shape=jax.ShapeDtypeStruct((rows, half), U32),
        compiler_params=_cparams(("arbitrary",), 56),
        name="moe_experts",
    )(tile_expert, n_valid, xs, w_gate, w_up, w_down)


def _combine_body(pos_ref, x_ref, gate_ref, wt_ref, ys_ref, *rest, prompt_tiles):
    o_refs, (buf_ref, sem) = rest[:-2], rest[-2:]
    tm = x_ref.shape[0]

    def issue(r, c):
        for k in range(TOP_K):
            _row_copy(ys_ref, pos_ref[k, r], buf_ref.at[k], r, sem).start()
        return c

    lax.fori_loop(0, tm, issue, 0, unroll=DMA_ISSUE_UNROLL)

    def drain(r, c):
        for k in range(TOP_K):
            _row_copy(ys_ref, 0, buf_ref.at[k], 0, sem).wait()
        return c

    lax.fori_loop(0, tm, drain, 0, unroll=DMA_ISSUE_UNROLL)
    wt = wt_ref[...]
    a0, b0 = _unpack_bf16_pair(buf_ref[0])
    a1, b1 = _unpack_bf16_pair(buf_ref[1])
    w0 = wt[:, 0:1]
    w1 = wt[:, 1:2]
    half = a0.shape[1]
    gate = gate_ref[...]
    lo = x_ref[:, :half] + gate[:, :half] * (w0 * a0 + w1 * a1)
    hi = x_ref[:, half:] + gate[:, half:] * (w0 * b0 + w1 * b1)
    if len(o_refs) == 1:
        o_refs[0][:, :half] = lo
        o_refs[0][:, half:] = hi
    else:
        i = pl.program_id(0)
        for o_ref, mine in zip(o_refs, (i < prompt_tiles, i >= prompt_tiles)):
            @pl.when(mine)
            def _():
                o_ref[:, :half] = lo
                o_ref[:, half:] = hi


def _combine(x, mod, layer, ys, pos, wts_t, n_prompt, sample_seq, split):
    t, d = x.shape
    tm = TOKEN_TILE
    prompt_tiles = n_prompt // tm
    if split:
        out_specs = _group_specs((tm, d), prompt_tiles)
        out_shape = [jax.ShapeDtypeStruct((n_prompt, d), F32), jax.ShapeDtypeStruct((t - n_prompt, d), F32)]
    else:
        out_specs = [pl.BlockSpec((tm, d), lambda i: (i, 0))]
        out_shape = [jax.ShapeDtypeStruct((t, d), F32)]
    return pl.pallas_call(
        functools.partial(_combine_body, prompt_tiles=prompt_tiles),
        grid=(t // tm,),
        in_specs=[
            pl.BlockSpec((TOP_K, tm), lambda i: (0, i), memory_space=pltpu.SMEM),
            pl.BlockSpec((tm, d), lambda i: (i, 0)),
            _mod_spec(layer, 5, tm, n_prompt, sample_seq, d),
            pl.BlockSpec((tm, TOP_K), lambda i: (i, 0)),
            pl.BlockSpec(memory_space=pl.ANY),
        ],
        out_specs=out_specs,
        out_shape=out_shape,
        scratch_shapes=[pltpu.VMEM((TOP_K, tm, d // 2), U32), pltpu.SemaphoreType.DMA],
        compiler_params=_cparams(("arbitrary",), 32),
        name="moe_combine",
    )(pos, x, mod, wts_t, ys)


def _moe(x, mod, layer, gain, w_hi, w_lo, r_bias, w_gate, w_up, w_down, n_prompt, sample_seq, split):
    t, d = x.shape
    n_experts = w_hi.shape[0]
    hp, ex, wt, rk, cnt = _router(x, mod, layer, gain, w_hi, w_lo, r_bias, n_prompt, sample_seq)
    counts = cnt[:, 0].astype(I32)
    tiles = (counts + EXPERT_TILE - 1) // EXPERT_TILE
    tile_end = jnp.cumsum(tiles)
    seg_start = (tile_end - tiles) * EXPERT_TILE
    e_ids = jnp.arange(n_experts, dtype=I32)
    pos = rk + jnp.sum(jnp.where(ex[..., None] == e_ids, seg_start, 0), axis=-1)
    sorted_rows = TOP_K * t + n_experts * EXPERT_TILE
    n_tiles = sorted_rows // EXPERT_TILE
    n_valid = tile_end[-1:]
    tile_ids = jnp.arange(n_tiles, dtype=I32)
    ends_before = lambda i: jnp.sum((tile_end[None, :] <= i[:, None]).astype(I32), axis=1)
    tile_expert = jnp.where(tile_ids < n_valid[0], ends_before(tile_ids), ends_before(n_valid - 1))
    tile_expert = jnp.minimum(tile_expert, n_experts - 1).astype(I32)
    last_tile_row = jnp.where(tiles > 0, (tile_end - 1) * EXPERT_TILE, -1).astype(I32)
    n_valid = n_valid.astype(I32)
    xs = _dispatch(hp, pos, last_tile_row, n_valid, sorted_rows)
    ys = _experts(xs, tile_expert, n_valid, w_gate, w_up, w_down, layer)
    return _combine(x, mod, layer, ys, pos, wt.T, n_prompt, sample_seq, split)


def _rope_tables(sample_seq, rope_dim, tile):
    n_freq = rope_dim // 4
    pos = jnp.arange(sample_seq)
    row_pos = (pos // GRID_W).astype(F32)
    col_pos = (pos % GRID_W).astype(F32)
    inv_freq = ROPE_BASE ** (-jnp.arange(n_freq, dtype=F32) / n_freq)
    ar = row_pos[:, None] * inv_freq
    ac = col_pos[:, None] * inv_freq
    zeros = jnp.zeros((sample_seq, LANES - rope_dim), F32)
    cos = jnp.concatenate([jnp.cos(ar), jnp.cos(ar), jnp.cos(ac), jnp.cos(ac), zeros], axis=1)
    sin = jnp.concatenate([-jnp.sin(ar), jnp.sin(ar), -jnp.sin(ac), jnp.sin(ac), zeros], axis=1)
    ident_c = jnp.concatenate([jnp.ones((tile, rope_dim), F32), jnp.zeros((tile, LANES - rope_dim), F32)], axis=1)
    return (jnp.concatenate([ident_c, cos], axis=0),
            jnp.concatenate([jnp.zeros((tile, LANES), F32), sin], axis=0))


def _norm_rows(gain, nope, rope_dim):
    quarter = rope_dim // 4
    gr = gain[nope:]
    grs = jnp.concatenate([gr[quarter:2 * quarter], gr[:quarter], gr[3 * quarter:], gr[2 * quarter:3 * quarter]])
    zpad = jnp.zeros((LANES - rope_dim,), F32)
    return jnp.stack([gain[:nope], jnp.concatenate([gr, zpad]), jnp.concatenate([grs, zpad])])


def kernel(x_prompt, x_sample, state_lru_fwd, state_lru_bwd, cache_mla_ckv, cache_mla_krope, c, c_ctx,
           ada_w, ada_b, norm_mix, norm_ffn, mix0_w_in, mix0_w_out, lru_conv_w, lru_conv_b,
           lru_w_r, lru_b_r, lru_w_i, lru_b_i, lru_lambda, gmlp_v_norm, gmlp_w_s, gmlp_b_s,
           mla_w_down, mla_q_a_norm, mla_kv_a_norm, mla_w_uq, mla_w_ukv, mla_q_norm, mla_k_norm, mla_w_o,
           router_w, router_bias, moe_w_gate, moe_w_up, moe_w_down):
    batch, seq, d = x_prompt.shape
    dec_batch, dec_seq, _ = x_sample.shape
    depth = ada_w.shape[0]
    n_prompt = batch * seq
    n_sample = dec_batch * dec_seq
    assert n_prompt % dec_seq == 0 and seq % TOKEN_TILE == 0 and dec_seq % TOKEN_TILE == 0
    assert 1 + dec_batch <= SUBLANES

    x_parts = (x_prompt.reshape(n_prompt, d), x_sample.reshape(n_sample, d))

    cond = jnp.concatenate([c_ctx[None, :], c, jnp.zeros((SUBLANES - 1 - dec_batch, d), F32)], axis=0)
    mod = _modulation(cond, ada_w, ada_b).reshape(depth, SUBLANES, 6, 1, d)

    n_experts = router_w.shape[1]
    rw_t = router_w.T
    rw_hi = rw_t.astype(BF16)
    rw_lo = (rw_t - rw_hi.astype(F32)).astype(BF16)

    fwd_states, bwd_states, ckv_caches, krope_caches = [], [], [], []
    for layer in range(depth):
        j = layer // 2
        if layer % 2 == 0:
            width = lru_conv_w.shape[2]
            heads = lru_w_r.shape[2]
            xb, gg, gu, gv = _even_in_proj(x_parts, mod, layer, norm_mix[layer], mix0_w_in[j], n_prompt, dec_seq)
            wcat = jnp.concatenate([lru_w_r[j, 0], lru_w_i[j, 0], lru_w_r[j, 1], lru_w_i[j, 1]], axis=-1).astype(BF16)
            hb = lambda v: v.reshape(heads, 1, width // heads)
            bcat = jnp.concatenate([hb(lru_b_r[j, 0]), hb(lru_b_i[j, 0]), hb(lru_b_r[j, 1]), hb(lru_b_i[j, 1])], axis=-1)
            lru_args = (lru_conv_w[j], lru_conv_b[j].reshape(1, width), wcat, bcat, lru_lambda[j])
            zero_state = jnp.zeros((batch, 1, width), F32)
            nseq_p = LRU_PROMPT_SEQS_PER_STEP if batch % LRU_PROMPT_SEQS_PER_STEP == 0 else 1
            ya_p, fin_f, fin_b = _lru_mixer(xb, gg, zero_state, zero_state, *lru_args,
                                            row0=0, batch=batch, seq=seq, nseq=nseq_p)
            ya_s, _, _ = _lru_mixer(xb, gg, state_lru_fwd[:, j][:, None, :], state_lru_bwd[:, j][:, None, :],
                                    *lru_args, row0=n_prompt, batch=dec_batch, seq=dec_seq, nseq=1)
            fwd_states.append(fin_f[:, 0, :])
            bwd_states.append(fin_b[:, 0, :])
            groups, chunk, _ = gmlp_w_s[j].shape
            gd = width // groups
            b_full = jnp.repeat(gmlp_b_s[j].T, gd, axis=1)
            x = _even_out_proj(x_parts, mod, layer, ya_p, ya_s, gu, gv, gmlp_v_norm[j], gmlp_w_s[j], b_full,
                               mix0_w_out[j], n_prompt, dec_seq)
        else:
            q_lora = mla_q_a_norm.shape[1]
            kv_lora = mla_kv_a_norm.shape[1]
            qk_dim = mla_q_norm.shape[1]
            rope_dim = cache_mla_krope.shape[-1]
            nope = qk_dim - rope_dim
            heads = mla_w_uq.shape[2] // qk_dim
            v_dim = mla_w_ukv.shape[2] // heads - nope
            past = cache_mla_ckv.shape[2]
            assert nope == LANES and v_dim == LANES and rope_dim == 4 * ROPE_QUARTER
            wd = mla_w_down[j]
            wd_ext = jnp.concatenate([wd, jnp.zeros((d, LANES - rope_dim), F32)], axis=1).astype(BF16)
            wuq = mla_w_uq[j].reshape(q_lora, heads, qk_dim)
            wuq_pad = jnp.concatenate([wuq, jnp.zeros((q_lora, heads, 2 * LANES - qk_dim), F32)], axis=-1)
            wuq_pad = wuq_pad.reshape(q_lora, heads * 2 * LANES).astype(BF16)
            wukv = mla_w_ukv[j].reshape(kv_lora, heads, nope + v_dim)
            w_ukv_re = jnp.concatenate([wukv[:, :, :nope].reshape(kv_lora, heads * nope),
                                        wukv[:, :, nope:].reshape(kv_lora, heads * v_dim)], axis=1).astype(BF16)
            qn_rows = _norm_rows(mla_q_norm[j], nope, rope_dim)
            kn_rows = _norm_rows(mla_k_norm[j], nope, rope_dim)
            cos_tab, sin_tab = _rope_tables(dec_seq, rope_dim, MLA_TOKEN_TILE)
            x = x_parts[0] if len(x_parts) == 1 else jnp.concatenate(x_parts, axis=0)
            q, ckv, kr = _mla_in_proj(x, mod, layer, norm_mix[layer], wd_ext, mla_q_a_norm[j], mla_kv_a_norm[j],
                                      wuq_pad, qn_rows, cos_tab, sin_tab, n_prompt, dec_seq,
                                      heads=heads, q_lora=q_lora, kv_lora=kv_lora, qk_dim=qk_dim)
            ckv_caches.append(ckv[:n_prompt].reshape(batch, seq, kv_lora))
            krope_caches.append(kr[:n_prompt, :rope_dim].reshape(batch, seq, rope_dim))
            expand = functools.partial(_kv_expand, w_ukv_re=w_ukv_re, kn_rows=kn_rows, cos_tab=cos_tab,
                                       sin_tab=sin_tab, heads=heads, qk_dim=qk_dim)
            k_p, v_p = expand(ckv[:n_prompt], kr[:n_prompt], rope_index=lambda i: (0, 0))
            kv_len = past + dec_seq
            ctx_kr = jnp.pad(cache_mla_krope[:, j], ((0, 0), (0, 0), (0, LANES - rope_dim)))
            ckv_s = jnp.concatenate([cache_mla_ckv[:, j], ckv[n_prompt:].reshape(dec_batch, dec_seq, kv_lora)], axis=1)
            kr_s = jnp.concatenate([ctx_kr, kr[n_prompt:].reshape(dec_batch, dec_seq, LANES)], axis=1)
            tiles_per_req = kv_len // TOKEN_TILE
            ctx_tiles = past // TOKEN_TILE
            assert past % TOKEN_TILE == 0

            def latent_rope(i):
                r = i % tiles_per_req
                return (jnp.where(r < ctx_tiles, 0, MLA_TOKEN_TILE // TOKEN_TILE + r - ctx_tiles), 0)

            k_s, v_s = expand(ckv_s.reshape(dec_batch * kv_len, kv_lora), kr_s.reshape(dec_batch * kv_len, LANES),
                              rope_index=latent_rope)
            o_p = _attention(q, k_p, v_p, row0=0, batch=batch, seq=seq, kv_len=seq, heads=heads, tq=seq,
                             heads_per_step=heads)
            o_s = _attention(q, k_s, v_s, row0=n_prompt, batch=dec_batch, seq=dec_seq, kv_len=kv_len,
                             heads=heads, tq=ATTN_Q_TILE, heads_per_step=ATTN_LATENT_HEADS_PER_STEP)
            x = _mla_out_proj(x, mod, layer, o_p, o_s, mla_w_o[j], n_prompt, dec_seq)
        x_parts = _moe(x, mod, layer, norm_ffn[layer], rw_hi, rw_lo, router_bias, moe_w_gate, moe_w_up, moe_w_down,
                       n_prompt, dec_seq, split=layer == depth - 1)

    xp = x_parts[0].reshape(batch, seq, d)
    xs = x_parts[1].reshape(dec_batch, dec_seq, d)
    return (xp, xs,
            jnp.stack(fwd_states, axis=1), jnp.stack(bwd_states, axis=1),
            jnp.stack(ckv_caches, axis=1), jnp.stack(krope_caches, axis=1))
```

```python
import functools

import jax
import jax.numpy as jnp
from jax import lax
from jax.experimental import pallas as pl
from jax.experimental.pallas import tpu as pltpu

F32 = jnp.float32
BF16 = jnp.bfloat16
U32 = jnp.uint32
I32 = jnp.int32

EPS = 1e-6
LRU_C = 8.0
GRID_W = 64
ROPE_BASE = 10000.0
ROPE_QUARTER = 16
N_EXPERT_GROUPS = 4
TOP_K = 2

LANES = 128
SUBLANES = 8
VMEM_BYTES_V7X = 64 * 1024 * 1024

TOKEN_TILE = 256
MLA_TOKEN_TILE = 512
EXPERT_TILE = 256
MOD_COL_TILE = 1024
ATTN_Q_TILE = 512
ATTN_LATENT_HEADS_PER_STEP = 2
LRU_GATE_CHUNK = 256
RUN_PAD_ROWS = 128
LRU_PROMPT_SEQS_PER_STEP = 8


def _cparams(semantics, vmem_mb):
    return pltpu.CompilerParams(dimension_semantics=semantics, vmem_limit_bytes=vmem_mb * 1024 * 1024)


def _resident(shape):
    nd = len(shape)
    return pl.BlockSpec(shape, lambda *_: (0,) * nd, pipeline_mode=pl.Buffered(1))


def _mod_spec(layer, chunk, tile, n_prompt, sample_seq, d):
    def index(i, *_):
        t = i * tile
        row = jnp.where(t < n_prompt, 0, 1 + (t - n_prompt) // sample_seq)
        return (layer, row, chunk, 0, 0)

    return pl.BlockSpec((None, None, None, 1, d), index)


def _modulated(x, gain, shift, scale):
    y = x * lax.rsqrt(jnp.mean(x * x, axis=-1, keepdims=True) + EPS)
    return (y * gain) * (1.0 + scale) + shift


def _mod_body(c_ref, w_ref, b_ref, o_ref):
    c = c_ref[...]
    s = (c * jax.nn.sigmoid(c)).astype(BF16)
    o_ref[...] = jnp.dot(s, w_ref[...].astype(BF16), preferred_element_type=F32) + b_ref[...]


def _modulation(cond, ada_w, ada_b):
    depth, d, n = ada_w.shape
    tn = MOD_COL_TILE
    return pl.pallas_call(
        _mod_body,
        grid=(depth, n // tn),
        in_specs=[
            pl.BlockSpec((SUBLANES, d), lambda l, j: (0, 0)),
            pl.BlockSpec((None, d, tn), lambda l, j: (l, 0, j)),
            pl.BlockSpec((None, 1, tn), lambda l, j: (l, 0, j)),
        ],
        out_specs=pl.BlockSpec((None, SUBLANES, tn), lambda l, j: (l, 0, j)),
        out_shape=jax.ShapeDtypeStruct((depth, SUBLANES, n), F32),
        compiler_params=_cparams(("parallel", "parallel"), 40),
        name="adaln_projection",
    )(cond, ada_w, ada_b.reshape(depth, 1, n))


def _group_select(i, prompt_tiles, p_ref, s_ref):
    return jnp.where(i < prompt_tiles, p_ref[...], s_ref[...])


def _group_specs(block, prompt_tiles):
    return [pl.BlockSpec(block, lambda i, *_: (jnp.minimum(i, prompt_tiles - 1), 0)),
            pl.BlockSpec(block, lambda i, *_: (jnp.maximum(i - prompt_tiles, 0), 0))]


def _token_specs(x_parts, tile, prompt_tiles):
    d = x_parts[0].shape[1]
    if len(x_parts) == 1:
        return [pl.BlockSpec((tile, d), lambda i, *_: (i, 0))]
    return _group_specs((tile, d), prompt_tiles)


def _token_tile(i, prompt_tiles, x_refs):
    return x_refs[0][...] if len(x_refs) == 1 else _group_select(i, prompt_tiles, *x_refs)


def _in0_body(*refs, n_x, prompt_tiles):
    x_refs = refs[:n_x]
    g_ref, sh_ref, sc_ref, w_ref, xb_ref, gg_ref, gu_ref, gv_ref = refs[n_x:]
    x = _token_tile(pl.program_id(0), prompt_tiles, x_refs)
    h = _modulated(x, g_ref[...], sh_ref[...], sc_ref[...])
    z = jnp.dot(h.astype(BF16), w_ref[...], preferred_element_type=F32)
    w = xb_ref.shape[1]
    xb_ref[...] = z[:, :w]
    gg_ref[...] = jax.nn.gelu(z[:, w:2 * w]).astype(BF16)
    gu_ref[...] = jax.nn.gelu(z[:, 2 * w:3 * w]).astype(BF16)
    gv_ref[...] = jax.nn.gelu(z[:, 3 * w:]).astype(BF16)


def _even_in_proj(x_parts, mod, layer, gain, w_in, n_prompt, sample_seq):
    t = sum(p.shape[0] for p in x_parts)
    d = x_parts[0].shape[1]
    tm = TOKEN_TILE
    w4 = w_in.shape[1]
    w = w4 // 4
    row = lambda i: (i, 0)
    ms = functools.partial(_mod_spec, layer, tile=tm, n_prompt=n_prompt, sample_seq=sample_seq, d=d)
    return pl.pallas_call(
        functools.partial(_in0_body, n_x=len(x_parts), prompt_tiles=n_prompt // tm),
        grid=(t // tm,),
        in_specs=[
            *_token_specs(x_parts, tm, n_prompt // tm),
            _resident((1, d)),
            ms(chunk=0),
            ms(chunk=1),
            _resident((d, w4)),
        ],
        out_specs=[pl.BlockSpec((tm, w), row)] * 4,
        out_shape=[
            jax.ShapeDtypeStruct((t, w), F32),
            jax.ShapeDtypeStruct((t, w), BF16),
            jax.ShapeDtypeStruct((t, w), BF16),
            jax.ShapeDtypeStruct((t, w), BF16),
        ],
        compiler_params=_cparams(("parallel",), 48),
        name="even_in_proj",
    )(*x_parts, gain.reshape(1, d), mod, mod, w_in.astype(BF16))


def _tile_scan(a, b, row, reverse):
    for d in (1, 2, 4):
        if reverse:
            keep = row < SUBLANES - d
            a_s = jnp.where(keep, pltpu.roll(a, SUBLANES - d, 0), 1.0)
            b_s = jnp.where(keep, pltpu.roll(b, SUBLANES - d, 0), 0.0)
        else:
            keep = row >= d
            a_s = jnp.where(keep, pltpu.roll(a, d, 0), 1.0)
            b_s = jnp.where(keep, pltpu.roll(b, d, 0), 0.0)
        b = b + a * b_s
        a = a * a_s
    return a, b


def _sigmoid(x):
    return 0.5 * jnp.tanh(0.5 * x) + 0.5


def _lru_body(xb_ref, gg_ref, h0f_ref, h0b_ref, cw_ref, cb_ref, w_ref, bias_ref, lam_ref,
              ya_ref, ff_ref, fb_ref,
              xp_ref, af_ref, bf_ref, ab_ref, bb_ref, hf_ref, hb_ref, *, seq, chunk, nseq):
    hw = LANES
    pad = SUBLANES
    pitch = seq + 2 * pad
    for s in range(nseq):
        xp_ref[pl.ds(s * pitch, pad), :] = jnp.zeros((pad, hw), F32)
        xp_ref[pl.ds(s * pitch + pad + seq, pad), :] = jnp.zeros((pad, hw), F32)
        xp_ref[pl.ds(s * pitch + pad, seq), :] = xb_ref[pl.ds(s * seq, seq), :]

    cw = cw_ref[...]
    cb = cb_ref[...]
    lam = lam_ref[...]
    neg = -lam
    softplus = jnp.maximum(neg, 0.0) + jnp.log1p(jnp.exp(-jnp.abs(neg)))
    nsp = -LRU_C * softplus
    w = w_ref[...]
    bias = bias_ref[...]

    for s in range(nseq):
        for c in range(seq // chunk):
            src = s * pitch + pad + c * chunk
            dst = s * seq + c * chunk
            xc = (cw[0:1] * xp_ref[pl.ds(src - 2, chunk), :]
                  + cw[1:2] * xp_ref[pl.ds(src - 1, chunk), :]
                  + cw[2:3] * xp_ref[pl.ds(src, chunk), :]
                  + cw[3:4] * xp_ref[pl.ds(src + 1, chunk), :]) + cb
            g = jnp.dot(xc.astype(BF16), w, preferred_element_type=F32) + bias
            for direction, (a_ref, b_ref) in enumerate(((af_ref, bf_ref), (ab_ref, bb_ref))):
                r = _sigmoid(g[:, (2 * direction) * hw:(2 * direction + 1) * hw])
                gi = _sigmoid(g[:, (2 * direction + 1) * hw:(2 * direction + 2) * hw])
                log_a = r * nsp[direction:direction + 1]
                a = jnp.exp(log_a)
                a_ref[pl.ds(dst, chunk), :] = a
                b_ref[pl.ds(dst, chunk), :] = jnp.sqrt(1.0 - a * a) * gi * xc

    n_tiles = seq // SUBLANES
    row = lax.broadcasted_iota(I32, (SUBLANES, hw), 0)

    def step(i, carry):
        new = []
        for s in range(nseq):
            hf, hb = carry[2 * s], carry[2 * s + 1]
            rf = pl.multiple_of(s * seq + i * SUBLANES, SUBLANES)
            rb = pl.multiple_of(s * seq + (n_tiles - 1 - i) * SUBLANES, SUBLANES)
            a, b = _tile_scan(af_ref[pl.ds(rf, SUBLANES), :], bf_ref[pl.ds(rf, SUBLANES), :], row, False)
            h = b + a * hf
            hf_ref[pl.ds(rf, SUBLANES), :] = h
            new.append(jnp.broadcast_to(h[SUBLANES - 1:SUBLANES, :], (SUBLANES, hw)))
            a, b = _tile_scan(ab_ref[pl.ds(rb, SUBLANES), :], bb_ref[pl.ds(rb, SUBLANES), :], row, True)
            h = b + a * hb
            hb_ref[pl.ds(rb, SUBLANES), :] = h
            new.append(jnp.broadcast_to(h[0:1, :], (SUBLANES, hw)))
        return tuple(new)

    init = []
    for s in range(nseq):
        init.append(jnp.broadcast_to(h0f_ref[s], (SUBLANES, hw)))
        init.append(jnp.broadcast_to(h0b_ref[s], (SUBLANES, hw)))
    final = lax.fori_loop(0, n_tiles, step, tuple(init), unroll=2 if nseq == 1 else 1)
    for s in range(nseq):
        ff_ref[s] = final[2 * s][0:1, :]
        fb_ref[s] = final[2 * s + 1][0:1, :]
    ya_ref[...] = ((hf_ref[...] + hb_ref[...]) * gg_ref[...].astype(F32)).astype(BF16)


def _lru_mixer(xb, gg, h0f, h0b, conv_w, conv_b, wcat, bcat, lam, *, row0, batch, seq, nseq):
    _, width = xb.shape
    heads = width // LANES
    rows = nseq * seq
    assert batch % nseq == 0 and row0 % rows == 0
    blk0 = row0 // rows
    tok = lambda b, h: (blk0 + b, h)
    state = lambda b, h: (b, 0, h)
    seq_buf = pltpu.VMEM((rows, LANES), F32)
    return pl.pallas_call(
        functools.partial(_lru_body, seq=seq, chunk=min(seq, LRU_GATE_CHUNK), nseq=nseq),
        grid=(batch // nseq, heads),
        in_specs=[
            pl.BlockSpec((rows, LANES), tok),
            pl.BlockSpec((rows, LANES), tok),
            pl.BlockSpec((nseq, 1, LANES), state),
            pl.BlockSpec((nseq, 1, LANES), state),
            pl.BlockSpec((conv_w.shape[0], LANES), lambda b, h: (0, h)),
            pl.BlockSpec((1, LANES), lambda b, h: (0, h)),
            pl.BlockSpec((None, LANES, 4 * LANES), lambda b, h: (h, 0, 0)),
            pl.BlockSpec((None, 1, 4 * LANES), lambda b, h: (h, 0, 0)),
            pl.BlockSpec((2, LANES), lambda b, h: (0, h)),
        ],
        out_specs=[
            pl.BlockSpec((rows, LANES), lambda b, h: (b, h)),
            pl.BlockSpec((nseq, 1, LANES), state),
            pl.BlockSpec((nseq, 1, LANES), state),
        ],
        out_shape=[
            jax.ShapeDtypeStruct((batch * seq, width), BF16),
            jax.ShapeDtypeStruct((batch, 1, width), F32),
            jax.ShapeDtypeStruct((batch, 1, width), F32),
        ],
        scratch_shapes=[pltpu.VMEM((nseq * (seq + 2 * SUBLANES), LANES), F32)] + [seq_buf] * 6,
        compiler_params=_cparams(("parallel", "parallel"), 40),
        name=f"rglru_seq{seq}",
    )(xb, gg, h0f, h0b, conv_w, conv_b, wcat, bcat, lam)


def _out0_body(*refs, n_x, chunk, prompt_tiles):
    x_refs = refs[:n_x]
    (gate_ref, yap_ref, yas_ref, gu_ref, gv_ref, vg_ref, ws_ref, bs_ref, wa_ref, wb_ref,
     o_ref, yb_ref) = refs[n_x:]
    tm = o_ref.shape[0]
    x = _token_tile(pl.program_id(0), prompt_tiles, x_refs)
    ya = _group_select(pl.program_id(0), prompt_tiles, yap_ref, yas_ref)
    v = gv_ref[...].astype(F32)
    vn = (v * lax.rsqrt(jnp.mean(v * v, axis=-1, keepdims=True) + EPS) * vg_ref[...]).astype(BF16)
    groups = ws_ref.shape[0]
    gd = vn.shape[1] // groups
    for c in range(tm // chunk):
        rows = slice(c * chunk, (c + 1) * chunk)
        for g in range(groups):
            cols = slice(g * gd, (g + 1) * gd)
            mixed = jnp.dot(ws_ref[g], vn[rows, cols], preferred_element_type=F32) + bs_ref[:, cols]
            yb_ref[rows, cols] = (gu_ref[rows, cols].astype(F32) * mixed).astype(BF16)
    y = (jnp.dot(ya, wa_ref[...], preferred_element_type=F32)
         + jnp.dot(yb_ref[...], wb_ref[...], preferred_element_type=F32))
    o_ref[...] = x + gate_ref[...] * y


def _even_out_proj(x_parts, mod, layer, ya_p, ya_s, gu, gv, v_gain, w_s, b_full, w_out, n_prompt, sample_seq):
    t = sum(p.shape[0] for p in x_parts)
    d = x_parts[0].shape[1]
    tm = TOKEN_TILE
    w = ya_p.shape[1]
    row = lambda i: (i, 0)
    chunk = w_s.shape[1]
    prompt_tiles = n_prompt // tm
    return pl.pallas_call(
        functools.partial(_out0_body, n_x=len(x_parts), chunk=chunk, prompt_tiles=prompt_tiles),
        grid=(t // tm,),
        in_specs=[
            *_token_specs(x_parts, tm, prompt_tiles),
            _mod_spec(layer, 2, tm, n_prompt, sample_seq, d),
            *_group_specs((tm, w), prompt_tiles),
            pl.BlockSpec((tm, w), row),
            pl.BlockSpec((tm, w), row),
            _resident((1, w)),
            _resident(w_s.shape),
            _resident(b_full.shape),
            _resident((w, d)),
            _resident((w, d)),
        ],
        out_specs=pl.BlockSpec((tm, d), row),
        out_shape=jax.ShapeDtypeStruct((t, d), F32),
        scratch_shapes=[pltpu.VMEM((tm, w), BF16)],
        compiler_params=_cparams(("parallel",), 40),
        name="even_out_proj",
    )(*x_parts, mod, ya_p, ya_s, gu, gv, v_gain.reshape(1, w), w_s.astype(BF16), b_full,
      w_out[:w].astype(BF16), w_out[w:].astype(BF16))


def _swap_halves(x, lane):
    quarter = ROPE_QUARTER
    up = pltpu.roll(x, LANES - quarter, 1)
    down = pltpu.roll(x, quarter, 1)
    return jnp.where((lane % (2 * quarter)) < quarter, up, down)


def _mla_in_body(x_ref, g_ref, sh_ref, sc_ref, wd_ref, qan_ref, kvan_ref, wuq_ref, qn_ref, cos_ref, sin_ref,
                 q_ref, ckv_ref, kr_ref, *, heads, q_lora, kv_lora, qk_dim, sm_scale, sub):
    gn = qn_ref[0:1, :]
    gr = qn_ref[1:2, :]
    grs = qn_ref[2:3, :]
    lane = lax.broadcasted_iota(I32, (sub, LANES), 1)
    for r0 in range(0, x_ref.shape[0], sub):
        rows = pl.ds(r0, sub)
        h = _modulated(x_ref[rows, :], g_ref[...], sh_ref[...], sc_ref[...])
        z = jnp.dot(h.astype(BF16), wd_ref[...], preferred_element_type=F32)
        cq = z[:, :q_lora]
        cq = cq * lax.rsqrt(jnp.mean(cq * cq, axis=-1, keepdims=True) + EPS) * qan_ref[...]
        ckv = z[:, q_lora:q_lora + kv_lora]
        ckv_ref[rows, :] = ckv * lax.rsqrt(jnp.mean(ckv * ckv, axis=-1, keepdims=True) + EPS) * kvan_ref[...]
        kr_ref[rows, :] = z[:, q_lora + kv_lora:]
        q = jnp.dot(cq.astype(BF16), wuq_ref[...], preferred_element_type=F32)
        cos = cos_ref[rows, :]
        sin = sin_ref[rows, :]
        for hd in range(heads):
            qn = q[:, 2 * hd * LANES:(2 * hd + 1) * LANES]
            qr = q[:, (2 * hd + 1) * LANES:(2 * hd + 2) * LANES]
            ss = jnp.sum(qn * qn, axis=-1, keepdims=True) + jnp.sum(qr * qr, axis=-1, keepdims=True)
            rinv = lax.rsqrt(ss * (1.0 / qk_dim) + EPS) * sm_scale
            q_ref[rows, 2 * hd * LANES:(2 * hd + 1) * LANES] = (qn * gn * rinv).astype(BF16)
            rot = (qr * gr) * cos + (_swap_halves(qr, lane) * grs) * sin
            q_ref[rows, (2 * hd + 1) * LANES:(2 * hd + 2) * LANES] = (rot * rinv).astype(BF16)


def _rope_spec(tile, n_prompt, sample_seq):
    def index(i):
        t = i * tile
        return (jnp.where(t < n_prompt, 0, 1 + ((t - n_prompt) % sample_seq) // tile), 0)

    return pl.BlockSpec((tile, LANES), index)


def _mla_in_proj(x, mod, layer, gain, wd_ext, q_a_norm, kv_a_norm, wuq_pad, qn_rows, cos_tab, sin_tab,
                 n_prompt, sample_seq, *, heads, q_lora, kv_lora, qk_dim):
    t, d = x.shape
    tm = MLA_TOKEN_TILE
    row = lambda i: (i, 0)
    ms = functools.partial(_mod_spec, layer, tile=tm, n_prompt=n_prompt, sample_seq=sample_seq, d=d)
    body = functools.partial(_mla_in_body, heads=heads, q_lora=q_lora, kv_lora=kv_lora, qk_dim=qk_dim,
                             sm_scale=float(qk_dim) ** -0.5, sub=TOKEN_TILE)
    return pl.pallas_call(
        body,
        grid=(t // tm,),
        in_specs=[
            pl.BlockSpec((tm, d), row),
            _resident((1, d)),
            ms(chunk=0),
            ms(chunk=1),
            _resident(wd_ext.shape),
            _resident((1, q_lora)),
            _resident((1, kv_lora)),
            _resident(wuq_pad.shape),
            _resident(qn_rows.shape),
            _rope_spec(tm, n_prompt, sample_seq),
            _rope_spec(tm, n_prompt, sample_seq),
        ],
        out_specs=[
            pl.BlockSpec((tm, heads * 2 * LANES), row),
            pl.BlockSpec((tm, kv_lora), row),
            pl.BlockSpec((tm, LANES), row),
        ],
        out_shape=[
            jax.ShapeDtypeStruct((t, heads * 2 * LANES), BF16),
            jax.ShapeDtypeStruct((t, kv_lora), F32),
            jax.ShapeDtypeStruct((t, LANES), F32),
        ],
        compiler_params=_cparams(("parallel",), 56),
        name="mla_in_proj",
    )(x, gain.reshape(1, d), mod, mod, wd_ext, q_a_norm.reshape(1, q_lora), kv_a_norm.reshape(1, kv_lora),
      wuq_pad, qn_rows, cos_tab, sin_tab)


def _kv_body(ckv_ref, kr_ref, w_ref, kn_ref, cos_ref, sin_ref, k_ref, v_ref, *, heads, qk_dim):
    kv = jnp.dot(ckv_ref[...].astype(BF16), w_ref[...], preferred_element_type=F32)
    tm = kv.shape[0]
    lane = lax.broadcasted_iota(I32, (tm, LANES), 1)
    kr = kr_ref[...]
    gn = kn_ref[0:1, :]
    gr = kn_ref[1:2, :]
    grs = kn_ref[2:3, :]
    ssr = jnp.sum(kr * kr, axis=-1, keepdims=True)
    rot = (kr * gr) * cos_ref[...] + (_swap_halves(kr, lane) * grs) * sin_ref[...]
    for hd in range(heads):
        kn = kv[:, hd * LANES:(hd + 1) * LANES]
        rinv = lax.rsqrt((jnp.sum(kn * kn, axis=-1, keepdims=True) + ssr) * (1.0 / qk_dim) + EPS)
        k_ref[:, 2 * hd * LANES:(2 * hd + 1) * LANES] = (kn * gn * rinv).astype(BF16)
        k_ref[:, (2 * hd + 1) * LANES:(2 * hd + 2) * LANES] = (rot * rinv).astype(BF16)
    v_ref[...] = kv[:, heads * LANES:].astype(BF16)


def _kv_expand(ckv, kr, w_ukv_re, kn_rows, cos_tab, sin_tab, rope_index, *, heads, qk_dim):
    rows, kv_lora = ckv.shape
    tm = TOKEN_TILE
    row = lambda i: (i, 0)
    return pl.pallas_call(
        functools.partial(_kv_body, heads=heads, qk_dim=qk_dim),
        grid=(rows // tm,),
        in_specs=[
            pl.BlockSpec((tm, kv_lora), row),
            pl.BlockSpec((tm, LANES), row),
            _resident(w_ukv_re.shape),
            _resident(kn_rows.shape),
            pl.BlockSpec((tm, LANES), rope_index),
            pl.BlockSpec((tm, LANES), rope_index),
        ],
        out_specs=[pl.BlockSpec((tm, heads * 2 * LANES), row), pl.BlockSpec((tm, heads * LANES), row)],
        out_shape=[
            jax.ShapeDtypeStruct((rows, heads * 2 * LANES), BF16),
            jax.ShapeDtypeStruct((rows, heads * LANES), BF16),
        ],
        compiler_params=_cparams(("parallel",), 40),
        name=f"mla_kv_expand_{rows}",
    )(ckv, kr, w_ukv_re, kn_rows, cos_tab, sin_tab)


def _attn_body(q_ref, k_ref, v_ref, o_ref, *, heads_per_step):
    for hd in range(heads_per_step):
        qk = slice(2 * hd * LANES, 2 * (hd + 1) * LANES)
        vo = slice(hd * LANES, (hd + 1) * LANES)
        s = lax.dot_general(q_ref[:, qk], k_ref[:, qk], (((1,), (1,)), ((), ())), preferred_element_type=F32)
        m = jnp.max(s, axis=-1, keepdims=True)
        p = jnp.exp(s - m)
        l = jnp.sum(p, axis=-1, keepdims=True)
        o = jnp.dot(p.astype(BF16), v_ref[:, vo], preferred_element_type=F32)
        o_ref[:, vo] = (o / l).astype(BF16)


def _attention(q, k, v, *, row0, batch, seq, kv_len, heads, tq, heads_per_step):
    nq = seq // tq
    qblk0 = row0 // tq
    hps = heads_per_step
    return pl.pallas_call(
        functools.partial(_attn_body, heads_per_step=hps),
        grid=(batch, heads // hps, nq),
        in_specs=[
            pl.BlockSpec((tq, hps * 2 * LANES), lambda b, h, i: (qblk0 + b * nq + i, h)),
            pl.BlockSpec((kv_len, hps * 2 * LANES), lambda b, h, i: (b, h)),
            pl.BlockSpec((kv_len, hps * LANES), lambda b, h, i: (b, h)),
        ],
        out_specs=pl.BlockSpec((tq, hps * LANES), lambda b, h, i: (b * nq + i, h)),
        out_shape=jax.ShapeDtypeStruct((batch * seq, heads * LANES), BF16),
        compiler_params=_cparams(("parallel", "parallel", "parallel"), 48),
        name=f"mla_attention_kv{kv_len}",
    )(q, k, v)


def _oproj_body(x_ref, gate_ref, op_ref, os_ref, w_ref, out_ref, *, prompt_tiles):
    o = _group_select(pl.program_id(0), prompt_tiles, op_ref, os_ref)
    y = jnp.dot(o, w_ref[...], preferred_element_type=F32)
    out_ref[...] = x_ref[...] + gate_ref[...] * y


def _mla_out_proj(x, mod, layer, o_p, o_s, w_o, n_prompt, sample_seq):
    t, d = x.shape
    tm = TOKEN_TILE
    row = lambda i: (i, 0)
    prompt_tiles = n_prompt // tm
    return pl.pallas_call(
        functools.partial(_oproj_body, prompt_tiles=prompt_tiles),
        grid=(t // tm,),
        in_specs=[
            pl.BlockSpec((tm, d), row),
            _mod_spec(layer, 2, tm, n_prompt, sample_seq, d),
            *_group_specs((tm, o_p.shape[1]), prompt_tiles),
            _resident(w_o.shape),
        ],
        out_specs=pl.BlockSpec((tm, d), row),
        out_shape=jax.ShapeDtypeStruct((t, d), F32),
        compiler_params=_cparams(("parallel",), 40),
        name="mla_out_proj",
    )(x, mod, o_p, o_s, w_o.astype(BF16))


def _round_up(v, m):
    return (v + m - 1) // m * m


def _local_rows(tile):
    return TOP_K * tile + RUN_PAD_ROWS


def _router_body(x_ref, g_ref, sh_ref, sc_ref, whi_ref, wlo_ref, rb_ref,
                 hb_ref, wt_ref, lp_ref, cnt_ref, *, n_experts):
    h = _modulated(x_ref[...], g_ref[...], sh_ref[...], sc_ref[...])
    tm, d = h.shape

    h_hi = h.astype(BF16)
    hb_ref[...] = h_hi
    h_lo = (h - h_hi.astype(F32)).astype(BF16)
    nt = (((1,), (1,)), ((), ()))
    logits = (lax.dot_general(whi_ref[...], h_hi, nt, preferred_element_type=F32)
              + lax.dot_general(whi_ref[...], h_lo, nt, preferred_element_type=F32)
              + lax.dot_general(wlo_ref[...], h_hi, nt, preferred_element_type=F32))
    scores = jax.nn.sigmoid(logits)
    biased = scores + rb_ref[...]
    per_group = n_experts // N_EXPERT_GROUPS
    assert per_group == 4 and TOP_K == 2
    b_rows = [biased[e:e + 1, :] for e in range(n_experts)]
    s_rows = [scores[e:e + 1, :] for e in range(n_experts)]

    best = None
    sel = jnp.zeros((1, tm), I32)
    for g in range(N_EXPERT_GROUPS):
        b0, b1, b2, b3 = b_rows[g * 4:(g + 1) * 4]
        m1, n1 = jnp.maximum(b0, b1), jnp.minimum(b0, b1)
        m2, n2 = jnp.maximum(b2, b3), jnp.minimum(b2, b3)
        top1 = jnp.maximum(m1, m2)
        top2 = jnp.maximum(jnp.minimum(m1, m2), jnp.maximum(n1, n2))
        gsum = top1 + top2
        if best is None:
            best = gsum
        else:
            better = gsum > best
            sel = jnp.where(better, g, sel)
            best = jnp.where(better, gsum, best)

    def pick(rows, j):
        out = rows[j]
        for g in range(1, N_EXPERT_GROUPS):
            out = jnp.where(sel == g, rows[g * 4 + j], out)
        return out

    cand_b = [pick(b_rows, j) for j in range(4)]
    cand_s = [pick(s_rows, j) for j in range(4)]

    def argmax4(vals):
        bv, bi = vals[0], jnp.zeros((1, tm), I32)
        for j in range(1, 4):
            gt = vals[j] > bv
            bi = jnp.where(gt, j, bi)
            bv = jnp.where(gt, vals[j], bv)
        return bi

    i1 = argmax4(cand_b)
    i2 = argmax4([jnp.where(i1 == j, -jnp.inf, cand_b[j]) for j in range(4)])

    def take(vals, idx):
        out = vals[0]
        for j in range(1, 4):
            out = jnp.where(idx == j, vals[j], out)
        return out

    s1 = take(cand_s, i1)
    s2 = take(cand_s, i2)
    tot = s1 + s2
    e1 = sel * 4 + i1
    e2 = sel * 4 + i2
    wt_ref[0:1, :] = s1 / tot
    wt_ref[1:2, :] = s2 / tot

    eid = lax.broadcasted_iota(I32, (n_experts, tm), 0)
    is1 = eid == e1
    is2 = eid == e2
    chosen = jnp.where(is1 | is2, 1.0, 0.0)
    before = (lax.broadcasted_iota(I32, (tm, tm), 0) < lax.broadcasted_iota(I32, (tm, tm), 1))
    rank = jnp.dot(chosen.astype(BF16), jnp.where(before, 1.0, 0.0).astype(BF16), preferred_element_type=F32)
    count = jnp.sum(chosen, axis=1, keepdims=True)
    padded = jnp.floor((count + (SUBLANES - 1)) * (1.0 / SUBLANES)) * SUBLANES
    lower = (lax.broadcasted_iota(I32, (n_experts, n_experts), 1)
             < lax.broadcasted_iota(I32, (n_experts, n_experts), 0))
    run_start = jnp.dot(jnp.where(lower, 1.0, 0.0).astype(BF16),
                        jnp.broadcast_to(padded, (n_experts, LANES)).astype(BF16),
                        preferred_element_type=F32)[:, 0:1]
    row = run_start + rank
    lp_ref[0:1, :] = jnp.sum(jnp.where(is1, row, 0.0), axis=0, keepdims=True).astype(I32)
    lp_ref[1:2, :] = jnp.sum(jnp.where(is2, row, 0.0), axis=0, keepdims=True).astype(I32)
    cnt_ref[...] = jnp.broadcast_to(count, cnt_ref.shape)


def _router(x, mod, layer, gain, w_hi, w_lo, r_bias, n_prompt, sample_seq):
    t, d = x.shape
    tm = TOKEN_TILE
    n_experts = w_hi.shape[0]
    ms = functools.partial(_mod_spec, layer, tile=tm, n_prompt=n_prompt, sample_seq=sample_seq, d=d)
    col = lambda i: (0, i)
    return pl.pallas_call(
        functools.partial(_router_body, n_experts=n_experts),
        grid=(t // tm,),
        in_specs=[
            pl.BlockSpec((tm, d), lambda i: (i, 0)),
            _resident((1, d)),
            ms(chunk=3),
            ms(chunk=4),
            _resident(w_hi.shape),
            _resident(w_lo.shape),
            _resident((n_experts, 1)),
        ],
        out_specs=[
            pl.BlockSpec((tm, d), lambda i: (i, 0)),
            pl.BlockSpec((TOP_K, tm), col),
            pl.BlockSpec((TOP_K, tm), col),
            pl.BlockSpec((None, n_experts, LANES), lambda i: (i, 0, 0)),
        ],
        out_shape=[
            jax.ShapeDtypeStruct((t, d), BF16),
            jax.ShapeDtypeStruct((TOP_K, t), F32),
            jax.ShapeDtypeStruct((TOP_K, t), I32),
            jax.ShapeDtypeStruct((t // tm, n_experts, LANES), F32),
        ],
        compiler_params=_cparams(("parallel",), 40),
        name="moe_router",
    )(x, gain.reshape(1, d), mod, mod, w_hi, w_lo, r_bias.reshape(n_experts, 1))


def _run_copies(tile, local_ref, slot, sorted_ref, row_ref, chunk_ref, sem, n_experts, to_sorted):
    def apply(op):
        local_row = 0
        for e in range(n_experts):
            n = chunk_ref[tile * n_experts + e]
            first = row_ref[tile * n_experts + e]

            def one(c, carry, first=first, local_row=local_row):
                loc = local_ref.at[slot, pl.ds(pl.multiple_of(local_row + c * SUBLANES, SUBLANES), SUBLANES)]
                far = sorted_ref.at[pl.ds(pl.multiple_of(first + c * SUBLANES, SUBLANES), SUBLANES)]
                copy = pltpu.make_async_copy(loc, far, sem) if to_sorted else pltpu.make_async_copy(far, loc, sem)
                getattr(copy, op)()
                return carry

            lax.fori_loop(0, n, one, 0)
            local_row = local_row + n * SUBLANES

    return apply


def _one_hot_rows(lp_ref, rows, tm):
    j = lax.broadcasted_iota(I32, (rows, tm), 0)
    hit = (j == lp_ref[0:1, :]) | (j == lp_ref[1:2, :])
    return jnp.where(hit, 1.0, 0.0).astype(BF16)


def _dispatch_body(row_ref, chunk_ref, last_ref, nv_ref, lp_ref, h_ref, o_ref, zero_ref, loc_ref, sems, zsem,
                   *, n_experts):
    tm = h_ref.shape[0]
    zt = zero_ref.shape[0]
    n_tiles = o_ref.shape[0] // zt
    i = pl.program_id(0)
    last = pl.num_programs(0) - 1
    slot = i % 2

    def copies(tile, s):
        return _run_copies(tile, loc_ref, s, o_ref, row_ref, chunk_ref, sems.at[s], n_experts, True)

    @pl.when(i == 0)
    def _():
        zero_ref[...] = jnp.zeros(zero_ref.shape, zero_ref.dtype)

        def zero_tile(row):
            return pltpu.make_async_copy(zero_ref, o_ref.at[pl.ds(pl.multiple_of(row, zt), zt)], zsem)

        for e in range(last_ref.shape[0]):
            @pl.when(last_ref[e] >= 0)
            def _():
                zero_tile(last_ref[e]).start()

        def start_tail(j, c):
            zero_tile(j * zt).start()
            return c

        lax.fori_loop(nv_ref[0], n_tiles, start_tail, 0)
        for e in range(last_ref.shape[0]):
            @pl.when(last_ref[e] >= 0)
            def _():
                zero_tile(last_ref[e]).wait()

        def wait_tail(j, c):
            zero_tile(j * zt).wait()
            return c

        lax.fori_loop(nv_ref[0], n_tiles, wait_tail, 0)

    @pl.when(i >= 2)
    def _():
        copies(i - 2, slot)("wait")

    loc_ref[slot] = jnp.dot(_one_hot_rows(lp_ref, loc_ref.shape[1], tm), h_ref[...], preferred_element_type=F32)
    copies(i, slot)("start")

    @pl.when(i == last)
    def _():
        @pl.when(i >= 1)
        def _():
            copies(i - 1, 1 - slot)("wait")
        copies(i, slot)("wait")


def _dispatch(hb, lp, run_row, run_chunks, last_tile_row, n_valid, sorted_rows, n_experts):
    t, d = hb.shape
    tm = TOKEN_TILE
    grid_spec = pltpu.PrefetchScalarGridSpec(
        num_scalar_prefetch=4,
        grid=(t // tm,),
        in_specs=[
            pl.BlockSpec((TOP_K, tm), lambda i, *_: (0, i)),
            pl.BlockSpec((tm, d), lambda i, *_: (i, 0)),
        ],
        out_specs=pl.BlockSpec(memory_space=pl.ANY),
        scratch_shapes=[pltpu.VMEM((EXPERT_TILE, d), F32), pltpu.VMEM((2, _local_rows(tm), d), F32),
                        pltpu.SemaphoreType.DMA((2,)), pltpu.SemaphoreType.DMA],
    )
    return pl.pallas_call(
        functools.partial(_dispatch_body, n_experts=n_experts),
        grid_spec=grid_spec,
        out_shape=jax.ShapeDtypeStruct((sorted_rows, d), F32),
        compiler_params=_cparams(("arbitrary",), 40),
        name="moe_dispatch",
    )(run_row, run_chunks, last_tile_row, n_valid, lp, hb)


def _expert_body(te_ref, nv_ref, x_ref, wg_ref, wu_ref, wd_ref, y_ref, wgb_ref, wub_ref, wdb_ref):
    i = pl.program_id(0)
    valid = i < nv_ref[0]
    new_expert = (i == 0) | (te_ref[i] != te_ref[jnp.maximum(i - 1, 0)])

    @pl.when(valid & new_expert)
    def _():
        wgb_ref[...] = wg_ref[...].astype(BF16)
        wub_ref[...] = wu_ref[...].astype(BF16)
        wdb_ref[...] = wd_ref[...].astype(BF16)

    @pl.when(valid)
    def _():
        xb = x_ref[...].astype(BF16)
        g = jnp.dot(xb, wgb_ref[...], preferred_element_type=F32)
        u = jnp.dot(xb, wub_ref[...], preferred_element_type=F32)
        act = (g * jax.nn.sigmoid(g) * u).astype(BF16)
        y_ref[...] = jnp.dot(act, wdb_ref[...], preferred_element_type=F32)

    @pl.when(i >= nv_ref[0])
    def _():
        y_ref[...] = jnp.zeros(y_ref.shape, F32)


def _experts(xs, tile_expert, n_valid, w_gate, w_up, w_down, layer):
    rows, d = xs.shape
    tm = EXPERT_TILE
    _, n_experts, _, f = w_gate.shape

    def xrow(i, te, nv):
        return (jnp.minimum(i, nv[0] - 1), 0)

    grid_spec = pltpu.PrefetchScalarGridSpec(
        num_scalar_prefetch=2,
        grid=(rows // tm,),
        in_specs=[
            pl.BlockSpec((tm, d), xrow),
            pl.BlockSpec((None, None, d, f), lambda i, te, nv: (layer, te[i], 0, 0)),
            pl.BlockSpec((None, None, d, f), lambda i, te, nv: (layer, te[i], 0, 0)),
            pl.BlockSpec((None, None, f, d), lambda i, te, nv: (layer, te[i], 0, 0)),
        ],
        out_specs=pl.BlockSpec((tm, d), lambda i, te, nv: (i, 0)),
        scratch_shapes=[pltpu.VMEM((d, f), BF16), pltpu.VMEM((d, f), BF16), pltpu.VMEM((f, d), BF16)],
    )
    return pl.pallas_call(
        _expert_body,
        grid_spec=grid_spec,
        out_shape=jax.ShapeDtypeStruct((rows, d), F32),
        compiler_params=_cparams(("arbitrary",), 56),
        name="moe_experts",
    )(tile_expert, n_valid, xs, w_gate, w_up, w_down)


def _combine_body(row_ref, chunk_ref, lp_ref, lpt_ref, wt_ref, x_ref, gate_ref, ys_ref, *rest,
                  n_experts, prompt_tiles):
    o_refs, (loc_ref, sems) = rest[:-2], rest[-2:]
    tm = x_ref.shape[0]
    rows = loc_ref.shape[1]
    i = pl.program_id(0)
    slot = i % 2

    def copies(tile, s):
        return _run_copies(tile, loc_ref, s, ys_ref, row_ref, chunk_ref, sems.at[s], n_experts, False)

    @pl.when(i == 0)
    def _():
        loc_ref[...] = jnp.zeros(loc_ref.shape, F32)
        copies(0, 0)("start")

    @pl.when(i + 1 < pl.num_programs(0))
    def _():
        copies(i + 1, 1 - slot)("start")

    copies(i, slot)("wait")

    j = lax.broadcasted_iota(I32, (rows, tm), 0)
    w_row = jnp.sum(jnp.where(j == lp_ref[0:1, :], wt_ref[0:1, :], 0.0)
                    + jnp.where(j == lp_ref[1:2, :], wt_ref[1:2, :], 0.0), axis=1, keepdims=True)
    scaled = (loc_ref[slot] * w_row).astype(BF16)
    jt = lax.broadcasted_iota(I32, (tm, rows), 1)
    mine = jnp.where((jt == lpt_ref[:, 0:1]) | (jt == lpt_ref[:, 1:2]), 1.0, 0.0).astype(BF16)
    out = x_ref[...] + gate_ref[...] * jnp.dot(mine, scaled, preferred_element_type=F32)
    if len(o_refs) == 1:
        o_refs[0][...] = out
    else:
        for o_ref, own in zip(o_refs, (i < prompt_tiles, i >= prompt_tiles)):
            @pl.when(own)
            def _():
                o_ref[...] = out


def _combine(x, mod, layer, ys, lp, wt, run_row, run_chunks, n_prompt, sample_seq, n_experts, split):
    t, d = x.shape
    tm = TOKEN_TILE
    prompt_tiles = n_prompt // tm
    tok = lambda i, *_: (i, 0)
    col = lambda i, *_: (0, i)
    if split:
        out_specs = _group_specs((tm, d), prompt_tiles)
        out_shape = [jax.ShapeDtypeStruct((n_prompt, d), F32), jax.ShapeDtypeStruct((t - n_prompt, d), F32)]
    else:
        out_specs = [pl.BlockSpec((tm, d), tok)]
        out_shape = [jax.ShapeDtypeStruct((t, d), F32)]
    grid_spec = pltpu.PrefetchScalarGridSpec(
        num_scalar_prefetch=2,
        grid=(t // tm,),
        in_specs=[
            pl.BlockSpec((TOP_K, tm), col),
            pl.BlockSpec((tm, TOP_K), tok),
            pl.BlockSpec((TOP_K, tm), col),
            pl.BlockSpec((tm, d), tok),
            _mod_spec(layer, 5, tm, n_prompt, sample_seq, d),
            pl.BlockSpec(memory_space=pl.ANY),
        ],
        out_specs=out_specs,
        scratch_shapes=[pltpu.VMEM((2, _local_rows(tm), d), F32), pltpu.SemaphoreType.DMA((2,))],
    )
    return pl.pallas_call(
        functools.partial(_combine_body, n_experts=n_experts, prompt_tiles=prompt_tiles),
        grid_spec=grid_spec,
        out_shape=out_shape,
        compiler_params=_cparams(("arbitrary",), 40),
        name="moe_combine",
    )(run_row, run_chunks, lp, lp.T, wt, x, mod, ys)


def _moe(x, mod, layer, gain, w_hi, w_lo, r_bias, w_gate, w_up, w_down, n_prompt, sample_seq, split):
    t, d = x.shape
    n_experts = w_hi.shape[0]
    assert n_experts * (SUBLANES - 1) <= RUN_PAD_ROWS
    hb, wt, lp, cnt = _router(x, mod, layer, gain, w_hi, w_lo, r_bias, n_prompt, sample_seq)
    n_tok_tiles = t // TOKEN_TILE
    run_len = _round_up(cnt[:, :, 0].astype(I32), SUBLANES)
    run_before = jnp.cumsum(run_len, axis=0) - run_len
    tiles = (jnp.sum(run_len, axis=0) + EXPERT_TILE - 1) // EXPERT_TILE
    tile_end = jnp.cumsum(tiles)
    seg_start = (tile_end - tiles) * EXPERT_TILE
    run_row = (seg_start[None, :] + run_before).reshape(-1).astype(I32)
    run_chunks = (run_len // SUBLANES).reshape(-1).astype(I32)
    sorted_rows = (_round_up(TOP_K * t + n_tok_tiles * n_experts * (SUBLANES - 1), EXPERT_TILE)
                   + n_experts * EXPERT_TILE)
    n_tiles = sorted_rows // EXPERT_TILE
    n_valid = tile_end[-1:]
    tile_ids = jnp.arange(n_tiles, dtype=I32)
    ends_before = lambda i: jnp.sum((tile_end[None, :] <= i[:, None]).astype(I32), axis=1)
    tile_expert = jnp.where(tile_ids < n_valid[0], ends_before(tile_ids), ends_before(n_valid - 1))
    tile_expert = jnp.minimum(tile_expert, n_experts - 1).astype(I32)
    last_tile_row = jnp.where(tiles > 0, (tile_end - 1) * EXPERT_TILE, -1).astype(I32)
    n_valid = n_valid.astype(I32)
    xs = _dispatch(hb, lp, run_row, run_chunks, last_tile_row, n_valid, sorted_rows, n_experts)
    ys = _experts(xs, tile_expert, n_valid, w_gate, w_up, w_down, layer)
    return _combine(x, mod, layer, ys, lp, wt, run_row, run_chunks, n_prompt, sample_seq, n_experts, split)


def _rope_tables(sample_seq, rope_dim, tile):
    n_freq = rope_dim // 4
    pos = jnp.arange(sample_seq)
    row_pos = (pos // GRID_W).astype(F32)
    col_pos = (pos % GRID_W).astype(F32)
    inv_freq = ROPE_BASE ** (-jnp.arange(n_freq, dtype=F32) / n_freq)
    ar = row_pos[:, None] * inv_freq
    ac = col_pos[:, None] * inv_freq
    zeros = jnp.zeros((sample_seq, LANES - rope_dim), F32)
    cos = jnp.concatenate([jnp.cos(ar), jnp.cos(ar), jnp.cos(ac), jnp.cos(ac), zeros], axis=1)
    sin = jnp.concatenate([-jnp.sin(ar), jnp.sin(ar), -jnp.sin(ac), jnp.sin(ac), zeros], axis=1)
    ident_c = jnp.concatenate([jnp.ones((tile, rope_dim), F32), jnp.zeros((tile, LANES - rope_dim), F32)], axis=1)
    return (jnp.concatenate([ident_c, cos], axis=0),
            jnp.concatenate([jnp.zeros((tile, LANES), F32), sin], axis=0))


def _norm_rows(gain, nope, rope_dim):
    quarter = rope_dim // 4
    gr = gain[nope:]
    grs = jnp.concatenate([gr[quarter:2 * quarter], gr[:quarter], gr[3 * quarter:], gr[2 * quarter:3 * quarter]])
    zpad = jnp.zeros((LANES - rope_dim,), F32)
    return jnp.stack([gain[:nope], jnp.concatenate([gr, zpad]), jnp.concatenate([grs, zpad])])


def kernel(x_prompt, x_sample, state_lru_fwd, state_lru_bwd, cache_mla_ckv, cache_mla_krope, c, c_ctx,
           ada_w, ada_b, norm_mix, norm_ffn, mix0_w_in, mix0_w_out, lru_conv_w, lru_conv_b,
           lru_w_r, lru_b_r, lru_w_i, lru_b_i, lru_lambda, gmlp_v_norm, gmlp_w_s, gmlp_b_s,
           mla_w_down, mla_q_a_norm, mla_kv_a_norm, mla_w_uq, mla_w_ukv, mla_q_norm, mla_k_norm, mla_w_o,
           router_w, router_bias, moe_w_gate, moe_w_up, moe_w_down):
    batch, seq, d = x_prompt.shape
    dec_batch, dec_seq, _ = x_sample.shape
    depth = ada_w.shape[0]
    n_prompt = batch * seq
    n_sample = dec_batch * dec_seq
    assert n_prompt % dec_seq == 0 and seq % TOKEN_TILE == 0 and dec_seq % TOKEN_TILE == 0
    assert 1 + dec_batch <= SUBLANES

    x_parts = (x_prompt.reshape(n_prompt, d), x_sample.reshape(n_sample, d))

    cond = jnp.concatenate([c_ctx[None, :], c, jnp.zeros((SUBLANES - 1 - dec_batch, d), F32)], axis=0)
    mod = _modulation(cond, ada_w, ada_b).reshape(depth, SUBLANES, 6, 1, d)

    n_experts = router_w.shape[1]
    rw_t = router_w.T
    rw_hi = rw_t.astype(BF16)
    rw_lo = (rw_t - rw_hi.astype(F32)).astype(BF16)

    fwd_states, bwd_states, ckv_caches, krope_caches = [], [], [], []
    for layer in range(depth):
        j = layer // 2
        if layer % 2 == 0:
            width = lru_conv_w.shape[2]
            heads = lru_w_r.shape[2]
            xb, gg, gu, gv = _even_in_proj(x_parts, mod, layer, norm_mix[layer], mix0_w_in[j], n_prompt, dec_seq)
            wcat = jnp.concatenate([lru_w_r[j, 0], lru_w_i[j, 0], lru_w_r[j, 1], lru_w_i[j, 1]], axis=-1).astype(BF16)
            hb = lambda v: v.reshape(heads, 1, width // heads)
            bcat = jnp.concatenate([hb(lru_b_r[j, 0]), hb(lru_b_i[j, 0]), hb(lru_b_r[j, 1]), hb(lru_b_i[j, 1])], axis=-1)
            lru_args = (lru_conv_w[j], lru_conv_b[j].reshape(1, width), wcat, bcat, lru_lambda[j])
            zero_state = jnp.zeros((batch, 1, width), F32)
            nseq_p = LRU_PROMPT_SEQS_PER_STEP if batch % LRU_PROMPT_SEQS_PER_STEP == 0 else 1
            ya_p, fin_f, fin_b = _lru_mixer(xb, gg, zero_state, zero_state, *lru_args,
                                            row0=0, batch=batch, seq=seq, nseq=nseq_p)
            ya_s, _, _ = _lru_mixer(xb, gg, state_lru_fwd[:, j][:, None, :], state_lru_bwd[:, j][:, None, :],
                                    *lru_args, row0=n_prompt, batch=dec_batch, seq=dec_seq, nseq=1)
            fwd_states.append(fin_f[:, 0, :])
            bwd_states.append(fin_b[:, 0, :])
            groups, chunk, _ = gmlp_w_s[j].shape
            gd = width // groups
            b_full = jnp.repeat(gmlp_b_s[j].T, gd, axis=1)
            x = _even_out_proj(x_parts, mod, layer, ya_p, ya_s, gu, gv, gmlp_v_norm[j], gmlp_w_s[j], b_full,
                               mix0_w_out[j], n_prompt, dec_seq)
        else:
            q_lora = mla_q_a_norm.shape[1]
            kv_lora = mla_kv_a_norm.shape[1]
            qk_dim = mla_q_norm.shape[1]
            rope_dim = cache_mla_krope.shape[-1]
            nope = qk_dim - rope_dim
            heads = mla_w_uq.shape[2] // qk_dim
            v_dim = mla_w_ukv.shape[2] // heads - nope
            past = cache_mla_ckv.shape[2]
            assert nope == LANES and v_dim == LANES and rope_dim == 4 * ROPE_QUARTER
            wd = mla_w_down[j]
            wd_ext = jnp.concatenate([wd, jnp.zeros((d, LANES - rope_dim), F32)], axis=1).astype(BF16)
            wuq = mla_w_uq[j].reshape(q_lora, heads, qk_dim)
            wuq_pad = jnp.concatenate([wuq, jnp.zeros((q_lora, heads, 2 * LANES - qk_dim), F32)], axis=-1)
            wuq_pad = wuq_pad.reshape(q_lora, heads * 2 * LANES).astype(BF16)
            wukv = mla_w_ukv[j].reshape(kv_lora, heads, nope + v_dim)
            w_ukv_re = jnp.concatenate([wukv[:, :, :nope].reshape(kv_lora, heads * nope),
                                        wukv[:, :, nope:].reshape(kv_lora, heads * v_dim)], axis=1).astype(BF16)
            qn_rows = _norm_rows(mla_q_norm[j], nope, rope_dim)
            kn_rows = _norm_rows(mla_k_norm[j], nope, rope_dim)
            cos_tab, sin_tab = _rope_tables(dec_seq, rope_dim, MLA_TOKEN_TILE)
            x = x_parts[0] if len(x_parts) == 1 else jnp.concatenate(x_parts, axis=0)
            q, ckv, kr = _mla_in_proj(x, mod, layer, norm_mix[layer], wd_ext, mla_q_a_norm[j], mla_kv_a_norm[j],
                                      wuq_pad, qn_rows, cos_tab, sin_tab, n_prompt, dec_seq,
                                      heads=heads, q_lora=q_lora, kv_lora=kv_lora, qk_dim=qk_dim)
            ckv_caches.append(ckv[:n_prompt].reshape(batch, seq, kv_lora))
            krope_caches.append(kr[:n_prompt, :rope_dim].reshape(batch, seq, rope_dim))
            expand = functools.partial(_kv_expand, w_ukv_re=w_ukv_re, kn_rows=kn_rows, cos_tab=cos_tab,
                                       sin_tab=sin_tab, heads=heads, qk_dim=qk_dim)
            k_p, v_p = expand(ckv[:n_prompt], kr[:n_prompt], rope_index=lambda i: (0, 0))
            kv_len = past + dec_seq
            ctx_kr = jnp.pad(cache_mla_krope[:, j], ((0, 0), (0, 0), (0, LANES - rope_dim)))
            ckv_s = jnp.concatenate([cache_mla_ckv[:, j], ckv[n_prompt:].reshape(dec_batch, dec_seq, kv_lora)], axis=1)
            kr_s = jnp.concatenate([ctx_kr, kr[n_prompt:].reshape(dec_batch, dec_seq, LANES)], axis=1)
            tiles_per_req = kv_len // TOKEN_TILE
            ctx_tiles = past // TOKEN_TILE
            assert past % TOKEN_TILE == 0

            def latent_rope(i):
                r = i % tiles_per_req
                return (jnp.where(r < ctx_tiles, 0, MLA_TOKEN_TILE // TOKEN_TILE + r - ctx_tiles), 0)

            k_s, v_s = expand(ckv_s.reshape(dec_batch * kv_len, kv_lora), kr_s.reshape(dec_batch * kv_len, LANES),
                              rope_index=latent_rope)
            o_p = _attention(q, k_p, v_p, row0=0, batch=batch, seq=seq, kv_len=seq, heads=heads, tq=seq,
                             heads_per_step=heads)
            o_s = _attention(q, k_s, v_s, row0=n_prompt, batch=dec_batch, seq=dec_seq, kv_len=kv_len,
                             heads=heads, tq=ATTN_Q_TILE, heads_per_step=ATTN_LATENT_HEADS_PER_STEP)
            x = _mla_out_proj(x, mod, layer, o_p, o_s, mla_w_o[j], n_prompt, dec_seq)
        x_parts = _moe(x, mod, layer, norm_ffn[layer], rw_hi, rw_lo, router_bias, moe_w_gate, moe_w_up, moe_w_down,
                       n_prompt, dec_seq, split=layer == depth - 1)

    xp = x_parts[0].reshape(batch, seq, d)
    xs = x_parts[1].reshape(dec_batch, dec_seq, d)
    return (xp, xs,
            jnp.stack(fwd_states, axis=1), jnp.stack(bwd_states, axis=1),
            jnp.stack(ckv_caches, axis=1), jnp.stack(krope_caches, axis=1))
```

```python
import functools

import jax
import jax.numpy as jnp
from jax import lax
from jax.experimental import pallas as pl
from jax.experimental.pallas import tpu as pltpu

F32 = jnp.float32
BF16 = jnp.bfloat16
I32 = jnp.int32

EPS = 1e-6
LRU_C = 8.0
GRID_W = 64
ROPE_BASE = 10000.0
ROPE_QUARTER = 16
N_EXPERT_GROUPS = 4
TOP_K = 2

LANES = 128
SUBLANES = 8
VMEM_BYTES_V7X = 64 * 1024 * 1024

TOKEN_TILE = 256
MLA_TOKEN_TILE = 512
EXPERT_TILE = 256
MOD_COL_TILE = 1024
ATTN_Q_TILE = 512
ATTN_LATENT_HEADS_PER_STEP = 4
OUT_PROJ_TILE = 512
LRU_GATE_CHUNK = 256
RUN_PAD_ROWS = 128
LRU_PROMPT_SEQS_PER_STEP = 8


def _cparams(semantics, vmem_mb):
    return pltpu.CompilerParams(dimension_semantics=semantics, vmem_limit_bytes=vmem_mb * 1024 * 1024)


def _resident(shape):
    nd = len(shape)
    return pl.BlockSpec(shape, lambda *_: (0,) * nd, pipeline_mode=pl.Buffered(1))


def _mod_spec(layer, chunk, tile, n_prompt, sample_seq, d):
    def index(i, *_):
        t = i * tile
        row = jnp.where(t < n_prompt, 0, 1 + (t - n_prompt) // sample_seq)
        return (layer, row, chunk, 0, 0)

    return pl.BlockSpec((None, None, None, 1, d), index)


def _modulated(x, gain, shift, scale):
    y = x * lax.rsqrt(jnp.mean(x * x, axis=-1, keepdims=True) + EPS)
    return (y * gain) * (1.0 + scale) + shift


def _mod_body(c_ref, w_ref, b_ref, o_ref):
    c = c_ref[...]
    s = (c * jax.nn.sigmoid(c)).astype(BF16)
    o_ref[...] = jnp.dot(s, w_ref[...].astype(BF16), preferred_element_type=F32) + b_ref[...]


def _modulation(cond, ada_w, ada_b):
    depth, d, n = ada_w.shape
    tn = MOD_COL_TILE
    return pl.pallas_call(
        _mod_body,
        grid=(depth, n // tn),
        in_specs=[
            pl.BlockSpec((SUBLANES, d), lambda l, j: (0, 0)),
            pl.BlockSpec((None, d, tn), lambda l, j: (l, 0, j)),
            pl.BlockSpec((None, 1, tn), lambda l, j: (l, 0, j)),
        ],
        out_specs=pl.BlockSpec((None, SUBLANES, tn), lambda l, j: (l, 0, j)),
        out_shape=jax.ShapeDtypeStruct((depth, SUBLANES, n), F32),
        compiler_params=_cparams(("parallel", "parallel"), 40),
        name="adaln_projection",
    )(cond, ada_w, ada_b.reshape(depth, 1, n))


def _group_select(i, prompt_tiles, p_ref, s_ref):
    return jnp.where(i < prompt_tiles, p_ref[...], s_ref[...])


def _group_specs(block, prompt_tiles):
    return [pl.BlockSpec(block, lambda i, *_: (jnp.minimum(i, prompt_tiles - 1), 0)),
            pl.BlockSpec(block, lambda i, *_: (jnp.maximum(i - prompt_tiles, 0), 0))]


def _token_specs(x_parts, tile, prompt_tiles):
    d = x_parts[0].shape[1]
    if len(x_parts) == 1:
        return [pl.BlockSpec((tile, d), lambda i, *_: (i, 0))]
    return _group_specs((tile, d), prompt_tiles)


def _token_tile(i, prompt_tiles, x_refs):
    return x_refs[0][...] if len(x_refs) == 1 else _group_select(i, prompt_tiles, *x_refs)


def _in0_body(*refs, n_x, prompt_tiles):
    x_refs = refs[:n_x]
    g_ref, sh_ref, sc_ref, w_ref, xb_ref, gg_ref, gu_ref, gv_ref = refs[n_x:]
    x = _token_tile(pl.program_id(0), prompt_tiles, x_refs)
    h = _modulated(x, g_ref[...], sh_ref[...], sc_ref[...])
    z = jnp.dot(h.astype(BF16), w_ref[...], preferred_element_type=F32)
    w = xb_ref.shape[1]
    xb_ref[...] = z[:, :w]
    gg_ref[...] = jax.nn.gelu(z[:, w:2 * w]).astype(BF16)
    gu_ref[...] = jax.nn.gelu(z[:, 2 * w:3 * w]).astype(BF16)
    gv_ref[...] = jax.nn.gelu(z[:, 3 * w:]).astype(BF16)


def _even_in_proj(x_parts, mod, layer, gain, w_in, n_prompt, sample_seq):
    t = sum(p.shape[0] for p in x_parts)
    d = x_parts[0].shape[1]
    tm = TOKEN_TILE
    w4 = w_in.shape[1]
    w = w4 // 4
    row = lambda i: (i, 0)
    ms = functools.partial(_mod_spec, layer, tile=tm, n_prompt=n_prompt, sample_seq=sample_seq, d=d)
    return pl.pallas_call(
        functools.partial(_in0_body, n_x=len(x_parts), prompt_tiles=n_prompt // tm),
        grid=(t // tm,),
        in_specs=[
            *_token_specs(x_parts, tm, n_prompt // tm),
            _resident((1, d)),
            ms(chunk=0),
            ms(chunk=1),
            _resident((d, w4)),
        ],
        out_specs=[pl.BlockSpec((tm, w), row)] * 4,
        out_shape=[
            jax.ShapeDtypeStruct((t, w), F32),
            jax.ShapeDtypeStruct((t, w), BF16),
            jax.ShapeDtypeStruct((t, w), BF16),
            jax.ShapeDtypeStruct((t, w), BF16),
        ],
        compiler_params=_cparams(("parallel",), 48),
        name="even_in_proj",
    )(*x_parts, gain.reshape(1, d), mod, mod, w_in.astype(BF16))


def _tile_scan(a, b, row, reverse):
    for d in (1, 2, 4):
        if reverse:
            keep = row < SUBLANES - d
            a_s = jnp.where(keep, pltpu.roll(a, SUBLANES - d, 0), 1.0)
            b_s = jnp.where(keep, pltpu.roll(b, SUBLANES - d, 0), 0.0)
        else:
            keep = row >= d
            a_s = jnp.where(keep, pltpu.roll(a, d, 0), 1.0)
            b_s = jnp.where(keep, pltpu.roll(b, d, 0), 0.0)
        b = b + a * b_s
        a = a * a_s
    return a, b


def _sigmoid(x):
    return 0.5 * jnp.tanh(0.5 * x) + 0.5


def _lru_body(xb_ref, gg_ref, h0f_ref, h0b_ref, cw_ref, cb_ref, w_ref, bias_ref, lam_ref,
              ya_ref, ff_ref, fb_ref,
              xp_ref, af_ref, bf_ref, ab_ref, bb_ref, hf_ref, hb_ref, *, seq, chunk, nseq):
    hw = LANES
    pad = SUBLANES
    pitch = seq + 2 * pad
    for s in range(nseq):
        xp_ref[pl.ds(s * pitch, pad), :] = jnp.zeros((pad, hw), F32)
        xp_ref[pl.ds(s * pitch + pad + seq, pad), :] = jnp.zeros((pad, hw), F32)
        xp_ref[pl.ds(s * pitch + pad, seq), :] = xb_ref[pl.ds(s * seq, seq), :]

    cw = cw_ref[...]
    cb = cb_ref[...]
    lam = lam_ref[...]
    neg = -lam
    softplus = jnp.maximum(neg, 0.0) + jnp.log1p(jnp.exp(-jnp.abs(neg)))
    nsp = -LRU_C * softplus
    w = w_ref[...]
    bias = bias_ref[...]

    for s in range(nseq):
        for c in range(seq // chunk):
            src = s * pitch + pad + c * chunk
            dst = s * seq + c * chunk
            xc = (cw[0:1] * xp_ref[pl.ds(src - 2, chunk), :]
                  + cw[1:2] * xp_ref[pl.ds(src - 1, chunk), :]
                  + cw[2:3] * xp_ref[pl.ds(src, chunk), :]
                  + cw[3:4] * xp_ref[pl.ds(src + 1, chunk), :]) + cb
            g = jnp.dot(xc.astype(BF16), w, preferred_element_type=F32) + bias
            for direction, (a_ref, b_ref) in enumerate(((af_ref, bf_ref), (ab_ref, bb_ref))):
                r = _sigmoid(g[:, (2 * direction) * hw:(2 * direction + 1) * hw])
                gi = _sigmoid(g[:, (2 * direction + 1) * hw:(2 * direction + 2) * hw])
                log_a = r * nsp[direction:direction + 1]
                a = jnp.exp(log_a)
                a_ref[pl.ds(dst, chunk), :] = a
                b_ref[pl.ds(dst, chunk), :] = jnp.sqrt(1.0 - a * a) * gi * xc

    n_tiles = seq // SUBLANES
    row = lax.broadcasted_iota(I32, (SUBLANES, hw), 0)

    def step(i, carry):
        new = []
        for s in range(nseq):
            hf, hb = carry[2 * s], carry[2 * s + 1]
            rf = pl.multiple_of(s * seq + i * SUBLANES, SUBLANES)
            rb = pl.multiple_of(s * seq + (n_tiles - 1 - i) * SUBLANES, SUBLANES)
            a, b = _tile_scan(af_ref[pl.ds(rf, SUBLANES), :], bf_ref[pl.ds(rf, SUBLANES), :], row, False)
            h = b + a * hf
            hf_ref[pl.ds(rf, SUBLANES), :] = h
            new.append(jnp.broadcast_to(h[SUBLANES - 1:SUBLANES, :], (SUBLANES, hw)))
            a, b = _tile_scan(ab_ref[pl.ds(rb, SUBLANES), :], bb_ref[pl.ds(rb, SUBLANES), :], row, True)
            h = b + a * hb
            hb_ref[pl.ds(rb, SUBLANES), :] = h
            new.append(jnp.broadcast_to(h[0:1, :], (SUBLANES, hw)))
        return tuple(new)

    init = []
    for s in range(nseq):
        init.append(jnp.broadcast_to(h0f_ref[s], (SUBLANES, hw)))
        init.append(jnp.broadcast_to(h0b_ref[s], (SUBLANES, hw)))
    final = lax.fori_loop(0, n_tiles, step, tuple(init), unroll=2 if nseq == 1 else 1)
    for s in range(nseq):
        ff_ref[s] = final[2 * s][0:1, :]
        fb_ref[s] = final[2 * s + 1][0:1, :]
    ya_ref[...] = ((hf_ref[...] + hb_ref[...]) * gg_ref[...].astype(F32)).astype(BF16)


def _lru_mixer(xb, gg, h0f, h0b, conv_w, conv_b, wcat, bcat, lam, *, row0, batch, seq, nseq):
    _, width = xb.shape
    heads = width // LANES
    rows = nseq * seq
    assert batch % nseq == 0 and row0 % rows == 0
    blk0 = row0 // rows
    tok = lambda b, h: (blk0 + b, h)
    state = lambda b, h: (b, 0, h)
    seq_buf = pltpu.VMEM((rows, LANES), F32)
    return pl.pallas_call(
        functools.partial(_lru_body, seq=seq, chunk=min(seq, LRU_GATE_CHUNK), nseq=nseq),
        grid=(batch // nseq, heads),
        in_specs=[
            pl.BlockSpec((rows, LANES), tok),
            pl.BlockSpec((rows, LANES), tok),
            pl.BlockSpec((nseq, 1, LANES), state),
            pl.BlockSpec((nseq, 1, LANES), state),
            pl.BlockSpec((conv_w.shape[0], LANES), lambda b, h: (0, h)),
            pl.BlockSpec((1, LANES), lambda b, h: (0, h)),
            pl.BlockSpec((None, LANES, 4 * LANES), lambda b, h: (h, 0, 0)),
            pl.BlockSpec((None, 1, 4 * LANES), lambda b, h: (h, 0, 0)),
            pl.BlockSpec((2, LANES), lambda b, h: (0, h)),
        ],
        out_specs=[
            pl.BlockSpec((rows, LANES), lambda b, h: (b, h)),
            pl.BlockSpec((nseq, 1, LANES), state),
            pl.BlockSpec((nseq, 1, LANES), state),
        ],
        out_shape=[
            jax.ShapeDtypeStruct((batch * seq, width), BF16),
            jax.ShapeDtypeStruct((batch, 1, width), F32),
            jax.ShapeDtypeStruct((batch, 1, width), F32),
        ],
        scratch_shapes=[pltpu.VMEM((nseq * (seq + 2 * SUBLANES), LANES), F32)] + [seq_buf] * 6,
        compiler_params=_cparams(("parallel", "parallel"), 40),
        name=f"rglru_seq{seq}",
    )(xb, gg, h0f, h0b, conv_w, conv_b, wcat, bcat, lam)


def _out0_body(*refs, n_x, chunk, prompt_tiles):
    x_refs = refs[:n_x]
    (gate_ref, yap_ref, yas_ref, gu_ref, gv_ref, vg_ref, ws_ref, bs_ref, wa_ref, wb_ref,
     o_ref, yb_ref) = refs[n_x:]
    tm = o_ref.shape[0]
    x = _token_tile(pl.program_id(0), prompt_tiles, x_refs)
    ya = _group_select(pl.program_id(0), prompt_tiles, yap_ref, yas_ref)
    v = gv_ref[...].astype(F32)
    vn = (v * lax.rsqrt(jnp.mean(v * v, axis=-1, keepdims=True) + EPS) * vg_ref[...]).astype(BF16)
    groups = ws_ref.shape[0]
    gd = vn.shape[1] // groups
    for c in range(tm // chunk):
        rows = slice(c * chunk, (c + 1) * chunk)
        for g in range(groups):
            cols = slice(g * gd, (g + 1) * gd)
            mixed = jnp.dot(ws_ref[g], vn[rows, cols], preferred_element_type=F32) + bs_ref[:, cols]
            yb_ref[rows, cols] = (gu_ref[rows, cols].astype(F32) * mixed).astype(BF16)
    y = (jnp.dot(ya, wa_ref[...], preferred_element_type=F32)
         + jnp.dot(yb_ref[...], wb_ref[...], preferred_element_type=F32))
    o_ref[...] = x + gate_ref[...] * y


def _even_out_proj(x_parts, mod, layer, ya_p, ya_s, gu, gv, v_gain, w_s, b_full, w_out, n_prompt, sample_seq):
    t = sum(p.shape[0] for p in x_parts)
    d = x_parts[0].shape[1]
    tm = OUT_PROJ_TILE
    w = ya_p.shape[1]
    row = lambda i: (i, 0)
    chunk = w_s.shape[1]
    prompt_tiles = n_prompt // tm
    return pl.pallas_call(
        functools.partial(_out0_body, n_x=len(x_parts), chunk=chunk, prompt_tiles=prompt_tiles),
        grid=(t // tm,),
        in_specs=[
            *_token_specs(x_parts, tm, prompt_tiles),
            _mod_spec(layer, 2, tm, n_prompt, sample_seq, d),
            *_group_specs((tm, w), prompt_tiles),
            pl.BlockSpec((tm, w), row),
            pl.BlockSpec((tm, w), row),
            _resident((1, w)),
            _resident(w_s.shape),
            _resident(b_full.shape),
            _resident((w, d)),
            _resident((w, d)),
        ],
        out_specs=pl.BlockSpec((tm, d), row),
        out_shape=jax.ShapeDtypeStruct((t, d), F32),
        scratch_shapes=[pltpu.VMEM((tm, w), BF16)],
        compiler_params=_cparams(("parallel",), 56),
        name="even_out_proj",
    )(*x_parts, mod, ya_p, ya_s, gu, gv, v_gain.reshape(1, w), w_s.astype(BF16), b_full,
      w_out[:w].astype(BF16), w_out[w:].astype(BF16))


def _swap_halves(x, lane):
    quarter = ROPE_QUARTER
    up = pltpu.roll(x, LANES - quarter, 1)
    down = pltpu.roll(x, quarter, 1)
    return jnp.where((lane % (2 * quarter)) < quarter, up, down)


def _mla_in_body(x_ref, g_ref, sh_ref, sc_ref, wd_ref, qan_ref, kvan_ref, wuq_ref, qn_ref, cos_ref, sin_ref,
                 q_ref, ckv_ref, kr_ref, *, heads, q_lora, kv_lora, qk_dim, sm_scale, sub):
    gn = qn_ref[0:1, :]
    gr = qn_ref[1:2, :]
    grs = qn_ref[2:3, :]
    lane = lax.broadcasted_iota(I32, (sub, LANES), 1)
    for r0 in range(0, x_ref.shape[0], sub):
        rows = pl.ds(r0, sub)
        h = _modulated(x_ref[rows, :], g_ref[...], sh_ref[...], sc_ref[...])
        z = jnp.dot(h.astype(BF16), wd_ref[...], preferred_element_type=F32)
        cq = z[:, :q_lora]
        cq = cq * lax.rsqrt(jnp.mean(cq * cq, axis=-1, keepdims=True) + EPS) * qan_ref[...]
        ckv = z[:, q_lora:q_lora + kv_lora]
        ckv_ref[rows, :] = ckv * lax.rsqrt(jnp.mean(ckv * ckv, axis=-1, keepdims=True) + EPS) * kvan_ref[...]
        kr_ref[rows, :] = z[:, q_lora + kv_lora:]
        q = jnp.dot(cq.astype(BF16), wuq_ref[...], preferred_element_type=F32)
        cos = cos_ref[rows, :]
        sin = sin_ref[rows, :]
        for hd in range(heads):
            qn = q[:, 2 * hd * LANES:(2 * hd + 1) * LANES]
            qr = q[:, (2 * hd + 1) * LANES:(2 * hd + 2) * LANES]
            ss = jnp.sum(qn * qn, axis=-1, keepdims=True) + jnp.sum(qr * qr, axis=-1, keepdims=True)
            rinv = lax.rsqrt(ss * (1.0 / qk_dim) + EPS) * sm_scale
            q_ref[rows, 2 * hd * LANES:(2 * hd + 1) * LANES] = (qn * gn * rinv).astype(BF16)
            rot = (qr * gr) * cos + (_swap_halves(qr, lane) * grs) * sin
            q_ref[rows, (2 * hd + 1) * LANES:(2 * hd + 2) * LANES] = (rot * rinv).astype(BF16)


def _rope_spec(tile, n_prompt, sample_seq):
    def index(i):
        t = i * tile
        return (jnp.where(t < n_prompt, 0, 1 + ((t - n_prompt) % sample_seq) // tile), 0)

    return pl.BlockSpec((tile, LANES), index)


def _mla_in_proj(x, mod, layer, gain, wd_ext, q_a_norm, kv_a_norm, wuq_pad, qn_rows, cos_tab, sin_tab,
                 n_prompt, sample_seq, *, heads, q_lora, kv_lora, qk_dim):
    t, d = x.shape
    tm = MLA_TOKEN_TILE
    row = lambda i: (i, 0)
    ms = functools.partial(_mod_spec, layer, tile=tm, n_prompt=n_prompt, sample_seq=sample_seq, d=d)
    body = functools.partial(_mla_in_body, heads=heads, q_lora=q_lora, kv_lora=kv_lora, qk_dim=qk_dim,
                             sm_scale=float(qk_dim) ** -0.5, sub=TOKEN_TILE)
    return pl.pallas_call(
        body,
        grid=(t // tm,),
        in_specs=[
            pl.BlockSpec((tm, d), row),
            _resident((1, d)),
            ms(chunk=0),
            ms(chunk=1),
            _resident(wd_ext.shape),
            _resident((1, q_lora)),
            _resident((1, kv_lora)),
            _resident(wuq_pad.shape),
            _resident(qn_rows.shape),
            _rope_spec(tm, n_prompt, sample_seq),
            _rope_spec(tm, n_prompt, sample_seq),
        ],
        out_specs=[
            pl.BlockSpec((tm, heads * 2 * LANES), row),
            pl.BlockSpec((tm, kv_lora), row),
            pl.BlockSpec((tm, LANES), row),
        ],
        out_shape=[
            jax.ShapeDtypeStruct((t, heads * 2 * LANES), BF16),
            jax.ShapeDtypeStruct((t, kv_lora), F32),
            jax.ShapeDtypeStruct((t, LANES), F32),
        ],
        compiler_params=_cparams(("parallel",), 56),
        name="mla_in_proj",
    )(x, gain.reshape(1, d), mod, mod, wd_ext, q_a_norm.reshape(1, q_lora), kv_a_norm.reshape(1, kv_lora),
      wuq_pad, qn_rows, cos_tab, sin_tab)


def _kv_body(ckv_ref, kr_ref, w_ref, kn_ref, cos_ref, sin_ref, k_ref, v_ref, *, heads, qk_dim):
    kv = jnp.dot(ckv_ref[...].astype(BF16), w_ref[...], preferred_element_type=F32)
    tm = kv.shape[0]
    lane = lax.broadcasted_iota(I32, (tm, LANES), 1)
    kr = kr_ref[...]
    gn = kn_ref[0:1, :]
    gr = kn_ref[1:2, :]
    grs = kn_ref[2:3, :]
    ssr = jnp.sum(kr * kr, axis=-1, keepdims=True)
    rot = (kr * gr) * cos_ref[...] + (_swap_halves(kr, lane) * grs) * sin_ref[...]
    for hd in range(heads):
        kn = kv[:, hd * LANES:(hd + 1) * LANES]
        rinv = lax.rsqrt((jnp.sum(kn * kn, axis=-1, keepdims=True) + ssr) * (1.0 / qk_dim) + EPS)
        k_ref[:, 2 * hd * LANES:(2 * hd + 1) * LANES] = (kn * gn * rinv).astype(BF16)
        k_ref[:, (2 * hd + 1) * LANES:(2 * hd + 2) * LANES] = (rot * rinv).astype(BF16)
    v_ref[...] = kv[:, heads * LANES:].astype(BF16)


def _kv_expand(ckv, kr, w_ukv_re, kn_rows, cos_tab, sin_tab, rope_index, *, heads, qk_dim):
    rows, kv_lora = ckv.shape
    tm = TOKEN_TILE
    row = lambda i: (i, 0)
    return pl.pallas_call(
        functools.partial(_kv_body, heads=heads, qk_dim=qk_dim),
        grid=(rows // tm,),
        in_specs=[
            pl.BlockSpec((tm, kv_lora), row),
            pl.BlockSpec((tm, LANES), row),
            _resident(w_ukv_re.shape),
            _resident(kn_rows.shape),
            pl.BlockSpec((tm, LANES), rope_index),
            pl.BlockSpec((tm, LANES), rope_index),
        ],
        out_specs=[pl.BlockSpec((tm, heads * 2 * LANES), row), pl.BlockSpec((tm, heads * LANES), row)],
        out_shape=[
            jax.ShapeDtypeStruct((rows, heads * 2 * LANES), BF16),
            jax.ShapeDtypeStruct((rows, heads * LANES), BF16),
        ],
        compiler_params=_cparams(("parallel",), 40),
        name=f"mla_kv_expand_{rows}",
    )(ckv, kr, w_ukv_re, kn_rows, cos_tab, sin_tab)


def _attn_body(q_ref, k_ref, v_ref, o_ref, *, heads_per_step):
    for hd in range(heads_per_step):
        qk = slice(2 * hd * LANES, 2 * (hd + 1) * LANES)
        vo = slice(hd * LANES, (hd + 1) * LANES)
        s = lax.dot_general(q_ref[:, qk], k_ref[:, qk], (((1,), (1,)), ((), ())), preferred_element_type=F32)
        m = jnp.max(s, axis=-1, keepdims=True)
        p = jnp.exp(s - m)
        l = jnp.sum(p, axis=-1, keepdims=True)
        o = jnp.dot(p.astype(BF16), v_ref[:, vo], preferred_element_type=F32)
        o_ref[:, vo] = (o / l).astype(BF16)


def _attention(q, k, v, *, row0, batch, seq, kv_len, heads, tq, heads_per_step):
    nq = seq // tq
    qblk0 = row0 // tq
    hps = heads_per_step
    return pl.pallas_call(
        functools.partial(_attn_body, heads_per_step=hps),
        grid=(batch, heads // hps, nq),
        in_specs=[
            pl.BlockSpec((tq, hps * 2 * LANES), lambda b, h, i: (qblk0 + b * nq + i, h)),
            pl.BlockSpec((kv_len, hps * 2 * LANES), lambda b, h, i: (b, h)),
            pl.BlockSpec((kv_len, hps * LANES), lambda b, h, i: (b, h)),
        ],
        out_specs=pl.BlockSpec((tq, hps * LANES), lambda b, h, i: (b * nq + i, h)),
        out_shape=jax.ShapeDtypeStruct((batch * seq, heads * LANES), BF16),
        compiler_params=_cparams(("parallel", "parallel", "parallel"), 48),
        name=f"mla_attention_kv{kv_len}",
    )(q, k, v)


def _oproj_body(x_ref, gate_ref, op_ref, os_ref, w_ref, out_ref, *, prompt_tiles):
    o = _group_select(pl.program_id(0), prompt_tiles, op_ref, os_ref)
    y = jnp.dot(o, w_ref[...], preferred_element_type=F32)
    out_ref[...] = x_ref[...] + gate_ref[...] * y


def _mla_out_proj(x, mod, layer, o_p, o_s, w_o, n_prompt, sample_seq):
    t, d = x.shape
    tm = OUT_PROJ_TILE
    row = lambda i: (i, 0)
    prompt_tiles = n_prompt // tm
    return pl.pallas_call(
        functools.partial(_oproj_body, prompt_tiles=prompt_tiles),
        grid=(t // tm,),
        in_specs=[
            pl.BlockSpec((tm, d), row),
            _mod_spec(layer, 2, tm, n_prompt, sample_seq, d),
            *_group_specs((tm, o_p.shape[1]), prompt_tiles),
            _resident(w_o.shape),
        ],
        out_specs=pl.BlockSpec((tm, d), row),
        out_shape=jax.ShapeDtypeStruct((t, d), F32),
        compiler_params=_cparams(("parallel",), 48),
        name="mla_out_proj",
    )(x, mod, o_p, o_s, w_o.astype(BF16))


def _round_up(v, m):
    return (v + m - 1) // m * m


def _local_rows(tile):
    return TOP_K * tile + RUN_PAD_ROWS


def _router_body(x_ref, g_ref, sh_ref, sc_ref, whi_ref, wlo_ref, rb_ref,
                 hb_ref, wt_ref, lp_ref, cnt_ref, *, n_experts):
    h = _modulated(x_ref[...], g_ref[...], sh_ref[...], sc_ref[...])
    tm, d = h.shape

    h_hi = h.astype(BF16)
    hb_ref[...] = h_hi
    h_lo = (h - h_hi.astype(F32)).astype(BF16)
    nt = (((1,), (1,)), ((), ()))
    logits = (lax.dot_general(whi_ref[...], h_hi, nt, preferred_element_type=F32)
              + lax.dot_general(whi_ref[...], h_lo, nt, preferred_element_type=F32)
              + lax.dot_general(wlo_ref[...], h_hi, nt, preferred_element_type=F32))
    scores = jax.nn.sigmoid(logits)
    biased = scores + rb_ref[...]
    per_group = n_experts // N_EXPERT_GROUPS
    assert per_group == 4 and TOP_K == 2
    b_rows = [biased[e:e + 1, :] for e in range(n_experts)]
    s_rows = [scores[e:e + 1, :] for e in range(n_experts)]

    best = None
    sel = jnp.zeros((1, tm), I32)
    for g in range(N_EXPERT_GROUPS):
        b0, b1, b2, b3 = b_rows[g * 4:(g + 1) * 4]
        m1, n1 = jnp.maximum(b0, b1), jnp.minimum(b0, b1)
        m2, n2 = jnp.maximum(b2, b3), jnp.minimum(b2, b3)
        top1 = jnp.maximum(m1, m2)
        top2 = jnp.maximum(jnp.minimum(m1, m2), jnp.maximum(n1, n2))
        gsum = top1 + top2
        if best is None:
            best = gsum
        else:
            better = gsum > best
            sel = jnp.where(better, g, sel)
            best = jnp.where(better, gsum, best)

    def pick(rows, j):
        out = rows[j]
        for g in range(1, N_EXPERT_GROUPS):
            out = jnp.where(sel == g, rows[g * 4 + j], out)
        return out

    cand_b = [pick(b_rows, j) for j in range(4)]
    cand_s = [pick(s_rows, j) for j in range(4)]

    def argmax4(vals):
        bv, bi = vals[0], jnp.zeros((1, tm), I32)
        for j in range(1, 4):
            gt = vals[j] > bv
            bi = jnp.where(gt, j, bi)
            bv = jnp.where(gt, vals[j], bv)
        return bi

    i1 = argmax4(cand_b)
    i2 = argmax4([jnp.where(i1 == j, -jnp.inf, cand_b[j]) for j in range(4)])

    def take(vals, idx):
        out = vals[0]
        for j in range(1, 4):
            out = jnp.where(idx == j, vals[j], out)
        return out

    s1 = take(cand_s, i1)
    s2 = take(cand_s, i2)
    tot = s1 + s2
    e1 = sel * 4 + i1
    e2 = sel * 4 + i2
    wt_ref[0:1, :] = s1 / tot
    wt_ref[1:2, :] = s2 / tot

    eid = lax.broadcasted_iota(I32, (n_experts, tm), 0)
    is1 = eid == e1
    is2 = eid == e2
    chosen = jnp.where(is1 | is2, 1.0, 0.0)
    before = (lax.broadcasted_iota(I32, (tm, tm), 0) < lax.broadcasted_iota(I32, (tm, tm), 1))
    rank = jnp.dot(chosen.astype(BF16), jnp.where(before, 1.0, 0.0).astype(BF16), preferred_element_type=F32)
    count = jnp.sum(chosen, axis=1, keepdims=True)
    padded = jnp.floor((count + (SUBLANES - 1)) * (1.0 / SUBLANES)) * SUBLANES
    lower = (lax.broadcasted_iota(I32, (n_experts, n_experts), 1)
             < lax.broadcasted_iota(I32, (n_experts, n_experts), 0))
    run_start = jnp.dot(jnp.where(lower, 1.0, 0.0).astype(BF16),
                        jnp.broadcast_to(padded, (n_experts, LANES)).astype(BF16),
                        preferred_element_type=F32)[:, 0:1]
    row = run_start + rank
    lp_ref[0:1, :] = jnp.sum(jnp.where(is1, row, 0.0), axis=0, keepdims=True).astype(I32)
    lp_ref[1:2, :] = jnp.sum(jnp.where(is2, row, 0.0), axis=0, keepdims=True).astype(I32)
    cnt_ref[...] = jnp.broadcast_to(count, cnt_ref.shape)


def _router(x, mod, layer, gain, w_hi, w_lo, r_bias, n_prompt, sample_seq):
    t, d = x.shape
    tm = TOKEN_TILE
    n_experts = w_hi.shape[0]
    ms = functools.partial(_mod_spec, layer, tile=tm, n_prompt=n_prompt, sample_seq=sample_seq, d=d)
    col = lambda i: (0, i)
    return pl.pallas_call(
        functools.partial(_router_body, n_experts=n_experts),
        grid=(t // tm,),
        in_specs=[
            pl.BlockSpec((tm, d), lambda i: (i, 0)),
            _resident((1, d)),
            ms(chunk=3),
            ms(chunk=4),
            _resident(w_hi.shape),
            _resident(w_lo.shape),
            _resident((n_experts, 1)),
        ],
        out_specs=[
            pl.BlockSpec((tm, d), lambda i: (i, 0)),
            pl.BlockSpec((TOP_K, tm), col),
            pl.BlockSpec((TOP_K, tm), col),
            pl.BlockSpec((None, n_experts, LANES), lambda i: (i, 0, 0)),
        ],
        out_shape=[
            jax.ShapeDtypeStruct((t, d), BF16),
            jax.ShapeDtypeStruct((TOP_K, t), F32),
            jax.ShapeDtypeStruct((TOP_K, t), I32),
            jax.ShapeDtypeStruct((t // tm, n_experts, LANES), F32),
        ],
        compiler_params=_cparams(("parallel",), 40),
        name="moe_router",
    )(x, gain.reshape(1, d), mod, mod, w_hi, w_lo, r_bias.reshape(n_experts, 1))


def _run_copies(tile, local_ref, slot, sorted_ref, row_ref, count_ref, sem, to_sorted):
    max_chunks = local_ref.shape[1] // SUBLANES
    min_chunks = TOP_K * TOKEN_TILE // SUBLANES

    def directed(loc, far):
        return pltpu.make_async_copy(loc, far, sem) if to_sorted else pltpu.make_async_copy(far, loc, sem)

    def chunk(c):
        loc = local_ref.at[slot, pl.ds(pl.multiple_of(c * SUBLANES, SUBLANES), SUBLANES)]
        first = row_ref[tile * max_chunks + c]
        return directed(loc, sorted_ref.at[pl.ds(pl.multiple_of(first, SUBLANES), SUBLANES)])

    def apply(op):
        def one(c, carry):
            getattr(chunk(c), op)()
            return carry

        if op == "start":
            lax.fori_loop(0, min_chunks, one, 0, unroll=SUBLANES)
        else:
            rows = min_chunks * SUBLANES
            directed(local_ref.at[slot, pl.ds(0, rows)], sorted_ref.at[pl.ds(0, rows)]).wait()
        lax.fori_loop(min_chunks, count_ref[tile], one, 0)

    return apply


def _one_hot_rows(lp_ref, rows, tm):
    j = lax.broadcasted_iota(I32, (rows, tm), 0)
    hit = (j == lp_ref[0:1, :]) | (j == lp_ref[1:2, :])
    return jnp.where(hit, 1.0, 0.0).astype(BF16)


def _dispatch_body(row_ref, chunk_ref, last_ref, nv_ref, lp_ref, h_ref, o_ref, zero_ref, loc_ref, sems, zsem):
    tm = h_ref.shape[0]
    zt = zero_ref.shape[0]
    n_tiles = o_ref.shape[0] // zt
    i = pl.program_id(0)
    last = pl.num_programs(0) - 1
    slot = i % 2

    def copies(tile, s):
        return _run_copies(tile, loc_ref, s, o_ref, row_ref, chunk_ref, sems.at[s], True)

    @pl.when(i == 0)
    def _():
        zero_ref[...] = jnp.zeros(zero_ref.shape, zero_ref.dtype)

        def zero_tile(row):
            return pltpu.make_async_copy(zero_ref, o_ref.at[pl.ds(pl.multiple_of(row, zt), zt)], zsem)

        for e in range(last_ref.shape[0]):
            @pl.when(last_ref[e] >= 0)
            def _():
                zero_tile(last_ref[e]).start()

        def start_tail(j, c):
            zero_tile(j * zt).start()
            return c

        lax.fori_loop(nv_ref[0], n_tiles, start_tail, 0)
        for e in range(last_ref.shape[0]):
            @pl.when(last_ref[e] >= 0)
            def _():
                zero_tile(last_ref[e]).wait()

        def wait_tail(j, c):
            zero_tile(j * zt).wait()
            return c

        lax.fori_loop(nv_ref[0], n_tiles, wait_tail, 0)

    @pl.when(i >= 2)
    def _():
        copies(i - 2, slot)("wait")

    loc_ref[slot] = jnp.dot(_one_hot_rows(lp_ref, loc_ref.shape[1], tm), h_ref[...], preferred_element_type=F32)
    copies(i, slot)("start")

    @pl.when(i == last)
    def _():
        @pl.when(i >= 1)
        def _():
            copies(i - 1, 1 - slot)("wait")
        copies(i, slot)("wait")


def _dispatch(hb, lp, chunk_row, tile_chunks, last_tile_row, n_valid, sorted_rows):
    t, d = hb.shape
    tm = TOKEN_TILE
    grid_spec = pltpu.PrefetchScalarGridSpec(
        num_scalar_prefetch=4,
        grid=(t // tm,),
        in_specs=[
            pl.BlockSpec((TOP_K, tm), lambda i, *_: (0, i)),
            pl.BlockSpec((tm, d), lambda i, *_: (i, 0)),
        ],
        out_specs=pl.BlockSpec(memory_space=pl.ANY),
        scratch_shapes=[pltpu.VMEM((EXPERT_TILE, d), F32), pltpu.VMEM((2, _local_rows(tm), d), F32),
                        pltpu.SemaphoreType.DMA((2,)), pltpu.SemaphoreType.DMA],
    )
    return pl.pallas_call(
        _dispatch_body,
        grid_spec=grid_spec,
        out_shape=jax.ShapeDtypeStruct((sorted_rows, d), F32),
        compiler_params=_cparams(("arbitrary",), 40),
        name="moe_dispatch",
    )(chunk_row, tile_chunks, last_tile_row, n_valid, lp, hb)


def _expert_body(te_ref, nv_ref, x_ref, wg_ref, wu_ref, wd_ref, y_ref, wgb_ref, wub_ref, wdb_ref):
    i = pl.program_id(0)
    valid = i < nv_ref[0]
    new_expert = (i == 0) | (te_ref[i] != te_ref[jnp.maximum(i - 1, 0)])

    @pl.when(valid & new_expert)
    def _():
        wgb_ref[...] = wg_ref[...].astype(BF16)
        wub_ref[...] = wu_ref[...].astype(BF16)
        wdb_ref[...] = wd_ref[...].astype(BF16)

    @pl.when(valid)
    def _():
        xb = x_ref[...].astype(BF16)
        g = jnp.dot(xb, wgb_ref[...], preferred_element_type=F32)
        u = jnp.dot(xb, wub_ref[...], preferred_element_type=F32)
        act = (g * jax.nn.sigmoid(g) * u).astype(BF16)
        y_ref[...] = jnp.dot(act, wdb_ref[...], preferred_element_type=F32)

    @pl.when(i >= nv_ref[0])
    def _():
        y_ref[...] = jnp.zeros(y_ref.shape, F32)


def _experts(xs, tile_expert, n_valid, w_gate, w_up, w_down, layer):
    rows, d = xs.shape
    tm = EXPERT_TILE
    _, n_experts, _, f = w_gate.shape

    def xrow(i, te, nv):
        return (jnp.minimum(i, nv[0] - 1), 0)

    grid_spec = pltpu.PrefetchScalarGridSpec(
        num_scalar_prefetch=2,
        grid=(rows // tm,),
        in_specs=[
            pl.BlockSpec((tm, d), xrow),
            pl.BlockSpec((None, None, d, f), lambda i, te, nv: (layer, te[i], 0, 0)),
            pl.BlockSpec((None, None, d, f), lambda i, te, nv: (layer, te[i], 0, 0)),
            pl.BlockSpec((None, None, f, d), lambda i, te, nv: (layer, te[i], 0, 0)),
        ],
        out_specs=pl.BlockSpec((tm, d), lambda i, te, nv: (i, 0)),
        scratch_shapes=[pltpu.VMEM((d, f), BF16), pltpu.VMEM((d, f), BF16), pltpu.VMEM((f, d), BF16)],
    )
    return pl.pallas_call(
        _expert_body,
        grid_spec=grid_spec,
        out_shape=jax.ShapeDtypeStruct((rows, d), F32),
        compiler_params=_cparams(("arbitrary",), 56),
        name="moe_experts",
    )(tile_expert, n_valid, xs, w_gate, w_up, w_down)


def _combine_body(row_ref, chunk_ref, lp_ref, lpt_ref, wt_ref, x_ref, gate_ref, ys_ref, *rest, prompt_tiles):
    o_refs, (loc_ref, sems) = rest[:-2], rest[-2:]
    tm = x_ref.shape[0]
    rows = loc_ref.shape[1]
    i = pl.program_id(0)
    slot = i % 2

    def copies(tile, s):
        return _run_copies(tile, loc_ref, s, ys_ref, row_ref, chunk_ref, sems.at[s], False)

    @pl.when(i == 0)
    def _():
        loc_ref[...] = jnp.zeros(loc_ref.shape, F32)
        copies(0, 0)("start")

    @pl.when(i + 1 < pl.num_programs(0))
    def _():
        copies(i + 1, 1 - slot)("start")

    copies(i, slot)("wait")

    j = lax.broadcasted_iota(I32, (rows, tm), 0)
    w_row = jnp.sum(jnp.where(j == lp_ref[0:1, :], wt_ref[0:1, :], 0.0)
                    + jnp.where(j == lp_ref[1:2, :], wt_ref[1:2, :], 0.0), axis=1, keepdims=True)
    scaled = (loc_ref[slot] * w_row).astype(BF16)
    jt = lax.broadcasted_iota(I32, (tm, rows), 1)
    mine = jnp.where((jt == lpt_ref[:, 0:1]) | (jt == lpt_ref[:, 1:2]), 1.0, 0.0).astype(BF16)
    out = x_ref[...] + gate_ref[...] * jnp.dot(mine, scaled, preferred_element_type=F32)
    if len(o_refs) == 1:
        o_refs[0][...] = out
    else:
        for o_ref, own in zip(o_refs, (i < prompt_tiles, i >= prompt_tiles)):
            @pl.when(own)
            def _():
                o_ref[...] = out


def _combine(x, mod, layer, ys, lp, wt, chunk_row, tile_chunks, n_prompt, sample_seq, split):
    t, d = x.shape
    tm = TOKEN_TILE
    prompt_tiles = n_prompt // tm
    tok = lambda i, *_: (i, 0)
    col = lambda i, *_: (0, i)
    if split:
        out_specs = _group_specs((tm, d), prompt_tiles)
        out_shape = [jax.ShapeDtypeStruct((n_prompt, d), F32), jax.ShapeDtypeStruct((t - n_prompt, d), F32)]
    else:
        out_specs = [pl.BlockSpec((tm, d), tok)]
        out_shape = [jax.ShapeDtypeStruct((t, d), F32)]
    grid_spec = pltpu.PrefetchScalarGridSpec(
        num_scalar_prefetch=2,
        grid=(t // tm,),
        in_specs=[
            pl.BlockSpec((TOP_K, tm), col),
            pl.BlockSpec((tm, TOP_K), tok),
            pl.BlockSpec((TOP_K, tm), col),
            pl.BlockSpec((tm, d), tok),
            _mod_spec(layer, 5, tm, n_prompt, sample_seq, d),
            pl.BlockSpec(memory_space=pl.ANY),
        ],
        out_specs=out_specs,
        scratch_shapes=[pltpu.VMEM((2, _local_rows(tm), d), F32), pltpu.SemaphoreType.DMA((2,))],
    )
    return pl.pallas_call(
        functools.partial(_combine_body, prompt_tiles=prompt_tiles),
        grid_spec=grid_spec,
        out_shape=out_shape,
        compiler_params=_cparams(("arbitrary",), 40),
        name="moe_combine",
    )(chunk_row, tile_chunks, lp, lp.T, wt, x, mod, ys)


def _moe(x, mod, layer, gain, w_hi, w_lo, r_bias, w_gate, w_up, w_down, n_prompt, sample_seq, split):
    t, d = x.shape
    n_experts = w_hi.shape[0]
    assert n_experts * (SUBLANES - 1) <= RUN_PAD_ROWS
    hb, wt, lp, cnt = _router(x, mod, layer, gain, w_hi, w_lo, r_bias, n_prompt, sample_seq)
    n_tok_tiles = t // TOKEN_TILE
    run_len = _round_up(cnt[:, :, 0].astype(I32), SUBLANES)
    run_before = jnp.cumsum(run_len, axis=0) - run_len
    tiles = (jnp.sum(run_len, axis=0) + EXPERT_TILE - 1) // EXPERT_TILE
    tile_end = jnp.cumsum(tiles)
    seg_start = (tile_end - tiles) * EXPERT_TILE
    run_row = seg_start[None, :] + run_before
    run_chunks = run_len // SUBLANES
    chunk_end = jnp.cumsum(run_chunks, axis=1)
    chunk_ids = jnp.arange(_local_rows(TOKEN_TILE) // SUBLANES, dtype=I32)
    owner = jnp.sum((chunk_end[:, None, :] <= chunk_ids[None, :, None]).astype(I32), axis=2)
    owner = jnp.minimum(owner, n_experts - 1)
    is_owner = owner[:, :, None] == jnp.arange(n_experts, dtype=I32)
    pick = lambda a: jnp.sum(jnp.where(is_owner, a[:, None, :], 0), axis=2)
    chunk_row = pick(run_row) + (chunk_ids[None, :] - pick(chunk_end - run_chunks)) * SUBLANES
    chunk_row = chunk_row.reshape(-1).astype(I32)
    tile_chunks = chunk_end[:, -1].astype(I32)
    sorted_rows = (_round_up(TOP_K * t + n_tok_tiles * n_experts * (SUBLANES - 1), EXPERT_TILE)
                   + n_experts * EXPERT_TILE)
    n_tiles = sorted_rows // EXPERT_TILE
    n_valid = tile_end[-1:]
    tile_ids = jnp.arange(n_tiles, dtype=I32)
    ends_before = lambda i: jnp.sum((tile_end[None, :] <= i[:, None]).astype(I32), axis=1)
    tile_expert = jnp.where(tile_ids < n_valid[0], ends_before(tile_ids), ends_before(n_valid - 1))
    tile_expert = jnp.minimum(tile_expert, n_experts - 1).astype(I32)
    last_tile_row = jnp.where(tiles > 0, (tile_end - 1) * EXPERT_TILE, -1).astype(I32)
    n_valid = n_valid.astype(I32)
    xs = _dispatch(hb, lp, chunk_row, tile_chunks, last_tile_row, n_valid, sorted_rows)
    ys = _experts(xs, tile_expert, n_valid, w_gate, w_up, w_down, layer)
    return _combine(x, mod, layer, ys, lp, wt, chunk_row, tile_chunks, n_prompt, sample_seq, split)


def _rope_tables(sample_seq, rope_dim, tile):
    n_freq = rope_dim // 4
    pos = jnp.arange(sample_seq)
    row_pos = (pos // GRID_W).astype(F32)
    col_pos = (pos % GRID_W).astype(F32)
    inv_freq = ROPE_BASE ** (-jnp.arange(n_freq, dtype=F32) / n_freq)
    ar = row_pos[:, None] * inv_freq
    ac = col_pos[:, None] * inv_freq
    zeros = jnp.zeros((sample_seq, LANES - rope_dim), F32)
    cos = jnp.concatenate([jnp.cos(ar), jnp.cos(ar), jnp.cos(ac), jnp.cos(ac), zeros], axis=1)
    sin = jnp.concatenate([-jnp.sin(ar), jnp.sin(ar), -jnp.sin(ac), jnp.sin(ac), zeros], axis=1)
    ident_c = jnp.concatenate([jnp.ones((tile, rope_dim), F32), jnp.zeros((tile, LANES - rope_dim), F32)], axis=1)
    return (jnp.concatenate([ident_c, cos], axis=0),
            jnp.concatenate([jnp.zeros((tile, LANES), F32), sin], axis=0))


def _norm_rows(gain, nope, rope_dim):
    quarter = rope_dim // 4
    gr = gain[nope:]
    grs = jnp.concatenate([gr[quarter:2 * quarter], gr[:quarter], gr[3 * quarter:], gr[2 * quarter:3 * quarter]])
    zpad = jnp.zeros((LANES - rope_dim,), F32)
    return jnp.stack([gain[:nope], jnp.concatenate([gr, zpad]), jnp.concatenate([grs, zpad])])


def kernel(x_prompt, x_sample, state_lru_fwd, state_lru_bwd, cache_mla_ckv, cache_mla_krope, c, c_ctx,
           ada_w, ada_b, norm_mix, norm_ffn, mix0_w_in, mix0_w_out, lru_conv_w, lru_conv_b,
           lru_w_r, lru_b_r, lru_w_i, lru_b_i, lru_lambda, gmlp_v_norm, gmlp_w_s, gmlp_b_s,
           mla_w_down, mla_q_a_norm, mla_kv_a_norm, mla_w_uq, mla_w_ukv, mla_q_norm, mla_k_norm, mla_w_o,
           router_w, router_bias, moe_w_gate, moe_w_up, moe_w_down):
    batch, seq, d = x_prompt.shape
    dec_batch, dec_seq, _ = x_sample.shape
    depth = ada_w.shape[0]
    n_prompt = batch * seq
    n_sample = dec_batch * dec_seq
    assert n_prompt % dec_seq == 0 and seq % TOKEN_TILE == 0 and dec_seq % TOKEN_TILE == 0
    assert 1 + dec_batch <= SUBLANES

    x_parts = (x_prompt.reshape(n_prompt, d), x_sample.reshape(n_sample, d))

    cond = jnp.concatenate([c_ctx[None, :], c, jnp.zeros((SUBLANES - 1 - dec_batch, d), F32)], axis=0)
    mod = _modulation(cond, ada_w, ada_b).reshape(depth, SUBLANES, 6, 1, d)

    n_experts = router_w.shape[1]
    rw_t = router_w.T
    rw_hi = rw_t.astype(BF16)
    rw_lo = (rw_t - rw_hi.astype(F32)).astype(BF16)

    fwd_states, bwd_states, ckv_caches, krope_caches = [], [], [], []
    for layer in range(depth):
        j = layer // 2
        if layer % 2 == 0:
            width = lru_conv_w.shape[2]
            heads = lru_w_r.shape[2]
            xb, gg, gu, gv = _even_in_proj(x_parts, mod, layer, norm_mix[layer], mix0_w_in[j], n_prompt, dec_seq)
            wcat = jnp.concatenate([lru_w_r[j, 0], lru_w_i[j, 0], lru_w_r[j, 1], lru_w_i[j, 1]], axis=-1).astype(BF16)
            hb = lambda v: v.reshape(heads, 1, width // heads)
            bcat = jnp.concatenate([hb(lru_b_r[j, 0]), hb(lru_b_i[j, 0]), hb(lru_b_r[j, 1]), hb(lru_b_i[j, 1])], axis=-1)
            lru_args = (lru_conv_w[j], lru_conv_b[j].reshape(1, width), wcat, bcat, lru_lambda[j])
            zero_state = jnp.zeros((batch, 1, width), F32)
            nseq_p = LRU_PROMPT_SEQS_PER_STEP if batch % LRU_PROMPT_SEQS_PER_STEP == 0 else 1
            ya_p, fin_f, fin_b = _lru_mixer(xb, gg, zero_state, zero_state, *lru_args,
                                            row0=0, batch=batch, seq=seq, nseq=nseq_p)
            ya_s, _, _ = _lru_mixer(xb, gg, state_lru_fwd[:, j][:, None, :], state_lru_bwd[:, j][:, None, :],
                                    *lru_args, row0=n_prompt, batch=dec_batch, seq=dec_seq, nseq=1)
            fwd_states.append(fin_f[:, 0, :])
            bwd_states.append(fin_b[:, 0, :])
            groups, chunk, _ = gmlp_w_s[j].shape
            gd = width // groups
            b_full = jnp.repeat(gmlp_b_s[j].T, gd, axis=1)
            x = _even_out_proj(x_parts, mod, layer, ya_p, ya_s, gu, gv, gmlp_v_norm[j], gmlp_w_s[j], b_full,
                               mix0_w_out[j], n_prompt, dec_seq)
        else:
            q_lora = mla_q_a_norm.shape[1]
            kv_lora = mla_kv_a_norm.shape[1]
            qk_dim = mla_q_norm.shape[1]
            rope_dim = cache_mla_krope.shape[-1]
            nope = qk_dim - rope_dim
            heads = mla_w_uq.shape[2] // qk_dim
            v_dim = mla_w_ukv.shape[2] // heads - nope
            past = cache_mla_ckv.shape[2]
            assert nope == LANES and v_dim == LANES and rope_dim == 4 * ROPE_QUARTER
            wd = mla_w_down[j]
            wd_ext = jnp.concatenate([wd, jnp.zeros((d, LANES - rope_dim), F32)], axis=1).astype(BF16)
            wuq = mla_w_uq[j].reshape(q_lora, heads, qk_dim)
            wuq_pad = jnp.concatenate([wuq, jnp.zeros((q_lora, heads, 2 * LANES - qk_dim), F32)], axis=-1)
            wuq_pad = wuq_pad.reshape(q_lora, heads * 2 * LANES).astype(BF16)
            wukv = mla_w_ukv[j].reshape(kv_lora, heads, nope + v_dim)
            w_ukv_re = jnp.concatenate([wukv[:, :, :nope].reshape(kv_lora, heads * nope),
                                        wukv[:, :, nope:].reshape(kv_lora, heads * v_dim)], axis=1).astype(BF16)
            qn_rows = _norm_rows(mla_q_norm[j], nope, rope_dim)
            kn_rows = _norm_rows(mla_k_norm[j], nope, rope_dim)
            cos_tab, sin_tab = _rope_tables(dec_seq, rope_dim, MLA_TOKEN_TILE)
            x = x_parts[0] if len(x_parts) == 1 else jnp.concatenate(x_parts, axis=0)
            q, ckv, kr = _mla_in_proj(x, mod, layer, norm_mix[layer], wd_ext, mla_q_a_norm[j], mla_kv_a_norm[j],
                                      wuq_pad, qn_rows, cos_tab, sin_tab, n_prompt, dec_seq,
                                      heads=heads, q_lora=q_lora, kv_lora=kv_lora, qk_dim=qk_dim)
            ckv_caches.append(ckv[:n_prompt].reshape(batch, seq, kv_lora))
            krope_caches.append(kr[:n_prompt, :rope_dim].reshape(batch, seq, rope_dim))
            expand = functools.partial(_kv_expand, w_ukv_re=w_ukv_re, kn_rows=kn_rows, cos_tab=cos_tab,
                                       sin_tab=sin_tab, heads=heads, qk_dim=qk_dim)
            k_p, v_p = expand(ckv[:n_prompt], kr[:n_prompt], rope_index=lambda i: (0, 0))
            kv_len = past + dec_seq
            ctx_kr = jnp.pad(cache_mla_krope[:, j], ((0, 0), (0, 0), (0, LANES - rope_dim)))
            ckv_s = jnp.concatenate([cache_mla_ckv[:, j], ckv[n_prompt:].reshape(dec_batch, dec_seq, kv_lora)], axis=1)
            kr_s = jnp.concatenate([ctx_kr, kr[n_prompt:].reshape(dec_batch, dec_seq, LANES)], axis=1)
            tiles_per_req = kv_len // TOKEN_TILE
            ctx_tiles = past // TOKEN_TILE
            assert past % TOKEN_TILE == 0

            def latent_rope(i):
                r = i % tiles_per_req
                return (jnp.where(r < ctx_tiles, 0, MLA_TOKEN_TILE // TOKEN_TILE + r - ctx_tiles), 0)

            k_s, v_s = expand(ckv_s.reshape(dec_batch * kv_len, kv_lora), kr_s.reshape(dec_batch * kv_len, LANES),
                              rope_index=latent_rope)
            o_p = _attention(q, k_p, v_p, row0=0, batch=batch, seq=seq, kv_len=seq, heads=heads, tq=seq,
                             heads_per_step=heads)
            o_s = _attention(q, k_s, v_s, row0=n_prompt, batch=dec_batch, seq=dec_seq, kv_len=kv_len,
                             heads=heads, tq=ATTN_Q_TILE, heads_per_step=ATTN_LATENT_HEADS_PER_STEP)
            x = _mla_out_proj(x, mod, layer, o_p, o_s, mla_w_o[j], n_prompt, dec_seq)
        x_parts = _moe(x, mod, layer, norm_ffn[layer], rw_hi, rw_lo, router_bias, moe_w_gate, moe_w_up, moe_w_down,
                       n_prompt, dec_seq, split=layer == depth - 1)

    xp = x_parts[0].reshape(batch, seq, d)
    xs = x_parts[1].reshape(dec_batch, dec_seq, d)
    return (xp, xs,
            jnp.stack(fwd_states, axis=1), jnp.stack(bwd_states, axis=1),
            jnp.stack(ckv_caches, axis=1), jnp.stack(krope_caches, axis=1))
```

```python
import functools

import jax
import jax.numpy as jnp
from jax import lax
from jax.experimental import pallas as pl
from jax.experimental.pallas import tpu as pltpu

F32 = jnp.float32
BF16 = jnp.bfloat16
I32 = jnp.int32

EPS = 1e-6
LRU_C = 8.0
GRID_W = 64
ROPE_BASE = 10000.0
ROPE_QUARTER = 16
N_EXPERT_GROUPS = 4
TOP_K = 2

LANES = 128
SUBLANES = 8
VMEM_BYTES_V7X = 64 * 1024 * 1024

TOKEN_TILE = 256
MLA_TOKEN_TILE = 512
EXPERT_TILE = 256
MOD_COL_TILE = 1024
ATTN_Q_TILE = 512
ATTN_LATENT_HEADS_PER_STEP = 4
OUT_PROJ_TILE = 512
LRU_GATE_CHUNK = 256
RUN_ALIGN = 16
RUN_PAD_ROWS = 256
LRU_PROMPT_SEQS_PER_STEP = 8


def _cparams(semantics, vmem_mb):
    return pltpu.CompilerParams(dimension_semantics=semantics, vmem_limit_bytes=vmem_mb * 1024 * 1024)


def _resident(shape):
    nd = len(shape)
    return pl.BlockSpec(shape, lambda *_: (0,) * nd, pipeline_mode=pl.Buffered(1))


def _mod_spec(layer, chunk, tile, n_prompt, sample_seq, d):
    def index(i, *_):
        t = i * tile
        row = jnp.where(t < n_prompt, 0, 1 + (t - n_prompt) // sample_seq)
        return (layer, row, chunk, 0, 0)

    return pl.BlockSpec((None, None, None, 1, d), index)


def _modulated(x, gain, shift, scale):
    y = x * lax.rsqrt(jnp.mean(x * x, axis=-1, keepdims=True) + EPS)
    return (y * gain) * (1.0 + scale) + shift


def _mod_body(c_ref, w_ref, b_ref, o_ref):
    c = c_ref[...]
    s = (c * jax.nn.sigmoid(c)).astype(BF16)
    o_ref[...] = jnp.dot(s, w_ref[...].astype(BF16), preferred_element_type=F32) + b_ref[...]


def _modulation(cond, ada_w, ada_b):
    depth, d, n = ada_w.shape
    tn = MOD_COL_TILE
    return pl.pallas_call(
        _mod_body,
        grid=(depth, n // tn),
        in_specs=[
            pl.BlockSpec((SUBLANES, d), lambda l, j: (0, 0)),
            pl.BlockSpec((None, d, tn), lambda l, j: (l, 0, j)),
            pl.BlockSpec((None, 1, tn), lambda l, j: (l, 0, j)),
        ],
        out_specs=pl.BlockSpec((None, SUBLANES, tn), lambda l, j: (l, 0, j)),
        out_shape=jax.ShapeDtypeStruct((depth, SUBLANES, n), F32),
        compiler_params=_cparams(("parallel", "parallel"), 40),
        name="adaln_projection",
    )(cond, ada_w, ada_b.reshape(depth, 1, n))


def _group_select(i, prompt_tiles, p_ref, s_ref):
    return jnp.where(i < prompt_tiles, p_ref[...], s_ref[...])


def _group_specs(block, prompt_tiles):
    return [pl.BlockSpec(block, lambda i, *_: (jnp.minimum(i, prompt_tiles - 1), 0)),
            pl.BlockSpec(block, lambda i, *_: (jnp.maximum(i - prompt_tiles, 0), 0))]


def _token_specs(x_parts, tile, prompt_tiles):
    d = x_parts[0].shape[1]
    if len(x_parts) == 1:
        return [pl.BlockSpec((tile, d), lambda i, *_: (i, 0))]
    return _group_specs((tile, d), prompt_tiles)


def _token_tile(i, prompt_tiles, x_refs):
    return x_refs[0][...] if len(x_refs) == 1 else _group_select(i, prompt_tiles, *x_refs)


def _in0_body(*refs, n_x, prompt_tiles):
    x_refs = refs[:n_x]
    g_ref, sh_ref, sc_ref, w_ref, xb_ref, gg_ref, gu_ref, gv_ref = refs[n_x:]
    x = _token_tile(pl.program_id(0), prompt_tiles, x_refs)
    h = _modulated(x, g_ref[...], sh_ref[...], sc_ref[...])
    z = jnp.dot(h.astype(BF16), w_ref[...], preferred_element_type=F32)
    w = xb_ref.shape[1]
    xb_ref[...] = z[:, :w]
    gg_ref[...] = jax.nn.gelu(z[:, w:2 * w]).astype(BF16)
    gu_ref[...] = jax.nn.gelu(z[:, 2 * w:3 * w]).astype(BF16)
    gv_ref[...] = jax.nn.gelu(z[:, 3 * w:]).astype(BF16)


def _even_in_proj(x_parts, mod, layer, gain, w_in, n_prompt, sample_seq):
    t = sum(p.shape[0] for p in x_parts)
    d = x_parts[0].shape[1]
    tm = TOKEN_TILE
    w4 = w_in.shape[1]
    w = w4 // 4
    row = lambda i: (i, 0)
    ms = functools.partial(_mod_spec, layer, tile=tm, n_prompt=n_prompt, sample_seq=sample_seq, d=d)
    return pl.pallas_call(
        functools.partial(_in0_body, n_x=len(x_parts), prompt_tiles=n_prompt // tm),
        grid=(t // tm,),
        in_specs=[
            *_token_specs(x_parts, tm, n_prompt // tm),
            _resident((1, d)),
            ms(chunk=0),
            ms(chunk=1),
            _resident((d, w4)),
        ],
        out_specs=[pl.BlockSpec((tm, w), row)] * 4,
        out_shape=[
            jax.ShapeDtypeStruct((t, w), F32),
            jax.ShapeDtypeStruct((t, w), BF16),
            jax.ShapeDtypeStruct((t, w), BF16),
            jax.ShapeDtypeStruct((t, w), BF16),
        ],
        compiler_params=_cparams(("parallel",), 48),
        name="even_in_proj",
    )(*x_parts, gain.reshape(1, d), mod, mod, w_in.astype(BF16))


def _tile_scan(a, b, row, reverse):
    for d in (1, 2, 4):
        if reverse:
            keep = row < SUBLANES - d
            a_s = jnp.where(keep, pltpu.roll(a, SUBLANES - d, 0), 1.0)
            b_s = jnp.where(keep, pltpu.roll(b, SUBLANES - d, 0), 0.0)
        else:
            keep = row >= d
            a_s = jnp.where(keep, pltpu.roll(a, d, 0), 1.0)
            b_s = jnp.where(keep, pltpu.roll(b, d, 0), 0.0)
        b = b + a * b_s
        a = a * a_s
    return a, b


def _sigmoid(x):
    return 0.5 * jnp.tanh(0.5 * x) + 0.5


def _lru_body(xb_ref, gg_ref, h0f_ref, h0b_ref, cw_ref, cb_ref, w_ref, bias_ref, lam_ref,
              ya_ref, ff_ref, fb_ref,
              xp_ref, af_ref, bf_ref, ab_ref, bb_ref, hf_ref, hb_ref, *, seq, chunk, nseq):
    hw = LANES
    pad = SUBLANES
    pitch = seq + 2 * pad
    for s in range(nseq):
        xp_ref[pl.ds(s * pitch, pad), :] = jnp.zeros((pad, hw), F32)
        xp_ref[pl.ds(s * pitch + pad + seq, pad), :] = jnp.zeros((pad, hw), F32)
        xp_ref[pl.ds(s * pitch + pad, seq), :] = xb_ref[pl.ds(s * seq, seq), :]

    cw = cw_ref[...]
    cb = cb_ref[...]
    lam = lam_ref[...]
    neg = -lam
    softplus = jnp.maximum(neg, 0.0) + jnp.log1p(jnp.exp(-jnp.abs(neg)))
    nsp = -LRU_C * softplus
    w = w_ref[...]
    bias = bias_ref[...]

    for s in range(nseq):
        for c in range(seq // chunk):
            src = s * pitch + pad + c * chunk
            dst = s * seq + c * chunk
            xc = (cw[0:1] * xp_ref[pl.ds(src - 2, chunk), :]
                  + cw[1:2] * xp_ref[pl.ds(src - 1, chunk), :]
                  + cw[2:3] * xp_ref[pl.ds(src, chunk), :]
                  + cw[3:4] * xp_ref[pl.ds(src + 1, chunk), :]) + cb
            g = jnp.dot(xc.astype(BF16), w, preferred_element_type=F32) + bias
            for direction, (a_ref, b_ref) in enumerate(((af_ref, bf_ref), (ab_ref, bb_ref))):
                r = _sigmoid(g[:, (2 * direction) * hw:(2 * direction + 1) * hw])
                gi = _sigmoid(g[:, (2 * direction + 1) * hw:(2 * direction + 2) * hw])
                log_a = r * nsp[direction:direction + 1]
                a = jnp.exp(log_a)
                a_ref[pl.ds(dst, chunk), :] = a
                b_ref[pl.ds(dst, chunk), :] = jnp.sqrt(1.0 - a * a) * gi * xc

    n_tiles = seq // SUBLANES
    row = lax.broadcasted_iota(I32, (SUBLANES, hw), 0)

    def step(i, carry):
        new = []
        for s in range(nseq):
            hf, hb = carry[2 * s], carry[2 * s + 1]
            rf = pl.multiple_of(s * seq + i * SUBLANES, SUBLANES)
            rb = pl.multiple_of(s * seq + (n_tiles - 1 - i) * SUBLANES, SUBLANES)
            a, b = _tile_scan(af_ref[pl.ds(rf, SUBLANES), :], bf_ref[pl.ds(rf, SUBLANES), :], row, False)
            h = b + a * hf
            hf_ref[pl.ds(rf, SUBLANES), :] = h
            new.append(jnp.broadcast_to(h[SUBLANES - 1:SUBLANES, :], (SUBLANES, hw)))
            a, b = _tile_scan(ab_ref[pl.ds(rb, SUBLANES), :], bb_ref[pl.ds(rb, SUBLANES), :], row, True)
            h = b + a * hb
            hb_ref[pl.ds(rb, SUBLANES), :] = h
            new.append(jnp.broadcast_to(h[0:1, :], (SUBLANES, hw)))
        return tuple(new)

    init = []
    for s in range(nseq):
        init.append(jnp.broadcast_to(h0f_ref[s], (SUBLANES, hw)))
        init.append(jnp.broadcast_to(h0b_ref[s], (SUBLANES, hw)))
    final = lax.fori_loop(0, n_tiles, step, tuple(init), unroll=2 if nseq == 1 else 1)
    for s in range(nseq):
        ff_ref[s] = final[2 * s][0:1, :]
        fb_ref[s] = final[2 * s + 1][0:1, :]
    ya_ref[...] = ((hf_ref[...] + hb_ref[...]) * gg_ref[...].astype(F32)).astype(BF16)


def _lru_mixer(xb, gg, h0f, h0b, conv_w, conv_b, wcat, bcat, lam, *, row0, batch, seq, nseq):
    _, width = xb.shape
    heads = width // LANES
    rows = nseq * seq
    assert batch % nseq == 0 and row0 % rows == 0
    blk0 = row0 // rows
    tok = lambda b, h: (blk0 + b, h)
    state = lambda b, h: (b, 0, h)
    seq_buf = pltpu.VMEM((rows, LANES), F32)
    return pl.pallas_call(
        functools.partial(_lru_body, seq=seq, chunk=min(seq, LRU_GATE_CHUNK), nseq=nseq),
        grid=(batch // nseq, heads),
        in_specs=[
            pl.BlockSpec((rows, LANES), tok),
            pl.BlockSpec((rows, LANES), tok),
            pl.BlockSpec((nseq, 1, LANES), state),
            pl.BlockSpec((nseq, 1, LANES), state),
            pl.BlockSpec((conv_w.shape[0], LANES), lambda b, h: (0, h)),
            pl.BlockSpec((1, LANES), lambda b, h: (0, h)),
            pl.BlockSpec((None, LANES, 4 * LANES), lambda b, h: (h, 0, 0)),
            pl.BlockSpec((None, 1, 4 * LANES), lambda b, h: (h, 0, 0)),
            pl.BlockSpec((2, LANES), lambda b, h: (0, h)),
        ],
        out_specs=[
            pl.BlockSpec((rows, LANES), lambda b, h: (b, h)),
            pl.BlockSpec((nseq, 1, LANES), state),
            pl.BlockSpec((nseq, 1, LANES), state),
        ],
        out_shape=[
            jax.ShapeDtypeStruct((batch * seq, width), BF16),
            jax.ShapeDtypeStruct((batch, 1, width), F32),
            jax.ShapeDtypeStruct((batch, 1, width), F32),
        ],
        scratch_shapes=[pltpu.VMEM((nseq * (seq + 2 * SUBLANES), LANES), F32)] + [seq_buf] * 6,
        compiler_params=_cparams(("parallel", "parallel"), 40),
        name=f"rglru_seq{seq}",
    )(xb, gg, h0f, h0b, conv_w, conv_b, wcat, bcat, lam)


def _out0_body(*refs, n_x, chunk, prompt_tiles):
    x_refs = refs[:n_x]
    (gate_ref, yap_ref, yas_ref, gu_ref, gv_ref, vg_ref, ws_ref, bs_ref, wa_ref, wb_ref) = refs[n_x:n_x + 10]
    route_in = refs[n_x + 10:n_x + 16]
    o_ref = refs[n_x + 16]
    route_out = refs[n_x + 17:n_x + 21]
    yb_ref = refs[n_x + 21]
    tm = o_ref.shape[0]
    x = _token_tile(pl.program_id(0), prompt_tiles, x_refs)
    ya = _group_select(pl.program_id(0), prompt_tiles, yap_ref, yas_ref)
    v = gv_ref[...].astype(F32)
    vn = (v * lax.rsqrt(jnp.mean(v * v, axis=-1, keepdims=True) + EPS) * vg_ref[...]).astype(BF16)
    groups = ws_ref.shape[0]
    gd = vn.shape[1] // groups
    for c in range(tm // chunk):
        rows = slice(c * chunk, (c + 1) * chunk)
        for g in range(groups):
            cols = slice(g * gd, (g + 1) * gd)
            mixed = jnp.dot(ws_ref[g], vn[rows, cols], preferred_element_type=F32) + bs_ref[:, cols]
            yb_ref[rows, cols] = (gu_ref[rows, cols].astype(F32) * mixed).astype(BF16)
    y = (jnp.dot(ya, wa_ref[...], preferred_element_type=F32)
         + jnp.dot(yb_ref[...], wb_ref[...], preferred_element_type=F32))
    x_new = x + gate_ref[...] * y
    o_ref[...] = x_new
    _route_tiles(x_new, (*route_in, *route_out))


def _even_out_proj(x_parts, mod, layer, ya_p, ya_s, gu, gv, v_gain, w_s, b_full, w_out, ffn_gain, router,
                   n_prompt, sample_seq):
    t = sum(p.shape[0] for p in x_parts)
    d = x_parts[0].shape[1]
    tm = TOKEN_TILE
    w = ya_p.shape[1]
    row = lambda i: (i, 0)
    chunk = w_s.shape[1]
    prompt_tiles = n_prompt // tm
    r_in, r_args, r_out, r_shape = _route_specs(mod, layer, ffn_gain, router, tm, t, n_prompt, sample_seq)
    return pl.pallas_call(
        functools.partial(_out0_body, n_x=len(x_parts), chunk=chunk, prompt_tiles=prompt_tiles),
        grid=(t // tm,),
        in_specs=[
            *_token_specs(x_parts, tm, prompt_tiles),
            _mod_spec(layer, 2, tm, n_prompt, sample_seq, d),
            *_group_specs((tm, w), prompt_tiles),
            pl.BlockSpec((tm, w), row),
            pl.BlockSpec((tm, w), row),
            _resident((1, w)),
            _resident(w_s.shape),
            _resident(b_full.shape),
            _resident((w, d)),
            _resident((w, d)),
            *r_in,
        ],
        out_specs=[pl.BlockSpec((tm, d), row), *r_out],
        out_shape=[jax.ShapeDtypeStruct((t, d), F32), *r_shape],
        scratch_shapes=[pltpu.VMEM((tm, w), BF16)],
        compiler_params=_cparams(("parallel",), 48),
        name="even_out_proj",
    )(*x_parts, mod, ya_p, ya_s, gu, gv, v_gain.reshape(1, w), w_s.astype(BF16), b_full,
      w_out[:w].astype(BF16), w_out[w:].astype(BF16), *r_args)


def _swap_halves(x, lane):
    quarter = ROPE_QUARTER
    up = pltpu.roll(x, LANES - quarter, 1)
    down = pltpu.roll(x, quarter, 1)
    return jnp.where((lane % (2 * quarter)) < quarter, up, down)


def _mla_in_body(x_ref, g_ref, sh_ref, sc_ref, wd_ref, qan_ref, kvan_ref, wuq_ref, qn_ref, cos_ref, sin_ref,
                 q_ref, ckv_ref, kr_ref, *, heads, q_lora, kv_lora, qk_dim, sm_scale, sub):
    gn = qn_ref[0:1, :]
    gr = qn_ref[1:2, :]
    grs = qn_ref[2:3, :]
    lane = lax.broadcasted_iota(I32, (sub, LANES), 1)
    for r0 in range(0, x_ref.shape[0], sub):
        rows = pl.ds(r0, sub)
        h = _modulated(x_ref[rows, :], g_ref[...], sh_ref[...], sc_ref[...])
        z = jnp.dot(h.astype(BF16), wd_ref[...], preferred_element_type=F32)
        cq = z[:, :q_lora]
        cq = cq * lax.rsqrt(jnp.mean(cq * cq, axis=-1, keepdims=True) + EPS) * qan_ref[...]
        ckv = z[:, q_lora:q_lora + kv_lora]
        ckv_ref[rows, :] = ckv * lax.rsqrt(jnp.mean(ckv * ckv, axis=-1, keepdims=True) + EPS) * kvan_ref[...]
        kr_ref[rows, :] = z[:, q_lora + kv_lora:]
        q = jnp.dot(cq.astype(BF16), wuq_ref[...], preferred_element_type=F32)
        cos = cos_ref[rows, :]
        sin = sin_ref[rows, :]
        for hd in range(heads):
            qn = q[:, 2 * hd * LANES:(2 * hd + 1) * LANES]
            qr = q[:, (2 * hd + 1) * LANES:(2 * hd + 2) * LANES]
            ss = jnp.sum(qn * qn, axis=-1, keepdims=True) + jnp.sum(qr * qr, axis=-1, keepdims=True)
            rinv = lax.rsqrt(ss * (1.0 / qk_dim) + EPS) * sm_scale
            q_ref[rows, 2 * hd * LANES:(2 * hd + 1) * LANES] = (qn * gn * rinv).astype(BF16)
            rot = (qr * gr) * cos + (_swap_halves(qr, lane) * grs) * sin
            q_ref[rows, (2 * hd + 1) * LANES:(2 * hd + 2) * LANES] = (rot * rinv).astype(BF16)


def _rope_spec(tile, n_prompt, sample_seq):
    def index(i):
        t = i * tile
        return (jnp.where(t < n_prompt, 0, 1 + ((t - n_prompt) % sample_seq) // tile), 0)

    return pl.BlockSpec((tile, LANES), index)


def _mla_in_proj(x, mod, layer, gain, wd_ext, q_a_norm, kv_a_norm, wuq_pad, qn_rows, cos_tab, sin_tab,
                 n_prompt, sample_seq, *, heads, q_lora, kv_lora, qk_dim):
    t, d = x.shape
    tm = MLA_TOKEN_TILE
    row = lambda i: (i, 0)
    ms = functools.partial(_mod_spec, layer, tile=tm, n_prompt=n_prompt, sample_seq=sample_seq, d=d)
    body = functools.partial(_mla_in_body, heads=heads, q_lora=q_lora, kv_lora=kv_lora, qk_dim=qk_dim,
                             sm_scale=float(qk_dim) ** -0.5, sub=TOKEN_TILE)
    return pl.pallas_call(
        body,
        grid=(t // tm,),
        in_specs=[
            pl.BlockSpec((tm, d), row),
            _resident((1, d)),
            ms(chunk=0),
            ms(chunk=1),
            _resident(wd_ext.shape),
            _resident((1, q_lora)),
            _resident((1, kv_lora)),
            _resident(wuq_pad.shape),
            _resident(qn_rows.shape),
            _rope_spec(tm, n_prompt, sample_seq),
            _rope_spec(tm, n_prompt, sample_seq),
        ],
        out_specs=[
            pl.BlockSpec((tm, heads * 2 * LANES), row),
            pl.BlockSpec((tm, kv_lora), row),
            pl.BlockSpec((tm, LANES), row),
        ],
        out_shape=[
            jax.ShapeDtypeStruct((t, heads * 2 * LANES), BF16),
            jax.ShapeDtypeStruct((t, kv_lora), F32),
            jax.ShapeDtypeStruct((t, LANES), F32),
        ],
        compiler_params=_cparams(("parallel",), 56),
        name="mla_in_proj",
    )(x, gain.reshape(1, d), mod, mod, wd_ext, q_a_norm.reshape(1, q_lora), kv_a_norm.reshape(1, kv_lora),
      wuq_pad, qn_rows, cos_tab, sin_tab)


def _kv_body(ckv_ref, kr_ref, w_ref, kn_ref, cos_ref, sin_ref, k_ref, v_ref, *, heads, qk_dim):
    kv = jnp.dot(ckv_ref[...].astype(BF16), w_ref[...], preferred_element_type=F32)
    tm = kv.shape[0]
    lane = lax.broadcasted_iota(I32, (tm, LANES), 1)
    kr = kr_ref[...]
    gn = kn_ref[0:1, :]
    gr = kn_ref[1:2, :]
    grs = kn_ref[2:3, :]
    ssr = jnp.sum(kr * kr, axis=-1, keepdims=True)
    rot = (kr * gr) * cos_ref[...] + (_swap_halves(kr, lane) * grs) * sin_ref[...]
    for hd in range(heads):
        kn = kv[:, hd * LANES:(hd + 1) * LANES]
        rinv = lax.rsqrt((jnp.sum(kn * kn, axis=-1, keepdims=True) + ssr) * (1.0 / qk_dim) + EPS)
        k_ref[:, 2 * hd * LANES:(2 * hd + 1) * LANES] = (kn * gn * rinv).astype(BF16)
        k_ref[:, (2 * hd + 1) * LANES:(2 * hd + 2) * LANES] = (rot * rinv).astype(BF16)
    v_ref[...] = kv[:, heads * LANES:].astype(BF16)


def _kv_expand(ckv, kr, w_ukv_re, kn_rows, cos_tab, sin_tab, rope_index, *, heads, qk_dim):
    rows, kv_lora = ckv.shape
    tm = TOKEN_TILE
    row = lambda i: (i, 0)
    return pl.pallas_call(
        functools.partial(_kv_body, heads=heads, qk_dim=qk_dim),
        grid=(rows // tm,),
        in_specs=[
            pl.BlockSpec((tm, kv_lora), row),
            pl.BlockSpec((tm, LANES), row),
            _resident(w_ukv_re.shape),
            _resident(kn_rows.shape),
            pl.BlockSpec((tm, LANES), rope_index),
            pl.BlockSpec((tm, LANES), rope_index),
        ],
        out_specs=[pl.BlockSpec((tm, heads * 2 * LANES), row), pl.BlockSpec((tm, heads * LANES), row)],
        out_shape=[
            jax.ShapeDtypeStruct((rows, heads * 2 * LANES), BF16),
            jax.ShapeDtypeStruct((rows, heads * LANES), BF16),
        ],
        compiler_params=_cparams(("parallel",), 40),
        name=f"mla_kv_expand_{rows}",
    )(ckv, kr, w_ukv_re, kn_rows, cos_tab, sin_tab)


def _attn_body(q_ref, k_ref, v_ref, o_ref, *, heads_per_step):
    for hd in range(heads_per_step):
        qk = slice(2 * hd * LANES, 2 * (hd + 1) * LANES)
        vo = slice(hd * LANES, (hd + 1) * LANES)
        s = lax.dot_general(q_ref[:, qk], k_ref[:, qk], (((1,), (1,)), ((), ())), preferred_element_type=F32)
        m = jnp.max(s, axis=-1, keepdims=True)
        p = jnp.exp(s - m)
        l = jnp.sum(p, axis=-1, keepdims=True)
        o = jnp.dot(p.astype(BF16), v_ref[:, vo], preferred_element_type=F32)
        o_ref[:, vo] = (o / l).astype(BF16)


def _attention(q, k, v, *, row0, batch, seq, kv_len, heads, tq, heads_per_step):
    nq = seq // tq
    qblk0 = row0 // tq
    hps = heads_per_step
    return pl.pallas_call(
        functools.partial(_attn_body, heads_per_step=hps),
        grid=(batch, heads // hps, nq),
        in_specs=[
            pl.BlockSpec((tq, hps * 2 * LANES), lambda b, h, i: (qblk0 + b * nq + i, h)),
            pl.BlockSpec((kv_len, hps * 2 * LANES), lambda b, h, i: (b, h)),
            pl.BlockSpec((kv_len, hps * LANES), lambda b, h, i: (b, h)),
        ],
        out_specs=pl.BlockSpec((tq, hps * LANES), lambda b, h, i: (b * nq + i, h)),
        out_shape=jax.ShapeDtypeStruct((batch * seq, heads * LANES), BF16),
        compiler_params=_cparams(("parallel", "parallel", "parallel"), 48),
        name=f"mla_attention_kv{kv_len}",
    )(q, k, v)


def _oproj_body(x_ref, gate_ref, op_ref, os_ref, w_ref, *rest, prompt_tiles):
    route_in, out_ref, route_out = rest[:6], rest[6], rest[7:]
    o = _group_select(pl.program_id(0), prompt_tiles, op_ref, os_ref)
    y = jnp.dot(o, w_ref[...], preferred_element_type=F32)
    x_new = x_ref[...] + gate_ref[...] * y
    out_ref[...] = x_new
    _route_tiles(x_new, (*route_in, *route_out))


def _mla_out_proj(x, mod, layer, o_p, o_s, w_o, ffn_gain, router, n_prompt, sample_seq):
    t, d = x.shape
    tm = OUT_PROJ_TILE
    row = lambda i: (i, 0)
    prompt_tiles = n_prompt // tm
    r_in, r_args, r_out, r_shape = _route_specs(mod, layer, ffn_gain, router, tm, t, n_prompt, sample_seq)
    return pl.pallas_call(
        functools.partial(_oproj_body, prompt_tiles=prompt_tiles),
        grid=(t // tm,),
        in_specs=[
            pl.BlockSpec((tm, d), row),
            _mod_spec(layer, 2, tm, n_prompt, sample_seq, d),
            *_group_specs((tm, o_p.shape[1]), prompt_tiles),
            _resident(w_o.shape),
            *r_in,
        ],
        out_specs=[pl.BlockSpec((tm, d), row), *r_out],
        out_shape=[jax.ShapeDtypeStruct((t, d), F32), *r_shape],
        compiler_params=_cparams(("parallel",), 56),
        name="mla_out_proj",
    )(x, mod, o_p, o_s, w_o.astype(BF16), *r_args)


def _round_up(v, m):
    return (v + m - 1) // m * m


def _local_rows(tile):
    return TOP_K * tile + RUN_PAD_ROWS


def _route(h, whi_ref, wlo_ref, rb_ref, hb_ref, wt_ref, lp_ref, cnt_ref, r0):
    tm, d = h.shape
    n_experts = whi_ref.shape[0]
    rows = pl.ds(r0, tm)

    h_hi = h.astype(BF16)
    hb_ref[rows, :] = h_hi
    h_lo = (h - h_hi.astype(F32)).astype(BF16)
    nt = (((1,), (1,)), ((), ()))
    logits = (lax.dot_general(whi_ref[...], h_hi, nt, preferred_element_type=F32)
              + lax.dot_general(whi_ref[...], h_lo, nt, preferred_element_type=F32)
              + lax.dot_general(wlo_ref[...], h_hi, nt, preferred_element_type=F32))
    scores = jax.nn.sigmoid(logits)
    biased = scores + rb_ref[...]
    per_group = n_experts // N_EXPERT_GROUPS
    assert per_group == 4 and TOP_K == 2
    b_rows = [biased[e:e + 1, :] for e in range(n_experts)]
    s_rows = [scores[e:e + 1, :] for e in range(n_experts)]

    best = None
    sel = jnp.zeros((1, tm), I32)
    for g in range(N_EXPERT_GROUPS):
        b0, b1, b2, b3 = b_rows[g * 4:(g + 1) * 4]
        m1, n1 = jnp.maximum(b0, b1), jnp.minimum(b0, b1)
        m2, n2 = jnp.maximum(b2, b3), jnp.minimum(b2, b3)
        top1 = jnp.maximum(m1, m2)
        top2 = jnp.maximum(jnp.minimum(m1, m2), jnp.maximum(n1, n2))
        gsum = top1 + top2
        if best is None:
            best = gsum
        else:
            better = gsum > best
            sel = jnp.where(better, g, sel)
            best = jnp.where(better, gsum, best)

    def pick(rows, j):
        out = rows[j]
        for g in range(1, N_EXPERT_GROUPS):
            out = jnp.where(sel == g, rows[g * 4 + j], out)
        return out

    cand_b = [pick(b_rows, j) for j in range(4)]
    cand_s = [pick(s_rows, j) for j in range(4)]

    def argmax4(vals):
        bv, bi = vals[0], jnp.zeros((1, tm), I32)
        for j in range(1, 4):
            gt = vals[j] > bv
            bi = jnp.where(gt, j, bi)
            bv = jnp.where(gt, vals[j], bv)
        return bi

    i1 = argmax4(cand_b)
    i2 = argmax4([jnp.where(i1 == j, -jnp.inf, cand_b[j]) for j in range(4)])

    def take(vals, idx):
        out = vals[0]
        for j in range(1, 4):
            out = jnp.where(idx == j, vals[j], out)
        return out

    s1 = take(cand_s, i1)
    s2 = take(cand_s, i2)
    tot = s1 + s2
    e1 = sel * 4 + i1
    e2 = sel * 4 + i2
    wt_ref[0:1, rows] = s1 / tot
    wt_ref[1:2, rows] = s2 / tot

    eid = lax.broadcasted_iota(I32, (n_experts, tm), 0)
    is1 = eid == e1
    is2 = eid == e2
    chosen = jnp.where(is1 | is2, 1.0, 0.0)
    before = (lax.broadcasted_iota(I32, (tm, tm), 0) < lax.broadcasted_iota(I32, (tm, tm), 1))
    rank = jnp.dot(chosen.astype(BF16), jnp.where(before, 1.0, 0.0).astype(BF16), preferred_element_type=F32)
    count = jnp.sum(chosen, axis=1, keepdims=True)
    padded = jnp.floor((count + (RUN_ALIGN - 1)) * (1.0 / RUN_ALIGN)) * RUN_ALIGN
    lower = (lax.broadcasted_iota(I32, (n_experts, n_experts), 1)
             < lax.broadcasted_iota(I32, (n_experts, n_experts), 0))
    run_start = jnp.dot(jnp.where(lower, 1.0, 0.0).astype(BF16),
                        jnp.broadcast_to(padded, (n_experts, LANES)).astype(BF16),
                        preferred_element_type=F32)[:, 0:1]
    row = run_start + rank
    lp_ref[0:1, rows] = jnp.sum(jnp.where(is1, row, 0.0), axis=0, keepdims=True).astype(I32)
    lp_ref[1:2, rows] = jnp.sum(jnp.where(is2, row, 0.0), axis=0, keepdims=True).astype(I32)
    cnt_ref[r0 // tm] = jnp.broadcast_to(count, (n_experts, LANES))


def _route_tiles(x_new, route_refs):
    g_ref, sh_ref, sc_ref, whi_ref, wlo_ref, rb_ref, hb_ref, wt_ref, lp_ref, cnt_ref = route_refs
    for r0 in range(0, x_new.shape[0], TOKEN_TILE):
        h = _modulated(x_new[r0:r0 + TOKEN_TILE, :], g_ref[...], sh_ref[...], sc_ref[...])
        _route(h, whi_ref, wlo_ref, rb_ref, hb_ref, wt_ref, lp_ref, cnt_ref, r0)


def _route_specs(mod, layer, ffn_gain, router, tile, t, n_prompt, sample_seq):
    w_hi, w_lo, r_bias = router
    n_experts, d = w_hi.shape
    ms = functools.partial(_mod_spec, layer, tile=tile, n_prompt=n_prompt, sample_seq=sample_seq, d=d)
    col = lambda i: (0, i)
    sub = tile // TOKEN_TILE
    in_specs = [_resident((1, d)), ms(chunk=3), ms(chunk=4), _resident(w_hi.shape), _resident(w_lo.shape),
                _resident((n_experts, 1))]
    args = [ffn_gain.reshape(1, d), mod, mod, w_hi, w_lo, r_bias.reshape(n_experts, 1)]
    out_specs = [
        pl.BlockSpec((tile, d), lambda i: (i, 0)),
        pl.BlockSpec((TOP_K, tile), col),
        pl.BlockSpec((TOP_K, tile), col),
        pl.BlockSpec((sub, n_experts, LANES), lambda i: (i, 0, 0)),
    ]
    out_shape = [
        jax.ShapeDtypeStruct((t, d), BF16),
        jax.ShapeDtypeStruct((TOP_K, t), F32),
        jax.ShapeDtypeStruct((TOP_K, t), I32),
        jax.ShapeDtypeStruct((t // TOKEN_TILE, n_experts, LANES), F32),
    ]
    return in_specs, args, out_specs, out_shape


def _run_copies(tile, local_ref, slot, sorted_ref, row_ref, count_ref, sem, to_sorted):
    max_chunks = local_ref.shape[1] // RUN_ALIGN
    min_chunks = TOP_K * TOKEN_TILE // RUN_ALIGN

    def directed(loc, far):
        return pltpu.make_async_copy(loc, far, sem) if to_sorted else pltpu.make_async_copy(far, loc, sem)

    def chunk(c):
        loc = local_ref.at[slot, pl.ds(pl.multiple_of(c * RUN_ALIGN, RUN_ALIGN), RUN_ALIGN)]
        first = row_ref[tile * max_chunks + c]
        return directed(loc, sorted_ref.at[pl.ds(pl.multiple_of(first, RUN_ALIGN), RUN_ALIGN)])

    def apply(op):
        def one(c, carry):
            getattr(chunk(c), op)()
            return carry

        if op == "start":
            lax.fori_loop(0, min_chunks, one, 0, unroll=8)
        else:
            rows = min_chunks * RUN_ALIGN
            directed(local_ref.at[slot, pl.ds(0, rows)], sorted_ref.at[pl.ds(0, rows)]).wait()
        lax.fori_loop(min_chunks, count_ref[tile], one, 0)

    return apply


def _one_hot_rows(lp_ref, rows, tm):
    j = lax.broadcasted_iota(I32, (rows, tm), 0)
    hit = (j == lp_ref[0:1, :]) | (j == lp_ref[1:2, :])
    return jnp.where(hit, 1.0, 0.0).astype(BF16)


def _dispatch_body(row_ref, chunk_ref, last_ref, nv_ref, lp_ref, h_ref, o_ref, zero_ref, loc_ref, sems, zsem):
    tm = h_ref.shape[0]
    zt = zero_ref.shape[0]
    n_tiles = o_ref.shape[0] // zt
    i = pl.program_id(0)
    last = pl.num_programs(0) - 1
    slot = i % 2

    def copies(tile, s):
        return _run_copies(tile, loc_ref, s, o_ref, row_ref, chunk_ref, sems.at[s], True)

    @pl.when(i == 0)
    def _():
        zero_ref[...] = jnp.zeros(zero_ref.shape, zero_ref.dtype)

        def zero_tile(row):
            return pltpu.make_async_copy(zero_ref, o_ref.at[pl.ds(pl.multiple_of(row, zt), zt)], zsem)

        for e in range(last_ref.shape[0]):
            @pl.when(last_ref[e] >= 0)
            def _():
                zero_tile(last_ref[e]).start()

        def start_tail(j, c):
            zero_tile(j * zt).start()
            return c

        lax.fori_loop(nv_ref[0], n_tiles, start_tail, 0)
        for e in range(last_ref.shape[0]):
            @pl.when(last_ref[e] >= 0)
            def _():
                zero_tile(last_ref[e]).wait()

        def wait_tail(j, c):
            zero_tile(j * zt).wait()
            return c

        lax.fori_loop(nv_ref[0], n_tiles, wait_tail, 0)

    @pl.when(i >= 2)
    def _():
        copies(i - 2, slot)("wait")

    loc_ref[slot] = jnp.dot(_one_hot_rows(lp_ref, loc_ref.shape[1], tm), h_ref[...],
                            preferred_element_type=F32).astype(BF16)
    copies(i, slot)("start")

    @pl.when(i == last)
    def _():
        @pl.when(i >= 1)
        def _():
            copies(i - 1, 1 - slot)("wait")
        copies(i, slot)("wait")


def _dispatch(hb, lp, chunk_row, tile_chunks, last_tile_row, n_valid, sorted_rows):
    t, d = hb.shape
    tm = TOKEN_TILE
    grid_spec = pltpu.PrefetchScalarGridSpec(
        num_scalar_prefetch=4,
        grid=(t // tm,),
        in_specs=[
            pl.BlockSpec((TOP_K, tm), lambda i, *_: (0, i)),
            pl.BlockSpec((tm, d), lambda i, *_: (i, 0)),
        ],
        out_specs=pl.BlockSpec(memory_space=pl.ANY),
        scratch_shapes=[pltpu.VMEM((EXPERT_TILE, d), BF16), pltpu.VMEM((2, _local_rows(tm), d), BF16),
                        pltpu.SemaphoreType.DMA((2,)), pltpu.SemaphoreType.DMA],
    )
    return pl.pallas_call(
        _dispatch_body,
        grid_spec=grid_spec,
        out_shape=jax.ShapeDtypeStruct((sorted_rows, d), BF16),
        compiler_params=_cparams(("arbitrary",), 40),
        name="moe_dispatch",
    )(chunk_row, tile_chunks, last_tile_row, n_valid, lp, hb)


def _expert_body(te_ref, nv_ref, x_ref, wg_ref, wu_ref, wd_ref, y_ref, wgb_ref, wub_ref, wdb_ref):
    i = pl.program_id(0)
    valid = i < nv_ref[0]
    new_expert = (i == 0) | (te_ref[i] != te_ref[jnp.maximum(i - 1, 0)])

    @pl.when(valid & new_expert)
    def _():
        wgb_ref[...] = wg_ref[...].astype(BF16)
        wub_ref[...] = wu_ref[...].astype(BF16)
        wdb_ref[...] = wd_ref[...].astype(BF16)

    @pl.when(valid)
    def _():
        xb = x_ref[...]
        g = jnp.dot(xb, wgb_ref[...], preferred_element_type=F32)
        u = jnp.dot(xb, wub_ref[...], preferred_element_type=F32)
        act = (g * jax.nn.sigmoid(g) * u).astype(BF16)
        y_ref[...] = jnp.dot(act, wdb_ref[...], preferred_element_type=F32).astype(BF16)

    @pl.when(i >= nv_ref[0])
    def _():
        y_ref[...] = jnp.zeros(y_ref.shape, BF16)


def _experts(xs, tile_expert, n_valid, w_gate, w_up, w_down, layer):
    rows, d = xs.shape
    tm = EXPERT_TILE
    _, n_experts, _, f = w_gate.shape

    def xrow(i, te, nv):
        return (jnp.minimum(i, nv[0] - 1), 0)

    grid_spec = pltpu.PrefetchScalarGridSpec(
        num_scalar_prefetch=2,
        grid=(rows // tm,),
        in_specs=[
            pl.BlockSpec((tm, d), xrow),
            pl.BlockSpec((None, None, d, f), lambda i, te, nv: (layer, te[i], 0, 0)),
            pl.BlockSpec((None, None, d, f), lambda i, te, nv: (layer, te[i], 0, 0)),
            pl.BlockSpec((None, None, f, d), lambda i, te, nv: (layer, te[i], 0, 0)),
        ],
        out_specs=pl.BlockSpec((tm, d), lambda i, te, nv: (i, 0)),
        scratch_shapes=[pltpu.VMEM((d, f), BF16), pltpu.VMEM((d, f), BF16), pltpu.VMEM((f, d), BF16)],
    )
    return pl.pallas_call(
        _expert_body,
        grid_spec=grid_spec,
        out_shape=jax.ShapeDtypeStruct((rows, d), BF16),
        compiler_params=_cparams(("arbitrary",), 56),
        name="moe_experts",
    )(tile_expert, n_valid, xs, w_gate, w_up, w_down)


def _combine_body(row_ref, chunk_ref, lpt_ref, wtt_ref, x_ref, gate_ref, ys_ref, *rest, prompt_tiles):
    o_refs, (loc_ref, sems) = rest[:-2], rest[-2:]
    tm = x_ref.shape[0]
    rows = loc_ref.shape[1]
    i = pl.program_id(0)
    slot = i % 2

    def copies(tile, s):
        return _run_copies(tile, loc_ref, s, ys_ref, row_ref, chunk_ref, sems.at[s], False)

    @pl.when(i == 0)
    def _():
        loc_ref[...] = jnp.zeros(loc_ref.shape, loc_ref.dtype)
        copies(0, 0)("start")

    @pl.when(i + 1 < pl.num_programs(0))
    def _():
        copies(i + 1, 1 - slot)("start")

    copies(i, slot)("wait")

    jt = lax.broadcasted_iota(I32, (tm, rows), 1)
    local = loc_ref[slot]
    moe = jnp.zeros((tm, local.shape[1]), F32)
    for k in range(TOP_K):
        pick = jnp.where(jt == lpt_ref[:, k:k + 1], 1.0, 0.0).astype(BF16)
        moe = moe + wtt_ref[:, k:k + 1] * jnp.dot(pick, local, preferred_element_type=F32)
    out = x_ref[...] + gate_ref[...] * moe
    if len(o_refs) == 1:
        o_refs[0][...] = out
    else:
        for o_ref, own in zip(o_refs, (i < prompt_tiles, i >= prompt_tiles)):
            @pl.when(own)
            def _():
                o_ref[...] = out


def _combine(x, mod, layer, ys, lp, wt, chunk_row, tile_chunks, n_prompt, sample_seq, split):
    t, d = x.shape
    tm = TOKEN_TILE
    prompt_tiles = n_prompt // tm
    tok = lambda i, *_: (i, 0)
    col = lambda i, *_: (0, i)
    if split:
        out_specs = _group_specs((tm, d), prompt_tiles)
        out_shape = [jax.ShapeDtypeStruct((n_prompt, d), F32), jax.ShapeDtypeStruct((t - n_prompt, d), F32)]
    else:
        out_specs = [pl.BlockSpec((tm, d), tok)]
        out_shape = [jax.ShapeDtypeStruct((t, d), F32)]
    grid_spec = pltpu.PrefetchScalarGridSpec(
        num_scalar_prefetch=2,
        grid=(t // tm,),
        in_specs=[
            pl.BlockSpec((tm, TOP_K), tok),
            pl.BlockSpec((tm, TOP_K), tok),
            pl.BlockSpec((tm, d), tok),
            _mod_spec(layer, 5, tm, n_prompt, sample_seq, d),
            pl.BlockSpec(memory_space=pl.ANY),
        ],
        out_specs=out_specs,
        scratch_shapes=[pltpu.VMEM((2, _local_rows(tm), d), BF16), pltpu.SemaphoreType.DMA((2,))],
    )
    return pl.pallas_call(
        functools.partial(_combine_body, prompt_tiles=prompt_tiles),
        grid_spec=grid_spec,
        out_shape=out_shape,
        compiler_params=_cparams(("arbitrary",), 40),
        name="moe_combine",
    )(chunk_row, tile_chunks, lp.T, wt.T, x, mod, ys)


def _moe(x, routed, mod, layer, w_gate, w_up, w_down, n_prompt, sample_seq, split):
    t, d = x.shape
    hb, wt, lp, cnt = routed
    n_experts = cnt.shape[1]
    assert n_experts * (RUN_ALIGN - 1) <= RUN_PAD_ROWS
    n_tok_tiles = t // TOKEN_TILE
    run_len = _round_up(cnt[:, :, 0].astype(I32), RUN_ALIGN)
    run_before = jnp.cumsum(run_len, axis=0) - run_len
    tiles = (jnp.sum(run_len, axis=0) + EXPERT_TILE - 1) // EXPERT_TILE
    tile_end = jnp.cumsum(tiles)
    seg_start = (tile_end - tiles) * EXPERT_TILE
    run_row = seg_start[None, :] + run_before
    run_chunks = run_len // RUN_ALIGN
    chunk_end = jnp.cumsum(run_chunks, axis=1)
    chunk_ids = jnp.arange(_local_rows(TOKEN_TILE) // RUN_ALIGN, dtype=I32)
    owner = jnp.sum((chunk_end[:, None, :] <= chunk_ids[None, :, None]).astype(I32), axis=2)
    owner = jnp.minimum(owner, n_experts - 1)
    is_owner = owner[:, :, None] == jnp.arange(n_experts, dtype=I32)
    pick = lambda a: jnp.sum(jnp.where(is_owner, a[:, None, :], 0), axis=2)
    chunk_row = pick(run_row) + (chunk_ids[None, :] - pick(chunk_end - run_chunks)) * RUN_ALIGN
    chunk_row = chunk_row.reshape(-1).astype(I32)
    tile_chunks = chunk_end[:, -1].astype(I32)
    sorted_rows = (_round_up(TOP_K * t + n_tok_tiles * n_experts * (RUN_ALIGN - 1), EXPERT_TILE)
                   + n_experts * EXPERT_TILE)
    n_tiles = sorted_rows // EXPERT_TILE
    n_valid = tile_end[-1:]
    tile_ids = jnp.arange(n_tiles, dtype=I32)
    ends_before = lambda i: jnp.sum((tile_end[None, :] <= i[:, None]).astype(I32), axis=1)
    tile_expert = jnp.where(tile_ids < n_valid[0], ends_before(tile_ids), ends_before(n_valid - 1))
    tile_expert = jnp.minimum(tile_expert, n_experts - 1).astype(I32)
    last_tile_row = jnp.where(tiles > 0, (tile_end - 1) * EXPERT_TILE, -1).astype(I32)
    n_valid = n_valid.astype(I32)
    xs = _dispatch(hb, lp, chunk_row, tile_chunks, last_tile_row, n_valid, sorted_rows)
    ys = _experts(xs, tile_expert, n_valid, w_gate, w_up, w_down, layer)
    return _combine(x, mod, layer, ys, lp, wt, chunk_row, tile_chunks, n_prompt, sample_seq, split)


def _rope_tables(sample_seq, rope_dim, tile):
    n_freq = rope_dim // 4
    pos = jnp.arange(sample_seq)
    row_pos = (pos // GRID_W).astype(F32)
    col_pos = (pos % GRID_W).astype(F32)
    inv_freq = ROPE_BASE ** (-jnp.arange(n_freq, dtype=F32) / n_freq)
    ar = row_pos[:, None] * inv_freq
    ac = col_pos[:, None] * inv_freq
    zeros = jnp.zeros((sample_seq, LANES - rope_dim), F32)
    cos = jnp.concatenate([jnp.cos(ar), jnp.cos(ar), jnp.cos(ac), jnp.cos(ac), zeros], axis=1)
    sin = jnp.concatenate([-jnp.sin(ar), jnp.sin(ar), -jnp.sin(ac), jnp.sin(ac), zeros], axis=1)
    ident_c = jnp.concatenate([jnp.ones((tile, rope_dim), F32), jnp.zeros((tile, LANES - rope_dim), F32)], axis=1)
    return (jnp.concatenate([ident_c, cos], axis=0),
            jnp.concatenate([jnp.zeros((tile, LANES), F32), sin], axis=0))


def _norm_rows(gain, nope, rope_dim):
    quarter = rope_dim // 4
    gr = gain[nope:]
    grs = jnp.concatenate([gr[quarter:2 * quarter], gr[:quarter], gr[3 * quarter:], gr[2 * quarter:3 * quarter]])
    zpad = jnp.zeros((LANES - rope_dim,), F32)
    return jnp.stack([gain[:nope], jnp.concatenate([gr, zpad]), jnp.concatenate([grs, zpad])])


def kernel(x_prompt, x_sample, state_lru_fwd, state_lru_bwd, cache_mla_ckv, cache_mla_krope, c, c_ctx,
           ada_w, ada_b, norm_mix, norm_ffn, mix0_w_in, mix0_w_out, lru_conv_w, lru_conv_b,
           lru_w_r, lru_b_r, lru_w_i, lru_b_i, lru_lambda, gmlp_v_norm, gmlp_w_s, gmlp_b_s,
           mla_w_down, mla_q_a_norm, mla_kv_a_norm, mla_w_uq, mla_w_ukv, mla_q_norm, mla_k_norm, mla_w_o,
           router_w, router_bias, moe_w_gate, moe_w_up, moe_w_down):
    batch, seq, d = x_prompt.shape
    dec_batch, dec_seq, _ = x_sample.shape
    depth = ada_w.shape[0]
    n_prompt = batch * seq
    n_sample = dec_batch * dec_seq
    assert n_prompt % dec_seq == 0 and seq % TOKEN_TILE == 0 and dec_seq % TOKEN_TILE == 0
    assert 1 + dec_batch <= SUBLANES

    x_parts = (x_prompt.reshape(n_prompt, d), x_sample.reshape(n_sample, d))

    cond = jnp.concatenate([c_ctx[None, :], c, jnp.zeros((SUBLANES - 1 - dec_batch, d), F32)], axis=0)
    mod = _modulation(cond, ada_w, ada_b).reshape(depth, SUBLANES, 6, 1, d)

    rw_t = router_w.T
    rw_hi = rw_t.astype(BF16)
    router = (rw_hi, (rw_t - rw_hi.astype(F32)).astype(BF16), router_bias)

    fwd_states, bwd_states, ckv_caches, krope_caches = [], [], [], []
    for layer in range(depth):
        j = layer // 2
        if layer % 2 == 0:
            width = lru_conv_w.shape[2]
            heads = lru_w_r.shape[2]
            xb, gg, gu, gv = _even_in_proj(x_parts, mod, layer, norm_mix[layer], mix0_w_in[j], n_prompt, dec_seq)
            wcat = jnp.concatenate([lru_w_r[j, 0], lru_w_i[j, 0], lru_w_r[j, 1], lru_w_i[j, 1]], axis=-1).astype(BF16)
            hb = lambda v: v.reshape(heads, 1, width // heads)
            bcat = jnp.concatenate([hb(lru_b_r[j, 0]), hb(lru_b_i[j, 0]), hb(lru_b_r[j, 1]), hb(lru_b_i[j, 1])], axis=-1)
            lru_args = (lru_conv_w[j], lru_conv_b[j].reshape(1, width), wcat, bcat, lru_lambda[j])
            zero_state = jnp.zeros((batch, 1, width), F32)
            nseq_p = LRU_PROMPT_SEQS_PER_STEP if batch % LRU_PROMPT_SEQS_PER_STEP == 0 else 1
            ya_p, fin_f, fin_b = _lru_mixer(xb, gg, zero_state, zero_state, *lru_args,
                                            row0=0, batch=batch, seq=seq, nseq=nseq_p)
            ya_s, _, _ = _lru_mixer(xb, gg, state_lru_fwd[:, j][:, None, :], state_lru_bwd[:, j][:, None, :],
                                    *lru_args, row0=n_prompt, batch=dec_batch, seq=dec_seq, nseq=1)
            fwd_states.append(fin_f[:, 0, :])
            bwd_states.append(fin_b[:, 0, :])
            groups, chunk, _ = gmlp_w_s[j].shape
            gd = width // groups
            b_full = jnp.repeat(gmlp_b_s[j].T, gd, axis=1)
            x, *routed = _even_out_proj(x_parts, mod, layer, ya_p, ya_s, gu, gv, gmlp_v_norm[j], gmlp_w_s[j],
                                        b_full, mix0_w_out[j], norm_ffn[layer], router, n_prompt, dec_seq)
        else:
            q_lora = mla_q_a_norm.shape[1]
            kv_lora = mla_kv_a_norm.shape[1]
            qk_dim = mla_q_norm.shape[1]
            rope_dim = cache_mla_krope.shape[-1]
            nope = qk_dim - rope_dim
            heads = mla_w_uq.shape[2] // qk_dim
            v_dim = mla_w_ukv.shape[2] // heads - nope
            past = cache_mla_ckv.shape[2]
            assert nope == LANES and v_dim == LANES and rope_dim == 4 * ROPE_QUARTER
            wd = mla_w_down[j]
            wd_ext = jnp.concatenate([wd, jnp.zeros((d, LANES - rope_dim), F32)], axis=1).astype(BF16)
            wuq = mla_w_uq[j].reshape(q_lora, heads, qk_dim)
            wuq_pad = jnp.concatenate([wuq, jnp.zeros((q_lora, heads, 2 * LANES - qk_dim), F32)], axis=-1)
            wuq_pad = wuq_pad.reshape(q_lora, heads * 2 * LANES).astype(BF16)
            wukv = mla_w_ukv[j].reshape(kv_lora, heads, nope + v_dim)
            w_ukv_re = jnp.concatenate([wukv[:, :, :nope].reshape(kv_lora, heads * nope),
                                        wukv[:, :, nope:].reshape(kv_lora, heads * v_dim)], axis=1).astype(BF16)
            qn_rows = _norm_rows(mla_q_norm[j], nope, rope_dim)
            kn_rows = _norm_rows(mla_k_norm[j], nope, rope_dim)
            cos_tab, sin_tab = _rope_tables(dec_seq, rope_dim, MLA_TOKEN_TILE)
            x = x_parts[0] if len(x_parts) == 1 else jnp.concatenate(x_parts, axis=0)
            q, ckv, kr = _mla_in_proj(x, mod, layer, norm_mix[layer], wd_ext, mla_q_a_norm[j], mla_kv_a_norm[j],
                                      wuq_pad, qn_rows, cos_tab, sin_tab, n_prompt, dec_seq,
                                      heads=heads, q_lora=q_lora, kv_lora=kv_lora, qk_dim=qk_dim)
            ckv_caches.append(ckv[:n_prompt].reshape(batch, seq, kv_lora))
            krope_caches.append(kr[:n_prompt, :rope_dim].reshape(batch, seq, rope_dim))
            expand = functools.partial(_kv_expand, w_ukv_re=w_ukv_re, kn_rows=kn_rows, cos_tab=cos_tab,
                                       sin_tab=sin_tab, heads=heads, qk_dim=qk_dim)
            k_p, v_p = expand(ckv[:n_prompt], kr[:n_prompt], rope_index=lambda i: (0, 0))
            kv_len = past + dec_seq
            ctx_kr = jnp.pad(cache_mla_krope[:, j], ((0, 0), (0, 0), (0, LANES - rope_dim)))
            ckv_s = jnp.concatenate([cache_mla_ckv[:, j], ckv[n_prompt:].reshape(dec_batch, dec_seq, kv_lora)], axis=1)
            kr_s = jnp.concatenate([ctx_kr, kr[n_prompt:].reshape(dec_batch, dec_seq, LANES)], axis=1)
            tiles_per_req = kv_len // TOKEN_TILE
            ctx_tiles = past // TOKEN_TILE
            assert past % TOKEN_TILE == 0

            def latent_rope(i):
                r = i % tiles_per_req
                return (jnp.where(r < ctx_tiles, 0, MLA_TOKEN_TILE // TOKEN_TILE + r - ctx_tiles), 0)

            k_s, v_s = expand(ckv_s.reshape(dec_batch * kv_len, kv_lora), kr_s.reshape(dec_batch * kv_len, LANES),
                              rope_index=latent_rope)
            o_p = _attention(q, k_p, v_p, row0=0, batch=batch, seq=seq, kv_len=seq, heads=heads, tq=seq,
                             heads_per_step=heads)
            o_s = _attention(q, k_s, v_s, row0=n_prompt, batch=dec_batch, seq=dec_seq, kv_len=kv_len,
                             heads=heads, tq=ATTN_Q_TILE, heads_per_step=ATTN_LATENT_HEADS_PER_STEP)
            x, *routed = _mla_out_proj(x, mod, layer, o_p, o_s, mla_w_o[j], norm_ffn[layer], router,
                                       n_prompt, dec_seq)
        x_parts = _moe(x, routed, mod, layer, moe_w_gate, moe_w_up, moe_w_down, n_prompt, dec_seq,
                       split=layer == depth - 1)

    xp = x_parts[0].reshape(batch, seq, d)
    xs = x_parts[1].reshape(dec_batch, dec_seq, d)
    return (xp, xs,
            jnp.stack(fwd_states, axis=1), jnp.stack(bwd_states, axis=1),
            jnp.stack(ckv_caches, axis=1), jnp.stack(krope_caches, axis=1))
```

```python
import functools

import jax
import jax.numpy as jnp
from jax import lax
from jax.experimental import pallas as pl
from jax.experimental.pallas import tpu as pltpu

F32 = jnp.float32
BF16 = jnp.bfloat16
I32 = jnp.int32

EPS = 1e-6
LRU_C = 8.0
GRID_W = 64
ROPE_BASE = 10000.0
ROPE_QUARTER = 16
N_EXPERT_GROUPS = 4
TOP_K = 2

LANES = 128
SUBLANES = 8
VMEM_BYTES_V7X = 64 * 1024 * 1024

TOKEN_TILE = 256
MLA_TOKEN_TILE = 512
EXPERT_TILE = 512
MOD_COL_TILE = 1024
ATTN_Q_TILE = 512
ATTN_LATENT_HEADS_PER_STEP = 4
OUT_PROJ_TILE = 512
LRU_GATE_CHUNK = 256
RUN_ALIGN = 16
RUN_PAD_ROWS = 256
LRU_PROMPT_SEQS_PER_STEP = 8


def _cparams(semantics, vmem_mb):
    return pltpu.CompilerParams(dimension_semantics=semantics, vmem_limit_bytes=vmem_mb * 1024 * 1024)


def _resident(shape):
    nd = len(shape)
    return pl.BlockSpec(shape, lambda *_: (0,) * nd, pipeline_mode=pl.Buffered(1))


def _mod_spec(layer, chunk, tile, n_prompt, sample_seq, d):
    def index(i, *_):
        t = i * tile
        row = jnp.where(t < n_prompt, 0, 1 + (t - n_prompt) // sample_seq)
        return (layer, row, chunk, 0, 0)

    return pl.BlockSpec((None, None, None, 1, d), index)


def _modulated(x, gain, shift, scale):
    y = x * lax.rsqrt(jnp.mean(x * x, axis=-1, keepdims=True) + EPS)
    return (y * gain) * (1.0 + scale) + shift


def _mod_body(c_ref, w_ref, b_ref, o_ref):
    c = c_ref[...]
    s = (c * jax.nn.sigmoid(c)).astype(BF16)
    o_ref[...] = jnp.dot(s, w_ref[...].astype(BF16), preferred_element_type=F32) + b_ref[...]


def _modulation(cond, ada_w, ada_b):
    depth, d, n = ada_w.shape
    tn = MOD_COL_TILE
    return pl.pallas_call(
        _mod_body,
        grid=(depth, n // tn),
        in_specs=[
            pl.BlockSpec((SUBLANES, d), lambda l, j: (0, 0)),
            pl.BlockSpec((None, d, tn), lambda l, j: (l, 0, j)),
            pl.BlockSpec((None, 1, tn), lambda l, j: (l, 0, j)),
        ],
        out_specs=pl.BlockSpec((None, SUBLANES, tn), lambda l, j: (l, 0, j)),
        out_shape=jax.ShapeDtypeStruct((depth, SUBLANES, n), F32),
        compiler_params=_cparams(("parallel", "parallel"), 40),
        name="adaln_projection",
    )(cond, ada_w, ada_b.reshape(depth, 1, n))


def _group_select(i, prompt_tiles, p_ref, s_ref):
    return jnp.where(i < prompt_tiles, p_ref[...], s_ref[...])


def _group_specs(block, prompt_tiles):
    return [pl.BlockSpec(block, lambda i, *_: (jnp.minimum(i, prompt_tiles - 1), 0)),
            pl.BlockSpec(block, lambda i, *_: (jnp.maximum(i - prompt_tiles, 0), 0))]


def _token_specs(x_parts, tile, prompt_tiles):
    d = x_parts[0].shape[1]
    if len(x_parts) == 1:
        return [pl.BlockSpec((tile, d), lambda i, *_: (i, 0))]
    return _group_specs((tile, d), prompt_tiles)


def _token_tile(i, prompt_tiles, x_refs):
    return x_refs[0][...] if len(x_refs) == 1 else _group_select(i, prompt_tiles, *x_refs)


def _in0_body(*refs, n_x, prompt_tiles):
    x_refs = refs[:n_x]
    g_ref, sh_ref, sc_ref, w_ref, xb_ref, gg_ref, gu_ref, gv_ref = refs[n_x:]
    x = _token_tile(pl.program_id(0), prompt_tiles, x_refs)
    h = _modulated(x, g_ref[...], sh_ref[...], sc_ref[...])
    z = jnp.dot(h.astype(BF16), w_ref[...], preferred_element_type=F32)
    w = xb_ref.shape[1]
    xb_ref[...] = z[:, :w]
    gg_ref[...] = jax.nn.gelu(z[:, w:2 * w]).astype(BF16)
    gu_ref[...] = jax.nn.gelu(z[:, 2 * w:3 * w]).astype(BF16)
    gv_ref[...] = jax.nn.gelu(z[:, 3 * w:]).astype(BF16)


def _even_in_proj(x_parts, mod, layer, gain, w_in, n_prompt, sample_seq):
    t = sum(p.shape[0] for p in x_parts)
    d = x_parts[0].shape[1]
    tm = TOKEN_TILE
    w4 = w_in.shape[1]
    w = w4 // 4
    row = lambda i: (i, 0)
    ms = functools.partial(_mod_spec, layer, tile=tm, n_prompt=n_prompt, sample_seq=sample_seq, d=d)
    return pl.pallas_call(
        functools.partial(_in0_body, n_x=len(x_parts), prompt_tiles=n_prompt // tm),
        grid=(t // tm,),
        in_specs=[
            *_token_specs(x_parts, tm, n_prompt // tm),
            _resident((1, d)),
            ms(chunk=0),
            ms(chunk=1),
            _resident((d, w4)),
        ],
        out_specs=[pl.BlockSpec((tm, w), row)] * 4,
        out_shape=[
            jax.ShapeDtypeStruct((t, w), F32),
            jax.ShapeDtypeStruct((t, w), BF16),
            jax.ShapeDtypeStruct((t, w), BF16),
            jax.ShapeDtypeStruct((t, w), BF16),
        ],
        compiler_params=_cparams(("parallel",), 48),
        name="even_in_proj",
    )(*x_parts, gain.reshape(1, d), mod, mod, w_in.astype(BF16))


def _tile_scan(a, b, row, reverse):
    for d in (1, 2, 4):
        if reverse:
            keep = row < SUBLANES - d
            a_s = jnp.where(keep, pltpu.roll(a, SUBLANES - d, 0), 1.0)
            b_s = jnp.where(keep, pltpu.roll(b, SUBLANES - d, 0), 0.0)
        else:
            keep = row >= d
            a_s = jnp.where(keep, pltpu.roll(a, d, 0), 1.0)
            b_s = jnp.where(keep, pltpu.roll(b, d, 0), 0.0)
        b = b + a * b_s
        a = a * a_s
    return a, b


def _sigmoid(x):
    return 0.5 * jnp.tanh(0.5 * x) + 0.5


def _lru_body(xb_ref, gg_ref, h0f_ref, h0b_ref, cw_ref, cb_ref, w_ref, bias_ref, lam_ref,
              ya_ref, ff_ref, fb_ref,
              xp_ref, af_ref, bf_ref, ab_ref, bb_ref, hf_ref, hb_ref, *, seq, chunk, nseq):
    hw = LANES
    pad = SUBLANES
    pitch = seq + 2 * pad
    for s in range(nseq):
        xp_ref[pl.ds(s * pitch, pad), :] = jnp.zeros((pad, hw), F32)
        xp_ref[pl.ds(s * pitch + pad + seq, pad), :] = jnp.zeros((pad, hw), F32)
        xp_ref[pl.ds(s * pitch + pad, seq), :] = xb_ref[pl.ds(s * seq, seq), :]

    cw = cw_ref[...]
    cb = cb_ref[...]
    lam = lam_ref[...]
    neg = -lam
    softplus = jnp.maximum(neg, 0.0) + jnp.log1p(jnp.exp(-jnp.abs(neg)))
    nsp = -LRU_C * softplus
    w = w_ref[...]
    bias = bias_ref[...]

    for s in range(nseq):
        for c in range(seq // chunk):
            src = s * pitch + pad + c * chunk
            dst = s * seq + c * chunk
            xc = (cw[0:1] * xp_ref[pl.ds(src - 2, chunk), :]
                  + cw[1:2] * xp_ref[pl.ds(src - 1, chunk), :]
                  + cw[2:3] * xp_ref[pl.ds(src, chunk), :]
                  + cw[3:4] * xp_ref[pl.ds(src + 1, chunk), :]) + cb
            g = jnp.dot(xc.astype(BF16), w, preferred_element_type=F32) + bias
            for direction, (a_ref, b_ref) in enumerate(((af_ref, bf_ref), (ab_ref, bb_ref))):
                r = _sigmoid(g[:, (2 * direction) * hw:(2 * direction + 1) * hw])
                gi = _sigmoid(g[:, (2 * direction + 1) * hw:(2 * direction + 2) * hw])
                log_a = r * nsp[direction:direction + 1]
                a = jnp.exp(log_a)
                a_ref[pl.ds(dst, chunk), :] = a
                b_ref[pl.ds(dst, chunk), :] = jnp.sqrt(1.0 - a * a) * gi * xc

    n_tiles = seq // SUBLANES
    row = lax.broadcasted_iota(I32, (SUBLANES, hw), 0)

    def step(i, carry):
        new = []
        for s in range(nseq):
            hf, hb = carry[2 * s], carry[2 * s + 1]
            rf = pl.multiple_of(s * seq + i * SUBLANES, SUBLANES)
            rb = pl.multiple_of(s * seq + (n_tiles - 1 - i) * SUBLANES, SUBLANES)
            a, b = _tile_scan(af_ref[pl.ds(rf, SUBLANES), :], bf_ref[pl.ds(rf, SUBLANES), :], row, False)
            h = b + a * hf
            hf_ref[pl.ds(rf, SUBLANES), :] = h
            new.append(jnp.broadcast_to(h[SUBLANES - 1:SUBLANES, :], (SUBLANES, hw)))
            a, b = _tile_scan(ab_ref[pl.ds(rb, SUBLANES), :], bb_ref[pl.ds(rb, SUBLANES), :], row, True)
            h = b + a * hb
            hb_ref[pl.ds(rb, SUBLANES), :] = h
            new.append(jnp.broadcast_to(h[0:1, :], (SUBLANES, hw)))
        return tuple(new)

    init = []
    for s in range(nseq):
        init.append(jnp.broadcast_to(h0f_ref[s], (SUBLANES, hw)))
        init.append(jnp.broadcast_to(h0b_ref[s], (SUBLANES, hw)))
    final = lax.fori_loop(0, n_tiles, step, tuple(init), unroll=2 if nseq == 1 else 1)
    for s in range(nseq):
        ff_ref[s] = final[2 * s][0:1, :]
        fb_ref[s] = final[2 * s + 1][0:1, :]
    ya_ref[...] = ((hf_ref[...] + hb_ref[...]) * gg_ref[...].astype(F32)).astype(BF16)


def _lru_mixer(xb, gg, h0f, h0b, conv_w, conv_b, wcat, bcat, lam, *, row0, batch, seq, nseq):
    _, width = xb.shape
    heads = width // LANES
    rows = nseq * seq
    assert batch % nseq == 0 and row0 % rows == 0
    blk0 = row0 // rows
    tok = lambda b, h: (blk0 + b, h)
    state = lambda b, h: (b, 0, h)
    seq_buf = pltpu.VMEM((rows, LANES), F32)
    return pl.pallas_call(
        functools.partial(_lru_body, seq=seq, chunk=min(seq, LRU_GATE_CHUNK), nseq=nseq),
        grid=(batch // nseq, heads),
        in_specs=[
            pl.BlockSpec((rows, LANES), tok),
            pl.BlockSpec((rows, LANES), tok),
            pl.BlockSpec((nseq, 1, LANES), state),
            pl.BlockSpec((nseq, 1, LANES), state),
            pl.BlockSpec((conv_w.shape[0], LANES), lambda b, h: (0, h)),
            pl.BlockSpec((1, LANES), lambda b, h: (0, h)),
            pl.BlockSpec((None, LANES, 4 * LANES), lambda b, h: (h, 0, 0)),
            pl.BlockSpec((None, 1, 4 * LANES), lambda b, h: (h, 0, 0)),
            pl.BlockSpec((2, LANES), lambda b, h: (0, h)),
        ],
        out_specs=[
            pl.BlockSpec((rows, LANES), lambda b, h: (b, h)),
            pl.BlockSpec((nseq, 1, LANES), state),
            pl.BlockSpec((nseq, 1, LANES), state),
        ],
        out_shape=[
            jax.ShapeDtypeStruct((batch * seq, width), BF16),
            jax.ShapeDtypeStruct((batch, 1, width), F32),
            jax.ShapeDtypeStruct((batch, 1, width), F32),
        ],
        scratch_shapes=[pltpu.VMEM((nseq * (seq + 2 * SUBLANES), LANES), F32)] + [seq_buf] * 6,
        compiler_params=_cparams(("parallel", "parallel"), 40),
        name=f"rglru_seq{seq}",
    )(xb, gg, h0f, h0b, conv_w, conv_b, wcat, bcat, lam)


def _out0_body(*refs, n_x, chunk, prompt_tiles):
    x_refs = refs[:n_x]
    (gate_ref, yap_ref, yas_ref, gu_ref, gv_ref, vg_ref, ws_ref, bs_ref, wa_ref, wb_ref,
     o_ref, yb_ref) = refs[n_x:]
    tm = o_ref.shape[0]
    x = _token_tile(pl.program_id(0), prompt_tiles, x_refs)
    ya = _group_select(pl.program_id(0), prompt_tiles, yap_ref, yas_ref)
    v = gv_ref[...].astype(F32)
    vn = (v * lax.rsqrt(jnp.mean(v * v, axis=-1, keepdims=True) + EPS) * vg_ref[...]).astype(BF16)
    groups = ws_ref.shape[0]
    gd = vn.shape[1] // groups
    for c in range(tm // chunk):
        rows = slice(c * chunk, (c + 1) * chunk)
        for g in range(groups):
            cols = slice(g * gd, (g + 1) * gd)
            mixed = jnp.dot(ws_ref[g], vn[rows, cols], preferred_element_type=F32) + bs_ref[:, cols]
            yb_ref[rows, cols] = (gu_ref[rows, cols].astype(F32) * mixed).astype(BF16)
    y = (jnp.dot(ya, wa_ref[...], preferred_element_type=F32)
         + jnp.dot(yb_ref[...], wb_ref[...], preferred_element_type=F32))
    o_ref[...] = x + gate_ref[...] * y


def _even_out_proj(x_parts, mod, layer, ya_p, ya_s, gu, gv, v_gain, w_s, b_full, w_out, n_prompt, sample_seq):
    t = sum(p.shape[0] for p in x_parts)
    d = x_parts[0].shape[1]
    tm = OUT_PROJ_TILE
    w = ya_p.shape[1]
    row = lambda i: (i, 0)
    chunk = w_s.shape[1]
    prompt_tiles = n_prompt // tm
    return pl.pallas_call(
        functools.partial(_out0_body, n_x=len(x_parts), chunk=chunk, prompt_tiles=prompt_tiles),
        grid=(t // tm,),
        in_specs=[
            *_token_specs(x_parts, tm, prompt_tiles),
            _mod_spec(layer, 2, tm, n_prompt, sample_seq, d),
            *_group_specs((tm, w), prompt_tiles),
            pl.BlockSpec((tm, w), row),
            pl.BlockSpec((tm, w), row),
            _resident((1, w)),
            _resident(w_s.shape),
            _resident(b_full.shape),
            _resident((w, d)),
            _resident((w, d)),
        ],
        out_specs=pl.BlockSpec((tm, d), row),
        out_shape=jax.ShapeDtypeStruct((t, d), F32),
        scratch_shapes=[pltpu.VMEM((tm, w), BF16)],
        compiler_params=_cparams(("parallel",), 56),
        name="even_out_proj",
    )(*x_parts, mod, ya_p, ya_s, gu, gv, v_gain.reshape(1, w), w_s.astype(BF16), b_full,
      w_out[:w].astype(BF16), w_out[w:].astype(BF16))


def _swap_halves(x, lane):
    quarter = ROPE_QUARTER
    up = pltpu.roll(x, LANES - quarter, 1)
    down = pltpu.roll(x, quarter, 1)
    return jnp.where((lane % (2 * quarter)) < quarter, up, down)


def _mla_in_body(x_ref, g_ref, sh_ref, sc_ref, wd_ref, qan_ref, kvan_ref, wuq_ref, qn_ref, cos_ref, sin_ref,
                 q_ref, ckv_ref, kr_ref, *, heads, q_lora, kv_lora, qk_dim, sm_scale, sub):
    gn = qn_ref[0:1, :]
    gr = qn_ref[1:2, :]
    grs = qn_ref[2:3, :]
    lane = lax.broadcasted_iota(I32, (sub, LANES), 1)
    for r0 in range(0, x_ref.shape[0], sub):
        rows = pl.ds(r0, sub)
        h = _modulated(x_ref[rows, :], g_ref[...], sh_ref[...], sc_ref[...])
        z = jnp.dot(h.astype(BF16), wd_ref[...], preferred_element_type=F32)
        cq = z[:, :q_lora]
        cq = cq * lax.rsqrt(jnp.mean(cq * cq, axis=-1, keepdims=True) + EPS) * qan_ref[...]
        ckv = z[:, q_lora:q_lora + kv_lora]
        ckv_ref[rows, :] = ckv * lax.rsqrt(jnp.mean(ckv * ckv, axis=-1, keepdims=True) + EPS) * kvan_ref[...]
        kr_ref[rows, :] = z[:, q_lora + kv_lora:]
        q = jnp.dot(cq.astype(BF16), wuq_ref[...], preferred_element_type=F32)
        cos = gr * cos_ref[rows, :]
        sin = grs * sin_ref[rows, :]
        for hd in range(heads):
            qn = q[:, 2 * hd * LANES:(2 * hd + 1) * LANES]
            qr = q[:, (2 * hd + 1) * LANES:(2 * hd + 2) * LANES]
            ss = jnp.sum(qn * qn, axis=-1, keepdims=True) + jnp.sum(qr * qr, axis=-1, keepdims=True)
            rinv = lax.rsqrt(ss * (1.0 / qk_dim) + EPS) * sm_scale
            q_ref[rows, 2 * hd * LANES:(2 * hd + 1) * LANES] = (qn * gn * rinv).astype(BF16)
            rot = qr * cos + _swap_halves(qr, lane) * sin
            q_ref[rows, (2 * hd + 1) * LANES:(2 * hd + 2) * LANES] = (rot * rinv).astype(BF16)


def _rope_spec(tile, n_prompt, sample_seq):
    def index(i):
        t = i * tile
        return (jnp.where(t < n_prompt, 0, 1 + ((t - n_prompt) % sample_seq) // tile), 0)

    return pl.BlockSpec((tile, LANES), index)


def _mla_in_proj(x, mod, layer, gain, wd_ext, q_a_norm, kv_a_norm, wuq_pad, qn_rows, cos_tab, sin_tab,
                 n_prompt, sample_seq, *, heads, q_lora, kv_lora, qk_dim):
    t, d = x.shape
    tm = MLA_TOKEN_TILE
    row = lambda i: (i, 0)
    ms = functools.partial(_mod_spec, layer, tile=tm, n_prompt=n_prompt, sample_seq=sample_seq, d=d)
    body = functools.partial(_mla_in_body, heads=heads, q_lora=q_lora, kv_lora=kv_lora, qk_dim=qk_dim,
                             sm_scale=float(qk_dim) ** -0.5, sub=TOKEN_TILE)
    return pl.pallas_call(
        body,
        grid=(t // tm,),
        in_specs=[
            pl.BlockSpec((tm, d), row),
            _resident((1, d)),
            ms(chunk=0),
            ms(chunk=1),
            _resident(wd_ext.shape),
            _resident((1, q_lora)),
            _resident((1, kv_lora)),
            _resident(wuq_pad.shape),
            _resident(qn_rows.shape),
            _rope_spec(tm, n_prompt, sample_seq),
            _rope_spec(tm, n_prompt, sample_seq),
        ],
        out_specs=[
            pl.BlockSpec((tm, heads * 2 * LANES), row),
            pl.BlockSpec((tm, kv_lora), row),
            pl.BlockSpec((tm, LANES), row),
        ],
        out_shape=[
            jax.ShapeDtypeStruct((t, heads * 2 * LANES), BF16),
            jax.ShapeDtypeStruct((t, kv_lora), F32),
            jax.ShapeDtypeStruct((t, LANES), F32),
        ],
        compiler_params=_cparams(("parallel",), 56),
        name="mla_in_proj",
    )(x, gain.reshape(1, d), mod, mod, wd_ext, q_a_norm.reshape(1, q_lora), kv_a_norm.reshape(1, kv_lora),
      wuq_pad, qn_rows, cos_tab, sin_tab)


def _kv_body(ckv_ref, kr_ref, w_ref, kn_ref, cos_ref, sin_ref, k_ref, v_ref, *, heads, qk_dim):
    kv = jnp.dot(ckv_ref[...].astype(BF16), w_ref[...], preferred_element_type=F32)
    tm = kv.shape[0]
    lane = lax.broadcasted_iota(I32, (tm, LANES), 1)
    kr = kr_ref[...]
    gn = kn_ref[0:1, :]
    gr = kn_ref[1:2, :]
    grs = kn_ref[2:3, :]
    ssr = jnp.sum(kr * kr, axis=-1, keepdims=True)
    rot = (kr * gr) * cos_ref[...] + (_swap_halves(kr, lane) * grs) * sin_ref[...]
    for hd in range(heads):
        kn = kv[:, hd * LANES:(hd + 1) * LANES]
        rinv = lax.rsqrt((jnp.sum(kn * kn, axis=-1, keepdims=True) + ssr) * (1.0 / qk_dim) + EPS)
        k_ref[:, 2 * hd * LANES:(2 * hd + 1) * LANES] = (kn * gn * rinv).astype(BF16)
        k_ref[:, (2 * hd + 1) * LANES:(2 * hd + 2) * LANES] = (rot * rinv).astype(BF16)
    v_ref[...] = kv[:, heads * LANES:].astype(BF16)


def _kv_expand(ckv, kr, w_ukv_re, kn_rows, cos_tab, sin_tab, rope_index, *, heads, qk_dim):
    rows, kv_lora = ckv.shape
    tm = TOKEN_TILE
    row = lambda i: (i, 0)
    return pl.pallas_call(
        functools.partial(_kv_body, heads=heads, qk_dim=qk_dim),
        grid=(rows // tm,),
        in_specs=[
            pl.BlockSpec((tm, kv_lora), row),
            pl.BlockSpec((tm, LANES), row),
            _resident(w_ukv_re.shape),
            _resident(kn_rows.shape),
            pl.BlockSpec((tm, LANES), rope_index),
            pl.BlockSpec((tm, LANES), rope_index),
        ],
        out_specs=[pl.BlockSpec((tm, heads * 2 * LANES), row), pl.BlockSpec((tm, heads * LANES), row)],
        out_shape=[
            jax.ShapeDtypeStruct((rows, heads * 2 * LANES), BF16),
            jax.ShapeDtypeStruct((rows, heads * LANES), BF16),
        ],
        compiler_params=_cparams(("parallel",), 40),
        name=f"mla_kv_expand_{rows}",
    )(ckv, kr, w_ukv_re, kn_rows, cos_tab, sin_tab)


def _attn_body(q_ref, k_ref, v_ref, o_ref, *, heads_per_step):
    for hd in range(heads_per_step):
        qk = slice(2 * hd * LANES, 2 * (hd + 1) * LANES)
        vo = slice(hd * LANES, (hd + 1) * LANES)
        s = lax.dot_general(q_ref[:, qk], k_ref[:, qk], (((1,), (1,)), ((), ())), preferred_element_type=F32)
        m = jnp.max(s, axis=-1, keepdims=True)
        p = jnp.exp(s - m)
        l = jnp.sum(p, axis=-1, keepdims=True)
        o = jnp.dot(p.astype(BF16), v_ref[:, vo], preferred_element_type=F32)
        o_ref[:, vo] = (o / l).astype(BF16)


def _attention(q, k, v, *, row0, batch, seq, kv_len, heads, tq, heads_per_step):
    nq = seq // tq
    qblk0 = row0 // tq
    hps = heads_per_step
    return pl.pallas_call(
        functools.partial(_attn_body, heads_per_step=hps),
        grid=(batch, heads // hps, nq),
        in_specs=[
            pl.BlockSpec((tq, hps * 2 * LANES), lambda b, h, i: (qblk0 + b * nq + i, h)),
            pl.BlockSpec((kv_len, hps * 2 * LANES), lambda b, h, i: (b, h)),
            pl.BlockSpec((kv_len, hps * LANES), lambda b, h, i: (b, h)),
        ],
        out_specs=pl.BlockSpec((tq, hps * LANES), lambda b, h, i: (b * nq + i, h)),
        out_shape=jax.ShapeDtypeStruct((batch * seq, heads * LANES), BF16),
        compiler_params=_cparams(("parallel", "parallel", "parallel"), 48),
        name=f"mla_attention_kv{kv_len}",
    )(q, k, v)


def _oproj_body(x_ref, gate_ref, op_ref, os_ref, w_ref, out_ref, *, prompt_tiles):
    o = _group_select(pl.program_id(0), prompt_tiles, op_ref, os_ref)
    y = jnp.dot(o, w_ref[...], preferred_element_type=F32)
    out_ref[...] = x_ref[...] + gate_ref[...] * y


def _mla_out_proj(x, mod, layer, o_p, o_s, w_o, n_prompt, sample_seq):
    t, d = x.shape
    tm = OUT_PROJ_TILE
    row = lambda i: (i, 0)
    prompt_tiles = n_prompt // tm
    return pl.pallas_call(
        functools.partial(_oproj_body, prompt_tiles=prompt_tiles),
        grid=(t // tm,),
        in_specs=[
            pl.BlockSpec((tm, d), row),
            _mod_spec(layer, 2, tm, n_prompt, sample_seq, d),
            *_group_specs((tm, o_p.shape[1]), prompt_tiles),
            _resident(w_o.shape),
        ],
        out_specs=pl.BlockSpec((tm, d), row),
        out_shape=jax.ShapeDtypeStruct((t, d), F32),
        compiler_params=_cparams(("parallel",), 48),
        name="mla_out_proj",
    )(x, mod, o_p, o_s, w_o.astype(BF16))


def _round_up(v, m):
    return (v + m - 1) // m * m


def _local_rows(tile):
    return TOP_K * tile + RUN_PAD_ROWS


def _router_body(x_ref, g_ref, sh_ref, sc_ref, wst_ref, rb_ref, hb_ref, wt_ref, lp_ref, cnt_ref):
    h = _modulated(x_ref[...], g_ref[...], sh_ref[...], sc_ref[...])
    tm, d = h.shape
    n_experts = rb_ref.shape[0]

    h_hi = h.astype(BF16)
    hb_ref[...] = h_hi
    h_lo = (h - h_hi.astype(F32)).astype(BF16)
    nt = (((1,), (1,)), ((), ()))
    both = lax.dot_general(wst_ref[...], h_hi, nt, preferred_element_type=F32)
    logits = (both[:n_experts] + both[n_experts:]
              + lax.dot_general(wst_ref[:n_experts, :], h_lo, nt, preferred_element_type=F32))
    scores = jax.nn.sigmoid(logits)
    biased = scores + rb_ref[...]
    per_group = n_experts // N_EXPERT_GROUPS
    assert per_group == 4 and TOP_K == 2
    b_rows = [biased[e:e + 1, :] for e in range(n_experts)]
    s_rows = [scores[e:e + 1, :] for e in range(n_experts)]

    best = None
    sel = jnp.zeros((1, tm), I32)
    for g in range(N_EXPERT_GROUPS):
        b0, b1, b2, b3 = b_rows[g * 4:(g + 1) * 4]
        m1, n1 = jnp.maximum(b0, b1), jnp.minimum(b0, b1)
        m2, n2 = jnp.maximum(b2, b3), jnp.minimum(b2, b3)
        top1 = jnp.maximum(m1, m2)
        top2 = jnp.maximum(jnp.minimum(m1, m2), jnp.maximum(n1, n2))
        gsum = top1 + top2
        if best is None:
            best = gsum
        else:
            better = gsum > best
            sel = jnp.where(better, g, sel)
            best = jnp.where(better, gsum, best)

    def pick(rows, j):
        out = rows[j]
        for g in range(1, N_EXPERT_GROUPS):
            out = jnp.where(sel == g, rows[g * 4 + j], out)
        return out

    cand_b = [pick(b_rows, j) for j in range(4)]
    cand_s = [pick(s_rows, j) for j in range(4)]

    def argmax4(vals):
        bv, bi = vals[0], jnp.zeros((1, tm), I32)
        for j in range(1, 4):
            gt = vals[j] > bv
            bi = jnp.where(gt, j, bi)
            bv = jnp.where(gt, vals[j], bv)
        return bi

    i1 = argmax4(cand_b)
    i2 = argmax4([jnp.where(i1 == j, -jnp.inf, cand_b[j]) for j in range(4)])

    def take(vals, idx):
        out = vals[0]
        for j in range(1, 4):
            out = jnp.where(idx == j, vals[j], out)
        return out

    s1 = take(cand_s, i1)
    s2 = take(cand_s, i2)
    tot = s1 + s2
    e1 = sel * 4 + i1
    e2 = sel * 4 + i2
    wt_ref[0:1, :] = s1 / tot
    wt_ref[1:2, :] = s2 / tot

    eid = lax.broadcasted_iota(I32, (n_experts, tm), 0)
    is1 = eid == e1
    is2 = eid == e2
    chosen = jnp.where(is1 | is2, 1.0, 0.0)
    before = (lax.broadcasted_iota(I32, (tm, tm), 0) < lax.broadcasted_iota(I32, (tm, tm), 1))
    rank = jnp.dot(chosen.astype(BF16), jnp.where(before, 1.0, 0.0).astype(BF16), preferred_element_type=F32)
    count = jnp.sum(chosen, axis=1, keepdims=True)
    padded = jnp.floor((count + (RUN_ALIGN - 1)) * (1.0 / RUN_ALIGN)) * RUN_ALIGN
    lower = (lax.broadcasted_iota(I32, (n_experts, n_experts), 1)
             < lax.broadcasted_iota(I32, (n_experts, n_experts), 0))
    run_start = jnp.dot(jnp.where(lower, 1.0, 0.0).astype(BF16),
                        jnp.broadcast_to(padded, (n_experts, LANES)).astype(BF16),
                        preferred_element_type=F32)[:, 0:1]
    row = run_start + rank
    lp_ref[0:1, :] = jnp.sum(jnp.where(is1, row, 0.0), axis=0, keepdims=True).astype(I32)
    lp_ref[1:2, :] = jnp.sum(jnp.where(is2, row, 0.0), axis=0, keepdims=True).astype(I32)
    cnt_ref[...] = jnp.broadcast_to(count, (n_experts, LANES))


def _router(x, mod, layer, gain, w_stack, r_bias, n_prompt, sample_seq):
    t, d = x.shape
    tm = TOKEN_TILE
    n_experts = r_bias.shape[0]
    ms = functools.partial(_mod_spec, layer, tile=tm, n_prompt=n_prompt, sample_seq=sample_seq, d=d)
    col = lambda i: (0, i)
    return pl.pallas_call(
        _router_body,
        grid=(t // tm,),
        in_specs=[
            pl.BlockSpec((tm, d), lambda i: (i, 0)),
            _resident((1, d)),
            ms(chunk=3),
            ms(chunk=4),
            _resident(w_stack.shape),
            _resident((n_experts, 1)),
        ],
        out_specs=[
            pl.BlockSpec((tm, d), lambda i: (i, 0)),
            pl.BlockSpec((TOP_K, tm), col),
            pl.BlockSpec((TOP_K, tm), col),
            pl.BlockSpec((None, n_experts, LANES), lambda i: (i, 0, 0)),
        ],
        out_shape=[
            jax.ShapeDtypeStruct((t, d), BF16),
            jax.ShapeDtypeStruct((TOP_K, t), F32),
            jax.ShapeDtypeStruct((TOP_K, t), I32),
            jax.ShapeDtypeStruct((t // tm, n_experts, LANES), F32),
        ],
        compiler_params=_cparams(("parallel",), 40),
        name="moe_router",
    )(x, gain.reshape(1, d), mod, mod, w_stack, r_bias.reshape(n_experts, 1))


def _run_copies(tile, local_ref, slot, sorted_ref, row_ref, count_ref, sem, to_sorted):
    max_chunks = local_ref.shape[1] // RUN_ALIGN
    min_chunks = TOP_K * TOKEN_TILE // RUN_ALIGN

    def directed(loc, far):
        return pltpu.make_async_copy(loc, far, sem) if to_sorted else pltpu.make_async_copy(far, loc, sem)

    def chunk(c):
        loc = local_ref.at[slot, pl.ds(pl.multiple_of(c * RUN_ALIGN, RUN_ALIGN), RUN_ALIGN)]
        first = row_ref[tile * max_chunks + c]
        return directed(loc, sorted_ref.at[pl.ds(pl.multiple_of(first, RUN_ALIGN), RUN_ALIGN)])

    def apply(op):
        def one(c, carry):
            getattr(chunk(c), op)()
            return carry

        if op == "start":
            lax.fori_loop(0, min_chunks, one, 0, unroll=8)
        else:
            rows = min_chunks * RUN_ALIGN
            directed(local_ref.at[slot, pl.ds(0, rows)], sorted_ref.at[pl.ds(0, rows)]).wait()
        lax.fori_loop(min_chunks, count_ref[tile], one, 0)

    return apply


def _one_hot_rows(lp_ref, rows, tm):
    j = lax.broadcasted_iota(I32, (rows, tm), 0)
    hit = (j == lp_ref[0:1, :]) | (j == lp_ref[1:2, :])
    return jnp.where(hit, 1.0, 0.0).astype(BF16)


def _dispatch_body(row_ref, chunk_ref, last_ref, nv_ref, lp_ref, h_ref, o_ref, zero_ref, loc_ref, sems, zsem):
    tm = h_ref.shape[0]
    zt = zero_ref.shape[0]
    n_tiles = o_ref.shape[0] // zt
    i = pl.program_id(0)
    last = pl.num_programs(0) - 1
    slot = i % 2

    def copies(tile, s):
        return _run_copies(tile, loc_ref, s, o_ref, row_ref, chunk_ref, sems.at[s], True)

    @pl.when(i == 0)
    def _():
        zero_ref[...] = jnp.zeros(zero_ref.shape, zero_ref.dtype)

        def zero_tile(row):
            return pltpu.make_async_copy(zero_ref, o_ref.at[pl.ds(pl.multiple_of(row, zt), zt)], zsem)

        for e in range(last_ref.shape[0]):
            @pl.when(last_ref[e] >= 0)
            def _():
                zero_tile(last_ref[e]).start()

        def start_tail(j, c):
            zero_tile(j * zt).start()
            return c

        lax.fori_loop(nv_ref[0], n_tiles, start_tail, 0)
        for e in range(last_ref.shape[0]):
            @pl.when(last_ref[e] >= 0)
            def _():
                zero_tile(last_ref[e]).wait()

        def wait_tail(j, c):
            zero_tile(j * zt).wait()
            return c

        lax.fori_loop(nv_ref[0], n_tiles, wait_tail, 0)

    @pl.when(i >= 2)
    def _():
        copies(i - 2, slot)("wait")

    loc_ref[slot] = jnp.dot(_one_hot_rows(lp_ref, loc_ref.shape[1], tm), h_ref[...],
                            preferred_element_type=F32).astype(BF16)
    copies(i, slot)("start")

    @pl.when(i == last)
    def _():
        @pl.when(i >= 1)
        def _():
            copies(i - 1, 1 - slot)("wait")
        copies(i, slot)("wait")


def _dispatch(hb, lp, chunk_row, tile_chunks, last_tile_row, n_valid, sorted_rows):
    t, d = hb.shape
    tm = TOKEN_TILE
    grid_spec = pltpu.PrefetchScalarGridSpec(
        num_scalar_prefetch=4,
        grid=(t // tm,),
        in_specs=[
            pl.BlockSpec((TOP_K, tm), lambda i, *_: (0, i)),
            pl.BlockSpec((tm, d), lambda i, *_: (i, 0)),
        ],
        out_specs=pl.BlockSpec(memory_space=pl.ANY),
        scratch_shapes=[pltpu.VMEM((EXPERT_TILE, d), BF16), pltpu.VMEM((2, _local_rows(tm), d), BF16),
                        pltpu.SemaphoreType.DMA((2,)), pltpu.SemaphoreType.DMA],
    )
    return pl.pallas_call(
        _dispatch_body,
        grid_spec=grid_spec,
        out_shape=jax.ShapeDtypeStruct((sorted_rows, d), BF16),
        compiler_params=_cparams(("arbitrary",), 40),
        name="moe_dispatch",
    )(chunk_row, tile_chunks, last_tile_row, n_valid, lp, hb)


def _expert_body(te_ref, nv_ref, x_ref, wg_ref, wu_ref, wd_ref, y_ref, wgb_ref, wub_ref, wdb_ref):
    i = pl.program_id(0)
    valid = i < nv_ref[0]
    new_expert = (i == 0) | (te_ref[i] != te_ref[jnp.maximum(i - 1, 0)])

    @pl.when(valid & new_expert)
    def _():
        wgb_ref[...] = wg_ref[...].astype(BF16)
        wub_ref[...] = wu_ref[...].astype(BF16)
        wdb_ref[...] = wd_ref[...].astype(BF16)

    @pl.when(valid)
    def _():
        xb = x_ref[...]
        g = jnp.dot(xb, wgb_ref[...], preferred_element_type=F32)
        u = jnp.dot(xb, wub_ref[...], preferred_element_type=F32)
        act = (g * jax.nn.sigmoid(g) * u).astype(BF16)
        y_ref[...] = jnp.dot(act, wdb_ref[...], preferred_element_type=F32).astype(BF16)

    @pl.when(i >= nv_ref[0])
    def _():
        y_ref[...] = jnp.zeros(y_ref.shape, BF16)


def _experts(xs, tile_expert, n_valid, w_gate, w_up, w_down, layer):
    rows, d = xs.shape
    tm = EXPERT_TILE
    _, n_experts, _, f = w_gate.shape

    def xrow(i, te, nv):
        return (jnp.minimum(i, nv[0] - 1), 0)

    grid_spec = pltpu.PrefetchScalarGridSpec(
        num_scalar_prefetch=2,
        grid=(rows // tm,),
        in_specs=[
            pl.BlockSpec((tm, d), xrow),
            pl.BlockSpec((None, None, d, f), lambda i, te, nv: (layer, te[i], 0, 0)),
            pl.BlockSpec((None, None, d, f), lambda i, te, nv: (layer, te[i], 0, 0)),
            pl.BlockSpec((None, None, f, d), lambda i, te, nv: (layer, te[i], 0, 0)),
        ],
        out_specs=pl.BlockSpec((tm, d), lambda i, te, nv: (i, 0)),
        scratch_shapes=[pltpu.VMEM((d, f), BF16), pltpu.VMEM((d, f), BF16), pltpu.VMEM((f, d), BF16)],
    )
    return pl.pallas_call(
        _expert_body,
        grid_spec=grid_spec,
        out_shape=jax.ShapeDtypeStruct((rows, d), BF16),
        compiler_params=_cparams(("arbitrary",), 56),
        name="moe_experts",
    )(tile_expert, n_valid, xs, w_gate, w_up, w_down)


def _combine_body(row_ref, chunk_ref, lpt_ref, wtt_ref, x_ref, gate_ref, ys_ref, *rest, prompt_tiles):
    o_refs, (loc_ref, sems) = rest[:-2], rest[-2:]
    tm = x_ref.shape[0]
    rows = loc_ref.shape[1]
    i = pl.program_id(0)
    slot = i % 2

    def copies(tile, s):
        return _run_copies(tile, loc_ref, s, ys_ref, row_ref, chunk_ref, sems.at[s], False)

    @pl.when(i == 0)
    def _():
        loc_ref[...] = jnp.zeros(loc_ref.shape, loc_ref.dtype)
        copies(0, 0)("start")

    @pl.when(i + 1 < pl.num_programs(0))
    def _():
        copies(i + 1, 1 - slot)("start")

    copies(i, slot)("wait")

    jt = lax.broadcasted_iota(I32, (tm, rows), 1)
    local = loc_ref[slot]
    moe = jnp.zeros((tm, local.shape[1]), F32)
    for k in range(TOP_K):
        pick = jnp.where(jt == lpt_ref[:, k:k + 1], 1.0, 0.0).astype(BF16)
        moe = moe + wtt_ref[:, k:k + 1] * jnp.dot(pick, local, preferred_element_type=F32)
    out = x_ref[...] + gate_ref[...] * moe
    if len(o_refs) == 1:
        o_refs[0][...] = out
    else:
        for o_ref, own in zip(o_refs, (i < prompt_tiles, i >= prompt_tiles)):
            @pl.when(own)
            def _():
                o_ref[...] = out


def _combine(x, mod, layer, ys, lp, wt, chunk_row, tile_chunks, n_prompt, sample_seq, split):
    t, d = x.shape
    tm = TOKEN_TILE
    prompt_tiles = n_prompt // tm
    tok = lambda i, *_: (i, 0)
    col = lambda i, *_: (0, i)
    if split:
        out_specs = _group_specs((tm, d), prompt_tiles)
        out_shape = [jax.ShapeDtypeStruct((n_prompt, d), F32), jax.ShapeDtypeStruct((t - n_prompt, d), F32)]
    else:
        out_specs = [pl.BlockSpec((tm, d), tok)]
        out_shape = [jax.ShapeDtypeStruct((t, d), F32)]
    grid_spec = pltpu.PrefetchScalarGridSpec(
        num_scalar_prefetch=2,
        grid=(t // tm,),
        in_specs=[
            pl.BlockSpec((tm, TOP_K), tok),
            pl.BlockSpec((tm, TOP_K), tok),
            pl.BlockSpec((tm, d), tok),
            _mod_spec(layer, 5, tm, n_prompt, sample_seq, d),
            pl.BlockSpec(memory_space=pl.ANY),
        ],
        out_specs=out_specs,
        scratch_shapes=[pltpu.VMEM((2, _local_rows(tm), d), BF16), pltpu.SemaphoreType.DMA((2,))],
    )
    return pl.pallas_call(
        functools.partial(_combine_body, prompt_tiles=prompt_tiles),
        grid_spec=grid_spec,
        out_shape=out_shape,
        compiler_params=_cparams(("arbitrary",), 40),
        name="moe_combine",
    )(chunk_row, tile_chunks, lp.T, wt.T, x, mod, ys)


def _moe(x, mod, layer, gain, w_stack, r_bias, w_gate, w_up, w_down, n_prompt, sample_seq, split):
    t, d = x.shape
    n_experts = r_bias.shape[0]
    assert n_experts * (RUN_ALIGN - 1) <= RUN_PAD_ROWS
    hb, wt, lp, cnt = _router(x, mod, layer, gain, w_stack, r_bias, n_prompt, sample_seq)
    n_tok_tiles = t // TOKEN_TILE
    run_len = _round_up(cnt[:, :, 0].astype(I32), RUN_ALIGN)
    run_before = jnp.cumsum(run_len, axis=0) - run_len
    tiles = (jnp.sum(run_len, axis=0) + EXPERT_TILE - 1) // EXPERT_TILE
    tile_end = jnp.cumsum(tiles)
    seg_start = (tile_end - tiles) * EXPERT_TILE
    run_row = seg_start[None, :] + run_before
    run_chunks = run_len // RUN_ALIGN
    chunk_end = jnp.cumsum(run_chunks, axis=1)
    chunk_ids = jnp.arange(_local_rows(TOKEN_TILE) // RUN_ALIGN, dtype=I32)
    owner = jnp.sum((chunk_end[:, None, :] <= chunk_ids[None, :, None]).astype(I32), axis=2)
    owner = jnp.minimum(owner, n_experts - 1)
    is_owner = owner[:, :, None] == jnp.arange(n_experts, dtype=I32)
    pick = lambda a: jnp.sum(jnp.where(is_owner, a[:, None, :], 0), axis=2)
    chunk_row = pick(run_row) + (chunk_ids[None, :] - pick(chunk_end - run_chunks)) * RUN_ALIGN
    chunk_row = chunk_row.reshape(-1).astype(I32)
    tile_chunks = chunk_end[:, -1].astype(I32)
    sorted_rows = (_round_up(TOP_K * t + n_tok_tiles * n_experts * (RUN_ALIGN - 1), EXPERT_TILE)
                   + n_experts * EXPERT_TILE)
    n_tiles = sorted_rows // EXPERT_TILE
    n_valid = tile_end[-1:]
    tile_ids = jnp.arange(n_tiles, dtype=I32)
    ends_before = lambda i: jnp.sum((tile_end[None, :] <= i[:, None]).astype(I32), axis=1)
    tile_expert = jnp.where(tile_ids < n_valid[0], ends_before(tile_ids), ends_before(n_valid - 1))
    tile_expert = jnp.minimum(tile_expert, n_experts - 1).astype(I32)
    last_tile_row = jnp.where(tiles > 0, (tile_end - 1) * EXPERT_TILE, -1).astype(I32)
    n_valid = n_valid.astype(I32)
    xs = _dispatch(hb, lp, chunk_row, tile_chunks, last_tile_row, n_valid, sorted_rows)
    ys = _experts(xs, tile_expert, n_valid, w_gate, w_up, w_down, layer)
    return _combine(x, mod, layer, ys, lp, wt, chunk_row, tile_chunks, n_prompt, sample_seq, split)


def _rope_tables(sample_seq, rope_dim, tile):
    n_freq = rope_dim // 4
    pos = jnp.arange(sample_seq)
    row_pos = (pos // GRID_W).astype(F32)
    col_pos = (pos % GRID_W).astype(F32)
    inv_freq = ROPE_BASE ** (-jnp.arange(n_freq, dtype=F32) / n_freq)
    ar = row_pos[:, None] * inv_freq
    ac = col_pos[:, None] * inv_freq
    zeros = jnp.zeros((sample_seq, LANES - rope_dim), F32)
    cos = jnp.concatenate([jnp.cos(ar), jnp.cos(ar), jnp.cos(ac), jnp.cos(ac), zeros], axis=1)
    sin = jnp.concatenate([-jnp.sin(ar), jnp.sin(ar), -jnp.sin(ac), jnp.sin(ac), zeros], axis=1)
    ident_c = jnp.concatenate([jnp.ones((tile, rope_dim), F32), jnp.zeros((tile, LANES - rope_dim), F32)], axis=1)
    return (jnp.concatenate([ident_c, cos], axis=0),
            jnp.concatenate([jnp.zeros((tile, LANES), F32), sin], axis=0))


def _norm_rows(gain, nope, rope_dim):
    quarter = rope_dim // 4
    gr = gain[nope:]
    grs = jnp.concatenate([gr[quarter:2 * quarter], gr[:quarter], gr[3 * quarter:], gr[2 * quarter:3 * quarter]])
    zpad = jnp.zeros((LANES - rope_dim,), F32)
    return jnp.stack([gain[:nope], jnp.concatenate([gr, zpad]), jnp.concatenate([grs, zpad])])


def kernel(x_prompt, x_sample, state_lru_fwd, state_lru_bwd, cache_mla_ckv, cache_mla_krope, c, c_ctx,
           ada_w, ada_b, norm_mix, norm_ffn, mix0_w_in, mix0_w_out, lru_conv_w, lru_conv_b,
           lru_w_r, lru_b_r, lru_w_i, lru_b_i, lru_lambda, gmlp_v_norm, gmlp_w_s, gmlp_b_s,
           mla_w_down, mla_q_a_norm, mla_kv_a_norm, mla_w_uq, mla_w_ukv, mla_q_norm, mla_k_norm, mla_w_o,
           router_w, router_bias, moe_w_gate, moe_w_up, moe_w_down):
    batch, seq, d = x_prompt.shape
    dec_batch, dec_seq, _ = x_sample.shape
    depth = ada_w.shape[0]
    n_prompt = batch * seq
    n_sample = dec_batch * dec_seq
    assert n_prompt % dec_seq == 0 and seq % TOKEN_TILE == 0 and dec_seq % TOKEN_TILE == 0
    assert 1 + dec_batch <= SUBLANES

    x_parts = (x_prompt.reshape(n_prompt, d), x_sample.reshape(n_sample, d))

    cond = jnp.concatenate([c_ctx[None, :], c, jnp.zeros((SUBLANES - 1 - dec_batch, d), F32)], axis=0)
    mod = _modulation(cond, ada_w, ada_b).reshape(depth, SUBLANES, 6, 1, d)

    rw_t = router_w.T
    rw_hi = rw_t.astype(BF16)
    rw_stack = jnp.concatenate([rw_hi, (rw_t - rw_hi.astype(F32)).astype(BF16)], axis=0)

    fwd_states, bwd_states, ckv_caches, krope_caches = [], [], [], []
    for layer in range(depth):
        j = layer // 2
        if layer % 2 == 0:
            width = lru_conv_w.shape[2]
            heads = lru_w_r.shape[2]
            xb, gg, gu, gv = _even_in_proj(x_parts, mod, layer, norm_mix[layer], mix0_w_in[j], n_prompt, dec_seq)
            wcat = jnp.concatenate([lru_w_r[j, 0], lru_w_i[j, 0], lru_w_r[j, 1], lru_w_i[j, 1]], axis=-1).astype(BF16)
            hb = lambda v: v.reshape(heads, 1, width // heads)
            bcat = jnp.concatenate([hb(lru_b_r[j, 0]), hb(lru_b_i[j, 0]), hb(lru_b_r[j, 1]), hb(lru_b_i[j, 1])], axis=-1)
            lru_args = (lru_conv_w[j], lru_conv_b[j].reshape(1, width), wcat, bcat, lru_lambda[j])
            zero_state = jnp.zeros((batch, 1, width), F32)
            nseq_p = LRU_PROMPT_SEQS_PER_STEP if batch % LRU_PROMPT_SEQS_PER_STEP == 0 else 1
            ya_p, fin_f, fin_b = _lru_mixer(xb, gg, zero_state, zero_state, *lru_args,
                                            row0=0, batch=batch, seq=seq, nseq=nseq_p)
            ya_s, _, _ = _lru_mixer(xb, gg, state_lru_fwd[:, j][:, None, :], state_lru_bwd[:, j][:, None, :],
                                    *lru_args, row0=n_prompt, batch=dec_batch, seq=dec_seq, nseq=1)
            fwd_states.append(fin_f[:, 0, :])
            bwd_states.append(fin_b[:, 0, :])
            groups, chunk, _ = gmlp_w_s[j].shape
            gd = width // groups
            b_full = jnp.repeat(gmlp_b_s[j].T, gd, axis=1)
            x = _even_out_proj(x_parts, mod, layer, ya_p, ya_s, gu, gv, gmlp_v_norm[j], gmlp_w_s[j], b_full,
                               mix0_w_out[j], n_prompt, dec_seq)
        else:
            q_lora = mla_q_a_norm.shape[1]
            kv_lora = mla_kv_a_norm.shape[1]
            qk_dim = mla_q_norm.shape[1]
            rope_dim = cache_mla_krope.shape[-1]
            nope = qk_dim - rope_dim
            heads = mla_w_uq.shape[2] // qk_dim
            v_dim = mla_w_ukv.shape[2] // heads - nope
            past = cache_mla_ckv.shape[2]
            assert nope == LANES and v_dim == LANES and rope_dim == 4 * ROPE_QUARTER
            wd = mla_w_down[j]
            wd_ext = jnp.concatenate([wd, jnp.zeros((d, LANES - rope_dim), F32)], axis=1).astype(BF16)
            wuq = mla_w_uq[j].reshape(q_lora, heads, qk_dim)
            wuq_pad = jnp.concatenate([wuq, jnp.zeros((q_lora, heads, 2 * LANES - qk_dim), F32)], axis=-1)
            wuq_pad = wuq_pad.reshape(q_lora, heads * 2 * LANES).astype(BF16)
            wukv = mla_w_ukv[j].reshape(kv_lora, heads, nope + v_dim)
            w_ukv_re = jnp.concatenate([wukv[:, :, :nope].reshape(kv_lora, heads * nope),
                                        wukv[:, :, nope:].reshape(kv_lora, heads * v_dim)], axis=1).astype(BF16)
            qn_rows = _norm_rows(mla_q_norm[j], nope, rope_dim)
            kn_rows = _norm_rows(mla_k_norm[j], nope, rope_dim)
            cos_tab, sin_tab = _rope_tables(dec_seq, rope_dim, MLA_TOKEN_TILE)
            x = x_parts[0] if len(x_parts) == 1 else jnp.concatenate(x_parts, axis=0)
            q, ckv, kr = _mla_in_proj(x, mod, layer, norm_mix[layer], wd_ext, mla_q_a_norm[j], mla_kv_a_norm[j],
                                      wuq_pad, qn_rows, cos_tab, sin_tab, n_prompt, dec_seq,
                                      heads=heads, q_lora=q_lora, kv_lora=kv_lora, qk_dim=qk_dim)
            ckv_caches.append(ckv[:n_prompt].reshape(batch, seq, kv_lora))
            krope_caches.append(kr[:n_prompt, :rope_dim].reshape(batch, seq, rope_dim))
            expand = functools.partial(_kv_expand, w_ukv_re=w_ukv_re, kn_rows=kn_rows, cos_tab=cos_tab,
                                       sin_tab=sin_tab, heads=heads, qk_dim=qk_dim)
            k_p, v_p = expand(ckv[:n_prompt], kr[:n_prompt], rope_index=lambda i: (0, 0))
            kv_len = past + dec_seq
            ctx_kr = jnp.pad(cache_mla_krope[:, j], ((0, 0), (0, 0), (0, LANES - rope_dim)))
            ckv_s = jnp.concatenate([cache_mla_ckv[:, j], ckv[n_prompt:].reshape(dec_batch, dec_seq, kv_lora)], axis=1)
            kr_s = jnp.concatenate([ctx_kr, kr[n_prompt:].reshape(dec_batch, dec_seq, LANES)], axis=1)
            tiles_per_req = kv_len // TOKEN_TILE
            ctx_tiles = past // TOKEN_TILE
            assert past % TOKEN_TILE == 0

            def latent_rope(i):
                r = i % tiles_per_req
                return (jnp.where(r < ctx_tiles, 0, MLA_TOKEN_TILE // TOKEN_TILE + r - ctx_tiles), 0)

            k_s, v_s = expand(ckv_s.reshape(dec_batch * kv_len, kv_lora), kr_s.reshape(dec_batch * kv_len, LANES),
                              rope_index=latent_rope)
            o_p = _attention(q, k_p, v_p, row0=0, batch=batch, seq=seq, kv_len=seq, heads=heads, tq=seq,
                             heads_per_step=heads)
            o_s = _attention(q, k_s, v_s, row0=n_prompt, batch=dec_batch, seq=dec_seq, kv_len=kv_len,
                             heads=heads, tq=ATTN_Q_TILE, heads_per_step=ATTN_LATENT_HEADS_PER_STEP)
            x = _mla_out_proj(x, mod, layer, o_p, o_s, mla_w_o[j], n_prompt, dec_seq)
        x_parts = _moe(x, mod, layer, norm_ffn[layer], rw_stack, router_bias, moe_w_gate, moe_w_up, moe_w_down,
                       n_prompt, dec_seq, split=layer == depth - 1)

    xp = x_parts[0].reshape(batch, seq, d)
    xs = x_parts[1].reshape(dec_batch, dec_seq, d)
    return (xp, xs,
            jnp.stack(fwd_states, axis=1), jnp.stack(bwd_states, axis=1),
            jnp.stack(ckv_caches, axis=1), jnp.stack(krope_caches, axis=1))
```

```python
import functools

import jax
import jax.numpy as jnp
from jax import lax
from jax.experimental import pallas as pl
from jax.experimental.pallas import tpu as pltpu

F32 = jnp.float32
BF16 = jnp.bfloat16
I32 = jnp.int32

EPS = 1e-6
LRU_C = 8.0
GRID_W = 64
ROPE_BASE = 10000.0
ROPE_QUARTER = 16
N_EXPERT_GROUPS = 4
TOP_K = 2

LANES = 128
SUBLANES = 8
VMEM_BYTES_V7X = 64 * 1024 * 1024

TOKEN_TILE = 256
MLA_TOKEN_TILE = 512
EXPERT_TILE = 512
MOD_COL_TILE = 1024
ATTN_Q_TILE = 512
ATTN_LATENT_HEADS_PER_STEP = 4
OUT_PROJ_TILE = 512
LRU_GATE_CHUNK = 256
RUN_ALIGN = 16
RUN_PAD_ROWS = 256
LRU_PROMPT_SEQS_PER_STEP = 8


def _cparams(semantics, vmem_mb):
    return pltpu.CompilerParams(dimension_semantics=semantics, vmem_limit_bytes=vmem_mb * 1024 * 1024)


def _resident(shape):
    nd = len(shape)
    return pl.BlockSpec(shape, lambda *_: (0,) * nd, pipeline_mode=pl.Buffered(1))


def _mod_spec(layer, chunk, tile, n_prompt, sample_seq, d):
    def index(i, *_):
        t = i * tile
        row = jnp.where(t < n_prompt, 0, 1 + (t - n_prompt) // sample_seq)
        return (layer, row, chunk, 0, 0)

    return pl.BlockSpec((None, None, None, 1, d), index)


def _modulated(x, gain, shift, scale):
    y = x * lax.rsqrt(jnp.mean(x * x, axis=-1, keepdims=True) + EPS)
    return (y * gain) * (1.0 + scale) + shift


def _mod_body(c_ref, w_ref, b_ref, o_ref):
    c = c_ref[...]
    s = (c * jax.nn.sigmoid(c)).astype(BF16)
    o_ref[...] = jnp.dot(s, w_ref[...].astype(BF16), preferred_element_type=F32) + b_ref[...]


def _modulation(cond, ada_w, ada_b):
    depth, d, n = ada_w.shape
    tn = MOD_COL_TILE
    return pl.pallas_call(
        _mod_body,
        grid=(depth, n // tn),
        in_specs=[
            pl.BlockSpec((SUBLANES, d), lambda l, j: (0, 0)),
            pl.BlockSpec((None, d, tn), lambda l, j: (l, 0, j)),
            pl.BlockSpec((None, 1, tn), lambda l, j: (l, 0, j)),
        ],
        out_specs=pl.BlockSpec((None, SUBLANES, tn), lambda l, j: (l, 0, j)),
        out_shape=jax.ShapeDtypeStruct((depth, SUBLANES, n), F32),
        compiler_params=_cparams(("parallel", "parallel"), 40),
        name="adaln_projection",
    )(cond, ada_w, ada_b.reshape(depth, 1, n))


def _group_select(i, prompt_tiles, p_ref, s_ref):
    return jnp.where(i < prompt_tiles, p_ref[...], s_ref[...])


def _group_specs(block, prompt_tiles):
    return [pl.BlockSpec(block, lambda i, *_: (jnp.minimum(i, prompt_tiles - 1), 0)),
            pl.BlockSpec(block, lambda i, *_: (jnp.maximum(i - prompt_tiles, 0), 0))]


def _token_specs(x_parts, tile, prompt_tiles):
    d = x_parts[0].shape[1]
    if len(x_parts) == 1:
        return [pl.BlockSpec((tile, d), lambda i, *_: (i, 0))]
    return _group_specs((tile, d), prompt_tiles)


def _token_tile(i, prompt_tiles, x_refs):
    return x_refs[0][...] if len(x_refs) == 1 else _group_select(i, prompt_tiles, *x_refs)


def _in0_body(*refs, n_x, prompt_tiles):
    x_refs = refs[:n_x]
    g_ref, sh_ref, sc_ref, w_ref, xb_ref, gg_ref, gu_ref, gv_ref = refs[n_x:]
    x = _token_tile(pl.program_id(0), prompt_tiles, x_refs)
    h = _modulated(x, g_ref[...], sh_ref[...], sc_ref[...])
    z = jnp.dot(h.astype(BF16), w_ref[...], preferred_element_type=F32)
    w = xb_ref.shape[1]
    xb_ref[...] = z[:, :w]
    gg_ref[...] = jax.nn.gelu(z[:, w:2 * w]).astype(BF16)
    gu_ref[...] = jax.nn.gelu(z[:, 2 * w:3 * w]).astype(BF16)
    gv_ref[...] = jax.nn.gelu(z[:, 3 * w:]).astype(BF16)


def _even_in_proj(x_parts, mod, layer, gain, w_in, n_prompt, sample_seq):
    t = sum(p.shape[0] for p in x_parts)
    d = x_parts[0].shape[1]
    tm = TOKEN_TILE
    w4 = w_in.shape[1]
    w = w4 // 4
    row = lambda i: (i, 0)
    ms = functools.partial(_mod_spec, layer, tile=tm, n_prompt=n_prompt, sample_seq=sample_seq, d=d)
    return pl.pallas_call(
        functools.partial(_in0_body, n_x=len(x_parts), prompt_tiles=n_prompt // tm),
        grid=(t // tm,),
        in_specs=[
            *_token_specs(x_parts, tm, n_prompt // tm),
            _resident((1, d)),
            ms(chunk=0),
            ms(chunk=1),
            _resident((d, w4)),
        ],
        out_specs=[pl.BlockSpec((tm, w), row)] * 4,
        out_shape=[
            jax.ShapeDtypeStruct((t, w), F32),
            jax.ShapeDtypeStruct((t, w), BF16),
            jax.ShapeDtypeStruct((t, w), BF16),
            jax.ShapeDtypeStruct((t, w), BF16),
        ],
        compiler_params=_cparams(("parallel",), 48),
        name="even_in_proj",
    )(*x_parts, gain.reshape(1, d), mod, mod, w_in.astype(BF16))


def _tile_scan(a, b, row, reverse):
    for d in (1, 2, 4):
        if reverse:
            keep = row < SUBLANES - d
            a_s = jnp.where(keep, pltpu.roll(a, SUBLANES - d, 0), 1.0)
            b_s = jnp.where(keep, pltpu.roll(b, SUBLANES - d, 0), 0.0)
        else:
            keep = row >= d
            a_s = jnp.where(keep, pltpu.roll(a, d, 0), 1.0)
            b_s = jnp.where(keep, pltpu.roll(b, d, 0), 0.0)
        b = b + a * b_s
        a = a * a_s
    return a, b


def _sigmoid(x):
    return 0.5 * jnp.tanh(0.5 * x) + 0.5


def _lru_body(xb_ref, gg_ref, h0f_ref, h0b_ref, cw_ref, cb_ref, w_ref, bias_ref, lam_ref,
              ya_ref, ff_ref, fb_ref,
              xp_ref, af_ref, bf_ref, ab_ref, bb_ref, hf_ref, hb_ref, *, seq, chunk, nseq):
    hw = LANES
    pad = SUBLANES
    pitch = seq + 2 * pad
    for s in range(nseq):
        xp_ref[pl.ds(s * pitch, pad), :] = jnp.zeros((pad, hw), F32)
        xp_ref[pl.ds(s * pitch + pad + seq, pad), :] = jnp.zeros((pad, hw), F32)
        xp_ref[pl.ds(s * pitch + pad, seq), :] = xb_ref[pl.ds(s * seq, seq), :]

    cw = cw_ref[...]
    cb = cb_ref[...]
    lam = lam_ref[...]
    neg = -lam
    softplus = jnp.maximum(neg, 0.0) + jnp.log1p(jnp.exp(-jnp.abs(neg)))
    nsp = -LRU_C * softplus
    w = w_ref[...]
    bias = bias_ref[...]

    for s in range(nseq):
        for c in range(seq // chunk):
            src = s * pitch + pad + c * chunk
            dst = s * seq + c * chunk
            xc = (cw[0:1] * xp_ref[pl.ds(src - 2, chunk), :]
                  + cw[1:2] * xp_ref[pl.ds(src - 1, chunk), :]
                  + cw[2:3] * xp_ref[pl.ds(src, chunk), :]
                  + cw[3:4] * xp_ref[pl.ds(src + 1, chunk), :]) + cb
            g = jnp.dot(xc.astype(BF16), w, preferred_element_type=F32) + bias
            for direction, (a_ref, b_ref) in enumerate(((af_ref, bf_ref), (ab_ref, bb_ref))):
                r = _sigmoid(g[:, (2 * direction) * hw:(2 * direction + 1) * hw])
                gi = _sigmoid(g[:, (2 * direction + 1) * hw:(2 * direction + 2) * hw])
                log_a = r * nsp[direction:direction + 1]
                a = jnp.exp(log_a)
                a_ref[pl.ds(dst, chunk), :] = a
                b_ref[pl.ds(dst, chunk), :] = jnp.sqrt(1.0 - a * a) * gi * xc

    n_tiles = seq // SUBLANES
    row = lax.broadcasted_iota(I32, (SUBLANES, hw), 0)

    def step(i, carry):
        new = []
        for s in range(nseq):
            hf, hb = carry[2 * s], carry[2 * s + 1]
            rf = pl.multiple_of(s * seq + i * SUBLANES, SUBLANES)
            rb = pl.multiple_of(s * seq + (n_tiles - 1 - i) * SUBLANES, SUBLANES)
            a, b = _tile_scan(af_ref[pl.ds(rf, SUBLANES), :], bf_ref[pl.ds(rf, SUBLANES), :], row, False)
            h = b + a * hf
            hf_ref[pl.ds(rf, SUBLANES), :] = h
            new.append(jnp.broadcast_to(h[SUBLANES - 1:SUBLANES, :], (SUBLANES, hw)))
            a, b = _tile_scan(ab_ref[pl.ds(rb, SUBLANES), :], bb_ref[pl.ds(rb, SUBLANES), :], row, True)
            h = b + a * hb
            hb_ref[pl.ds(rb, SUBLANES), :] = h
            new.append(jnp.broadcast_to(h[0:1, :], (SUBLANES, hw)))
        return tuple(new)

    init = []
    for s in range(nseq):
        init.append(jnp.broadcast_to(h0f_ref[s], (SUBLANES, hw)))
        init.append(jnp.broadcast_to(h0b_ref[s], (SUBLANES, hw)))
    final = lax.fori_loop(0, n_tiles, step, tuple(init), unroll=2 if nseq == 1 else 1)
    for s in range(nseq):
        ff_ref[s] = final[2 * s][0:1, :]
        fb_ref[s] = final[2 * s + 1][0:1, :]
    ya_ref[...] = ((hf_ref[...] + hb_ref[...]) * gg_ref[...].astype(F32)).astype(BF16)


def _lru_mixer(xb, gg, h0f, h0b, conv_w, conv_b, wcat, bcat, lam, *, row0, batch, seq, nseq):
    _, width = xb.shape
    heads = width // LANES
    rows = nseq * seq
    assert batch % nseq == 0 and row0 % rows == 0
    blk0 = row0 // rows
    tok = lambda b, h: (blk0 + b, h)
    state = lambda b, h: (b, 0, h)
    seq_buf = pltpu.VMEM((rows, LANES), F32)
    return pl.pallas_call(
        functools.partial(_lru_body, seq=seq, chunk=min(seq, LRU_GATE_CHUNK), nseq=nseq),
        grid=(batch // nseq, heads),
        in_specs=[
            pl.BlockSpec((rows, LANES), tok),
            pl.BlockSpec((rows, LANES), tok),
            pl.BlockSpec((nseq, 1, LANES), state),
            pl.BlockSpec((nseq, 1, LANES), state),
            pl.BlockSpec((conv_w.shape[0], LANES), lambda b, h: (0, h)),
            pl.BlockSpec((1, LANES), lambda b, h: (0, h)),
            pl.BlockSpec((None, LANES, 4 * LANES), lambda b, h: (h, 0, 0)),
            pl.BlockSpec((None, 1, 4 * LANES), lambda b, h: (h, 0, 0)),
            pl.BlockSpec((2, LANES), lambda b, h: (0, h)),
        ],
        out_specs=[
            pl.BlockSpec((rows, LANES), lambda b, h: (b, h)),
            pl.BlockSpec((nseq, 1, LANES), state),
            pl.BlockSpec((nseq, 1, LANES), state),
        ],
        out_shape=[
            jax.ShapeDtypeStruct((batch * seq, width), BF16),
            jax.ShapeDtypeStruct((batch, 1, width), F32),
            jax.ShapeDtypeStruct((batch, 1, width), F32),
        ],
        scratch_shapes=[pltpu.VMEM((nseq * (seq + 2 * SUBLANES), LANES), F32)] + [seq_buf] * 6,
        compiler_params=_cparams(("parallel", "parallel"), 40),
        name=f"rglru_seq{seq}",
    )(xb, gg, h0f, h0b, conv_w, conv_b, wcat, bcat, lam)


def _out0_body(*refs, n_x, chunk, prompt_tiles):
    x_refs = refs[:n_x]
    (gate_ref, yap_ref, yas_ref, gu_ref, gv_ref, vg_ref, ws_ref, bs_ref, wa_ref, wb_ref,
     o_ref, yb_ref) = refs[n_x:]
    tm = o_ref.shape[0]
    x = _token_tile(pl.program_id(0), prompt_tiles, x_refs)
    ya = _group_select(pl.program_id(0), prompt_tiles, yap_ref, yas_ref)
    v = gv_ref[...].astype(F32)
    vn = (v * lax.rsqrt(jnp.mean(v * v, axis=-1, keepdims=True) + EPS) * vg_ref[...]).astype(BF16)
    groups = ws_ref.shape[0]
    gd = vn.shape[1] // groups
    for c in range(tm // chunk):
        rows = slice(c * chunk, (c + 1) * chunk)
        for g in range(groups):
            cols = slice(g * gd, (g + 1) * gd)
            mixed = jnp.dot(ws_ref[g], vn[rows, cols], preferred_element_type=F32) + bs_ref[:, cols]
            yb_ref[rows, cols] = (gu_ref[rows, cols].astype(F32) * mixed).astype(BF16)
    y = (jnp.dot(ya, wa_ref[...], preferred_element_type=F32)
         + jnp.dot(yb_ref[...], wb_ref[...], preferred_element_type=F32))
    o_ref[...] = x + gate_ref[...] * y


def _even_out_proj(x_parts, mod, layer, ya_p, ya_s, gu, gv, v_gain, w_s, b_full, w_out, n_prompt, sample_seq):
    t = sum(p.shape[0] for p in x_parts)
    d = x_parts[0].shape[1]
    tm = OUT_PROJ_TILE
    w = ya_p.shape[1]
    row = lambda i: (i, 0)
    chunk = w_s.shape[1]
    prompt_tiles = n_prompt // tm
    return pl.pallas_call(
        functools.partial(_out0_body, n_x=len(x_parts), chunk=chunk, prompt_tiles=prompt_tiles),
        grid=(t // tm,),
        in_specs=[
            *_token_specs(x_parts, tm, prompt_tiles),
            _mod_spec(layer, 2, tm, n_prompt, sample_seq, d),
            *_group_specs((tm, w), prompt_tiles),
            pl.BlockSpec((tm, w), row),
            pl.BlockSpec((tm, w), row),
            _resident((1, w)),
            _resident(w_s.shape),
            _resident(b_full.shape),
            _resident((w, d)),
            _resident((w, d)),
        ],
        out_specs=pl.BlockSpec((tm, d), row),
        out_shape=jax.ShapeDtypeStruct((t, d), F32),
        scratch_shapes=[pltpu.VMEM((tm, w), BF16)],
        compiler_params=_cparams(("parallel",), 56),
        name="even_out_proj",
    )(*x_parts, mod, ya_p, ya_s, gu, gv, v_gain.reshape(1, w), w_s.astype(BF16), b_full,
      w_out[:w].astype(BF16), w_out[w:].astype(BF16))


def _swap_halves(x, lane):
    quarter = ROPE_QUARTER
    up = pltpu.roll(x, LANES - quarter, 1)
    down = pltpu.roll(x, quarter, 1)
    return jnp.where((lane % (2 * quarter)) < quarter, up, down)


def _mla_in_body(x_ref, g_ref, sh_ref, sc_ref, wd_ref, qan_ref, kvan_ref, wuq_ref, qn_ref, cos_ref, sin_ref,
                 q_ref, ckv_ref, kr_ref, *, heads, q_lora, kv_lora, qk_dim, sm_scale, sub):
    gn = qn_ref[0:1, :]
    gr = qn_ref[1:2, :]
    grs = qn_ref[2:3, :]
    lane = lax.broadcasted_iota(I32, (sub, LANES), 1)
    for r0 in range(0, x_ref.shape[0], sub):
        rows = pl.ds(r0, sub)
        h = _modulated(x_ref[rows, :], g_ref[...], sh_ref[...], sc_ref[...])
        z = jnp.dot(h.astype(BF16), wd_ref[...], preferred_element_type=F32)
        cq = z[:, :q_lora]
        cq = cq * lax.rsqrt(jnp.mean(cq * cq, axis=-1, keepdims=True) + EPS) * qan_ref[...]
        ckv = z[:, q_lora:q_lora + kv_lora]
        ckv_ref[rows, :] = ckv * lax.rsqrt(jnp.mean(ckv * ckv, axis=-1, keepdims=True) + EPS) * kvan_ref[...]
        kr_ref[rows, :] = z[:, q_lora + kv_lora:]
        q = jnp.dot(cq.astype(BF16), wuq_ref[...], preferred_element_type=F32)
        cos = gr * cos_ref[rows, :]
        sin = grs * sin_ref[rows, :]
        for hd in range(heads):
            qn = q[:, 2 * hd * LANES:(2 * hd + 1) * LANES]
            qr = q[:, (2 * hd + 1) * LANES:(2 * hd + 2) * LANES]
            ss = jnp.sum(qn * qn + qr * qr, axis=-1, keepdims=True)
            rinv = lax.rsqrt(ss * (1.0 / qk_dim) + EPS) * sm_scale
            q_ref[rows, 2 * hd * LANES:(2 * hd + 1) * LANES] = (qn * gn * rinv).astype(BF16)
            rot = qr * cos + _swap_halves(qr, lane) * sin
            q_ref[rows, (2 * hd + 1) * LANES:(2 * hd + 2) * LANES] = (rot * rinv).astype(BF16)


def _rope_spec(tile, n_prompt, sample_seq):
    def index(i):
        t = i * tile
        return (jnp.where(t < n_prompt, 0, 1 + ((t - n_prompt) % sample_seq) // tile), 0)

    return pl.BlockSpec((tile, LANES), index)


def _mla_in_proj(x, mod, layer, gain, wd_ext, q_a_norm, kv_a_norm, wuq_pad, qn_rows, cos_tab, sin_tab,
                 n_prompt, sample_seq, *, heads, q_lora, kv_lora, qk_dim):
    t, d = x.shape
    tm = MLA_TOKEN_TILE
    row = lambda i: (i, 0)
    ms = functools.partial(_mod_spec, layer, tile=tm, n_prompt=n_prompt, sample_seq=sample_seq, d=d)
    body = functools.partial(_mla_in_body, heads=heads, q_lora=q_lora, kv_lora=kv_lora, qk_dim=qk_dim,
                             sm_scale=float(qk_dim) ** -0.5, sub=TOKEN_TILE)
    return pl.pallas_call(
        body,
        grid=(t // tm,),
        in_specs=[
            pl.BlockSpec((tm, d), row),
            _resident((1, d)),
            ms(chunk=0),
            ms(chunk=1),
            _resident(wd_ext.shape),
            _resident((1, q_lora)),
            _resident((1, kv_lora)),
            _resident(wuq_pad.shape),
            _resident(qn_rows.shape),
            _rope_spec(tm, n_prompt, sample_seq),
            _rope_spec(tm, n_prompt, sample_seq),
        ],
        out_specs=[
            pl.BlockSpec((tm, heads * 2 * LANES), row),
            pl.BlockSpec((tm, kv_lora), row),
            pl.BlockSpec((tm, LANES), row),
        ],
        out_shape=[
            jax.ShapeDtypeStruct((t, heads * 2 * LANES), BF16),
            jax.ShapeDtypeStruct((t, kv_lora), F32),
            jax.ShapeDtypeStruct((t, LANES), F32),
        ],
        compiler_params=_cparams(("parallel",), 56),
        name="mla_in_proj",
    )(x, gain.reshape(1, d), mod, mod, wd_ext, q_a_norm.reshape(1, q_lora), kv_a_norm.reshape(1, kv_lora),
      wuq_pad, qn_rows, cos_tab, sin_tab)


def _kv_body(ckv_ref, kr_ref, w_ref, kn_ref, cos_ref, sin_ref, k_ref, v_ref, *, heads, qk_dim):
    kv = jnp.dot(ckv_ref[...].astype(BF16), w_ref[...], preferred_element_type=F32)
    tm = kv.shape[0]
    lane = lax.broadcasted_iota(I32, (tm, LANES), 1)
    kr = kr_ref[...]
    gn = kn_ref[0:1, :]
    gr = kn_ref[1:2, :]
    grs = kn_ref[2:3, :]
    ssr = jnp.sum(kr * kr, axis=-1, keepdims=True)
    rot = (kr * gr) * cos_ref[...] + (_swap_halves(kr, lane) * grs) * sin_ref[...]
    for hd in range(heads):
        kn = kv[:, hd * LANES:(hd + 1) * LANES]
        rinv = lax.rsqrt((jnp.sum(kn * kn, axis=-1, keepdims=True) + ssr) * (1.0 / qk_dim) + EPS)
        k_ref[:, 2 * hd * LANES:(2 * hd + 1) * LANES] = (kn * gn * rinv).astype(BF16)
        k_ref[:, (2 * hd + 1) * LANES:(2 * hd + 2) * LANES] = (rot * rinv).astype(BF16)
    v_ref[...] = kv[:, heads * LANES:].astype(BF16)


def _kv_expand(ckv, kr, w_ukv_re, kn_rows, cos_tab, sin_tab, rope_index, *, heads, qk_dim):
    rows, kv_lora = ckv.shape
    tm = TOKEN_TILE
    row = lambda i: (i, 0)
    return pl.pallas_call(
        functools.partial(_kv_body, heads=heads, qk_dim=qk_dim),
        grid=(rows // tm,),
        in_specs=[
            pl.BlockSpec((tm, kv_lora), row),
            pl.BlockSpec((tm, LANES), row),
            _resident(w_ukv_re.shape),
            _resident(kn_rows.shape),
            pl.BlockSpec((tm, LANES), rope_index),
            pl.BlockSpec((tm, LANES), rope_index),
        ],
        out_specs=[pl.BlockSpec((tm, heads * 2 * LANES), row), pl.BlockSpec((tm, heads * LANES), row)],
        out_shape=[
            jax.ShapeDtypeStruct((rows, heads * 2 * LANES), BF16),
            jax.ShapeDtypeStruct((rows, heads * LANES), BF16),
        ],
        compiler_params=_cparams(("parallel",), 40),
        name=f"mla_kv_expand_{rows}",
    )(ckv, kr, w_ukv_re, kn_rows, cos_tab, sin_tab)


def _attn_body(q_ref, k_ref, v_ref, o_ref, *, heads_per_step):
    for hd in range(heads_per_step):
        qk = slice(2 * hd * LANES, 2 * (hd + 1) * LANES)
        vo = slice(hd * LANES, (hd + 1) * LANES)
        s = lax.dot_general(q_ref[:, qk], k_ref[:, qk], (((1,), (1,)), ((), ())), preferred_element_type=F32)
        m = jnp.max(s, axis=-1, keepdims=True)
        p = jnp.exp(s - m)
        l = jnp.sum(p, axis=-1, keepdims=True)
        o = jnp.dot(p.astype(BF16), v_ref[:, vo], preferred_element_type=F32)
        o_ref[:, vo] = (o / l).astype(BF16)


def _attention(q, k, v, *, row0, batch, seq, kv_len, heads, tq, heads_per_step):
    nq = seq // tq
    qblk0 = row0 // tq
    hps = heads_per_step
    return pl.pallas_call(
        functools.partial(_attn_body, heads_per_step=hps),
        grid=(batch, heads // hps, nq),
        in_specs=[
            pl.BlockSpec((tq, hps * 2 * LANES), lambda b, h, i: (qblk0 + b * nq + i, h)),
            pl.BlockSpec((kv_len, hps * 2 * LANES), lambda b, h, i: (b, h)),
            pl.BlockSpec((kv_len, hps * LANES), lambda b, h, i: (b, h)),
        ],
        out_specs=pl.BlockSpec((tq, hps * LANES), lambda b, h, i: (b * nq + i, h)),
        out_shape=jax.ShapeDtypeStruct((batch * seq, heads * LANES), BF16),
        compiler_params=_cparams(("parallel", "parallel", "parallel"), 48),
        name=f"mla_attention_kv{kv_len}",
    )(q, k, v)


def _self_attn_body(q_ref, ckv_ref, kr_ref, w_ref, kn_ref, cos_ref, sin_ref, o_ref, k_scr, v_scr, *, heads, qk_dim):
    _kv_body(ckv_ref, kr_ref, w_ref, kn_ref, cos_ref, sin_ref, k_scr, v_scr, heads=heads, qk_dim=qk_dim)
    _attn_body(q_ref, k_scr, v_scr, o_ref, heads_per_step=heads)


def _self_attention(q, ckv, kr, w_ukv_re, kn_rows, cos_tab, sin_tab, *, batch, seq, heads, qk_dim):
    kv_lora = ckv.shape[1]
    tok = lambda b: (b, 0)
    return pl.pallas_call(
        functools.partial(_self_attn_body, heads=heads, qk_dim=qk_dim),
        grid=(batch,),
        in_specs=[
            pl.BlockSpec((seq, heads * 2 * LANES), tok),
            pl.BlockSpec((seq, kv_lora), tok),
            pl.BlockSpec((seq, LANES), tok),
            _resident(w_ukv_re.shape),
            _resident(kn_rows.shape),
            pl.BlockSpec((seq, LANES), lambda b: (0, 0)),
            pl.BlockSpec((seq, LANES), lambda b: (0, 0)),
        ],
        out_specs=pl.BlockSpec((seq, heads * LANES), tok),
        out_shape=jax.ShapeDtypeStruct((batch * seq, heads * LANES), BF16),
        scratch_shapes=[pltpu.VMEM((seq, heads * 2 * LANES), BF16), pltpu.VMEM((seq, heads * LANES), BF16)],
        compiler_params=_cparams(("parallel",), 40),
        name="mla_self_attention",
    )(q, ckv, kr, w_ukv_re, kn_rows, cos_tab, sin_tab)


def _oproj_body(x_ref, gate_ref, op_ref, os_ref, w_ref, out_ref, *, prompt_tiles):
    o = _group_select(pl.program_id(0), prompt_tiles, op_ref, os_ref)
    y = jnp.dot(o, w_ref[...], preferred_element_type=F32)
    out_ref[...] = x_ref[...] + gate_ref[...] * y


def _mla_out_proj(x, mod, layer, o_p, o_s, w_o, n_prompt, sample_seq):
    t, d = x.shape
    tm = OUT_PROJ_TILE
    row = lambda i: (i, 0)
    prompt_tiles = n_prompt // tm
    return pl.pallas_call(
        functools.partial(_oproj_body, prompt_tiles=prompt_tiles),
        grid=(t // tm,),
        in_specs=[
            pl.BlockSpec((tm, d), row),
            _mod_spec(layer, 2, tm, n_prompt, sample_seq, d),
            *_group_specs((tm, o_p.shape[1]), prompt_tiles),
            _resident(w_o.shape),
        ],
        out_specs=pl.BlockSpec((tm, d), row),
        out_shape=jax.ShapeDtypeStruct((t, d), F32),
        compiler_params=_cparams(("parallel",), 48),
        name="mla_out_proj",
    )(x, mod, o_p, o_s, w_o.astype(BF16))


def _round_up(v, m):
    return (v + m - 1) // m * m


def _local_rows(tile):
    return TOP_K * tile + RUN_PAD_ROWS


def _router_body(x_ref, g_ref, sh_ref, sc_ref, wst_ref, rb_ref, hb_ref, wt_ref, lp_ref, cnt_ref):
    h = _modulated(x_ref[...], g_ref[...], sh_ref[...], sc_ref[...])
    tm, d = h.shape
    n_experts = rb_ref.shape[0]

    h_hi = h.astype(BF16)
    hb_ref[...] = h_hi
    h_lo = (h - h_hi.astype(F32)).astype(BF16)
    nt = (((1,), (1,)), ((), ()))
    both = lax.dot_general(wst_ref[...], h_hi, nt, preferred_element_type=F32)
    logits = (both[:n_experts] + both[n_experts:]
              + lax.dot_general(wst_ref[:n_experts, :], h_lo, nt, preferred_element_type=F32))
    scores = jax.nn.sigmoid(logits)
    biased = scores + rb_ref[...]
    per_group = n_experts // N_EXPERT_GROUPS
    assert per_group == 4 and TOP_K == 2
    b_rows = [biased[e:e + 1, :] for e in range(n_experts)]
    s_rows = [scores[e:e + 1, :] for e in range(n_experts)]

    best = None
    sel = jnp.zeros((1, tm), I32)
    for g in range(N_EXPERT_GROUPS):
        b0, b1, b2, b3 = b_rows[g * 4:(g + 1) * 4]
        m1, n1 = jnp.maximum(b0, b1), jnp.minimum(b0, b1)
        m2, n2 = jnp.maximum(b2, b3), jnp.minimum(b2, b3)
        top1 = jnp.maximum(m1, m2)
        top2 = jnp.maximum(jnp.minimum(m1, m2), jnp.maximum(n1, n2))
        gsum = top1 + top2
        if best is None:
            best = gsum
        else:
            better = gsum > best
            sel = jnp.where(better, g, sel)
            best = jnp.where(better, gsum, best)

    def pick(rows, j):
        out = rows[j]
        for g in range(1, N_EXPERT_GROUPS):
            out = jnp.where(sel == g, rows[g * 4 + j], out)
        return out

    cand_b = [pick(b_rows, j) for j in range(4)]
    cand_s = [pick(s_rows, j) for j in range(4)]

    def argmax4(vals):
        bv, bi = vals[0], jnp.zeros((1, tm), I32)
        for j in range(1, 4):
            gt = vals[j] > bv
            bi = jnp.where(gt, j, bi)
            bv = jnp.where(gt, vals[j], bv)
        return bi

    i1 = argmax4(cand_b)
    i2 = argmax4([jnp.where(i1 == j, -jnp.inf, cand_b[j]) for j in range(4)])

    def take(vals, idx):
        out = vals[0]
        for j in range(1, 4):
            out = jnp.where(idx == j, vals[j], out)
        return out

    s1 = take(cand_s, i1)
    s2 = take(cand_s, i2)
    tot = s1 + s2
    e1 = sel * 4 + i1
    e2 = sel * 4 + i2
    wt_ref[0:1, :] = s1 / tot
    wt_ref[1:2, :] = s2 / tot

    eid = lax.broadcasted_iota(I32, (n_experts, tm), 0)
    is1 = eid == e1
    is2 = eid == e2
    chosen = jnp.where(is1 | is2, 1.0, 0.0)
    before = (lax.broadcasted_iota(I32, (tm, tm), 0) < lax.broadcasted_iota(I32, (tm, tm), 1))
    rank = jnp.dot(chosen.astype(BF16), jnp.where(before, 1.0, 0.0).astype(BF16), preferred_element_type=F32)
    count = jnp.sum(chosen, axis=1, keepdims=True)
    padded = jnp.floor((count + (RUN_ALIGN - 1)) * (1.0 / RUN_ALIGN)) * RUN_ALIGN
    lower = (lax.broadcasted_iota(I32, (n_experts, n_experts), 1)
             < lax.broadcasted_iota(I32, (n_experts, n_experts), 0))
    run_start = jnp.dot(jnp.where(lower, 1.0, 0.0).astype(BF16),
                        jnp.broadcast_to(padded, (n_experts, LANES)).astype(BF16),
                        preferred_element_type=F32)[:, 0:1]
    row = run_start + rank
    lp_ref[0:1, :] = jnp.sum(jnp.where(is1, row, 0.0), axis=0, keepdims=True).astype(I32)
    lp_ref[1:2, :] = jnp.sum(jnp.where(is2, row, 0.0), axis=0, keepdims=True).astype(I32)
    cnt_ref[...] = jnp.broadcast_to(count, (n_experts, LANES))


def _router(x, mod, layer, gain, w_stack, r_bias, n_prompt, sample_seq):
    t, d = x.shape
    tm = TOKEN_TILE
    n_experts = r_bias.shape[0]
    ms = functools.partial(_mod_spec, layer, tile=tm, n_prompt=n_prompt, sample_seq=sample_seq, d=d)
    col = lambda i: (0, i)
    return pl.pallas_call(
        _router_body,
        grid=(t // tm,),
        in_specs=[
            pl.BlockSpec((tm, d), lambda i: (i, 0)),
            _resident((1, d)),
            ms(chunk=3),
            ms(chunk=4),
            _resident(w_stack.shape),
            _resident((n_experts, 1)),
        ],
        out_specs=[
            pl.BlockSpec((tm, d), lambda i: (i, 0)),
            pl.BlockSpec((TOP_K, tm), col),
            pl.BlockSpec((TOP_K, tm), col),
            pl.BlockSpec((None, n_experts, LANES), lambda i: (i, 0, 0)),
        ],
        out_shape=[
            jax.ShapeDtypeStruct((t, d), BF16),
            jax.ShapeDtypeStruct((TOP_K, t), F32),
            jax.ShapeDtypeStruct((TOP_K, t), I32),
            jax.ShapeDtypeStruct((t // tm, n_experts, LANES), F32),
        ],
        compiler_params=_cparams(("parallel",), 40),
        name="moe_router",
    )(x, gain.reshape(1, d), mod, mod, w_stack, r_bias.reshape(n_experts, 1))


def _run_copies(tile, local_ref, slot, sorted_ref, row_ref, count_ref, sem, to_sorted):
    max_chunks = local_ref.shape[1] // RUN_ALIGN
    min_chunks = TOP_K * TOKEN_TILE // RUN_ALIGN

    def directed(loc, far):
        return pltpu.make_async_copy(loc, far, sem) if to_sorted else pltpu.make_async_copy(far, loc, sem)

    def chunk(c):
        loc = local_ref.at[slot, pl.ds(pl.multiple_of(c * RUN_ALIGN, RUN_ALIGN), RUN_ALIGN)]
        first = row_ref[tile * max_chunks + c]
        return directed(loc, sorted_ref.at[pl.ds(pl.multiple_of(first, RUN_ALIGN), RUN_ALIGN)])

    def apply(op):
        def one(c, carry):
            getattr(chunk(c), op)()
            return carry

        if op == "start":
            lax.fori_loop(0, min_chunks, one, 0, unroll=8)
        else:
            rows = min_chunks * RUN_ALIGN
            directed(local_ref.at[slot, pl.ds(0, rows)], sorted_ref.at[pl.ds(0, rows)]).wait()
        lax.fori_loop(min_chunks, count_ref[tile], one, 0)

    return apply


def _one_hot_rows(lp_ref, rows, tm):
    j = lax.broadcasted_iota(I32, (rows, tm), 0)
    hit = (j == lp_ref[0:1, :]) | (j == lp_ref[1:2, :])
    return jnp.where(hit, 1.0, 0.0).astype(BF16)


def _dispatch_body(row_ref, chunk_ref, last_ref, nv_ref, lp_ref, h_ref, o_ref, zero_ref, loc_ref, sems, zsem):
    tm = h_ref.shape[0]
    zt = zero_ref.shape[0]
    n_tiles = o_ref.shape[0] // zt
    i = pl.program_id(0)
    last = pl.num_programs(0) - 1
    slot = i % 2

    def copies(tile, s):
        return _run_copies(tile, loc_ref, s, o_ref, row_ref, chunk_ref, sems.at[s], True)

    @pl.when(i == 0)
    def _():
        zero_ref[...] = jnp.zeros(zero_ref.shape, zero_ref.dtype)

        def zero_tile(row):
            return pltpu.make_async_copy(zero_ref, o_ref.at[pl.ds(pl.multiple_of(row, zt), zt)], zsem)

        for e in range(last_ref.shape[0]):
            @pl.when(last_ref[e] >= 0)
            def _():
                zero_tile(last_ref[e]).start()

        def start_tail(j, c):
            zero_tile(j * zt).start()
            return c

        lax.fori_loop(nv_ref[0], n_tiles, start_tail, 0)
        for e in range(last_ref.shape[0]):
            @pl.when(last_ref[e] >= 0)
            def _():
                zero_tile(last_ref[e]).wait()

        def wait_tail(j, c):
            zero_tile(j * zt).wait()
            return c

        lax.fori_loop(nv_ref[0], n_tiles, wait_tail, 0)

    @pl.when(i >= 2)
    def _():
        copies(i - 2, slot)("wait")

    loc_ref[slot] = jnp.dot(_one_hot_rows(lp_ref, loc_ref.shape[1], tm), h_ref[...],
                            preferred_element_type=F32).astype(BF16)
    copies(i, slot)("start")

    @pl.when(i == last)
    def _():
        @pl.when(i >= 1)
        def _():
            copies(i - 1, 1 - slot)("wait")
        copies(i, slot)("wait")


def _dispatch(hb, lp, chunk_row, tile_chunks, last_tile_row, n_valid, sorted_rows):
    t, d = hb.shape
    tm = TOKEN_TILE
    grid_spec = pltpu.PrefetchScalarGridSpec(
        num_scalar_prefetch=4,
        grid=(t // tm,),
        in_specs=[
            pl.BlockSpec((TOP_K, tm), lambda i, *_: (0, i)),
            pl.BlockSpec((tm, d), lambda i, *_: (i, 0)),
        ],
        out_specs=pl.BlockSpec(memory_space=pl.ANY),
        scratch_shapes=[pltpu.VMEM((EXPERT_TILE, d), BF16), pltpu.VMEM((2, _local_rows(tm), d), BF16),
                        pltpu.SemaphoreType.DMA((2,)), pltpu.SemaphoreType.DMA],
    )
    return pl.pallas_call(
        _dispatch_body,
        grid_spec=grid_spec,
        out_shape=jax.ShapeDtypeStruct((sorted_rows, d), BF16),
        compiler_params=_cparams(("arbitrary",), 40),
        name="moe_dispatch",
    )(chunk_row, tile_chunks, last_tile_row, n_valid, lp, hb)


def _expert_body(te_ref, nv_ref, x_ref, wg_ref, wu_ref, wd_ref, y_ref, wgb_ref, wub_ref, wdb_ref):
    i = pl.program_id(0)
    valid = i < nv_ref[0]
    new_expert = (i == 0) | (te_ref[i] != te_ref[jnp.maximum(i - 1, 0)])

    @pl.when(valid & new_expert)
    def _():
        wgb_ref[...] = wg_ref[...].astype(BF16)
        wub_ref[...] = wu_ref[...].astype(BF16)
        wdb_ref[...] = wd_ref[...].astype(BF16)

    @pl.when(valid)
    def _():
        xb = x_ref[...]
        g = jnp.dot(xb, wgb_ref[...], preferred_element_type=F32)
        u = jnp.dot(xb, wub_ref[...], preferred_element_type=F32)
        act = (g * jax.nn.sigmoid(g) * u).astype(BF16)
        y_ref[...] = jnp.dot(act, wdb_ref[...], preferred_element_type=F32).astype(BF16)

    @pl.when(i >= nv_ref[0])
    def _():
        y_ref[...] = jnp.zeros(y_ref.shape, BF16)


def _experts(xs, tile_expert, n_valid, w_gate, w_up, w_down, layer):
    rows, d = xs.shape
    tm = EXPERT_TILE
    _, n_experts, _, f = w_gate.shape

    def xrow(i, te, nv):
        return (jnp.minimum(i, nv[0] - 1), 0)

    grid_spec = pltpu.PrefetchScalarGridSpec(
        num_scalar_prefetch=2,
        grid=(rows // tm,),
        in_specs=[
            pl.BlockSpec((tm, d), xrow),
            pl.BlockSpec((None, None, d, f), lambda i, te, nv: (layer, te[i], 0, 0)),
            pl.BlockSpec((None, None, d, f), lambda i, te, nv: (layer, te[i], 0, 0)),
            pl.BlockSpec((None, None, f, d), lambda i, te, nv: (layer, te[i], 0, 0)),
        ],
        out_specs=pl.BlockSpec((tm, d), lambda i, te, nv: (i, 0)),
        scratch_shapes=[pltpu.VMEM((d, f), BF16), pltpu.VMEM((d, f), BF16), pltpu.VMEM((f, d), BF16)],
    )
    return pl.pallas_call(
        _expert_body,
        grid_spec=grid_spec,
        out_shape=jax.ShapeDtypeStruct((rows, d), BF16),
        compiler_params=_cparams(("arbitrary",), 56),
        name="moe_experts",
    )(tile_expert, n_valid, xs, w_gate, w_up, w_down)


def _combine_body(row_ref, chunk_ref, lpt_ref, wtt_ref, x_ref, gate_ref, ys_ref, *rest, prompt_tiles):
    o_refs, (loc_ref, sems) = rest[:-2], rest[-2:]
    tm = x_ref.shape[0]
    rows = loc_ref.shape[1]
    i = pl.program_id(0)
    slot = i % 2

    def copies(tile, s):
        return _run_copies(tile, loc_ref, s, ys_ref, row_ref, chunk_ref, sems.at[s], False)

    @pl.when(i == 0)
    def _():
        loc_ref[...] = jnp.zeros(loc_ref.shape, loc_ref.dtype)
        copies(0, 0)("start")

    @pl.when(i + 1 < pl.num_programs(0))
    def _():
        copies(i + 1, 1 - slot)("start")

    copies(i, slot)("wait")

    jt = lax.broadcasted_iota(I32, (tm, rows), 1)
    local = loc_ref[slot]
    moe = jnp.zeros((tm, local.shape[1]), F32)
    for k in range(TOP_K):
        pick = jnp.where(jt == lpt_ref[:, k:k + 1], 1.0, 0.0).astype(BF16)
        moe = moe + wtt_ref[:, k:k + 1] * jnp.dot(pick, local, preferred_element_type=F32)
    out = x_ref[...] + gate_ref[...] * moe
    if len(o_refs) == 1:
        o_refs[0][...] = out
    else:
        for o_ref, own in zip(o_refs, (i < prompt_tiles, i >= prompt_tiles)):
            @pl.when(own)
            def _():
                o_ref[...] = out


def _combine(x, mod, layer, ys, lp, wt, chunk_row, tile_chunks, n_prompt, sample_seq, split):
    t, d = x.shape
    tm = TOKEN_TILE
    prompt_tiles = n_prompt // tm
    tok = lambda i, *_: (i, 0)
    col = lambda i, *_: (0, i)
    if split:
        out_specs = _group_specs((tm, d), prompt_tiles)
        out_shape = [jax.ShapeDtypeStruct((n_prompt, d), F32), jax.ShapeDtypeStruct((t - n_prompt, d), F32)]
    else:
        out_specs = [pl.BlockSpec((tm, d), tok)]
        out_shape = [jax.ShapeDtypeStruct((t, d), F32)]
    grid_spec = pltpu.PrefetchScalarGridSpec(
        num_scalar_prefetch=2,
        grid=(t // tm,),
        in_specs=[
            pl.BlockSpec((tm, TOP_K), tok),
            pl.BlockSpec((tm, TOP_K), tok),
            pl.BlockSpec((tm, d), tok),
            _mod_spec(layer, 5, tm, n_prompt, sample_seq, d),
            pl.BlockSpec(memory_space=pl.ANY),
        ],
        out_specs=out_specs,
        scratch_shapes=[pltpu.VMEM((2, _local_rows(tm), d), BF16), pltpu.SemaphoreType.DMA((2,))],
    )
    return pl.pallas_call(
        functools.partial(_combine_body, prompt_tiles=prompt_tiles),
        grid_spec=grid_spec,
        out_shape=out_shape,
        compiler_params=_cparams(("arbitrary",), 40),
        name="moe_combine",
    )(chunk_row, tile_chunks, lp.T, wt.T, x, mod, ys)


def _moe(x, mod, layer, gain, w_stack, r_bias, w_gate, w_up, w_down, n_prompt, sample_seq, split):
    t, d = x.shape
    n_experts = r_bias.shape[0]
    assert n_experts * (RUN_ALIGN - 1) <= RUN_PAD_ROWS
    hb, wt, lp, cnt = _router(x, mod, layer, gain, w_stack, r_bias, n_prompt, sample_seq)
    n_tok_tiles = t // TOKEN_TILE
    run_len = _round_up(cnt[:, :, 0].astype(I32), RUN_ALIGN)
    run_before = jnp.cumsum(run_len, axis=0) - run_len
    tiles = (jnp.sum(run_len, axis=0) + EXPERT_TILE - 1) // EXPERT_TILE
    tile_end = jnp.cumsum(tiles)
    seg_start = (tile_end - tiles) * EXPERT_TILE
    run_row = seg_start[None, :] + run_before
    run_chunks = run_len // RUN_ALIGN
    chunk_end = jnp.cumsum(run_chunks, axis=1)
    chunk_ids = jnp.arange(_local_rows(TOKEN_TILE) // RUN_ALIGN, dtype=I32)
    owner = jnp.sum((chunk_end[:, None, :] <= chunk_ids[None, :, None]).astype(I32), axis=2)
    owner = jnp.minimum(owner, n_experts - 1)
    is_owner = owner[:, :, None] == jnp.arange(n_experts, dtype=I32)
    pick = lambda a: jnp.sum(jnp.where(is_owner, a[:, None, :], 0), axis=2)
    chunk_row = pick(run_row) + (chunk_ids[None, :] - pick(chunk_end - run_chunks)) * RUN_ALIGN
    chunk_row = chunk_row.reshape(-1).astype(I32)
    tile_chunks = chunk_end[:, -1].astype(I32)
    sorted_rows = (_round_up(TOP_K * t + n_tok_tiles * n_experts * (RUN_ALIGN - 1), EXPERT_TILE)
                   + n_experts * EXPERT_TILE)
    n_tiles = sorted_rows // EXPERT_TILE
    n_valid = tile_end[-1:]
    tile_ids = jnp.arange(n_tiles, dtype=I32)
    ends_before = lambda i: jnp.sum((tile_end[None, :] <= i[:, None]).astype(I32), axis=1)
    tile_expert = jnp.where(tile_ids < n_valid[0], ends_before(tile_ids), ends_before(n_valid - 1))
    tile_expert = jnp.minimum(tile_expert, n_experts - 1).astype(I32)
    last_tile_row = jnp.where(tiles > 0, (tile_end - 1) * EXPERT_TILE, -1).astype(I32)
    n_valid = n_valid.astype(I32)
    xs = _dispatch(hb, lp, chunk_row, tile_chunks, last_tile_row, n_valid, sorted_rows)
    ys = _experts(xs, tile_expert, n_valid, w_gate, w_up, w_down, layer)
    return _combine(x, mod, layer, ys, lp, wt, chunk_row, tile_chunks, n_prompt, sample_seq, split)


def _rope_tables(sample_seq, rope_dim, tile):
    n_freq = rope_dim // 4
    pos = jnp.arange(sample_seq)
    row_pos = (pos // GRID_W).astype(F32)
    col_pos = (pos % GRID_W).astype(F32)
    inv_freq = ROPE_BASE ** (-jnp.arange(n_freq, dtype=F32) / n_freq)
    ar = row_pos[:, None] * inv_freq
    ac = col_pos[:, None] * inv_freq
    zeros = jnp.zeros((sample_seq, LANES - rope_dim), F32)
    cos = jnp.concatenate([jnp.cos(ar), jnp.cos(ar), jnp.cos(ac), jnp.cos(ac), zeros], axis=1)
    sin = jnp.concatenate([-jnp.sin(ar), jnp.sin(ar), -jnp.sin(ac), jnp.sin(ac), zeros], axis=1)
    ident_c = jnp.concatenate([jnp.ones((tile, rope_dim), F32), jnp.zeros((tile, LANES - rope_dim), F32)], axis=1)
    return (jnp.concatenate([ident_c, cos], axis=0),
            jnp.concatenate([jnp.zeros((tile, LANES), F32), sin], axis=0))


def _norm_rows(gain, nope, rope_dim):
    quarter = rope_dim // 4
    gr = gain[nope:]
    grs = jnp.concatenate([gr[quarter:2 * quarter], gr[:quarter], gr[3 * quarter:], gr[2 * quarter:3 * quarter]])
    zpad = jnp.zeros((LANES - rope_dim,), F32)
    return jnp.stack([gain[:nope], jnp.concatenate([gr, zpad]), jnp.concatenate([grs, zpad])])


def kernel(x_prompt, x_sample, state_lru_fwd, state_lru_bwd, cache_mla_ckv, cache_mla_krope, c, c_ctx,
           ada_w, ada_b, norm_mix, norm_ffn, mix0_w_in, mix0_w_out, lru_conv_w, lru_conv_b,
           lru_w_r, lru_b_r, lru_w_i, lru_b_i, lru_lambda, gmlp_v_norm, gmlp_w_s, gmlp_b_s,
           mla_w_down, mla_q_a_norm, mla_kv_a_norm, mla_w_uq, mla_w_ukv, mla_q_norm, mla_k_norm, mla_w_o,
           router_w, router_bias, moe_w_gate, moe_w_up, moe_w_down):
    batch, seq, d = x_prompt.shape
    dec_batch, dec_seq, _ = x_sample.shape
    depth = ada_w.shape[0]
    n_prompt = batch * seq
    n_sample = dec_batch * dec_seq
    assert n_prompt % dec_seq == 0 and seq % TOKEN_TILE == 0 and dec_seq % TOKEN_TILE == 0
    assert 1 + dec_batch <= SUBLANES

    x_parts = (x_prompt.reshape(n_prompt, d), x_sample.reshape(n_sample, d))

    cond = jnp.concatenate([c_ctx[None, :], c, jnp.zeros((SUBLANES - 1 - dec_batch, d), F32)], axis=0)
    mod = _modulation(cond, ada_w, ada_b).reshape(depth, SUBLANES, 6, 1, d)

    rw_t = router_w.T
    rw_hi = rw_t.astype(BF16)
    rw_stack = jnp.concatenate([rw_hi, (rw_t - rw_hi.astype(F32)).astype(BF16)], axis=0)

    fwd_states, bwd_states, ckv_caches, krope_caches = [], [], [], []
    for layer in range(depth):
        j = layer // 2
        if layer % 2 == 0:
            width = lru_conv_w.shape[2]
            heads = lru_w_r.shape[2]
            xb, gg, gu, gv = _even_in_proj(x_parts, mod, layer, norm_mix[layer], mix0_w_in[j], n_prompt, dec_seq)
            wcat = jnp.concatenate([lru_w_r[j, 0], lru_w_i[j, 0], lru_w_r[j, 1], lru_w_i[j, 1]], axis=-1).astype(BF16)
            hb = lambda v: v.reshape(heads, 1, width // heads)
            bcat = jnp.concatenate([hb(lru_b_r[j, 0]), hb(lru_b_i[j, 0]), hb(lru_b_r[j, 1]), hb(lru_b_i[j, 1])], axis=-1)
            lru_args = (lru_conv_w[j], lru_conv_b[j].reshape(1, width), wcat, bcat, lru_lambda[j])
            zero_state = jnp.zeros((batch, 1, width), F32)
            nseq_p = LRU_PROMPT_SEQS_PER_STEP if batch % LRU_PROMPT_SEQS_PER_STEP == 0 else 1
            ya_p, fin_f, fin_b = _lru_mixer(xb, gg, zero_state, zero_state, *lru_args,
                                            row0=0, batch=batch, seq=seq, nseq=nseq_p)
            ya_s, _, _ = _lru_mixer(xb, gg, state_lru_fwd[:, j][:, None, :], state_lru_bwd[:, j][:, None, :],
                                    *lru_args, row0=n_prompt, batch=dec_batch, seq=dec_seq, nseq=1)
            fwd_states.append(fin_f[:, 0, :])
            bwd_states.append(fin_b[:, 0, :])
            groups, chunk, _ = gmlp_w_s[j].shape
            gd = width // groups
            b_full = jnp.repeat(gmlp_b_s[j].T, gd, axis=1)
            x = _even_out_proj(x_parts, mod, layer, ya_p, ya_s, gu, gv, gmlp_v_norm[j], gmlp_w_s[j], b_full,
                               mix0_w_out[j], n_prompt, dec_seq)
        else:
            q_lora = mla_q_a_norm.shape[1]
            kv_lora = mla_kv_a_norm.shape[1]
            qk_dim = mla_q_norm.shape[1]
            rope_dim = cache_mla_krope.shape[-1]
            nope = qk_dim - rope_dim
            heads = mla_w_uq.shape[2] // qk_dim
            v_dim = mla_w_ukv.shape[2] // heads - nope
            past = cache_mla_ckv.shape[2]
            assert nope == LANES and v_dim == LANES and rope_dim == 4 * ROPE_QUARTER
            wd = mla_w_down[j]
            wd_ext = jnp.concatenate([wd, jnp.zeros((d, LANES - rope_dim), F32)], axis=1).astype(BF16)
            wuq = mla_w_uq[j].reshape(q_lora, heads, qk_dim)
            wuq_pad = jnp.concatenate([wuq, jnp.zeros((q_lora, heads, 2 * LANES - qk_dim), F32)], axis=-1)
            wuq_pad = wuq_pad.reshape(q_lora, heads * 2 * LANES).astype(BF16)
            wukv = mla_w_ukv[j].reshape(kv_lora, heads, nope + v_dim)
            w_ukv_re = jnp.concatenate([wukv[:, :, :nope].reshape(kv_lora, heads * nope),
                                        wukv[:, :, nope:].reshape(kv_lora, heads * v_dim)], axis=1).astype(BF16)
            qn_rows = _norm_rows(mla_q_norm[j], nope, rope_dim)
            kn_rows = _norm_rows(mla_k_norm[j], nope, rope_dim)
            cos_tab, sin_tab = _rope_tables(dec_seq, rope_dim, MLA_TOKEN_TILE)
            x = x_parts[0] if len(x_parts) == 1 else jnp.concatenate(x_parts, axis=0)
            q, ckv, kr = _mla_in_proj(x, mod, layer, norm_mix[layer], wd_ext, mla_q_a_norm[j], mla_kv_a_norm[j],
                                      wuq_pad, qn_rows, cos_tab, sin_tab, n_prompt, dec_seq,
                                      heads=heads, q_lora=q_lora, kv_lora=kv_lora, qk_dim=qk_dim)
            ckv_caches.append(ckv[:n_prompt].reshape(batch, seq, kv_lora))
            krope_caches.append(kr[:n_prompt, :rope_dim].reshape(batch, seq, rope_dim))
            expand = functools.partial(_kv_expand, w_ukv_re=w_ukv_re, kn_rows=kn_rows, cos_tab=cos_tab,
                                       sin_tab=sin_tab, heads=heads, qk_dim=qk_dim)
            assert seq <= MLA_TOKEN_TILE
            o_p = _self_attention(q, ckv, kr, w_ukv_re, kn_rows, cos_tab, sin_tab, batch=batch, seq=seq,
                                  heads=heads, qk_dim=qk_dim)
            kv_len = past + dec_seq
            ctx_kr = jnp.pad(cache_mla_krope[:, j], ((0, 0), (0, 0), (0, LANES - rope_dim)))
            ckv_s = jnp.concatenate([cache_mla_ckv[:, j], ckv[n_prompt:].reshape(dec_batch, dec_seq, kv_lora)], axis=1)
            kr_s = jnp.concatenate([ctx_kr, kr[n_prompt:].reshape(dec_batch, dec_seq, LANES)], axis=1)
            tiles_per_req = kv_len // TOKEN_TILE
            ctx_tiles = past // TOKEN_TILE
            assert past % TOKEN_TILE == 0

            def latent_rope(i):
                r = i % tiles_per_req
                return (jnp.where(r < ctx_tiles, 0, MLA_TOKEN_TILE // TOKEN_TILE + r - ctx_tiles), 0)

            k_s, v_s = expand(ckv_s.reshape(dec_batch * kv_len, kv_lora), kr_s.reshape(dec_batch * kv_len, LANES),
                              rope_index=latent_rope)
            o_s = _attention(q, k_s, v_s, row0=n_prompt, batch=dec_batch, seq=dec_seq, kv_len=kv_len,
                             heads=heads, tq=ATTN_Q_TILE, heads_per_step=ATTN_LATENT_HEADS_PER_STEP)
            x = _mla_out_proj(x, mod, layer, o_p, o_s, mla_w_o[j], n_prompt, dec_seq)
        x_parts = _moe(x, mod, layer, norm_ffn[layer], rw_stack, router_bias, moe_w_gate, moe_w_up, moe_w_down,
                       n_prompt, dec_seq, split=layer == depth - 1)

    xp = x_parts[0].reshape(batch, seq, d)
    xs = x_parts[1].reshape(dec_batch, dec_seq, d)
    return (xp, xs,
            jnp.stack(fwd_states, axis=1), jnp.stack(bwd_states, axis=1),
            jnp.stack(ckv_caches, axis=1), jnp.stack(krope_caches, axis=1))
```

```python
import functools

import jax
import jax.numpy as jnp
from jax import lax
from jax.experimental import pallas as pl
from jax.experimental.pallas import tpu as pltpu

F32 = jnp.float32
BF16 = jnp.bfloat16
I32 = jnp.int32

EPS = 1e-6
LRU_C = 8.0
GRID_W = 64
ROPE_BASE = 10000.0
ROPE_QUARTER = 16
N_EXPERT_GROUPS = 4
TOP_K = 2

LANES = 128
SUBLANES = 8
VMEM_BYTES_V7X = 64 * 1024 * 1024

TOKEN_TILE = 256
MLA_TOKEN_TILE = 512
EXPERT_TILE = 512
MOD_COL_TILE = 1024
ATTN_Q_TILE = 512
ATTN_LATENT_HEADS_PER_STEP = 4
OUT_PROJ_TILE = 512
LRU_GATE_CHUNK = 256
RUN_ALIGN = 16
RUN_PAD_ROWS = 256
LRU_PROMPT_SEQS_PER_STEP = 8
LRU_LATENT_SEQS_PER_STEP = 2


def _cparams(semantics, vmem_mb):
    return pltpu.CompilerParams(dimension_semantics=semantics, vmem_limit_bytes=vmem_mb * 1024 * 1024)


def _resident(shape):
    nd = len(shape)
    return pl.BlockSpec(shape, lambda *_: (0,) * nd, pipeline_mode=pl.Buffered(1))


def _mod_spec(layer, chunk, tile, n_prompt, sample_seq, d):
    def index(i, *_):
        t = i * tile
        row = jnp.where(t < n_prompt, 0, 1 + (t - n_prompt) // sample_seq)
        return (layer, row, chunk, 0, 0)

    return pl.BlockSpec((None, None, None, 1, d), index)


def _modulated(x, gain, shift, scale):
    y = x * lax.rsqrt(jnp.mean(x * x, axis=-1, keepdims=True) + EPS)
    return (y * gain) * (1.0 + scale) + shift


def _mod_body(c_ref, w_ref, b_ref, o_ref):
    c = c_ref[...]
    s = (c * jax.nn.sigmoid(c)).astype(BF16)
    o_ref[...] = jnp.dot(s, w_ref[...].astype(BF16), preferred_element_type=F32) + b_ref[...]


def _modulation(cond, ada_w, ada_b):
    depth, d, n = ada_w.shape
    tn = MOD_COL_TILE
    return pl.pallas_call(
        _mod_body,
        grid=(depth, n // tn),
        in_specs=[
            pl.BlockSpec((SUBLANES, d), lambda l, j: (0, 0)),
            pl.BlockSpec((None, d, tn), lambda l, j: (l, 0, j)),
            pl.BlockSpec((None, 1, tn), lambda l, j: (l, 0, j)),
        ],
        out_specs=pl.BlockSpec((None, SUBLANES, tn), lambda l, j: (l, 0, j)),
        out_shape=jax.ShapeDtypeStruct((depth, SUBLANES, n), F32),
        compiler_params=_cparams(("parallel", "parallel"), 40),
        name="adaln_projection",
    )(cond, ada_w, ada_b.reshape(depth, 1, n))


def _group_select(i, prompt_tiles, p_ref, s_ref):
    return jnp.where(i < prompt_tiles, p_ref[...], s_ref[...])


def _group_specs(block, prompt_tiles):
    return [pl.BlockSpec(block, lambda i, *_: (jnp.minimum(i, prompt_tiles - 1), 0)),
            pl.BlockSpec(block, lambda i, *_: (jnp.maximum(i - prompt_tiles, 0), 0))]


def _token_specs(x_parts, tile, prompt_tiles):
    d = x_parts[0].shape[1]
    if len(x_parts) == 1:
        return [pl.BlockSpec((tile, d), lambda i, *_: (i, 0))]
    return _group_specs((tile, d), prompt_tiles)


def _token_tile(i, prompt_tiles, x_refs):
    return x_refs[0][...] if len(x_refs) == 1 else _group_select(i, prompt_tiles, *x_refs)


def _in0_body(*refs, n_x, prompt_tiles):
    x_refs = refs[:n_x]
    g_ref, sh_ref, sc_ref, w_ref, xb_ref, gg_ref, gu_ref, gv_ref = refs[n_x:]
    x = _token_tile(pl.program_id(0), prompt_tiles, x_refs)
    h = _modulated(x, g_ref[...], sh_ref[...], sc_ref[...])
    z = jnp.dot(h.astype(BF16), w_ref[...], preferred_element_type=F32)
    w = xb_ref.shape[1]
    xb_ref[...] = z[:, :w]
    gg_ref[...] = jax.nn.gelu(z[:, w:2 * w]).astype(BF16)
    gu_ref[...] = jax.nn.gelu(z[:, 2 * w:3 * w]).astype(BF16)
    gv_ref[...] = jax.nn.gelu(z[:, 3 * w:]).astype(BF16)


def _even_in_proj(x_parts, mod, layer, gain, w_in, n_prompt, sample_seq):
    t = sum(p.shape[0] for p in x_parts)
    d = x_parts[0].shape[1]
    tm = TOKEN_TILE
    w4 = w_in.shape[1]
    w = w4 // 4
    row = lambda i: (i, 0)
    ms = functools.partial(_mod_spec, layer, tile=tm, n_prompt=n_prompt, sample_seq=sample_seq, d=d)
    return pl.pallas_call(
        functools.partial(_in0_body, n_x=len(x_parts), prompt_tiles=n_prompt // tm),
        grid=(t // tm,),
        in_specs=[
            *_token_specs(x_parts, tm, n_prompt // tm),
            _resident((1, d)),
            ms(chunk=0),
            ms(chunk=1),
            _resident((d, w4)),
        ],
        out_specs=[pl.BlockSpec((tm, w), row)] * 4,
        out_shape=[
            jax.ShapeDtypeStruct((t, w), F32),
            jax.ShapeDtypeStruct((t, w), BF16),
            jax.ShapeDtypeStruct((t, w), BF16),
            jax.ShapeDtypeStruct((t, w), BF16),
        ],
        compiler_params=_cparams(("parallel",), 48),
        name="even_in_proj",
    )(*x_parts, gain.reshape(1, d), mod, mod, w_in.astype(BF16))


def _tile_scan(a, b, row, reverse):
    for d in (1, 2, 4):
        if reverse:
            keep = row < SUBLANES - d
            a_s = jnp.where(keep, pltpu.roll(a, SUBLANES - d, 0), 1.0)
            b_s = jnp.where(keep, pltpu.roll(b, SUBLANES - d, 0), 0.0)
        else:
            keep = row >= d
            a_s = jnp.where(keep, pltpu.roll(a, d, 0), 1.0)
            b_s = jnp.where(keep, pltpu.roll(b, d, 0), 0.0)
        b = b + a * b_s
        a = a * a_s
    return a, b


def _sigmoid(x):
    return 0.5 * jnp.tanh(0.5 * x) + 0.5


def _lru_body(xb_ref, gg_ref, h0f_ref, h0b_ref, cw_ref, cb_ref, w_ref, bias_ref, lam_ref,
              ya_ref, ff_ref, fb_ref,
              xp_ref, af_ref, bf_ref, ab_ref, bb_ref, hf_ref, hb_ref, *, seq, chunk, nseq):
    hw = LANES
    pad = SUBLANES
    pitch = seq + 2 * pad
    for s in range(nseq):
        xp_ref[pl.ds(s * pitch, pad), :] = jnp.zeros((pad, hw), F32)
        xp_ref[pl.ds(s * pitch + pad + seq, pad), :] = jnp.zeros((pad, hw), F32)
        xp_ref[pl.ds(s * pitch + pad, seq), :] = xb_ref[pl.ds(s * seq, seq), :]

    cw = cw_ref[...]
    cb = cb_ref[...]
    lam = lam_ref[...]
    neg = -lam
    softplus = jnp.maximum(neg, 0.0) + jnp.log1p(jnp.exp(-jnp.abs(neg)))
    nsp = -LRU_C * softplus
    w = w_ref[...]
    bias = bias_ref[...]

    for s in range(nseq):
        for c in range(seq // chunk):
            src = s * pitch + pad + c * chunk
            dst = s * seq + c * chunk
            xc = (cw[0:1] * xp_ref[pl.ds(src - 2, chunk), :]
                  + cw[1:2] * xp_ref[pl.ds(src - 1, chunk), :]
                  + cw[2:3] * xp_ref[pl.ds(src, chunk), :]
                  + cw[3:4] * xp_ref[pl.ds(src + 1, chunk), :]) + cb
            g = jnp.dot(xc.astype(BF16), w, preferred_element_type=F32) + bias
            for direction, (a_ref, b_ref) in enumerate(((af_ref, bf_ref), (ab_ref, bb_ref))):
                r = _sigmoid(g[:, (2 * direction) * hw:(2 * direction + 1) * hw])
                gi = _sigmoid(g[:, (2 * direction + 1) * hw:(2 * direction + 2) * hw])
                log_a = r * nsp[direction:direction + 1]
                a = jnp.exp(log_a)
                a_ref[pl.ds(dst, chunk), :] = a
                b_ref[pl.ds(dst, chunk), :] = jnp.sqrt(1.0 - a * a) * gi * xc

    n_tiles = seq // SUBLANES
    row = lax.broadcasted_iota(I32, (SUBLANES, hw), 0)

    def step(i, carry):
        new = []
        for s in range(nseq):
            hf, hb = carry[2 * s], carry[2 * s + 1]
            rf = pl.multiple_of(s * seq + i * SUBLANES, SUBLANES)
            rb = pl.multiple_of(s * seq + (n_tiles - 1 - i) * SUBLANES, SUBLANES)
            a, b = _tile_scan(af_ref[pl.ds(rf, SUBLANES), :], bf_ref[pl.ds(rf, SUBLANES), :], row, False)
            h = b + a * hf
            hf_ref[pl.ds(rf, SUBLANES), :] = h
            new.append(jnp.broadcast_to(h[SUBLANES - 1:SUBLANES, :], (SUBLANES, hw)))
            a, b = _tile_scan(ab_ref[pl.ds(rb, SUBLANES), :], bb_ref[pl.ds(rb, SUBLANES), :], row, True)
            h = b + a * hb
            hb_ref[pl.ds(rb, SUBLANES), :] = h
            new.append(jnp.broadcast_to(h[0:1, :], (SUBLANES, hw)))
        return tuple(new)

    init = []
    for s in range(nseq):
        init.append(jnp.broadcast_to(h0f_ref[s], (SUBLANES, hw)))
        init.append(jnp.broadcast_to(h0b_ref[s], (SUBLANES, hw)))
    final = lax.fori_loop(0, n_tiles, step, tuple(init), unroll=2 if nseq == 1 else 1)
    for s in range(nseq):
        ff_ref[s] = final[2 * s][0:1, :]
        fb_ref[s] = final[2 * s + 1][0:1, :]
    ya_ref[...] = ((hf_ref[...] + hb_ref[...]) * gg_ref[...].astype(F32)).astype(BF16)


def _lru_mixer(xb, gg, h0f, h0b, conv_w, conv_b, wcat, bcat, lam, *, row0, batch, seq, nseq):
    _, width = xb.shape
    heads = width // LANES
    rows = nseq * seq
    assert batch % nseq == 0 and row0 % rows == 0
    blk0 = row0 // rows
    tok = lambda b, h: (blk0 + b, h)
    state = lambda b, h: (b, 0, h)
    seq_buf = pltpu.VMEM((rows, LANES), F32)
    return pl.pallas_call(
        functools.partial(_lru_body, seq=seq, chunk=min(seq, LRU_GATE_CHUNK), nseq=nseq),
        grid=(batch // nseq, heads),
        in_specs=[
            pl.BlockSpec((rows, LANES), tok),
            pl.BlockSpec((rows, LANES), tok),
            pl.BlockSpec((nseq, 1, LANES), state),
            pl.BlockSpec((nseq, 1, LANES), state),
            pl.BlockSpec((conv_w.shape[0], LANES), lambda b, h: (0, h)),
            pl.BlockSpec((1, LANES), lambda b, h: (0, h)),
            pl.BlockSpec((None, LANES, 4 * LANES), lambda b, h: (h, 0, 0)),
            pl.BlockSpec((None, 1, 4 * LANES), lambda b, h: (h, 0, 0)),
            pl.BlockSpec((2, LANES), lambda b, h: (0, h)),
        ],
        out_specs=[
            pl.BlockSpec((rows, LANES), lambda b, h: (b, h)),
            pl.BlockSpec((nseq, 1, LANES), state),
            pl.BlockSpec((nseq, 1, LANES), state),
        ],
        out_shape=[
            jax.ShapeDtypeStruct((batch * seq, width), BF16),
            jax.ShapeDtypeStruct((batch, 1, width), F32),
            jax.ShapeDtypeStruct((batch, 1, width), F32),
        ],
        scratch_shapes=[pltpu.VMEM((nseq * (seq + 2 * SUBLANES), LANES), F32)] + [seq_buf] * 6,
        compiler_params=_cparams(("parallel", "parallel"), 40),
        name=f"rglru_seq{seq}",
    )(xb, gg, h0f, h0b, conv_w, conv_b, wcat, bcat, lam)


def _out0_body(*refs, n_x, chunk, prompt_tiles):
    x_refs = refs[:n_x]
    (gate_ref, yap_ref, yas_ref, gu_ref, gv_ref, vg_ref, ws_ref, bs_ref, wa_ref, wb_ref,
     o_ref, yb_ref) = refs[n_x:]
    tm = o_ref.shape[0]
    x = _token_tile(pl.program_id(0), prompt_tiles, x_refs)
    ya = _group_select(pl.program_id(0), prompt_tiles, yap_ref, yas_ref)
    v = gv_ref[...].astype(F32)
    vn = (v * lax.rsqrt(jnp.mean(v * v, axis=-1, keepdims=True) + EPS) * vg_ref[...]).astype(BF16)
    groups = ws_ref.shape[0]
    gd = vn.shape[1] // groups
    for c in range(tm // chunk):
        rows = slice(c * chunk, (c + 1) * chunk)
        for g in range(groups):
            cols = slice(g * gd, (g + 1) * gd)
            mixed = jnp.dot(ws_ref[g], vn[rows, cols], preferred_element_type=F32) + bs_ref[:, cols]
            yb_ref[rows, cols] = (gu_ref[rows, cols].astype(F32) * mixed).astype(BF16)
    y = (jnp.dot(ya, wa_ref[...], preferred_element_type=F32)
         + jnp.dot(yb_ref[...], wb_ref[...], preferred_element_type=F32))
    o_ref[...] = x + gate_ref[...] * y


def _even_out_proj(x_parts, mod, layer, ya_p, ya_s, gu, gv, v_gain, w_s, b_full, w_out, n_prompt, sample_seq):
    t = sum(p.shape[0] for p in x_parts)
    d = x_parts[0].shape[1]
    tm = OUT_PROJ_TILE
    w = ya_p.shape[1]
    row = lambda i: (i, 0)
    chunk = w_s.shape[1]
    prompt_tiles = n_prompt // tm
    return pl.pallas_call(
        functools.partial(_out0_body, n_x=len(x_parts), chunk=chunk, prompt_tiles=prompt_tiles),
        grid=(t // tm,),
        in_specs=[
            *_token_specs(x_parts, tm, prompt_tiles),
            _mod_spec(layer, 2, tm, n_prompt, sample_seq, d),
            *_group_specs((tm, w), prompt_tiles),
            pl.BlockSpec((tm, w), row),
            pl.BlockSpec((tm, w), row),
            _resident((1, w)),
            _resident(w_s.shape),
            _resident(b_full.shape),
            _resident((w, d)),
            _resident((w, d)),
        ],
        out_specs=pl.BlockSpec((tm, d), row),
        out_shape=jax.ShapeDtypeStruct((t, d), F32),
        scratch_shapes=[pltpu.VMEM((tm, w), BF16)],
        compiler_params=_cparams(("parallel",), 56),
        name="even_out_proj",
    )(*x_parts, mod, ya_p, ya_s, gu, gv, v_gain.reshape(1, w), w_s.astype(BF16), b_full,
      w_out[:w].astype(BF16), w_out[w:].astype(BF16))


def _swap_halves(x, lane):
    quarter = ROPE_QUARTER
    up = pltpu.roll(x, LANES - quarter, 1)
    down = pltpu.roll(x, quarter, 1)
    return jnp.where((lane % (2 * quarter)) < quarter, up, down)


def _mla_in_body(x_ref, g_ref, sh_ref, sc_ref, wd_ref, qan_ref, kvan_ref, wuq_ref, qn_ref, cos_ref, sin_ref,
                 q_ref, ckv_ref, kr_ref, *, heads, q_lora, kv_lora, qk_dim, sm_scale, sub):
    gn = qn_ref[0:1, :]
    gr = qn_ref[1:2, :]
    grs = qn_ref[2:3, :]
    lane = lax.broadcasted_iota(I32, (sub, LANES), 1)
    for r0 in range(0, x_ref.shape[0], sub):
        rows = pl.ds(r0, sub)
        h = _modulated(x_ref[rows, :], g_ref[...], sh_ref[...], sc_ref[...])
        z = jnp.dot(h.astype(BF16), wd_ref[...], preferred_element_type=F32)
        cq = z[:, :q_lora]
        cq = cq * lax.rsqrt(jnp.mean(cq * cq, axis=-1, keepdims=True) + EPS) * qan_ref[...]
        ckv = z[:, q_lora:q_lora + kv_lora]
        ckv_ref[rows, :] = ckv * lax.rsqrt(jnp.mean(ckv * ckv, axis=-1, keepdims=True) + EPS) * kvan_ref[...]
        kr_ref[rows, :] = z[:, q_lora + kv_lora:]
        q = jnp.dot(cq.astype(BF16), wuq_ref[...], preferred_element_type=F32)
        cos = gr * cos_ref[rows, :]
        sin = grs * sin_ref[rows, :]
        for hd in range(heads):
            qn = q[:, 2 * hd * LANES:(2 * hd + 1) * LANES]
            qr = q[:, (2 * hd + 1) * LANES:(2 * hd + 2) * LANES]
            ss = jnp.sum(qn * qn + qr * qr, axis=-1, keepdims=True)
            rinv = lax.rsqrt(ss * (1.0 / qk_dim) + EPS) * sm_scale
            q_ref[rows, 2 * hd * LANES:(2 * hd + 1) * LANES] = (qn * gn * rinv).astype(BF16)
            rot = qr * cos + _swap_halves(qr, lane) * sin
            q_ref[rows, (2 * hd + 1) * LANES:(2 * hd + 2) * LANES] = (rot * rinv).astype(BF16)


def _rope_spec(tile, n_prompt, sample_seq):
    def index(i):
        t = i * tile
        return (jnp.where(t < n_prompt, 0, 1 + ((t - n_prompt) % sample_seq) // tile), 0)

    return pl.BlockSpec((tile, LANES), index)


def _mla_in_proj(x, mod, layer, gain, wd_ext, q_a_norm, kv_a_norm, wuq_pad, qn_rows, cos_tab, sin_tab,
                 n_prompt, sample_seq, *, heads, q_lora, kv_lora, qk_dim):
    t, d = x.shape
    tm = MLA_TOKEN_TILE
    row = lambda i: (i, 0)
    ms = functools.partial(_mod_spec, layer, tile=tm, n_prompt=n_prompt, sample_seq=sample_seq, d=d)
    body = functools.partial(_mla_in_body, heads=heads, q_lora=q_lora, kv_lora=kv_lora, qk_dim=qk_dim,
                             sm_scale=float(qk_dim) ** -0.5, sub=TOKEN_TILE)
    return pl.pallas_call(
        body,
        grid=(t // tm,),
        in_specs=[
            pl.BlockSpec((tm, d), row),
            _resident((1, d)),
            ms(chunk=0),
            ms(chunk=1),
            _resident(wd_ext.shape),
            _resident((1, q_lora)),
            _resident((1, kv_lora)),
            _resident(wuq_pad.shape),
            _resident(qn_rows.shape),
            _rope_spec(tm, n_prompt, sample_seq),
            _rope_spec(tm, n_prompt, sample_seq),
        ],
        out_specs=[
            pl.BlockSpec((tm, heads * 2 * LANES), row),
            pl.BlockSpec((tm, kv_lora), row),
            pl.BlockSpec((tm, LANES), row),
        ],
        out_shape=[
            jax.ShapeDtypeStruct((t, heads * 2 * LANES), BF16),
            jax.ShapeDtypeStruct((t, kv_lora), F32),
            jax.ShapeDtypeStruct((t, LANES), F32),
        ],
        compiler_params=_cparams(("parallel",), 56),
        name="mla_in_proj",
    )(x, gain.reshape(1, d), mod, mod, wd_ext, q_a_norm.reshape(1, q_lora), kv_a_norm.reshape(1, kv_lora),
      wuq_pad, qn_rows, cos_tab, sin_tab)


def _kv_body(ckv_ref, kr_ref, w_ref, kn_ref, cos_ref, sin_ref, k_ref, v_ref, *, heads, qk_dim):
    kv = jnp.dot(ckv_ref[...].astype(BF16), w_ref[...], preferred_element_type=F32)
    tm = kv.shape[0]
    lane = lax.broadcasted_iota(I32, (tm, LANES), 1)
    kr = kr_ref[...]
    gn = kn_ref[0:1, :]
    gr = kn_ref[1:2, :]
    grs = kn_ref[2:3, :]
    ssr = jnp.sum(kr * kr, axis=-1, keepdims=True)
    rot = (kr * gr) * cos_ref[...] + (_swap_halves(kr, lane) * grs) * sin_ref[...]
    for hd in range(heads):
        kn = kv[:, hd * LANES:(hd + 1) * LANES]
        rinv = lax.rsqrt((jnp.sum(kn * kn, axis=-1, keepdims=True) + ssr) * (1.0 / qk_dim) + EPS)
        k_ref[:, 2 * hd * LANES:(2 * hd + 1) * LANES] = (kn * gn * rinv).astype(BF16)
        k_ref[:, (2 * hd + 1) * LANES:(2 * hd + 2) * LANES] = (rot * rinv).astype(BF16)
    v_ref[...] = kv[:, heads * LANES:].astype(BF16)


def _kv_expand(ckv, kr, w_ukv_re, kn_rows, cos_tab, sin_tab, rope_index, *, heads, qk_dim):
    rows, kv_lora = ckv.shape
    tm = TOKEN_TILE
    row = lambda i: (i, 0)
    return pl.pallas_call(
        functools.partial(_kv_body, heads=heads, qk_dim=qk_dim),
        grid=(rows // tm,),
        in_specs=[
            pl.BlockSpec((tm, kv_lora), row),
            pl.BlockSpec((tm, LANES), row),
            _resident(w_ukv_re.shape),
            _resident(kn_rows.shape),
            pl.BlockSpec((tm, LANES), rope_index),
            pl.BlockSpec((tm, LANES), rope_index),
        ],
        out_specs=[pl.BlockSpec((tm, heads * 2 * LANES), row), pl.BlockSpec((tm, heads * LANES), row)],
        out_shape=[
            jax.ShapeDtypeStruct((rows, heads * 2 * LANES), BF16),
            jax.ShapeDtypeStruct((rows, heads * LANES), BF16),
        ],
        compiler_params=_cparams(("parallel",), 40),
        name=f"mla_kv_expand_{rows}",
    )(ckv, kr, w_ukv_re, kn_rows, cos_tab, sin_tab)


def _attn_body(q_ref, k_ref, v_ref, o_ref, *, heads_per_step):
    for hd in range(heads_per_step):
        qk = slice(2 * hd * LANES, 2 * (hd + 1) * LANES)
        vo = slice(hd * LANES, (hd + 1) * LANES)
        s = lax.dot_general(q_ref[:, qk], k_ref[:, qk], (((1,), (1,)), ((), ())), preferred_element_type=F32)
        m = jnp.max(s, axis=-1, keepdims=True)
        p = jnp.exp(s - m)
        l = jnp.sum(p, axis=-1, keepdims=True)
        o = jnp.dot(p.astype(BF16), v_ref[:, vo], preferred_element_type=F32)
        o_ref[:, vo] = (o / l).astype(BF16)


def _attention(q, k, v, *, row0, batch, seq, kv_len, heads, tq, heads_per_step):
    nq = seq // tq
    qblk0 = row0 // tq
    hps = heads_per_step
    return pl.pallas_call(
        functools.partial(_attn_body, heads_per_step=hps),
        grid=(batch, heads // hps, nq),
        in_specs=[
            pl.BlockSpec((tq, hps * 2 * LANES), lambda b, h, i: (qblk0 + b * nq + i, h)),
            pl.BlockSpec((kv_len, hps * 2 * LANES), lambda b, h, i: (b, h)),
            pl.BlockSpec((kv_len, hps * LANES), lambda b, h, i: (b, h)),
        ],
        out_specs=pl.BlockSpec((tq, hps * LANES), lambda b, h, i: (b * nq + i, h)),
        out_shape=jax.ShapeDtypeStruct((batch * seq, heads * LANES), BF16),
        compiler_params=_cparams(("parallel", "parallel", "parallel"), 48),
        name=f"mla_attention_kv{kv_len}",
    )(q, k, v)


def _self_attn_body(q_ref, ckv_ref, kr_ref, w_ref, kn_ref, cos_ref, sin_ref, o_ref, k_scr, v_scr, *, heads, qk_dim):
    _kv_body(ckv_ref, kr_ref, w_ref, kn_ref, cos_ref, sin_ref, k_scr, v_scr, heads=heads, qk_dim=qk_dim)
    _attn_body(q_ref, k_scr, v_scr, o_ref, heads_per_step=heads)


def _self_attention(q, ckv, kr, w_ukv_re, kn_rows, cos_tab, sin_tab, *, batch, seq, heads, qk_dim):
    kv_lora = ckv.shape[1]
    tok = lambda b: (b, 0)
    return pl.pallas_call(
        functools.partial(_self_attn_body, heads=heads, qk_dim=qk_dim),
        grid=(batch,),
        in_specs=[
            pl.BlockSpec((seq, heads * 2 * LANES), tok),
            pl.BlockSpec((seq, kv_lora), tok),
            pl.BlockSpec((seq, LANES), tok),
            _resident(w_ukv_re.shape),
            _resident(kn_rows.shape),
            pl.BlockSpec((seq, LANES), lambda b: (0, 0)),
            pl.BlockSpec((seq, LANES), lambda b: (0, 0)),
        ],
        out_specs=pl.BlockSpec((seq, heads * LANES), tok),
        out_shape=jax.ShapeDtypeStruct((batch * seq, heads * LANES), BF16),
        scratch_shapes=[pltpu.VMEM((seq, heads * 2 * LANES), BF16), pltpu.VMEM((seq, heads * LANES), BF16)],
        compiler_params=_cparams(("parallel",), 40),
        name="mla_self_attention",
    )(q, ckv, kr, w_ukv_re, kn_rows, cos_tab, sin_tab)


def _oproj_body(x_ref, gate_ref, op_ref, os_ref, w_ref, out_ref, *, prompt_tiles):
    o = _group_select(pl.program_id(0), prompt_tiles, op_ref, os_ref)
    y = jnp.dot(o, w_ref[...], preferred_element_type=F32)
    out_ref[...] = x_ref[...] + gate_ref[...] * y


def _mla_out_proj(x, mod, layer, o_p, o_s, w_o, n_prompt, sample_seq):
    t, d = x.shape
    tm = OUT_PROJ_TILE
    row = lambda i: (i, 0)
    prompt_tiles = n_prompt // tm
    return pl.pallas_call(
        functools.partial(_oproj_body, prompt_tiles=prompt_tiles),
        grid=(t // tm,),
        in_specs=[
            pl.BlockSpec((tm, d), row),
            _mod_spec(layer, 2, tm, n_prompt, sample_seq, d),
            *_group_specs((tm, o_p.shape[1]), prompt_tiles),
            _resident(w_o.shape),
        ],
        out_specs=pl.BlockSpec((tm, d), row),
        out_shape=jax.ShapeDtypeStruct((t, d), F32),
        compiler_params=_cparams(("parallel",), 48),
        name="mla_out_proj",
    )(x, mod, o_p, o_s, w_o.astype(BF16))


def _round_up(v, m):
    return (v + m - 1) // m * m


def _local_rows(tile):
    return TOP_K * tile + RUN_PAD_ROWS


def _router_body(x_ref, g_ref, sh_ref, sc_ref, wst_ref, rb_ref, hb_ref, wt_ref, lp_ref, cnt_ref):
    h = _modulated(x_ref[...], g_ref[...], sh_ref[...], sc_ref[...])
    tm, d = h.shape
    n_experts = rb_ref.shape[0]

    h_hi = h.astype(BF16)
    hb_ref[...] = h_hi
    h_lo = (h - h_hi.astype(F32)).astype(BF16)
    nt = (((1,), (1,)), ((), ()))
    both = lax.dot_general(wst_ref[...], h_hi, nt, preferred_element_type=F32)
    logits = (both[:n_experts] + both[n_experts:]
              + lax.dot_general(wst_ref[:n_experts, :], h_lo, nt, preferred_element_type=F32))
    scores = jax.nn.sigmoid(logits)
    biased = scores + rb_ref[...]
    per_group = n_experts // N_EXPERT_GROUPS
    assert per_group == 4 and TOP_K == 2
    b_rows = [biased[e:e + 1, :] for e in range(n_experts)]
    s_rows = [scores[e:e + 1, :] for e in range(n_experts)]

    best = None
    sel = jnp.zeros((1, tm), I32)
    for g in range(N_EXPERT_GROUPS):
        b0, b1, b2, b3 = b_rows[g * 4:(g + 1) * 4]
        m1, n1 = jnp.maximum(b0, b1), jnp.minimum(b0, b1)
        m2, n2 = jnp.maximum(b2, b3), jnp.minimum(b2, b3)
        top1 = jnp.maximum(m1, m2)
        top2 = jnp.maximum(jnp.minimum(m1, m2), jnp.maximum(n1, n2))
        gsum = top1 + top2
        if best is None:
            best = gsum
        else:
            better = gsum > best
            sel = jnp.where(better, g, sel)
            best = jnp.where(better, gsum, best)

    def pick(rows, j):
        out = rows[j]
        for g in range(1, N_EXPERT_GROUPS):
            out = jnp.where(sel == g, rows[g * 4 + j], out)
        return out

    cand_b = [pick(b_rows, j) for j in range(4)]
    cand_s = [pick(s_rows, j) for j in range(4)]

    def argmax4(vals):
        bv, bi = vals[0], jnp.zeros((1, tm), I32)
        for j in range(1, 4):
            gt = vals[j] > bv
            bi = jnp.where(gt, j, bi)
            bv = jnp.where(gt, vals[j], bv)
        return bi

    i1 = argmax4(cand_b)
    i2 = argmax4([jnp.where(i1 == j, -jnp.inf, cand_b[j]) for j in range(4)])

    def take(vals, idx):
        out = vals[0]
        for j in range(1, 4):
            out = jnp.where(idx == j, vals[j], out)
        return out

    s1 = take(cand_s, i1)
    s2 = take(cand_s, i2)
    tot = s1 + s2
    e1 = sel * 4 + i1
    e2 = sel * 4 + i2
    wt_ref[0:1, :] = s1 / tot
    wt_ref[1:2, :] = s2 / tot

    eid = lax.broadcasted_iota(I32, (n_experts, tm), 0)
    is1 = eid == e1
    is2 = eid == e2
    chosen = jnp.where(is1 | is2, 1.0, 0.0)
    before = (lax.broadcasted_iota(I32, (tm, tm), 0) < lax.broadcasted_iota(I32, (tm, tm), 1))
    rank = jnp.dot(chosen.astype(BF16), jnp.where(before, 1.0, 0.0).astype(BF16), preferred_element_type=F32)
    count = jnp.sum(chosen, axis=1, keepdims=True)
    padded = jnp.floor((count + (RUN_ALIGN - 1)) * (1.0 / RUN_ALIGN)) * RUN_ALIGN
    lower = (lax.broadcasted_iota(I32, (n_experts, n_experts), 1)
             < lax.broadcasted_iota(I32, (n_experts, n_experts), 0))
    run_start = jnp.dot(jnp.where(lower, 1.0, 0.0).astype(BF16),
                        jnp.broadcast_to(padded, (n_experts, LANES)).astype(BF16),
                        preferred_element_type=F32)[:, 0:1]
    row = run_start + rank
    lp_ref[0:1, :] = jnp.sum(jnp.where(is1, row, 0.0), axis=0, keepdims=True).astype(I32)
    lp_ref[1:2, :] = jnp.sum(jnp.where(is2, row, 0.0), axis=0, keepdims=True).astype(I32)
    cnt_ref[...] = jnp.broadcast_to(count, (n_experts, LANES))


def _router(x, mod, layer, gain, w_stack, r_bias, n_prompt, sample_seq):
    t, d = x.shape
    tm = TOKEN_TILE
    n_experts = r_bias.shape[0]
    ms = functools.partial(_mod_spec, layer, tile=tm, n_prompt=n_prompt, sample_seq=sample_seq, d=d)
    col = lambda i: (0, i)
    return pl.pallas_call(
        _router_body,
        grid=(t // tm,),
        in_specs=[
            pl.BlockSpec((tm, d), lambda i: (i, 0)),
            _resident((1, d)),
            ms(chunk=3),
            ms(chunk=4),
            _resident(w_stack.shape),
            _resident((n_experts, 1)),
        ],
        out_specs=[
            pl.BlockSpec((tm, d), lambda i: (i, 0)),
            pl.BlockSpec((TOP_K, tm), col),
            pl.BlockSpec((TOP_K, tm), col),
            pl.BlockSpec((None, n_experts, LANES), lambda i: (i, 0, 0)),
        ],
        out_shape=[
            jax.ShapeDtypeStruct((t, d), BF16),
            jax.ShapeDtypeStruct((TOP_K, t), F32),
            jax.ShapeDtypeStruct((TOP_K, t), I32),
            jax.ShapeDtypeStruct((t // tm, n_experts, LANES), F32),
        ],
        compiler_params=_cparams(("parallel",), 40),
        name="moe_router",
    )(x, gain.reshape(1, d), mod, mod, w_stack, r_bias.reshape(n_experts, 1))


def _run_copies(tile, local_ref, slot, sorted_ref, row_ref, count_ref, sem, to_sorted):
    max_chunks = local_ref.shape[1] // RUN_ALIGN
    min_chunks = TOP_K * TOKEN_TILE // RUN_ALIGN

    def directed(loc, far):
        return pltpu.make_async_copy(loc, far, sem) if to_sorted else pltpu.make_async_copy(far, loc, sem)

    def chunk(c):
        loc = local_ref.at[slot, pl.ds(pl.multiple_of(c * RUN_ALIGN, RUN_ALIGN), RUN_ALIGN)]
        first = row_ref[tile * max_chunks + c]
        return directed(loc, sorted_ref.at[pl.ds(pl.multiple_of(first, RUN_ALIGN), RUN_ALIGN)])

    def apply(op):
        def one(c, carry):
            getattr(chunk(c), op)()
            return carry

        if op == "start":
            lax.fori_loop(0, min_chunks, one, 0, unroll=8)
        else:
            rows = min_chunks * RUN_ALIGN
            directed(local_ref.at[slot, pl.ds(0, rows)], sorted_ref.at[pl.ds(0, rows)]).wait()
        lax.fori_loop(min_chunks, count_ref[tile], one, 0)

    return apply


def _one_hot_rows(lp_ref, rows, tm):
    j = lax.broadcasted_iota(I32, (rows, tm), 0)
    hit = (j == lp_ref[0:1, :]) | (j == lp_ref[1:2, :])
    return jnp.where(hit, 1.0, 0.0).astype(BF16)


def _dispatch_body(row_ref, chunk_ref, last_ref, nv_ref, lp_ref, h_ref, o_ref, zero_ref, loc_ref, sems, zsem):
    tm = h_ref.shape[0]
    zt = zero_ref.shape[0]
    n_tiles = o_ref.shape[0] // zt
    i = pl.program_id(0)
    last = pl.num_programs(0) - 1
    slot = i % 2

    def copies(tile, s):
        return _run_copies(tile, loc_ref, s, o_ref, row_ref, chunk_ref, sems.at[s], True)

    @pl.when(i == 0)
    def _():
        zero_ref[...] = jnp.zeros(zero_ref.shape, zero_ref.dtype)

        def zero_tile(row):
            return pltpu.make_async_copy(zero_ref, o_ref.at[pl.ds(pl.multiple_of(row, zt), zt)], zsem)

        for e in range(last_ref.shape[0]):
            @pl.when(last_ref[e] >= 0)
            def _():
                zero_tile(last_ref[e]).start()

        def start_tail(j, c):
            zero_tile(j * zt).start()
            return c

        lax.fori_loop(nv_ref[0], n_tiles, start_tail, 0)
        for e in range(last_ref.shape[0]):
            @pl.when(last_ref[e] >= 0)
            def _():
                zero_tile(last_ref[e]).wait()

        def wait_tail(j, c):
            zero_tile(j * zt).wait()
            return c

        lax.fori_loop(nv_ref[0], n_tiles, wait_tail, 0)

    @pl.when(i >= 2)
    def _():
        copies(i - 2, slot)("wait")

    loc_ref[slot] = jnp.dot(_one_hot_rows(lp_ref, loc_ref.shape[1], tm), h_ref[...],
                            preferred_element_type=F32).astype(BF16)
    copies(i, slot)("start")

    @pl.when(i == last)
    def _():
        @pl.when(i >= 1)
        def _():
            copies(i - 1, 1 - slot)("wait")
        copies(i, slot)("wait")


def _dispatch(hb, lp, chunk_row, tile_chunks, last_tile_row, n_valid, sorted_rows):
    t, d = hb.shape
    tm = TOKEN_TILE
    grid_spec = pltpu.PrefetchScalarGridSpec(
        num_scalar_prefetch=4,
        grid=(t // tm,),
        in_specs=[
            pl.BlockSpec((TOP_K, tm), lambda i, *_: (0, i)),
            pl.BlockSpec((tm, d), lambda i, *_: (i, 0)),
        ],
        out_specs=pl.BlockSpec(memory_space=pl.ANY),
        scratch_shapes=[pltpu.VMEM((EXPERT_TILE, d), BF16), pltpu.VMEM((2, _local_rows(tm), d), BF16),
                        pltpu.SemaphoreType.DMA((2,)), pltpu.SemaphoreType.DMA],
    )
    return pl.pallas_call(
        _dispatch_body,
        grid_spec=grid_spec,
        out_shape=jax.ShapeDtypeStruct((sorted_rows, d), BF16),
        compiler_params=_cparams(("arbitrary",), 40),
        name="moe_dispatch",
    )(chunk_row, tile_chunks, last_tile_row, n_valid, lp, hb)


def _expert_body(te_ref, nv_ref, nxt_ref, par_ref, x_ref, wg_hbm, wu_hbm, wd_hbm, y_ref,
                 wgf_ref, wuf_ref, wdf_ref, wgb_ref, wub_ref, wdb_ref, sems, *, layer):
    i = pl.program_id(0)
    valid = i < nv_ref[0]
    new_expert = (i == 0) | (te_ref[i] != te_ref[jnp.maximum(i - 1, 0)])

    def weight_copies(e, slot):
        return [pltpu.make_async_copy(hbm.at[layer, e], buf.at[slot], sems.at[slot])
                for hbm, buf in ((wg_hbm, wgf_ref), (wu_hbm, wuf_ref), (wd_hbm, wdf_ref))]

    @pl.when(valid & new_expert)
    def _():
        slot = par_ref[i]

        @pl.when(i == 0)
        def _():
            for c in weight_copies(te_ref[0], slot):
                c.start()

        for c in weight_copies(te_ref[i], slot):
            c.wait()

        @pl.when(nxt_ref[i] >= 0)
        def _():
            for c in weight_copies(nxt_ref[i], 1 - slot):
                c.start()

        wgb_ref[...] = wgf_ref[slot].astype(BF16)
        wub_ref[...] = wuf_ref[slot].astype(BF16)
        wdb_ref[...] = wdf_ref[slot].astype(BF16)

    @pl.when(valid)
    def _():
        xb = x_ref[...]
        g = jnp.dot(xb, wgb_ref[...], preferred_element_type=F32)
        u = jnp.dot(xb, wub_ref[...], preferred_element_type=F32)
        act = (g * jax.nn.sigmoid(g) * u).astype(BF16)
        y_ref[...] = jnp.dot(act, wdb_ref[...], preferred_element_type=F32).astype(BF16)

    @pl.when(i >= nv_ref[0])
    def _():
        y_ref[...] = jnp.zeros(y_ref.shape, BF16)


def _experts(xs, tile_expert, n_valid, tile_next, tile_slot, w_gate, w_up, w_down, layer):
    rows, d = xs.shape
    tm = EXPERT_TILE
    _, n_experts, _, f = w_gate.shape

    def xrow(i, te, nv, *_):
        return (jnp.minimum(i, nv[0] - 1), 0)

    hbm = pl.BlockSpec(memory_space=pl.ANY)
    grid_spec = pltpu.PrefetchScalarGridSpec(
        num_scalar_prefetch=4,
        grid=(rows // tm,),
        in_specs=[pl.BlockSpec((tm, d), xrow), hbm, hbm, hbm],
        out_specs=pl.BlockSpec((tm, d), lambda i, *_: (i, 0)),
        scratch_shapes=[pltpu.VMEM((2, d, f), F32), pltpu.VMEM((2, d, f), F32), pltpu.VMEM((2, f, d), F32),
                        pltpu.VMEM((d, f), BF16), pltpu.VMEM((d, f), BF16), pltpu.VMEM((f, d), BF16),
                        pltpu.SemaphoreType.DMA((2,))],
    )
    return pl.pallas_call(
        functools.partial(_expert_body, layer=layer),
        grid_spec=grid_spec,
        out_shape=jax.ShapeDtypeStruct((rows, d), BF16),
        compiler_params=_cparams(("arbitrary",), 56),
        name="moe_experts",
    )(tile_expert, n_valid, tile_next, tile_slot, xs, w_gate, w_up, w_down)


def _combine_body(row_ref, chunk_ref, lpt_ref, wtt_ref, x_ref, gate_ref, ys_ref, *rest, prompt_tiles):
    o_refs, (loc_ref, sems) = rest[:-2], rest[-2:]
    tm = x_ref.shape[0]
    rows = loc_ref.shape[1]
    i = pl.program_id(0)
    slot = i % 2

    def copies(tile, s):
        return _run_copies(tile, loc_ref, s, ys_ref, row_ref, chunk_ref, sems.at[s], False)

    @pl.when(i == 0)
    def _():
        loc_ref[...] = jnp.zeros(loc_ref.shape, loc_ref.dtype)
        copies(0, 0)("start")

    @pl.when(i + 1 < pl.num_programs(0))
    def _():
        copies(i + 1, 1 - slot)("start")

    copies(i, slot)("wait")

    jt = lax.broadcasted_iota(I32, (tm, rows), 1)
    local = loc_ref[slot]
    moe = jnp.zeros((tm, local.shape[1]), F32)
    for k in range(TOP_K):
        pick = jnp.where(jt == lpt_ref[:, k:k + 1], 1.0, 0.0).astype(BF16)
        moe = moe + wtt_ref[:, k:k + 1] * jnp.dot(pick, local, preferred_element_type=F32)
    out = x_ref[...] + gate_ref[...] * moe
    if len(o_refs) == 1:
        o_refs[0][...] = out
    else:
        for o_ref, own in zip(o_refs, (i < prompt_tiles, i >= prompt_tiles)):
            @pl.when(own)
            def _():
                o_ref[...] = out


def _combine(x, mod, layer, ys, lp, wt, chunk_row, tile_chunks, n_prompt, sample_seq, split):
    t, d = x.shape
    tm = TOKEN_TILE
    prompt_tiles = n_prompt // tm
    tok = lambda i, *_: (i, 0)
    col = lambda i, *_: (0, i)
    if split:
        out_specs = _group_specs((tm, d), prompt_tiles)
        out_shape = [jax.ShapeDtypeStruct((n_prompt, d), F32), jax.ShapeDtypeStruct((t - n_prompt, d), F32)]
    else:
        out_specs = [pl.BlockSpec((tm, d), tok)]
        out_shape = [jax.ShapeDtypeStruct((t, d), F32)]
    grid_spec = pltpu.PrefetchScalarGridSpec(
        num_scalar_prefetch=2,
        grid=(t // tm,),
        in_specs=[
            pl.BlockSpec((tm, TOP_K), tok),
            pl.BlockSpec((tm, TOP_K), tok),
            pl.BlockSpec((tm, d), tok),
            _mod_spec(layer, 5, tm, n_prompt, sample_seq, d),
            pl.BlockSpec(memory_space=pl.ANY),
        ],
        out_specs=out_specs,
        scratch_shapes=[pltpu.VMEM((2, _local_rows(tm), d), BF16), pltpu.SemaphoreType.DMA((2,))],
    )
    return pl.pallas_call(
        functools.partial(_combine_body, prompt_tiles=prompt_tiles),
        grid_spec=grid_spec,
        out_shape=out_shape,
        compiler_params=_cparams(("arbitrary",), 40),
        name="moe_combine",
    )(chunk_row, tile_chunks, lp.T, wt.T, x, mod, ys)


def _moe(x, mod, layer, gain, w_stack, r_bias, w_gate, w_up, w_down, n_prompt, sample_seq, split):
    t, d = x.shape
    n_experts = r_bias.shape[0]
    assert n_experts * (RUN_ALIGN - 1) <= RUN_PAD_ROWS
    hb, wt, lp, cnt = _router(x, mod, layer, gain, w_stack, r_bias, n_prompt, sample_seq)
    n_tok_tiles = t // TOKEN_TILE
    run_len = _round_up(cnt[:, :, 0].astype(I32), RUN_ALIGN)
    run_before = jnp.cumsum(run_len, axis=0) - run_len
    tiles = (jnp.sum(run_len, axis=0) + EXPERT_TILE - 1) // EXPERT_TILE
    tile_end = jnp.cumsum(tiles)
    seg_start = (tile_end - tiles) * EXPERT_TILE
    run_row = seg_start[None, :] + run_before
    run_chunks = run_len // RUN_ALIGN
    chunk_end = jnp.cumsum(run_chunks, axis=1)
    chunk_ids = jnp.arange(_local_rows(TOKEN_TILE) // RUN_ALIGN, dtype=I32)
    owner = jnp.sum((chunk_end[:, None, :] <= chunk_ids[None, :, None]).astype(I32), axis=2)
    owner = jnp.minimum(owner, n_experts - 1)
    is_owner = owner[:, :, None] == jnp.arange(n_experts, dtype=I32)
    pick = lambda a: jnp.sum(jnp.where(is_owner, a[:, None, :], 0), axis=2)
    chunk_row = pick(run_row) + (chunk_ids[None, :] - pick(chunk_end - run_chunks)) * RUN_ALIGN
    chunk_row = chunk_row.reshape(-1).astype(I32)
    tile_chunks = chunk_end[:, -1].astype(I32)
    sorted_rows = (_round_up(TOP_K * t + n_tok_tiles * n_experts * (RUN_ALIGN - 1), EXPERT_TILE)
                   + n_experts * EXPERT_TILE)
    n_tiles = sorted_rows // EXPERT_TILE
    n_valid = tile_end[-1:]
    tile_ids = jnp.arange(n_tiles, dtype=I32)
    ends_before = lambda i: jnp.sum((tile_end[None, :] <= i[:, None]).astype(I32), axis=1)
    tile_expert = jnp.where(tile_ids < n_valid[0], ends_before(tile_ids), ends_before(n_valid - 1))
    tile_expert = jnp.minimum(tile_expert, n_experts - 1).astype(I32)
    last_tile_row = jnp.where(tiles > 0, (tile_end - 1) * EXPERT_TILE, -1).astype(I32)
    n_valid = n_valid.astype(I32)
    e_ids = jnp.arange(n_experts, dtype=I32)
    used = tiles > 0
    later_used = (e_ids[None, :] > e_ids[:, None]) & used[None, :]
    next_used = jnp.min(jnp.where(later_used, e_ids[None, :], n_experts), axis=1)
    next_used = jnp.where(next_used < n_experts, next_used, -1)
    slot_of = (jnp.cumsum(used.astype(I32)) - 1) % 2
    of_tile = lambda a: jnp.sum(jnp.where(tile_expert[:, None] == e_ids[None, :], a[None, :], 0), axis=1).astype(I32)
    xs = _dispatch(hb, lp, chunk_row, tile_chunks, last_tile_row, n_valid, sorted_rows)
    ys = _experts(xs, tile_expert, n_valid, of_tile(next_used), of_tile(slot_of), w_gate, w_up, w_down, layer)
    return _combine(x, mod, layer, ys, lp, wt, chunk_row, tile_chunks, n_prompt, sample_seq, split)


def _rope_tables(sample_seq, rope_dim, tile):
    n_freq = rope_dim // 4
    pos = jnp.arange(sample_seq)
    row_pos = (pos // GRID_W).astype(F32)
    col_pos = (pos % GRID_W).astype(F32)
    inv_freq = ROPE_BASE ** (-jnp.arange(n_freq, dtype=F32) / n_freq)
    ar = row_pos[:, None] * inv_freq
    ac = col_pos[:, None] * inv_freq
    zeros = jnp.zeros((sample_seq, LANES - rope_dim), F32)
    cos = jnp.concatenate([jnp.cos(ar), jnp.cos(ar), jnp.cos(ac), jnp.cos(ac), zeros], axis=1)
    sin = jnp.concatenate([-jnp.sin(ar), jnp.sin(ar), -jnp.sin(ac), jnp.sin(ac), zeros], axis=1)
    ident_c = jnp.concatenate([jnp.ones((tile, rope_dim), F32), jnp.zeros((tile, LANES - rope_dim), F32)], axis=1)
    return (jnp.concatenate([ident_c, cos], axis=0),
            jnp.concatenate([jnp.zeros((tile, LANES), F32), sin], axis=0))


def _norm_rows(gain, nope, rope_dim):
    quarter = rope_dim // 4
    gr = gain[nope:]
    grs = jnp.concatenate([gr[quarter:2 * quarter], gr[:quarter], gr[3 * quarter:], gr[2 * quarter:3 * quarter]])
    zpad = jnp.zeros((LANES - rope_dim,), F32)
    return jnp.stack([gain[:nope], jnp.concatenate([gr, zpad]), jnp.concatenate([grs, zpad])])


def kernel(x_prompt, x_sample, state_lru_fwd, state_lru_bwd, cache_mla_ckv, cache_mla_krope, c, c_ctx,
           ada_w, ada_b, norm_mix, norm_ffn, mix0_w_in, mix0_w_out, lru_conv_w, lru_conv_b,
           lru_w_r, lru_b_r, lru_w_i, lru_b_i, lru_lambda, gmlp_v_norm, gmlp_w_s, gmlp_b_s,
           mla_w_down, mla_q_a_norm, mla_kv_a_norm, mla_w_uq, mla_w_ukv, mla_q_norm, mla_k_norm, mla_w_o,
           router_w, router_bias, moe_w_gate, moe_w_up, moe_w_down):
    batch, seq, d = x_prompt.shape
    dec_batch, dec_seq, _ = x_sample.shape
    depth = ada_w.shape[0]
    n_prompt = batch * seq
    n_sample = dec_batch * dec_seq
    assert n_prompt % dec_seq == 0 and seq % TOKEN_TILE == 0 and dec_seq % TOKEN_TILE == 0
    assert 1 + dec_batch <= SUBLANES

    x_parts = (x_prompt.reshape(n_prompt, d), x_sample.reshape(n_sample, d))

    cond = jnp.concatenate([c_ctx[None, :], c, jnp.zeros((SUBLANES - 1 - dec_batch, d), F32)], axis=0)
    mod = _modulation(cond, ada_w, ada_b).reshape(depth, SUBLANES, 6, 1, d)

    rw_t = router_w.T
    rw_hi = rw_t.astype(BF16)
    rw_stack = jnp.concatenate([rw_hi, (rw_t - rw_hi.astype(F32)).astype(BF16)], axis=0)

    fwd_states, bwd_states, ckv_caches, krope_caches = [], [], [], []
    for layer in range(depth):
        j = layer // 2
        if layer % 2 == 0:
            width = lru_conv_w.shape[2]
            heads = lru_w_r.shape[2]
            xb, gg, gu, gv = _even_in_proj(x_parts, mod, layer, norm_mix[layer], mix0_w_in[j], n_prompt, dec_seq)
            wcat = jnp.concatenate([lru_w_r[j, 0], lru_w_i[j, 0], lru_w_r[j, 1], lru_w_i[j, 1]], axis=-1).astype(BF16)
            hb = lambda v: v.reshape(heads, 1, width // heads)
            bcat = jnp.concatenate([hb(lru_b_r[j, 0]), hb(lru_b_i[j, 0]), hb(lru_b_r[j, 1]), hb(lru_b_i[j, 1])], axis=-1)
            lru_args = (lru_conv_w[j], lru_conv_b[j].reshape(1, width), wcat, bcat, lru_lambda[j])
            zero_state = jnp.zeros((batch, 1, width), F32)
            nseq_p = LRU_PROMPT_SEQS_PER_STEP if batch % LRU_PROMPT_SEQS_PER_STEP == 0 else 1
            ya_p, fin_f, fin_b = _lru_mixer(xb, gg, zero_state, zero_state, *lru_args,
                                            row0=0, batch=batch, seq=seq, nseq=nseq_p)
            nseq_s = LRU_LATENT_SEQS_PER_STEP
            if dec_batch % nseq_s or n_prompt % (nseq_s * dec_seq):
                nseq_s = 1
            ya_s, _, _ = _lru_mixer(xb, gg, state_lru_fwd[:, j][:, None, :], state_lru_bwd[:, j][:, None, :],
                                    *lru_args, row0=n_prompt, batch=dec_batch, seq=dec_seq, nseq=nseq_s)
            fwd_states.append(fin_f[:, 0, :])
            bwd_states.append(fin_b[:, 0, :])
            groups, chunk, _ = gmlp_w_s[j].shape
            gd = width // groups
            b_full = jnp.repeat(gmlp_b_s[j].T, gd, axis=1)
            x = _even_out_proj(x_parts, mod, layer, ya_p, ya_s, gu, gv, gmlp_v_norm[j], gmlp_w_s[j], b_full,
                               mix0_w_out[j], n_prompt, dec_seq)
        else:
            q_lora = mla_q_a_norm.shape[1]
            kv_lora = mla_kv_a_norm.shape[1]
            qk_dim = mla_q_norm.shape[1]
            rope_dim = cache_mla_krope.shape[-1]
            nope = qk_dim - rope_dim
            heads = mla_w_uq.shape[2] // qk_dim
            v_dim = mla_w_ukv.shape[2] // heads - nope
            past = cache_mla_ckv.shape[2]
            assert nope == LANES and v_dim == LANES and rope_dim == 4 * ROPE_QUARTER
            wd = mla_w_down[j]
            wd_ext = jnp.concatenate([wd, jnp.zeros((d, LANES - rope_dim), F32)], axis=1).astype(BF16)
            wuq = mla_w_uq[j].reshape(q_lora, heads, qk_dim)
            wuq_pad = jnp.concatenate([wuq, jnp.zeros((q_lora, heads, 2 * LANES - qk_dim), F32)], axis=-1)
            wuq_pad = wuq_pad.reshape(q_lora, heads * 2 * LANES).astype(BF16)
            wukv = mla_w_ukv[j].reshape(kv_lora, heads, nope + v_dim)
            w_ukv_re = jnp.concatenate([wukv[:, :, :nope].reshape(kv_lora, heads * nope),
                                        wukv[:, :, nope:].reshape(kv_lora, heads * v_dim)], axis=1).astype(BF16)
            qn_rows = _norm_rows(mla_q_norm[j], nope, rope_dim)
            kn_rows = _norm_rows(mla_k_norm[j], nope, rope_dim)
            cos_tab, sin_tab = _rope_tables(dec_seq, rope_dim, MLA_TOKEN_TILE)
            x = x_parts[0] if len(x_parts) == 1 else jnp.concatenate(x_parts, axis=0)
            q, ckv, kr = _mla_in_proj(x, mod, layer, norm_mix[layer], wd_ext, mla_q_a_norm[j], mla_kv_a_norm[j],
                                      wuq_pad, qn_rows, cos_tab, sin_tab, n_prompt, dec_seq,
                                      heads=heads, q_lora=q_lora, kv_lora=kv_lora, qk_dim=qk_dim)
            ckv_caches.append(ckv[:n_prompt].reshape(batch, seq, kv_lora))
            krope_caches.append(kr[:n_prompt, :rope_dim].reshape(batch, seq, rope_dim))
            expand = functools.partial(_kv_expand, w_ukv_re=w_ukv_re, kn_rows=kn_rows, cos_tab=cos_tab,
                                       sin_tab=sin_tab, heads=heads, qk_dim=qk_dim)
            assert seq <= MLA_TOKEN_TILE
            o_p = _self_attention(q, ckv, kr, w_ukv_re, kn_rows, cos_tab, sin_tab, batch=batch, seq=seq,
                                  heads=heads, qk_dim=qk_dim)
            kv_len = past + dec_seq
            ctx_kr = jnp.pad(cache_mla_krope[:, j], ((0, 0), (0, 0), (0, LANES - rope_dim)))
            ckv_s = jnp.concatenate([cache_mla_ckv[:, j], ckv[n_prompt:].reshape(dec_batch, dec_seq, kv_lora)], axis=1)
            kr_s = jnp.concatenate([ctx_kr, kr[n_prompt:].reshape(dec_batch, dec_seq, LANES)], axis=1)
            tiles_per_req = kv_len // TOKEN_TILE
            ctx_tiles = past // TOKEN_TILE
            assert past % TOKEN_TILE == 0

            def latent_rope(i):
                r = i % tiles_per_req
                return (jnp.where(r < ctx_tiles, 0, MLA_TOKEN_TILE // TOKEN_TILE + r - ctx_tiles), 0)

            k_s, v_s = expand(ckv_s.reshape(dec_batch * kv_len, kv_lora), kr_s.reshape(dec_batch * kv_len, LANES),
                              rope_index=latent_rope)
            o_s = _attention(q, k_s, v_s, row0=n_prompt, batch=dec_batch, seq=dec_seq, kv_len=kv_len,
                             heads=heads, tq=ATTN_Q_TILE, heads_per_step=ATTN_LATENT_HEADS_PER_STEP)
            x = _mla_out_proj(x, mod, layer, o_p, o_s, mla_w_o[j], n_prompt, dec_seq)
        x_parts = _moe(x, mod, layer, norm_ffn[layer], rw_stack, router_bias, moe_w_gate, moe_w_up, moe_w_down,
                       n_prompt, dec_seq, split=layer == depth - 1)

    xp = x_parts[0].reshape(batch, seq, d)
    xs = x_parts[1].reshape(dec_batch, dec_seq, d)
    return (xp, xs,
            jnp.stack(fwd_states, axis=1), jnp.stack(bwd_states, axis=1),
            jnp.stack(ckv_caches, axis=1), jnp.stack(krope_caches, axis=1))
```

```python
import functools

import jax
import jax.numpy as jnp
import numpy as np
from jax import lax
from jax.experimental import pallas as pl
from jax.experimental.pallas import tpu as pltpu

F32 = jnp.float32
BF16 = jnp.bfloat16
I32 = jnp.int32

EPS = 1e-6
LOG2_E = 1.4426950408889634
LRU_C = 8.0
GRID_W = 64
ROPE_BASE = 10000.0
ROPE_QUARTER = 16
N_EXPERT_GROUPS = 4
TOP_K = 2

LANES = 128
SUBLANES = 8
VMEM_BYTES_V7X = 64 * 1024 * 1024

TOKEN_TILE = 256
MLA_TOKEN_TILE = 512
EXPERT_TILE = 512
MOD_COL_TILE = 1024
ATTN_Q_TILE = 512
ATTN_LATENT_HEADS_PER_STEP = 4
OUT_PROJ_TILE = 512
LRU_GATE_CHUNK = 256
RUN_ALIGN = 16
RUN_PAD_ROWS = 256
LRU_PROMPT_SEQS_PER_STEP = 8
LRU_LATENT_SEQS_PER_STEP = 2


def _cparams(semantics, vmem_mb):
    return pltpu.CompilerParams(dimension_semantics=semantics, vmem_limit_bytes=vmem_mb * 1024 * 1024)


def _resident(shape):
    nd = len(shape)
    return pl.BlockSpec(shape, lambda *_: (0,) * nd, pipeline_mode=pl.Buffered(1))


def _mod_spec(layer, chunk, tile, n_prompt, sample_seq, d):
    def index(i, *_):
        t = i * tile
        row = jnp.where(t < n_prompt, 0, 1 + (t - n_prompt) // sample_seq)
        return (layer, row, chunk, 0, 0)

    return pl.BlockSpec((None, None, None, 1, d), index)


def _modulated(x, gain, shift, scale):
    y = x * lax.rsqrt(jnp.mean(x * x, axis=-1, keepdims=True) + EPS)
    return (y * gain) * (1.0 + scale) + shift


def _mod_body(c_ref, w_ref, b_ref, o_ref):
    c = c_ref[...]
    s = (c * jax.nn.sigmoid(c)).astype(BF16)
    o_ref[...] = jnp.dot(s, w_ref[...].astype(BF16), preferred_element_type=F32) + b_ref[...]


def _modulation(cond, ada_w, ada_b):
    depth, d, n = ada_w.shape
    tn = MOD_COL_TILE
    return pl.pallas_call(
        _mod_body,
        grid=(depth, n // tn),
        in_specs=[
            pl.BlockSpec((SUBLANES, d), lambda l, j: (0, 0)),
            pl.BlockSpec((None, d, tn), lambda l, j: (l, 0, j)),
            pl.BlockSpec((None, 1, tn), lambda l, j: (l, 0, j)),
        ],
        out_specs=pl.BlockSpec((None, SUBLANES, tn), lambda l, j: (l, 0, j)),
        out_shape=jax.ShapeDtypeStruct((depth, SUBLANES, n), F32),
        compiler_params=_cparams(("parallel", "parallel"), 40),
        name="adaln_projection",
    )(cond, ada_w, ada_b.reshape(depth, 1, n))


def _group_select(i, prompt_tiles, p_ref, s_ref):
    return jnp.where(i < prompt_tiles, p_ref[...], s_ref[...])


def _group_specs(block, prompt_tiles):
    return [pl.BlockSpec(block, lambda i, *_: (jnp.minimum(i, prompt_tiles - 1), 0)),
            pl.BlockSpec(block, lambda i, *_: (jnp.maximum(i - prompt_tiles, 0), 0))]


def _token_specs(x_parts, tile, prompt_tiles):
    d = x_parts[0].shape[1]
    if len(x_parts) == 1:
        return [pl.BlockSpec((tile, d), lambda i, *_: (i, 0))]
    return _group_specs((tile, d), prompt_tiles)


def _token_tile(i, prompt_tiles, x_refs):
    return x_refs[0][...] if len(x_refs) == 1 else _group_select(i, prompt_tiles, *x_refs)


def _in0_body(*refs, n_x, prompt_tiles):
    x_refs = refs[:n_x]
    g_ref, sh_ref, sc_ref, w_ref, xb_ref, gg_ref, gu_ref, gv_ref = refs[n_x:]
    x = _token_tile(pl.program_id(0), prompt_tiles, x_refs)
    h = _modulated(x, g_ref[...], sh_ref[...], sc_ref[...])
    z = jnp.dot(h.astype(BF16), w_ref[...], preferred_element_type=F32)
    w = xb_ref.shape[1]
    xb_ref[...] = z[:, :w]
    gg_ref[...] = jax.nn.gelu(z[:, w:2 * w]).astype(BF16)
    gu_ref[...] = jax.nn.gelu(z[:, 2 * w:3 * w]).astype(BF16)
    gv_ref[...] = jax.nn.gelu(z[:, 3 * w:]).astype(BF16)


def _even_in_proj(x_parts, mod, layer, gain, w_in, n_prompt, sample_seq):
    t = sum(p.shape[0] for p in x_parts)
    d = x_parts[0].shape[1]
    tm = TOKEN_TILE
    w4 = w_in.shape[1]
    w = w4 // 4
    row = lambda i: (i, 0)
    ms = functools.partial(_mod_spec, layer, tile=tm, n_prompt=n_prompt, sample_seq=sample_seq, d=d)
    return pl.pallas_call(
        functools.partial(_in0_body, n_x=len(x_parts), prompt_tiles=n_prompt // tm),
        grid=(t // tm,),
        in_specs=[
            *_token_specs(x_parts, tm, n_prompt // tm),
            _resident((1, d)),
            ms(chunk=0),
            ms(chunk=1),
            _resident((d, w4)),
        ],
        out_specs=[pl.BlockSpec((tm, w), row)] * 4,
        out_shape=[
            jax.ShapeDtypeStruct((t, w), F32),
            jax.ShapeDtypeStruct((t, w), BF16),
            jax.ShapeDtypeStruct((t, w), BF16),
            jax.ShapeDtypeStruct((t, w), BF16),
        ],
        compiler_params=_cparams(("parallel",), 48),
        name="even_in_proj",
    )(*x_parts, gain.reshape(1, d), mod, mod, w_in.astype(BF16))


def _tile_scan(a, b, row, reverse):
    for d in (1, 2, 4):
        if reverse:
            keep = row < SUBLANES - d
            a_s = jnp.where(keep, pltpu.roll(a, SUBLANES - d, 0), 1.0)
            b_s = jnp.where(keep, pltpu.roll(b, SUBLANES - d, 0), 0.0)
        else:
            keep = row >= d
            a_s = jnp.where(keep, pltpu.roll(a, d, 0), 1.0)
            b_s = jnp.where(keep, pltpu.roll(b, d, 0), 0.0)
        b = b + a * b_s
        a = a * a_s
    return a, b


def _sigmoid(x):
    return 0.5 * jnp.tanh(0.5 * x) + 0.5


def _lru_body(xb_ref, gg_ref, h0f_ref, h0b_ref, cw_ref, cb_ref, w_ref, bias_ref, lam_ref,
              ya_ref, ff_ref, fb_ref,
              xp_ref, af_ref, bf_ref, ab_ref, bb_ref, hf_ref, hb_ref, *, seq, chunk, nseq):
    hw = LANES
    pad = SUBLANES
    pitch = seq + 2 * pad
    for s in range(nseq):
        xp_ref[pl.ds(s * pitch, pad), :] = jnp.zeros((pad, hw), F32)
        xp_ref[pl.ds(s * pitch + pad + seq, pad), :] = jnp.zeros((pad, hw), F32)
        xp_ref[pl.ds(s * pitch + pad, seq), :] = xb_ref[pl.ds(s * seq, seq), :]

    cw = cw_ref[...]
    cb = cb_ref[...]
    lam = lam_ref[...]
    neg = -lam
    softplus = jnp.maximum(neg, 0.0) + jnp.log1p(jnp.exp(-jnp.abs(neg)))
    nsp = -LRU_C * softplus
    w = w_ref[...]
    bias = bias_ref[...]

    for s in range(nseq):
        for c in range(seq // chunk):
            src = s * pitch + pad + c * chunk
            dst = s * seq + c * chunk
            xc = (cw[0:1] * xp_ref[pl.ds(src - 2, chunk), :]
                  + cw[1:2] * xp_ref[pl.ds(src - 1, chunk), :]
                  + cw[2:3] * xp_ref[pl.ds(src, chunk), :]
                  + cw[3:4] * xp_ref[pl.ds(src + 1, chunk), :]) + cb
            g = jnp.dot(xc.astype(BF16), w, preferred_element_type=F32) + bias
            for direction, (a_ref, b_ref) in enumerate(((af_ref, bf_ref), (ab_ref, bb_ref))):
                r = _sigmoid(g[:, (2 * direction) * hw:(2 * direction + 1) * hw])
                gi = _sigmoid(g[:, (2 * direction + 1) * hw:(2 * direction + 2) * hw])
                log_a = r * nsp[direction:direction + 1]
                a = jnp.exp(log_a)
                a_ref[pl.ds(dst, chunk), :] = a
                b_ref[pl.ds(dst, chunk), :] = jnp.sqrt(1.0 - a * a) * gi * xc

    n_tiles = seq // SUBLANES
    row = lax.broadcasted_iota(I32, (SUBLANES, hw), 0)

    def step(i, carry):
        new = []
        for s in range(nseq):
            hf, hb = carry[2 * s], carry[2 * s + 1]
            rf = pl.multiple_of(s * seq + i * SUBLANES, SUBLANES)
            rb = pl.multiple_of(s * seq + (n_tiles - 1 - i) * SUBLANES, SUBLANES)
            a, b = _tile_scan(af_ref[pl.ds(rf, SUBLANES), :], bf_ref[pl.ds(rf, SUBLANES), :], row, False)
            h = b + a * hf
            hf_ref[pl.ds(rf, SUBLANES), :] = h
            new.append(jnp.broadcast_to(h[SUBLANES - 1:SUBLANES, :], (SUBLANES, hw)))
            a, b = _tile_scan(ab_ref[pl.ds(rb, SUBLANES), :], bb_ref[pl.ds(rb, SUBLANES), :], row, True)
            h = b + a * hb
            hb_ref[pl.ds(rb, SUBLANES), :] = h
            new.append(jnp.broadcast_to(h[0:1, :], (SUBLANES, hw)))
        return tuple(new)

    init = []
    for s in range(nseq):
        init.append(jnp.broadcast_to(h0f_ref[s], (SUBLANES, hw)))
        init.append(jnp.broadcast_to(h0b_ref[s], (SUBLANES, hw)))
    final = lax.fori_loop(0, n_tiles, step, tuple(init), unroll=2 if nseq == 1 else 1)
    for s in range(nseq):
        ff_ref[s] = final[2 * s][0:1, :]
        fb_ref[s] = final[2 * s + 1][0:1, :]
    ya_ref[...] = ((hf_ref[...] + hb_ref[...]) * gg_ref[...].astype(F32)).astype(BF16)


def _lru_mixer(xb, gg, h0f, h0b, conv_w, conv_b, wcat, bcat, lam, *, row0, batch, seq, nseq):
    _, width = xb.shape
    heads = width // LANES
    rows = nseq * seq
    assert batch % nseq == 0 and row0 % rows == 0
    blk0 = row0 // rows
    tok = lambda b, h: (blk0 + b, h)
    state = lambda b, h: (b, 0, h)
    seq_buf = pltpu.VMEM((rows, LANES), F32)
    return pl.pallas_call(
        functools.partial(_lru_body, seq=seq, chunk=min(seq, LRU_GATE_CHUNK), nseq=nseq),
        grid=(batch // nseq, heads),
        in_specs=[
            pl.BlockSpec((rows, LANES), tok),
            pl.BlockSpec((rows, LANES), tok),
            pl.BlockSpec((nseq, 1, LANES), state),
            pl.BlockSpec((nseq, 1, LANES), state),
            pl.BlockSpec((conv_w.shape[0], LANES), lambda b, h: (0, h)),
            pl.BlockSpec((1, LANES), lambda b, h: (0, h)),
            pl.BlockSpec((None, LANES, 4 * LANES), lambda b, h: (h, 0, 0)),
            pl.BlockSpec((None, 1, 4 * LANES), lambda b, h: (h, 0, 0)),
            pl.BlockSpec((2, LANES), lambda b, h: (0, h)),
        ],
        out_specs=[
            pl.BlockSpec((rows, LANES), lambda b, h: (b, h)),
            pl.BlockSpec((nseq, 1, LANES), state),
            pl.BlockSpec((nseq, 1, LANES), state),
        ],
        out_shape=[
            jax.ShapeDtypeStruct((batch * seq, width), BF16),
            jax.ShapeDtypeStruct((batch, 1, width), F32),
            jax.ShapeDtypeStruct((batch, 1, width), F32),
        ],
        scratch_shapes=[pltpu.VMEM((nseq * (seq + 2 * SUBLANES), LANES), F32)] + [seq_buf] * 6,
        compiler_params=_cparams(("parallel", "parallel"), 40),
        name=f"rglru_seq{seq}",
    )(xb, gg, h0f, h0b, conv_w, conv_b, wcat, bcat, lam)


def _out0_body(*refs, n_x, chunk, prompt_tiles):
    x_refs = refs[:n_x]
    (gate_ref, yap_ref, yas_ref, gu_ref, gv_ref, vg_ref, ws_ref, bs_ref, wa_ref, wb_ref,
     o_ref, yb_ref) = refs[n_x:]
    tm = o_ref.shape[0]
    x = _token_tile(pl.program_id(0), prompt_tiles, x_refs)
    ya = _group_select(pl.program_id(0), prompt_tiles, yap_ref, yas_ref)
    v = gv_ref[...].astype(F32)
    vn = (v * lax.rsqrt(jnp.mean(v * v, axis=-1, keepdims=True) + EPS) * vg_ref[...]).astype(BF16)
    groups = ws_ref.shape[0]
    gd = vn.shape[1] // groups
    for c in range(tm // chunk):
        rows = slice(c * chunk, (c + 1) * chunk)
        for g in range(groups):
            cols = slice(g * gd, (g + 1) * gd)
            mixed = jnp.dot(ws_ref[g], vn[rows, cols], preferred_element_type=F32) + bs_ref[:, cols]
            yb_ref[rows, cols] = (gu_ref[rows, cols].astype(F32) * mixed).astype(BF16)
    y = (jnp.dot(ya, wa_ref[...], preferred_element_type=F32)
         + jnp.dot(yb_ref[...], wb_ref[...], preferred_element_type=F32))
    o_ref[...] = x + gate_ref[...] * y


def _even_out_proj(x_parts, mod, layer, ya_p, ya_s, gu, gv, v_gain, w_s, b_full, w_out, n_prompt, sample_seq):
    t = sum(p.shape[0] for p in x_parts)
    d = x_parts[0].shape[1]
    tm = OUT_PROJ_TILE
    w = ya_p.shape[1]
    row = lambda i: (i, 0)
    chunk = w_s.shape[1]
    prompt_tiles = n_prompt // tm
    return pl.pallas_call(
        functools.partial(_out0_body, n_x=len(x_parts), chunk=chunk, prompt_tiles=prompt_tiles),
        grid=(t // tm,),
        in_specs=[
            *_token_specs(x_parts, tm, prompt_tiles),
            _mod_spec(layer, 2, tm, n_prompt, sample_seq, d),
            *_group_specs((tm, w), prompt_tiles),
            pl.BlockSpec((tm, w), row),
            pl.BlockSpec((tm, w), row),
            _resident((1, w)),
            _resident(w_s.shape),
            _resident(b_full.shape),
            _resident((w, d)),
            _resident((w, d)),
        ],
        out_specs=pl.BlockSpec((tm, d), row),
        out_shape=jax.ShapeDtypeStruct((t, d), F32),
        scratch_shapes=[pltpu.VMEM((tm, w), BF16)],
        compiler_params=_cparams(("parallel",), 56),
        name="even_out_proj",
    )(*x_parts, mod, ya_p, ya_s, gu, gv, v_gain.reshape(1, w), w_s.astype(BF16), b_full,
      w_out[:w].astype(BF16), w_out[w:].astype(BF16))


def _swap_halves(x, lane):
    quarter = ROPE_QUARTER
    up = pltpu.roll(x, LANES - quarter, 1)
    down = pltpu.roll(x, quarter, 1)
    return jnp.where((lane % (2 * quarter)) < quarter, up, down)


def _mla_in_body(x_ref, g_ref, sh_ref, sc_ref, wd_ref, qan_ref, kvan_ref, wuq_ref, qn_ref, cos_ref, sin_ref,
                 q_ref, ckv_ref, kr_ref, *, heads, q_lora, kv_lora, qk_dim, sm_scale, sub):
    gn = qn_ref[0:1, :]
    gr = qn_ref[1:2, :]
    grs = qn_ref[2:3, :]
    lane = lax.broadcasted_iota(I32, (sub, LANES), 1)
    for r0 in range(0, x_ref.shape[0], sub):
        rows = pl.ds(r0, sub)
        h = _modulated(x_ref[rows, :], g_ref[...], sh_ref[...], sc_ref[...])
        z = jnp.dot(h.astype(BF16), wd_ref[...], preferred_element_type=F32)
        cq = z[:, :q_lora]
        cq = cq * lax.rsqrt(jnp.mean(cq * cq, axis=-1, keepdims=True) + EPS) * qan_ref[...]
        ckv = z[:, q_lora:q_lora + kv_lora]
        ckv_ref[rows, :] = ckv * lax.rsqrt(jnp.mean(ckv * ckv, axis=-1, keepdims=True) + EPS) * kvan_ref[...]
        kr_ref[rows, :] = z[:, q_lora + kv_lora:]
        q = jnp.dot(cq.astype(BF16), wuq_ref[...], preferred_element_type=F32)
        cos = gr * cos_ref[rows, :]
        sin = grs * sin_ref[rows, :]
        for hd in range(heads):
            qn = q[:, 2 * hd * LANES:(2 * hd + 1) * LANES]
            qr = q[:, (2 * hd + 1) * LANES:(2 * hd + 2) * LANES]
            ss = jnp.sum(qn * qn + qr * qr, axis=-1, keepdims=True)
            rinv = lax.rsqrt(ss * (1.0 / qk_dim) + EPS) * sm_scale
            q_ref[rows, 2 * hd * LANES:(2 * hd + 1) * LANES] = (qn * gn * rinv).astype(BF16)
            rot = qr * cos + _swap_halves(qr, lane) * sin
            q_ref[rows, (2 * hd + 1) * LANES:(2 * hd + 2) * LANES] = (rot * rinv).astype(BF16)


def _rope_spec(tile, n_prompt, sample_seq):
    def index(i):
        t = i * tile
        return (jnp.where(t < n_prompt, 0, 1 + ((t - n_prompt) % sample_seq) // tile), 0)

    return pl.BlockSpec((tile, LANES), index)


def _mla_in_proj(x, mod, layer, gain, wd_ext, q_a_norm, kv_a_norm, wuq_pad, qn_rows, cos_tab, sin_tab,
                 n_prompt, sample_seq, *, heads, q_lora, kv_lora, qk_dim):
    t, d = x.shape
    tm = MLA_TOKEN_TILE
    row = lambda i: (i, 0)
    ms = functools.partial(_mod_spec, layer, tile=tm, n_prompt=n_prompt, sample_seq=sample_seq, d=d)
    body = functools.partial(_mla_in_body, heads=heads, q_lora=q_lora, kv_lora=kv_lora, qk_dim=qk_dim,
                             sm_scale=float(qk_dim) ** -0.5 * LOG2_E, sub=TOKEN_TILE)
    return pl.pallas_call(
        body,
        grid=(t // tm,),
        in_specs=[
            pl.BlockSpec((tm, d), row),
            _resident((1, d)),
            ms(chunk=0),
            ms(chunk=1),
            _resident(wd_ext.shape),
            _resident((1, q_lora)),
            _resident((1, kv_lora)),
            _resident(wuq_pad.shape),
            _resident(qn_rows.shape),
            _rope_spec(tm, n_prompt, sample_seq),
            _rope_spec(tm, n_prompt, sample_seq),
        ],
        out_specs=[
            pl.BlockSpec((tm, heads * 2 * LANES), row),
            pl.BlockSpec((tm, kv_lora), row),
            pl.BlockSpec((tm, LANES), row),
        ],
        out_shape=[
            jax.ShapeDtypeStruct((t, heads * 2 * LANES), BF16),
            jax.ShapeDtypeStruct((t, kv_lora), F32),
            jax.ShapeDtypeStruct((t, LANES), F32),
        ],
        compiler_params=_cparams(("parallel",), 56),
        name="mla_in_proj",
    )(x, gain.reshape(1, d), mod, mod, wd_ext, q_a_norm.reshape(1, q_lora), kv_a_norm.reshape(1, kv_lora),
      wuq_pad, qn_rows, cos_tab, sin_tab)


def _kv_body(ckv_ref, kr_ref, w_ref, kn_ref, cos_ref, sin_ref, k_ref, v_ref, *, heads, qk_dim):
    _kv_compute(ckv_ref[...], kr_ref[...], w_ref, kn_ref, cos_ref, sin_ref, k_ref, v_ref, heads=heads, qk_dim=qk_dim)


def _kv_latent_body(cc_ref, ck_ref, oc_ref, ok_ref, w_ref, kn_ref, cos_ref, sin_ref, k_ref, v_ref,
                    *, heads, qk_dim, tiles_per_req, ctx_tiles):
    from_ctx = (pl.program_id(0) % tiles_per_req) < ctx_tiles
    ckv = jnp.where(from_ctx, cc_ref[...], oc_ref[...])
    kr = jnp.where(from_ctx, ck_ref[...], ok_ref[...])
    _kv_compute(ckv, kr, w_ref, kn_ref, cos_ref, sin_ref, k_ref, v_ref, heads=heads, qk_dim=qk_dim)


def _kv_compute(ckv, kr, w_ref, kn_ref, cos_ref, sin_ref, k_ref, v_ref, *, heads, qk_dim):
    kv = jnp.dot(ckv.astype(BF16), w_ref[...], preferred_element_type=F32)
    tm = kv.shape[0]
    lane = lax.broadcasted_iota(I32, (tm, LANES), 1)
    gn = kn_ref[0:1, :]
    gr = kn_ref[1:2, :]
    grs = kn_ref[2:3, :]
    ssr = jnp.sum(kr * kr, axis=-1, keepdims=True)
    rot = (kr * gr) * cos_ref[...] + (_swap_halves(kr, lane) * grs) * sin_ref[...]
    for hd in range(heads):
        kn = kv[:, hd * LANES:(hd + 1) * LANES]
        rinv = lax.rsqrt((jnp.sum(kn * kn, axis=-1, keepdims=True) + ssr) * (1.0 / qk_dim) + EPS)
        k_ref[:, 2 * hd * LANES:(2 * hd + 1) * LANES] = (kn * gn * rinv).astype(BF16)
        k_ref[:, (2 * hd + 1) * LANES:(2 * hd + 2) * LANES] = (rot * rinv).astype(BF16)
    v_ref[...] = kv[:, heads * LANES:].astype(BF16)


def _kv_expand_latent(ctx_ckv, ctx_kr, ckv, kr, w_ukv_re, kn_rows, cos_tab, sin_tab,
                      *, row0, batch, past, seq, ident_rows, heads, qk_dim):
    kv_lora = ckv.shape[1]
    tm = TOKEN_TILE
    tiles_per_req = (past + seq) // tm
    ctx_tiles = past // tm
    own_tiles = seq // tm
    rows = batch * (past + seq)
    row = lambda i: (i, 0)

    def ctx_index(i):
        return ((i // tiles_per_req) * ctx_tiles + jnp.minimum(i % tiles_per_req, ctx_tiles - 1), 0)

    def own_index(i):
        return (row0 // tm + (i // tiles_per_req) * own_tiles + jnp.maximum(i % tiles_per_req - ctx_tiles, 0), 0)

    def rope_index(i):
        r = i % tiles_per_req
        return (jnp.where(r < ctx_tiles, 0, ident_rows // tm + r - ctx_tiles), 0)

    return pl.pallas_call(
        functools.partial(_kv_latent_body, heads=heads, qk_dim=qk_dim, tiles_per_req=tiles_per_req,
                          ctx_tiles=ctx_tiles),
        grid=(rows // tm,),
        in_specs=[
            pl.BlockSpec((tm, kv_lora), ctx_index),
            pl.BlockSpec((tm, LANES), ctx_index),
            pl.BlockSpec((tm, kv_lora), own_index),
            pl.BlockSpec((tm, LANES), own_index),
            _resident(w_ukv_re.shape),
            _resident(kn_rows.shape),
            pl.BlockSpec((tm, LANES), rope_index),
            pl.BlockSpec((tm, LANES), rope_index),
        ],
        out_specs=[pl.BlockSpec((tm, heads * 2 * LANES), row), pl.BlockSpec((tm, heads * LANES), row)],
        out_shape=[
            jax.ShapeDtypeStruct((rows, heads * 2 * LANES), BF16),
            jax.ShapeDtypeStruct((rows, heads * LANES), BF16),
        ],
        compiler_params=_cparams(("parallel",), 40),
        name="mla_kv_expand_latent",
    )(ctx_ckv, ctx_kr, ckv, kr, w_ukv_re, kn_rows, cos_tab, sin_tab)


def _attn_body(q_ref, k_ref, v_ref, o_ref, *, heads_per_step):
    for hd in range(heads_per_step):
        qk = slice(2 * hd * LANES, 2 * (hd + 1) * LANES)
        vo = slice(hd * LANES, (hd + 1) * LANES)
        s = lax.dot_general(q_ref[:, qk], k_ref[:, qk], (((1,), (1,)), ((), ())), preferred_element_type=F32)
        m = jnp.max(s, axis=-1, keepdims=True)
        p = jnp.exp2(s - m)
        l = jnp.sum(p, axis=-1, keepdims=True)
        o = jnp.dot(p.astype(BF16), v_ref[:, vo], preferred_element_type=F32)
        o_ref[:, vo] = (o * (1.0 / l)).astype(BF16)


def _attention(q, k, v, *, row0, batch, seq, kv_len, heads, tq, heads_per_step):
    nq = seq // tq
    qblk0 = row0 // tq
    hps = heads_per_step
    return pl.pallas_call(
        functools.partial(_attn_body, heads_per_step=hps),
        grid=(batch, heads // hps, nq),
        in_specs=[
            pl.BlockSpec((tq, hps * 2 * LANES), lambda b, h, i: (qblk0 + b * nq + i, h)),
            pl.BlockSpec((kv_len, hps * 2 * LANES), lambda b, h, i: (b, h)),
            pl.BlockSpec((kv_len, hps * LANES), lambda b, h, i: (b, h)),
        ],
        out_specs=pl.BlockSpec((tq, hps * LANES), lambda b, h, i: (b * nq + i, h)),
        out_shape=jax.ShapeDtypeStruct((batch * seq, heads * LANES), BF16),
        compiler_params=_cparams(("parallel", "parallel", "parallel"), 48),
        name=f"mla_attention_kv{kv_len}",
    )(q, k, v)


def _self_attn_body(q_ref, ckv_ref, kr_ref, w_ref, kn_ref, cos_ref, sin_ref, o_ref, k_scr, v_scr, *, heads, qk_dim):
    _kv_body(ckv_ref, kr_ref, w_ref, kn_ref, cos_ref, sin_ref, k_scr, v_scr, heads=heads, qk_dim=qk_dim)
    _attn_body(q_ref, k_scr, v_scr, o_ref, heads_per_step=heads)


def _self_attention(q, ckv, kr, w_ukv_re, kn_rows, cos_tab, sin_tab, *, batch, seq, heads, qk_dim):
    kv_lora = ckv.shape[1]
    tok = lambda b: (b, 0)
    return pl.pallas_call(
        functools.partial(_self_attn_body, heads=heads, qk_dim=qk_dim),
        grid=(batch,),
        in_specs=[
            pl.BlockSpec((seq, heads * 2 * LANES), tok),
            pl.BlockSpec((seq, kv_lora), tok),
            pl.BlockSpec((seq, LANES), tok),
            _resident(w_ukv_re.shape),
            _resident(kn_rows.shape),
            pl.BlockSpec((seq, LANES), lambda b: (0, 0)),
            pl.BlockSpec((seq, LANES), lambda b: (0, 0)),
        ],
        out_specs=pl.BlockSpec((seq, heads * LANES), tok),
        out_shape=jax.ShapeDtypeStruct((batch * seq, heads * LANES), BF16),
        scratch_shapes=[pltpu.VMEM((seq, heads * 2 * LANES), BF16), pltpu.VMEM((seq, heads * LANES), BF16)],
        compiler_params=_cparams(("parallel",), 40),
        name="mla_self_attention",
    )(q, ckv, kr, w_ukv_re, kn_rows, cos_tab, sin_tab)


def _oproj_body(x_ref, gate_ref, op_ref, os_ref, w_ref, out_ref, *, prompt_tiles):
    o = _group_select(pl.program_id(0), prompt_tiles, op_ref, os_ref)
    y = jnp.dot(o, w_ref[...], preferred_element_type=F32)
    out_ref[...] = x_ref[...] + gate_ref[...] * y


def _mla_out_proj(x, mod, layer, o_p, o_s, w_o, n_prompt, sample_seq):
    t, d = x.shape
    tm = OUT_PROJ_TILE
    row = lambda i: (i, 0)
    prompt_tiles = n_prompt // tm
    return pl.pallas_call(
        functools.partial(_oproj_body, prompt_tiles=prompt_tiles),
        grid=(t // tm,),
        in_specs=[
            pl.BlockSpec((tm, d), row),
            _mod_spec(layer, 2, tm, n_prompt, sample_seq, d),
            *_group_specs((tm, o_p.shape[1]), prompt_tiles),
            _resident(w_o.shape),
        ],
        out_specs=pl.BlockSpec((tm, d), row),
        out_shape=jax.ShapeDtypeStruct((t, d), F32),
        compiler_params=_cparams(("parallel",), 48),
        name="mla_out_proj",
    )(x, mod, o_p, o_s, w_o.astype(BF16))


def _round_up(v, m):
    return (v + m - 1) // m * m


def _local_rows(tile):
    return TOP_K * tile + RUN_PAD_ROWS


def _router_body(x_ref, g_ref, sh_ref, sc_ref, wst_ref, rb_ref, hb_ref, wt_ref, lp_ref, cnt_ref):
    h = _modulated(x_ref[...], g_ref[...], sh_ref[...], sc_ref[...])
    tm, d = h.shape
    n_experts = rb_ref.shape[0]

    h_hi = h.astype(BF16)
    hb_ref[...] = h_hi
    h_lo = (h - h_hi.astype(F32)).astype(BF16)
    nt = (((1,), (1,)), ((), ()))
    both = lax.dot_general(wst_ref[...], h_hi, nt, preferred_element_type=F32)
    logits = (both[:n_experts] + both[n_experts:]
              + lax.dot_general(wst_ref[:n_experts, :], h_lo, nt, preferred_element_type=F32))
    scores = jax.nn.sigmoid(logits)
    biased = scores + rb_ref[...]
    per_group = n_experts // N_EXPERT_GROUPS
    assert per_group == 4 and TOP_K == 2
    b_rows = [biased[e:e + 1, :] for e in range(n_experts)]
    s_rows = [scores[e:e + 1, :] for e in range(n_experts)]

    best = None
    sel = jnp.zeros((1, tm), I32)
    for g in range(N_EXPERT_GROUPS):
        b0, b1, b2, b3 = b_rows[g * 4:(g + 1) * 4]
        m1, n1 = jnp.maximum(b0, b1), jnp.minimum(b0, b1)
        m2, n2 = jnp.maximum(b2, b3), jnp.minimum(b2, b3)
        top1 = jnp.maximum(m1, m2)
        top2 = jnp.maximum(jnp.minimum(m1, m2), jnp.maximum(n1, n2))
        gsum = top1 + top2
        if best is None:
            best = gsum
        else:
            better = gsum > best
            sel = jnp.where(better, g, sel)
            best = jnp.where(better, gsum, best)

    def pick(rows, j):
        out = rows[j]
        for g in range(1, N_EXPERT_GROUPS):
            out = jnp.where(sel == g, rows[g * 4 + j], out)
        return out

    cand_b = [pick(b_rows, j) for j in range(4)]
    cand_s = [pick(s_rows, j) for j in range(4)]

    def argmax4(vals):
        bv, bi = vals[0], jnp.zeros((1, tm), I32)
        for j in range(1, 4):
            gt = vals[j] > bv
            bi = jnp.where(gt, j, bi)
            bv = jnp.where(gt, vals[j], bv)
        return bi

    i1 = argmax4(cand_b)
    i2 = argmax4([jnp.where(i1 == j, -jnp.inf, cand_b[j]) for j in range(4)])

    def take(vals, idx):
        out = vals[0]
        for j in range(1, 4):
            out = jnp.where(idx == j, vals[j], out)
        return out

    s1 = take(cand_s, i1)
    s2 = take(cand_s, i2)
    tot = s1 + s2
    e1 = sel * 4 + i1
    e2 = sel * 4 + i2
    wt_ref[0:1, :] = s1 / tot
    wt_ref[1:2, :] = s2 / tot

    eid = lax.broadcasted_iota(I32, (n_experts, tm), 0)
    is1 = eid == e1
    is2 = eid == e2
    chosen = jnp.where(is1 | is2, 1.0, 0.0)
    before = (lax.broadcasted_iota(I32, (tm, tm), 0) < lax.broadcasted_iota(I32, (tm, tm), 1))
    rank = jnp.dot(chosen.astype(BF16), jnp.where(before, 1.0, 0.0).astype(BF16), preferred_element_type=F32)
    count = jnp.sum(chosen, axis=1, keepdims=True)
    padded = jnp.floor((count + (RUN_ALIGN - 1)) * (1.0 / RUN_ALIGN)) * RUN_ALIGN
    lower = (lax.broadcasted_iota(I32, (n_experts, n_experts), 1)
             < lax.broadcasted_iota(I32, (n_experts, n_experts), 0))
    run_start = jnp.dot(jnp.where(lower, 1.0, 0.0).astype(BF16),
                        jnp.broadcast_to(padded, (n_experts, LANES)).astype(BF16),
                        preferred_element_type=F32)[:, 0:1]
    row = run_start + rank
    lp_ref[0:1, :] = jnp.sum(jnp.where(is1, row, 0.0), axis=0, keepdims=True).astype(I32)
    lp_ref[1:2, :] = jnp.sum(jnp.where(is2, row, 0.0), axis=0, keepdims=True).astype(I32)
    cnt_ref[...] = jnp.broadcast_to(count, (n_experts, LANES))


def _router(x, mod, layer, gain, w_stack, r_bias, n_prompt, sample_seq):
    t, d = x.shape
    tm = TOKEN_TILE
    n_experts = r_bias.shape[0]
    ms = functools.partial(_mod_spec, layer, tile=tm, n_prompt=n_prompt, sample_seq=sample_seq, d=d)
    col = lambda i: (0, i)
    return pl.pallas_call(
        _router_body,
        grid=(t // tm,),
        in_specs=[
            pl.BlockSpec((tm, d), lambda i: (i, 0)),
            _resident((1, d)),
            ms(chunk=3),
            ms(chunk=4),
            _resident(w_stack.shape),
            _resident((n_experts, 1)),
        ],
        out_specs=[
            pl.BlockSpec((tm, d), lambda i: (i, 0)),
            pl.BlockSpec((TOP_K, tm), col),
            pl.BlockSpec((TOP_K, tm), col),
            pl.BlockSpec((None, n_experts, LANES), lambda i: (i, 0, 0)),
        ],
        out_shape=[
            jax.ShapeDtypeStruct((t, d), BF16),
            jax.ShapeDtypeStruct((TOP_K, t), F32),
            jax.ShapeDtypeStruct((TOP_K, t), I32),
            jax.ShapeDtypeStruct((t // tm, n_experts, LANES), F32),
        ],
        compiler_params=_cparams(("parallel",), 40),
        name="moe_router",
    )(x, gain.reshape(1, d), mod, mod, w_stack, r_bias.reshape(n_experts, 1))


def _run_copies(tile, local_ref, slot, sorted_ref, row_ref, count_ref, sem, to_sorted):
    max_chunks = local_ref.shape[1] // RUN_ALIGN
    min_chunks = TOP_K * TOKEN_TILE // RUN_ALIGN

    def directed(loc, far):
        return pltpu.make_async_copy(loc, far, sem) if to_sorted else pltpu.make_async_copy(far, loc, sem)

    def chunk(c):
        loc = local_ref.at[slot, pl.ds(pl.multiple_of(c * RUN_ALIGN, RUN_ALIGN), RUN_ALIGN)]
        first = row_ref[tile * max_chunks + c]
        return directed(loc, sorted_ref.at[pl.ds(pl.multiple_of(first, RUN_ALIGN), RUN_ALIGN)])

    def apply(op):
        def one(c, carry):
            getattr(chunk(c), op)()
            return carry

        if op == "start":
            lax.fori_loop(0, min_chunks, one, 0, unroll=8)
        else:
            rows = min_chunks * RUN_ALIGN
            directed(local_ref.at[slot, pl.ds(0, rows)], sorted_ref.at[pl.ds(0, rows)]).wait()
        lax.fori_loop(min_chunks, count_ref[tile], one, 0)

    return apply


def _one_hot_rows(lp_ref, rows, tm):
    j = lax.broadcasted_iota(I32, (rows, tm), 0)
    hit = (j == lp_ref[0:1, :]) | (j == lp_ref[1:2, :])
    return jnp.where(hit, 1.0, 0.0).astype(BF16)


def _dispatch_body(row_ref, chunk_ref, last_ref, nv_ref, lp_ref, h_ref, o_ref, zero_ref, loc_ref, sems, zsem):
    tm = h_ref.shape[0]
    zt = zero_ref.shape[0]
    n_tiles = o_ref.shape[0] // zt
    i = pl.program_id(0)
    last = pl.num_programs(0) - 1
    slot = i % 2

    def copies(tile, s):
        return _run_copies(tile, loc_ref, s, o_ref, row_ref, chunk_ref, sems.at[s], True)

    @pl.when(i == 0)
    def _():
        zero_ref[...] = jnp.zeros(zero_ref.shape, zero_ref.dtype)

        def zero_tile(row):
            return pltpu.make_async_copy(zero_ref, o_ref.at[pl.ds(pl.multiple_of(row, zt), zt)], zsem)

        for e in range(last_ref.shape[0]):
            @pl.when(last_ref[e] >= 0)
            def _():
                zero_tile(last_ref[e]).start()

        def start_tail(j, c):
            zero_tile(j * zt).start()
            return c

        lax.fori_loop(nv_ref[0], n_tiles, start_tail, 0)
        for e in range(last_ref.shape[0]):
            @pl.when(last_ref[e] >= 0)
            def _():
                zero_tile(last_ref[e]).wait()

        def wait_tail(j, c):
            zero_tile(j * zt).wait()
            return c

        lax.fori_loop(nv_ref[0], n_tiles, wait_tail, 0)

    @pl.when(i >= 2)
    def _():
        copies(i - 2, slot)("wait")

    loc_ref[slot] = jnp.dot(_one_hot_rows(lp_ref, loc_ref.shape[1], tm), h_ref[...],
                            preferred_element_type=F32).astype(BF16)
    copies(i, slot)("start")

    @pl.when(i == last)
    def _():
        @pl.when(i >= 1)
        def _():
            copies(i - 1, 1 - slot)("wait")
        copies(i, slot)("wait")


def _dispatch(hb, lp, chunk_row, tile_chunks, last_tile_row, n_valid, sorted_rows):
    t, d = hb.shape
    tm = TOKEN_TILE
    grid_spec = pltpu.PrefetchScalarGridSpec(
        num_scalar_prefetch=4,
        grid=(t // tm,),
        in_specs=[
            pl.BlockSpec((TOP_K, tm), lambda i, *_: (0, i)),
            pl.BlockSpec((tm, d), lambda i, *_: (i, 0)),
        ],
        out_specs=pl.BlockSpec(memory_space=pl.ANY),
        scratch_shapes=[pltpu.VMEM((EXPERT_TILE, d), BF16), pltpu.VMEM((2, _local_rows(tm), d), BF16),
                        pltpu.SemaphoreType.DMA((2,)), pltpu.SemaphoreType.DMA],
    )
    return pl.pallas_call(
        _dispatch_body,
        grid_spec=grid_spec,
        out_shape=jax.ShapeDtypeStruct((sorted_rows, d), BF16),
        compiler_params=_cparams(("arbitrary",), 40),
        name="moe_dispatch",
    )(chunk_row, tile_chunks, last_tile_row, n_valid, lp, hb)


def _expert_body(te_ref, nv_ref, nxt_ref, par_ref, x_ref, wg_hbm, wu_hbm, wd_hbm, y_ref,
                 wgf_ref, wuf_ref, wdf_ref, wgb_ref, wub_ref, wdb_ref, sems, *, layer):
    i = pl.program_id(0)
    valid = i < nv_ref[0]
    new_expert = (i == 0) | (te_ref[i] != te_ref[jnp.maximum(i - 1, 0)])

    def weight_copies(e, slot):
        return [pltpu.make_async_copy(hbm.at[layer, e], buf.at[slot], sems.at[slot])
                for hbm, buf in ((wg_hbm, wgf_ref), (wu_hbm, wuf_ref), (wd_hbm, wdf_ref))]

    @pl.when(valid & new_expert)
    def _():
        slot = par_ref[i]

        @pl.when(i == 0)
        def _():
            for c in weight_copies(te_ref[0], slot):
                c.start()

        for c in weight_copies(te_ref[i], slot):
            c.wait()

        @pl.when(nxt_ref[i] >= 0)
        def _():
            for c in weight_copies(nxt_ref[i], 1 - slot):
                c.start()

        wgb_ref[...] = wgf_ref[slot].astype(BF16)
        wub_ref[...] = wuf_ref[slot].astype(BF16)
        wdb_ref[...] = wdf_ref[slot].astype(BF16)

    @pl.when(valid)
    def _():
        xb = x_ref[...]
        g = jnp.dot(xb, wgb_ref[...], preferred_element_type=F32)
        u = jnp.dot(xb, wub_ref[...], preferred_element_type=F32)
        act = (g * jax.nn.sigmoid(g) * u).astype(BF16)
        y_ref[...] = jnp.dot(act, wdb_ref[...], preferred_element_type=F32).astype(BF16)

    @pl.when(i >= nv_ref[0])
    def _():
        y_ref[...] = jnp.zeros(y_ref.shape, BF16)


def _experts(xs, tile_expert, n_valid, tile_next, tile_slot, w_gate, w_up, w_down, layer):
    rows, d = xs.shape
    tm = EXPERT_TILE
    _, n_experts, _, f = w_gate.shape

    def xrow(i, te, nv, *_):
        return (jnp.minimum(i, nv[0] - 1), 0)

    hbm = pl.BlockSpec(memory_space=pl.ANY)
    grid_spec = pltpu.PrefetchScalarGridSpec(
        num_scalar_prefetch=4,
        grid=(rows // tm,),
        in_specs=[pl.BlockSpec((tm, d), xrow), hbm, hbm, hbm],
        out_specs=pl.BlockSpec((tm, d), lambda i, *_: (i, 0)),
        scratch_shapes=[pltpu.VMEM((2, d, f), F32), pltpu.VMEM((2, d, f), F32), pltpu.VMEM((2, f, d), F32),
                        pltpu.VMEM((d, f), BF16), pltpu.VMEM((d, f), BF16), pltpu.VMEM((f, d), BF16),
                        pltpu.SemaphoreType.DMA((2,))],
    )
    return pl.pallas_call(
        functools.partial(_expert_body, layer=layer),
        grid_spec=grid_spec,
        out_shape=jax.ShapeDtypeStruct((rows, d), BF16),
        compiler_params=_cparams(("arbitrary",), 56),
        name="moe_experts",
    )(tile_expert, n_valid, tile_next, tile_slot, xs, w_gate, w_up, w_down)


def _combine_body(row_ref, chunk_ref, lpt_ref, wtt_ref, x_ref, gate_ref, ys_ref, *rest, prompt_tiles):
    o_refs, (loc_ref, sems) = rest[:-2], rest[-2:]
    tm = x_ref.shape[0]
    rows = loc_ref.shape[1]
    i = pl.program_id(0)
    slot = i % 2

    def copies(tile, s):
        return _run_copies(tile, loc_ref, s, ys_ref, row_ref, chunk_ref, sems.at[s], False)

    @pl.when(i == 0)
    def _():
        loc_ref[...] = jnp.zeros(loc_ref.shape, loc_ref.dtype)
        copies(0, 0)("start")

    @pl.when(i + 1 < pl.num_programs(0))
    def _():
        copies(i + 1, 1 - slot)("start")

    copies(i, slot)("wait")

    jt = lax.broadcasted_iota(I32, (tm, rows), 1)
    local = loc_ref[slot]
    moe = jnp.zeros((tm, local.shape[1]), F32)
    for k in range(TOP_K):
        pick = jnp.where(jt == lpt_ref[:, k:k + 1], 1.0, 0.0).astype(BF16)
        moe = moe + wtt_ref[:, k:k + 1] * jnp.dot(pick, local, preferred_element_type=F32)
    out = x_ref[...] + gate_ref[...] * moe
    if len(o_refs) == 1:
        o_refs[0][...] = out
    else:
        for o_ref, own in zip(o_refs, (i < prompt_tiles, i >= prompt_tiles)):
            @pl.when(own)
            def _():
                o_ref[...] = out


def _combine(x, mod, layer, ys, lp, wt, chunk_row, tile_chunks, n_prompt, sample_seq, split):
    t, d = x.shape
    tm = TOKEN_TILE
    prompt_tiles = n_prompt // tm
    tok = lambda i, *_: (i, 0)
    col = lambda i, *_: (0, i)
    if split:
        out_specs = _group_specs((tm, d), prompt_tiles)
        out_shape = [jax.ShapeDtypeStruct((n_prompt, d), F32), jax.ShapeDtypeStruct((t - n_prompt, d), F32)]
    else:
        out_specs = [pl.BlockSpec((tm, d), tok)]
        out_shape = [jax.ShapeDtypeStruct((t, d), F32)]
    grid_spec = pltpu.PrefetchScalarGridSpec(
        num_scalar_prefetch=2,
        grid=(t // tm,),
        in_specs=[
            pl.BlockSpec((tm, TOP_K), tok),
            pl.BlockSpec((tm, TOP_K), tok),
            pl.BlockSpec((tm, d), tok),
            _mod_spec(layer, 5, tm, n_prompt, sample_seq, d),
            pl.BlockSpec(memory_space=pl.ANY),
        ],
        out_specs=out_specs,
        scratch_shapes=[pltpu.VMEM((2, _local_rows(tm), d), BF16), pltpu.SemaphoreType.DMA((2,))],
    )
    return pl.pallas_call(
        functools.partial(_combine_body, prompt_tiles=prompt_tiles),
        grid_spec=grid_spec,
        out_shape=out_shape,
        compiler_params=_cparams(("arbitrary",), 40),
        name="moe_combine",
    )(chunk_row, tile_chunks, lp.T, wt.T, x, mod, ys)


def _moe(x, mod, layer, gain, w_stack, r_bias, w_gate, w_up, w_down, n_prompt, sample_seq, split):
    t, d = x.shape
    n_experts = r_bias.shape[0]
    assert n_experts * (RUN_ALIGN - 1) <= RUN_PAD_ROWS
    hb, wt, lp, cnt = _router(x, mod, layer, gain, w_stack, r_bias, n_prompt, sample_seq)
    n_tok_tiles = t // TOKEN_TILE
    run_len = _round_up(cnt[:, :, 0].astype(I32), RUN_ALIGN)
    run_before = jnp.cumsum(run_len, axis=0) - run_len
    tiles = (jnp.sum(run_len, axis=0) + EXPERT_TILE - 1) // EXPERT_TILE
    tile_end = jnp.cumsum(tiles)
    seg_start = (tile_end - tiles) * EXPERT_TILE
    run_row = seg_start[None, :] + run_before
    run_chunks = run_len // RUN_ALIGN
    chunk_end = jnp.cumsum(run_chunks, axis=1)
    chunk_ids = jnp.arange(_local_rows(TOKEN_TILE) // RUN_ALIGN, dtype=I32)
    owner = jnp.sum((chunk_end[:, None, :] <= chunk_ids[None, :, None]).astype(I32), axis=2)
    owner = jnp.minimum(owner, n_experts - 1)
    is_owner = owner[:, :, None] == jnp.arange(n_experts, dtype=I32)
    pick = lambda a: jnp.sum(jnp.where(is_owner, a[:, None, :], 0), axis=2)
    chunk_row = pick(run_row) + (chunk_ids[None, :] - pick(chunk_end - run_chunks)) * RUN_ALIGN
    chunk_row = chunk_row.reshape(-1).astype(I32)
    tile_chunks = chunk_end[:, -1].astype(I32)
    sorted_rows = (_round_up(TOP_K * t + n_tok_tiles * n_experts * (RUN_ALIGN - 1), EXPERT_TILE)
                   + n_experts * EXPERT_TILE)
    n_tiles = sorted_rows // EXPERT_TILE
    n_valid = tile_end[-1:]
    tile_ids = jnp.arange(n_tiles, dtype=I32)
    ends_before = lambda i: jnp.sum((tile_end[None, :] <= i[:, None]).astype(I32), axis=1)
    tile_expert = jnp.where(tile_ids < n_valid[0], ends_before(tile_ids), ends_before(n_valid - 1))
    tile_expert = jnp.minimum(tile_expert, n_experts - 1).astype(I32)
    last_tile_row = jnp.where(tiles > 0, (tile_end - 1) * EXPERT_TILE, -1).astype(I32)
    n_valid = n_valid.astype(I32)
    e_ids = jnp.arange(n_experts, dtype=I32)
    used = tiles > 0
    later_used = (e_ids[None, :] > e_ids[:, None]) & used[None, :]
    next_used = jnp.min(jnp.where(later_used, e_ids[None, :], n_experts), axis=1)
    next_used = jnp.where(next_used < n_experts, next_used, -1)
    slot_of = (jnp.cumsum(used.astype(I32)) - 1) % 2
    of_tile = lambda a: jnp.sum(jnp.where(tile_expert[:, None] == e_ids[None, :], a[None, :], 0), axis=1).astype(I32)
    xs = _dispatch(hb, lp, chunk_row, tile_chunks, last_tile_row, n_valid, sorted_rows)
    ys = _experts(xs, tile_expert, n_valid, of_tile(next_used), of_tile(slot_of), w_gate, w_up, w_down, layer)
    return _combine(x, mod, layer, ys, lp, wt, chunk_row, tile_chunks, n_prompt, sample_seq, split)


def _rope_tables(sample_seq, rope_dim, tile):
    n_freq = rope_dim // 4
    pos = np.arange(sample_seq)
    inv_freq = ROPE_BASE ** (-np.arange(n_freq, dtype=np.float64) / n_freq)
    ar = (pos // GRID_W)[:, None] * inv_freq
    ac = (pos % GRID_W)[:, None] * inv_freq
    zeros = np.zeros((sample_seq, LANES - rope_dim))
    cos = np.concatenate([np.cos(ar), np.cos(ar), np.cos(ac), np.cos(ac), zeros], axis=1)
    sin = np.concatenate([-np.sin(ar), np.sin(ar), -np.sin(ac), np.sin(ac), zeros], axis=1)
    ident_c = np.concatenate([np.ones((tile, rope_dim)), np.zeros((tile, LANES - rope_dim))], axis=1)
    return (jnp.asarray(np.concatenate([ident_c, cos], axis=0), F32),
            jnp.asarray(np.concatenate([np.zeros((tile, LANES)), sin], axis=0), F32))


def _norm_rows(gain, nope, rope_dim):
    quarter = rope_dim // 4
    gr = gain[nope:]
    grs = jnp.concatenate([gr[quarter:2 * quarter], gr[:quarter], gr[3 * quarter:], gr[2 * quarter:3 * quarter]])
    zpad = jnp.zeros((LANES - rope_dim,), F32)
    return jnp.stack([gain[:nope], jnp.concatenate([gr, zpad]), jnp.concatenate([grs, zpad])])


def kernel(x_prompt, x_sample, state_lru_fwd, state_lru_bwd, cache_mla_ckv, cache_mla_krope, c, c_ctx,
           ada_w, ada_b, norm_mix, norm_ffn, mix0_w_in, mix0_w_out, lru_conv_w, lru_conv_b,
           lru_w_r, lru_b_r, lru_w_i, lru_b_i, lru_lambda, gmlp_v_norm, gmlp_w_s, gmlp_b_s,
           mla_w_down, mla_q_a_norm, mla_kv_a_norm, mla_w_uq, mla_w_ukv, mla_q_norm, mla_k_norm, mla_w_o,
           router_w, router_bias, moe_w_gate, moe_w_up, moe_w_down):
    batch, seq, d = x_prompt.shape
    dec_batch, dec_seq, _ = x_sample.shape
    depth = ada_w.shape[0]
    n_prompt = batch * seq
    n_sample = dec_batch * dec_seq
    assert n_prompt % dec_seq == 0 and seq % TOKEN_TILE == 0 and dec_seq % TOKEN_TILE == 0
    assert 1 + dec_batch <= SUBLANES

    x_parts = (x_prompt.reshape(n_prompt, d), x_sample.reshape(n_sample, d))

    cond = jnp.concatenate([c_ctx[None, :], c, jnp.zeros((SUBLANES - 1 - dec_batch, d), F32)], axis=0)
    mod = _modulation(cond, ada_w, ada_b).reshape(depth, SUBLANES, 6, 1, d)

    rw_t = router_w.T
    rw_hi = rw_t.astype(BF16)
    rw_stack = jnp.concatenate([rw_hi, (rw_t - rw_hi.astype(F32)).astype(BF16)], axis=0)

    fwd_states, bwd_states, ckv_caches, krope_caches = [], [], [], []
    for layer in range(depth):
        j = layer // 2
        if layer % 2 == 0:
            width = lru_conv_w.shape[2]
            heads = lru_w_r.shape[2]
            xb, gg, gu, gv = _even_in_proj(x_parts, mod, layer, norm_mix[layer], mix0_w_in[j], n_prompt, dec_seq)
            wcat = jnp.concatenate([lru_w_r[j, 0], lru_w_i[j, 0], lru_w_r[j, 1], lru_w_i[j, 1]], axis=-1).astype(BF16)
            hb = lambda v: v.reshape(heads, 1, width // heads)
            bcat = jnp.concatenate([hb(lru_b_r[j, 0]), hb(lru_b_i[j, 0]), hb(lru_b_r[j, 1]), hb(lru_b_i[j, 1])], axis=-1)
            lru_args = (lru_conv_w[j], lru_conv_b[j].reshape(1, width), wcat, bcat, lru_lambda[j])
            zero_state = jnp.zeros((batch, 1, width), F32)
            nseq_p = LRU_PROMPT_SEQS_PER_STEP if batch % LRU_PROMPT_SEQS_PER_STEP == 0 else 1
            ya_p, fin_f, fin_b = _lru_mixer(xb, gg, zero_state, zero_state, *lru_args,
                                            row0=0, batch=batch, seq=seq, nseq=nseq_p)
            nseq_s = LRU_LATENT_SEQS_PER_STEP
            if dec_batch % nseq_s or n_prompt % (nseq_s * dec_seq):
                nseq_s = 1
            ya_s, _, _ = _lru_mixer(xb, gg, state_lru_fwd[:, j][:, None, :], state_lru_bwd[:, j][:, None, :],
                                    *lru_args, row0=n_prompt, batch=dec_batch, seq=dec_seq, nseq=nseq_s)
            fwd_states.append(fin_f[:, 0, :])
            bwd_states.append(fin_b[:, 0, :])
            groups, chunk, _ = gmlp_w_s[j].shape
            gd = width // groups
            b_full = jnp.repeat(gmlp_b_s[j].T, gd, axis=1)
            x = _even_out_proj(x_parts, mod, layer, ya_p, ya_s, gu, gv, gmlp_v_norm[j], gmlp_w_s[j], b_full,
                               mix0_w_out[j], n_prompt, dec_seq)
        else:
            q_lora = mla_q_a_norm.shape[1]
            kv_lora = mla_kv_a_norm.shape[1]
            qk_dim = mla_q_norm.shape[1]
            rope_dim = cache_mla_krope.shape[-1]
            nope = qk_dim - rope_dim
            heads = mla_w_uq.shape[2] // qk_dim
            v_dim = mla_w_ukv.shape[2] // heads - nope
            past = cache_mla_ckv.shape[2]
            assert nope == LANES and v_dim == LANES and rope_dim == 4 * ROPE_QUARTER
            wd = mla_w_down[j]
            wd_ext = jnp.concatenate([wd, jnp.zeros((d, LANES - rope_dim), F32)], axis=1).astype(BF16)
            wuq = mla_w_uq[j].reshape(q_lora, heads, qk_dim)
            wuq_pad = jnp.concatenate([wuq, jnp.zeros((q_lora, heads, 2 * LANES - qk_dim), F32)], axis=-1)
            wuq_pad = wuq_pad.reshape(q_lora, heads * 2 * LANES).astype(BF16)
            wukv = mla_w_ukv[j].reshape(kv_lora, heads, nope + v_dim)
            w_ukv_re = jnp.concatenate([wukv[:, :, :nope].reshape(kv_lora, heads * nope),
                                        wukv[:, :, nope:].reshape(kv_lora, heads * v_dim)], axis=1).astype(BF16)
            qn_rows = _norm_rows(mla_q_norm[j], nope, rope_dim)
            kn_rows = _norm_rows(mla_k_norm[j], nope, rope_dim)
            cos_tab, sin_tab = _rope_tables(dec_seq, rope_dim, MLA_TOKEN_TILE)
            x = x_parts[0] if len(x_parts) == 1 else jnp.concatenate(x_parts, axis=0)
            q, ckv, kr = _mla_in_proj(x, mod, layer, norm_mix[layer], wd_ext, mla_q_a_norm[j], mla_kv_a_norm[j],
                                      wuq_pad, qn_rows, cos_tab, sin_tab, n_prompt, dec_seq,
                                      heads=heads, q_lora=q_lora, kv_lora=kv_lora, qk_dim=qk_dim)
            ckv_caches.append(ckv[:n_prompt].reshape(batch, seq, kv_lora))
            krope_caches.append(kr[:n_prompt, :rope_dim].reshape(batch, seq, rope_dim))
            assert seq <= MLA_TOKEN_TILE
            o_p = _self_attention(q, ckv, kr, w_ukv_re, kn_rows, cos_tab, sin_tab, batch=batch, seq=seq,
                                  heads=heads, qk_dim=qk_dim)
            kv_len = past + dec_seq
            assert past % TOKEN_TILE == 0 and past > 0
            ctx_kr = jnp.pad(cache_mla_krope[:, j], ((0, 0), (0, 0), (0, LANES - rope_dim)))
            k_s, v_s = _kv_expand_latent(
                cache_mla_ckv[:, j].reshape(dec_batch * past, kv_lora), ctx_kr.reshape(dec_batch * past, LANES),
                ckv, kr, w_ukv_re, kn_rows, cos_tab, sin_tab, row0=n_prompt, batch=dec_batch, past=past,
                seq=dec_seq, ident_rows=MLA_TOKEN_TILE, heads=heads, qk_dim=qk_dim)
            o_s = _attention(q, k_s, v_s, row0=n_prompt, batch=dec_batch, seq=dec_seq, kv_len=kv_len,
                             heads=heads, tq=ATTN_Q_TILE, heads_per_step=ATTN_LATENT_HEADS_PER_STEP)
            x = _mla_out_proj(x, mod, layer, o_p, o_s, mla_w_o[j], n_prompt, dec_seq)
        x_parts = _moe(x, mod, layer, norm_ffn[layer], rw_stack, router_bias, moe_w_gate, moe_w_up, moe_w_down,
                       n_prompt, dec_seq, split=layer == depth - 1)

    xp = x_parts[0].reshape(batch, seq, d)
    xs = x_parts[1].reshape(dec_batch, dec_seq, d)
    return (xp, xs,
            jnp.stack(fwd_states, axis=1), jnp.stack(bwd_states, axis=1),
            jnp.stack(ckv_caches, axis=1), jnp.stack(krope_caches, axis=1))
```

```python
import functools

import jax
import jax.numpy as jnp
import numpy as np
from jax import lax
from jax.experimental import pallas as pl
from jax.experimental.pallas import tpu as pltpu

F32 = jnp.float32
BF16 = jnp.bfloat16
I32 = jnp.int32

EPS = 1e-6
LOG2_E = 1.4426950408889634
LRU_C = 8.0
GRID_W = 64
ROPE_BASE = 10000.0
ROPE_QUARTER = 16
N_EXPERT_GROUPS = 4
TOP_K = 2

LANES = 128
SUBLANES = 8
VMEM_BYTES_V7X = 64 * 1024 * 1024

TOKEN_TILE = 256
MLA_TOKEN_TILE = 512
EXPERT_TILE = 512
MOD_COL_TILE = 1024
ATTN_Q_TILE = 512
ATTN_LATENT_HEADS_PER_STEP = 4
OUT_PROJ_TILE = 512
EVEN_IN_TILE = 512
LRU_GATE_CHUNK = 256
RUN_ALIGN = 16
RUN_PAD_ROWS = 256
LRU_PROMPT_SEQS_PER_STEP = 8
LRU_LATENT_SEQS_PER_STEP = 2


def _cparams(semantics, vmem_mb):
    return pltpu.CompilerParams(dimension_semantics=semantics, vmem_limit_bytes=vmem_mb * 1024 * 1024)


def _resident(shape):
    nd = len(shape)
    return pl.BlockSpec(shape, lambda *_: (0,) * nd, pipeline_mode=pl.Buffered(1))


def _mod_spec(layer, chunk, tile, n_prompt, sample_seq, d):
    def index(i, *_):
        t = i * tile
        row = jnp.where(t < n_prompt, 0, 1 + (t - n_prompt) // sample_seq)
        return (layer, row, chunk, 0, 0)

    return pl.BlockSpec((None, None, None, 1, d), index)


def _modulated(x, gain, shift, scale):
    y = x * lax.rsqrt(jnp.mean(x * x, axis=-1, keepdims=True) + EPS)
    return (y * gain) * (1.0 + scale) + shift


def _mod_body(c_ref, w_ref, b_ref, o_ref):
    c = c_ref[...]
    s = (c * jax.nn.sigmoid(c)).astype(BF16)
    o_ref[...] = jnp.dot(s, w_ref[...].astype(BF16), preferred_element_type=F32) + b_ref[...]


def _modulation(cond, ada_w, ada_b):
    depth, d, n = ada_w.shape
    tn = MOD_COL_TILE
    return pl.pallas_call(
        _mod_body,
        grid=(depth, n // tn),
        in_specs=[
            pl.BlockSpec((SUBLANES, d), lambda l, j: (0, 0)),
            pl.BlockSpec((None, d, tn), lambda l, j: (l, 0, j)),
            pl.BlockSpec((None, 1, tn), lambda l, j: (l, 0, j)),
        ],
        out_specs=pl.BlockSpec((None, SUBLANES, tn), lambda l, j: (l, 0, j)),
        out_shape=jax.ShapeDtypeStruct((depth, SUBLANES, n), F32),
        compiler_params=_cparams(("parallel", "parallel"), 40),
        name="adaln_projection",
    )(cond, ada_w, ada_b.reshape(depth, 1, n))


def _group_select(i, prompt_tiles, p_ref, s_ref):
    return jnp.where(i < prompt_tiles, p_ref[...], s_ref[...])


def _group_specs(block, prompt_tiles):
    return [pl.BlockSpec(block, lambda i, *_: (jnp.minimum(i, prompt_tiles - 1), 0)),
            pl.BlockSpec(block, lambda i, *_: (jnp.maximum(i - prompt_tiles, 0), 0))]


def _token_specs(x_parts, tile, prompt_tiles):
    d = x_parts[0].shape[1]
    if len(x_parts) == 1:
        return [pl.BlockSpec((tile, d), lambda i, *_: (i, 0))]
    return _group_specs((tile, d), prompt_tiles)


def _token_tile(i, prompt_tiles, x_refs):
    return x_refs[0][...] if len(x_refs) == 1 else _group_select(i, prompt_tiles, *x_refs)


def _in0_body(*refs, n_x, prompt_tiles):
    x_refs = refs[:n_x]
    g_ref, sh_ref, sc_ref, w_ref, xb_ref, gg_ref, gu_ref, gv_ref = refs[n_x:]
    x = _token_tile(pl.program_id(0), prompt_tiles, x_refs)
    h = _modulated(x, g_ref[...], sh_ref[...], sc_ref[...])
    z = jnp.dot(h.astype(BF16), w_ref[...], preferred_element_type=F32)
    w = xb_ref.shape[1]
    xb_ref[...] = z[:, :w]
    gg_ref[...] = jax.nn.gelu(z[:, w:2 * w]).astype(BF16)
    gu_ref[...] = jax.nn.gelu(z[:, 2 * w:3 * w]).astype(BF16)
    gv_ref[...] = jax.nn.gelu(z[:, 3 * w:]).astype(BF16)


def _even_in_proj(x_parts, mod, layer, gain, w_in, n_prompt, sample_seq):
    t = sum(p.shape[0] for p in x_parts)
    d = x_parts[0].shape[1]
    tm = EVEN_IN_TILE
    w4 = w_in.shape[1]
    w = w4 // 4
    row = lambda i: (i, 0)
    ms = functools.partial(_mod_spec, layer, tile=tm, n_prompt=n_prompt, sample_seq=sample_seq, d=d)
    return pl.pallas_call(
        functools.partial(_in0_body, n_x=len(x_parts), prompt_tiles=n_prompt // tm),
        grid=(t // tm,),
        in_specs=[
            *_token_specs(x_parts, tm, n_prompt // tm),
            _resident((1, d)),
            ms(chunk=0),
            ms(chunk=1),
            _resident((d, w4)),
        ],
        out_specs=[pl.BlockSpec((tm, w), row)] * 4,
        out_shape=[
            jax.ShapeDtypeStruct((t, w), F32),
            jax.ShapeDtypeStruct((t, w), BF16),
            jax.ShapeDtypeStruct((t, w), BF16),
            jax.ShapeDtypeStruct((t, w), BF16),
        ],
        compiler_params=_cparams(("parallel",), 48),
        name="even_in_proj",
    )(*x_parts, gain.reshape(1, d), mod, mod, w_in.astype(BF16))


def _tile_scan(a, b, row, reverse):
    for d in (1, 2, 4):
        if reverse:
            keep = row < SUBLANES - d
            a_s = jnp.where(keep, pltpu.roll(a, SUBLANES - d, 0), 1.0)
            b_s = jnp.where(keep, pltpu.roll(b, SUBLANES - d, 0), 0.0)
        else:
            keep = row >= d
            a_s = jnp.where(keep, pltpu.roll(a, d, 0), 1.0)
            b_s = jnp.where(keep, pltpu.roll(b, d, 0), 0.0)
        b = b + a * b_s
        a = a * a_s
    return a, b


def _sigmoid(x):
    return 0.5 * jnp.tanh(0.5 * x) + 0.5


def _lru_body(xb_ref, gg_ref, h0f_ref, h0b_ref, cw_ref, cb_ref, w_ref, bias_ref, lam_ref,
              ya_ref, ff_ref, fb_ref,
              xp_ref, af_ref, bf_ref, ab_ref, bb_ref, hf_ref, hb_ref, *, seq, chunk, nseq):
    hw = LANES
    pad = SUBLANES
    pitch = seq + 2 * pad
    for s in range(nseq):
        xp_ref[pl.ds(s * pitch, pad), :] = jnp.zeros((pad, hw), F32)
        xp_ref[pl.ds(s * pitch + pad + seq, pad), :] = jnp.zeros((pad, hw), F32)
        xp_ref[pl.ds(s * pitch + pad, seq), :] = xb_ref[pl.ds(s * seq, seq), :]

    cw = cw_ref[...]
    cb = cb_ref[...]
    lam = lam_ref[...]
    neg = -lam
    softplus = jnp.maximum(neg, 0.0) + jnp.log1p(jnp.exp(-jnp.abs(neg)))
    nsp = -LRU_C * softplus
    w = w_ref[...]
    bias = bias_ref[...]

    for s in range(nseq):
        for c in range(seq // chunk):
            src = s * pitch + pad + c * chunk
            dst = s * seq + c * chunk
            xc = (cw[0:1] * xp_ref[pl.ds(src - 2, chunk), :]
                  + cw[1:2] * xp_ref[pl.ds(src - 1, chunk), :]
                  + cw[2:3] * xp_ref[pl.ds(src, chunk), :]
                  + cw[3:4] * xp_ref[pl.ds(src + 1, chunk), :]) + cb
            g = jnp.dot(xc.astype(BF16), w, preferred_element_type=F32) + bias
            for direction, (a_ref, b_ref) in enumerate(((af_ref, bf_ref), (ab_ref, bb_ref))):
                r = _sigmoid(g[:, (2 * direction) * hw:(2 * direction + 1) * hw])
                gi = _sigmoid(g[:, (2 * direction + 1) * hw:(2 * direction + 2) * hw])
                log_a = r * nsp[direction:direction + 1]
                a = jnp.exp(log_a)
                a_ref[pl.ds(dst, chunk), :] = a
                b_ref[pl.ds(dst, chunk), :] = jnp.sqrt(1.0 - a * a) * gi * xc

    n_tiles = seq // SUBLANES
    row = lax.broadcasted_iota(I32, (SUBLANES, hw), 0)

    def step(i, carry):
        new = []
        for s in range(nseq):
            hf, hb = carry[2 * s], carry[2 * s + 1]
            rf = pl.multiple_of(s * seq + i * SUBLANES, SUBLANES)
            rb = pl.multiple_of(s * seq + (n_tiles - 1 - i) * SUBLANES, SUBLANES)
            a, b = _tile_scan(af_ref[pl.ds(rf, SUBLANES), :], bf_ref[pl.ds(rf, SUBLANES), :], row, False)
            h = b + a * hf
            hf_ref[pl.ds(rf, SUBLANES), :] = h
            new.append(jnp.broadcast_to(h[SUBLANES - 1:SUBLANES, :], (SUBLANES, hw)))
            a, b = _tile_scan(ab_ref[pl.ds(rb, SUBLANES), :], bb_ref[pl.ds(rb, SUBLANES), :], row, True)
            h = b + a * hb
            hb_ref[pl.ds(rb, SUBLANES), :] = h
            new.append(jnp.broadcast_to(h[0:1, :], (SUBLANES, hw)))
        return tuple(new)

    init = []
    for s in range(nseq):
        init.append(jnp.broadcast_to(h0f_ref[s], (SUBLANES, hw)))
        init.append(jnp.broadcast_to(h0b_ref[s], (SUBLANES, hw)))
    final = lax.fori_loop(0, n_tiles, step, tuple(init), unroll=2 if nseq == 1 else 1)
    for s in range(nseq):
        ff_ref[s] = final[2 * s][0:1, :]
        fb_ref[s] = final[2 * s + 1][0:1, :]
    ya_ref[...] = ((hf_ref[...] + hb_ref[...]) * gg_ref[...].astype(F32)).astype(BF16)


def _lru_mixer(xb, gg, h0f, h0b, conv_w, conv_b, wcat, bcat, lam, *, row0, batch, seq, nseq):
    _, width = xb.shape
    heads = width // LANES
    rows = nseq * seq
    assert batch % nseq == 0 and row0 % rows == 0
    blk0 = row0 // rows
    tok = lambda b, h: (blk0 + b, h)
    state = lambda b, h: (b, 0, h)
    seq_buf = pltpu.VMEM((rows, LANES), F32)
    return pl.pallas_call(
        functools.partial(_lru_body, seq=seq, chunk=min(seq, LRU_GATE_CHUNK), nseq=nseq),
        grid=(batch // nseq, heads),
        in_specs=[
            pl.BlockSpec((rows, LANES), tok),
            pl.BlockSpec((rows, LANES), tok),
            pl.BlockSpec((nseq, 1, LANES), state),
            pl.BlockSpec((nseq, 1, LANES), state),
            pl.BlockSpec((conv_w.shape[0], LANES), lambda b, h: (0, h)),
            pl.BlockSpec((1, LANES), lambda b, h: (0, h)),
            pl.BlockSpec((None, LANES, 4 * LANES), lambda b, h: (h, 0, 0)),
            pl.BlockSpec((None, 1, 4 * LANES), lambda b, h: (h, 0, 0)),
            pl.BlockSpec((2, LANES), lambda b, h: (0, h)),
        ],
        out_specs=[
            pl.BlockSpec((rows, LANES), lambda b, h: (b, h)),
            pl.BlockSpec((nseq, 1, LANES), state),
            pl.BlockSpec((nseq, 1, LANES), state),
        ],
        out_shape=[
            jax.ShapeDtypeStruct((batch * seq, width), BF16),
            jax.ShapeDtypeStruct((batch, 1, width), F32),
            jax.ShapeDtypeStruct((batch, 1, width), F32),
        ],
        scratch_shapes=[pltpu.VMEM((nseq * (seq + 2 * SUBLANES), LANES), F32)] + [seq_buf] * 6,
        compiler_params=_cparams(("parallel", "parallel"), 40),
        name=f"rglru_seq{seq}",
    )(xb, gg, h0f, h0b, conv_w, conv_b, wcat, bcat, lam)


def _out0_body(*refs, n_x, chunk, prompt_tiles):
    x_refs = refs[:n_x]
    (gate_ref, yap_ref, yas_ref, gu_ref, gv_ref, vg_ref, ws_ref, bs_ref, wa_ref, wb_ref,
     o_ref, yb_ref) = refs[n_x:]
    tm = o_ref.shape[0]
    x = _token_tile(pl.program_id(0), prompt_tiles, x_refs)
    ya = _group_select(pl.program_id(0), prompt_tiles, yap_ref, yas_ref)
    v = gv_ref[...].astype(F32)
    vn = (v * lax.rsqrt(jnp.mean(v * v, axis=-1, keepdims=True) + EPS) * vg_ref[...]).astype(BF16)
    groups = ws_ref.shape[0]
    gd = vn.shape[1] // groups
    for c in range(tm // chunk):
        rows = slice(c * chunk, (c + 1) * chunk)
        for g in range(groups):
            cols = slice(g * gd, (g + 1) * gd)
            mixed = jnp.dot(ws_ref[g], vn[rows, cols], preferred_element_type=F32) + bs_ref[:, cols]
            yb_ref[rows, cols] = (gu_ref[rows, cols].astype(F32) * mixed).astype(BF16)
    y = (jnp.dot(ya, wa_ref[...], preferred_element_type=F32)
         + jnp.dot(yb_ref[...], wb_ref[...], preferred_element_type=F32))
    o_ref[...] = x + gate_ref[...] * y


def _even_out_proj(x_parts, mod, layer, ya_p, ya_s, gu, gv, v_gain, w_s, b_full, w_out, n_prompt, sample_seq):
    t = sum(p.shape[0] for p in x_parts)
    d = x_parts[0].shape[1]
    tm = OUT_PROJ_TILE
    w = ya_p.shape[1]
    row = lambda i: (i, 0)
    chunk = w_s.shape[1]
    prompt_tiles = n_prompt // tm
    return pl.pallas_call(
        functools.partial(_out0_body, n_x=len(x_parts), chunk=chunk, prompt_tiles=prompt_tiles),
        grid=(t // tm,),
        in_specs=[
            *_token_specs(x_parts, tm, prompt_tiles),
            _mod_spec(layer, 2, tm, n_prompt, sample_seq, d),
            *_group_specs((tm, w), prompt_tiles),
            pl.BlockSpec((tm, w), row),
            pl.BlockSpec((tm, w), row),
            _resident((1, w)),
            _resident(w_s.shape),
            _resident(b_full.shape),
            _resident((w, d)),
            _resident((w, d)),
        ],
        out_specs=pl.BlockSpec((tm, d), row),
        out_shape=jax.ShapeDtypeStruct((t, d), F32),
        scratch_shapes=[pltpu.VMEM((tm, w), BF16)],
        compiler_params=_cparams(("parallel",), 56),
        name="even_out_proj",
    )(*x_parts, mod, ya_p, ya_s, gu, gv, v_gain.reshape(1, w), w_s.astype(BF16), b_full,
      w_out[:w].astype(BF16), w_out[w:].astype(BF16))


def _swap_halves(x, lane):
    quarter = ROPE_QUARTER
    up = pltpu.roll(x, LANES - quarter, 1)
    down = pltpu.roll(x, quarter, 1)
    return jnp.where((lane % (2 * quarter)) < quarter, up, down)


def _mla_in_body(x_ref, g_ref, sh_ref, sc_ref, wd_ref, qan_ref, kvan_ref, wuq_ref, qn_ref, cos_ref, sin_ref,
                 q_ref, ckvp_ref, ckvs_ref, krp_ref, krs_ref,
                 *, heads, q_lora, kv_lora, qk_dim, sm_scale, sub, prompt_tiles):
    gn = qn_ref[0:1, :]
    gr = qn_ref[1:2, :]
    grs = qn_ref[2:3, :]
    lane = lax.broadcasted_iota(I32, (sub, LANES), 1)
    latents = []
    for r0 in range(0, x_ref.shape[0], sub):
        rows = pl.ds(r0, sub)
        h = _modulated(x_ref[rows, :], g_ref[...], sh_ref[...], sc_ref[...])
        z = jnp.dot(h.astype(BF16), wd_ref[...], preferred_element_type=F32)
        cq = z[:, :q_lora]
        cq = cq * lax.rsqrt(jnp.mean(cq * cq, axis=-1, keepdims=True) + EPS) * qan_ref[...]
        ckv = z[:, q_lora:q_lora + kv_lora]
        latents.append((rows, ckv * lax.rsqrt(jnp.mean(ckv * ckv, axis=-1, keepdims=True) + EPS) * kvan_ref[...],
                        z[:, q_lora + kv_lora:]))
        q = jnp.dot(cq.astype(BF16), wuq_ref[...], preferred_element_type=F32)
        cos = gr * cos_ref[rows, :]
        sin = grs * sin_ref[rows, :]
        for hd in range(heads):
            qn = q[:, 2 * hd * LANES:(2 * hd + 1) * LANES]
            qr = q[:, (2 * hd + 1) * LANES:(2 * hd + 2) * LANES]
            ss = jnp.sum(qn * qn + qr * qr, axis=-1, keepdims=True)
            rinv = lax.rsqrt(ss * (1.0 / qk_dim) + EPS) * sm_scale
            q_ref[rows, 2 * hd * LANES:(2 * hd + 1) * LANES] = (qn * gn * rinv).astype(BF16)
            rot = qr * cos + _swap_halves(qr, lane) * sin
            q_ref[rows, (2 * hd + 1) * LANES:(2 * hd + 2) * LANES] = (rot * rinv).astype(BF16)
    i = pl.program_id(0)
    for (c_ref, k_ref), own in zip(((ckvp_ref, krp_ref), (ckvs_ref, krs_ref)), (i < prompt_tiles, i >= prompt_tiles)):
        @pl.when(own)
        def _():
            for rows, ckv_n, kr_v in latents:
                c_ref[rows, :] = ckv_n
                k_ref[rows, :] = kr_v


def _rope_spec(tile, n_prompt, sample_seq):
    def index(i):
        t = i * tile
        return (jnp.where(t < n_prompt, 0, 1 + ((t - n_prompt) % sample_seq) // tile), 0)

    return pl.BlockSpec((tile, LANES), index)


def _mla_in_proj(x, mod, layer, gain, wd_ext, q_a_norm, kv_a_norm, wuq_pad, qn_rows, cos_tab, sin_tab,
                 n_prompt, sample_seq, *, heads, q_lora, kv_lora, qk_dim):
    t, d = x.shape
    tm = MLA_TOKEN_TILE
    row = lambda i: (i, 0)
    ms = functools.partial(_mod_spec, layer, tile=tm, n_prompt=n_prompt, sample_seq=sample_seq, d=d)
    prompt_tiles = n_prompt // tm
    body = functools.partial(_mla_in_body, heads=heads, q_lora=q_lora, kv_lora=kv_lora, qk_dim=qk_dim,
                             sm_scale=float(qk_dim) ** -0.5 * LOG2_E, sub=TOKEN_TILE, prompt_tiles=prompt_tiles)
    return pl.pallas_call(
        body,
        grid=(t // tm,),
        in_specs=[
            pl.BlockSpec((tm, d), row),
            _resident((1, d)),
            ms(chunk=0),
            ms(chunk=1),
            _resident(wd_ext.shape),
            _resident((1, q_lora)),
            _resident((1, kv_lora)),
            _resident(wuq_pad.shape),
            _resident(qn_rows.shape),
            _rope_spec(tm, n_prompt, sample_seq),
            _rope_spec(tm, n_prompt, sample_seq),
        ],
        out_specs=[
            pl.BlockSpec((tm, heads * 2 * LANES), row),
            *_group_specs((tm, kv_lora), prompt_tiles),
            *_group_specs((tm, LANES), prompt_tiles),
        ],
        out_shape=[
            jax.ShapeDtypeStruct((t, heads * 2 * LANES), BF16),
            jax.ShapeDtypeStruct((n_prompt, kv_lora), F32),
            jax.ShapeDtypeStruct((t - n_prompt, kv_lora), F32),
            jax.ShapeDtypeStruct((n_prompt, LANES), F32),
            jax.ShapeDtypeStruct((t - n_prompt, LANES), F32),
        ],
        compiler_params=_cparams(("arbitrary",), 56),
        name="mla_in_proj",
    )(x, gain.reshape(1, d), mod, mod, wd_ext, q_a_norm.reshape(1, q_lora), kv_a_norm.reshape(1, kv_lora),
      wuq_pad, qn_rows, cos_tab, sin_tab)


def _kv_body(ckv_ref, kr_ref, w_ref, kn_ref, cos_ref, sin_ref, k_ref, v_ref, *, heads, qk_dim):
    _kv_compute(ckv_ref[...], kr_ref[...], w_ref, kn_ref, cos_ref, sin_ref, k_ref, v_ref, heads=heads, qk_dim=qk_dim)


def _kv_latent_body(cc_ref, ck_ref, oc_ref, ok_ref, w_ref, kn_ref, cos_ref, sin_ref, k_ref, v_ref,
                    *, heads, qk_dim, tiles_per_req, ctx_tiles):
    from_ctx = (pl.program_id(0) % tiles_per_req) < ctx_tiles
    ckv = jnp.where(from_ctx, cc_ref[...], oc_ref[...])
    kr = jnp.where(from_ctx, ck_ref[...], ok_ref[...])
    _kv_compute(ckv, kr, w_ref, kn_ref, cos_ref, sin_ref, k_ref, v_ref, heads=heads, qk_dim=qk_dim)


def _kv_compute(ckv, kr, w_ref, kn_ref, cos_ref, sin_ref, k_ref, v_ref, *, heads, qk_dim):
    kv = jnp.dot(ckv.astype(BF16), w_ref[...], preferred_element_type=F32)
    tm = kv.shape[0]
    lane = lax.broadcasted_iota(I32, (tm, LANES), 1)
    gn = kn_ref[0:1, :]
    gr = kn_ref[1:2, :]
    grs = kn_ref[2:3, :]
    ssr = jnp.sum(kr * kr, axis=-1, keepdims=True)
    rot = (kr * gr) * cos_ref[...] + (_swap_halves(kr, lane) * grs) * sin_ref[...]
    for hd in range(heads):
        kn = kv[:, hd * LANES:(hd + 1) * LANES]
        rinv = lax.rsqrt((jnp.sum(kn * kn, axis=-1, keepdims=True) + ssr) * (1.0 / qk_dim) + EPS)
        k_ref[:, 2 * hd * LANES:(2 * hd + 1) * LANES] = (kn * gn * rinv).astype(BF16)
        k_ref[:, (2 * hd + 1) * LANES:(2 * hd + 2) * LANES] = (rot * rinv).astype(BF16)
    v_ref[...] = kv[:, heads * LANES:].astype(BF16)


def _kv_expand_latent(ctx_ckv, ctx_kr, ckv, kr, w_ukv_re, kn_rows, cos_tab, sin_tab,
                      *, row0, batch, past, seq, ident_rows, heads, qk_dim):
    kv_lora = ckv.shape[1]
    tm = TOKEN_TILE
    tiles_per_req = (past + seq) // tm
    ctx_tiles = past // tm
    own_tiles = seq // tm
    rows = batch * (past + seq)
    row = lambda i: (i, 0)

    def ctx_index(i):
        return ((i // tiles_per_req) * ctx_tiles + jnp.minimum(i % tiles_per_req, ctx_tiles - 1), 0)

    def own_index(i):
        return (row0 // tm + (i // tiles_per_req) * own_tiles + jnp.maximum(i % tiles_per_req - ctx_tiles, 0), 0)

    def rope_index(i):
        r = i % tiles_per_req
        return (jnp.where(r < ctx_tiles, 0, ident_rows // tm + r - ctx_tiles), 0)

    return pl.pallas_call(
        functools.partial(_kv_latent_body, heads=heads, qk_dim=qk_dim, tiles_per_req=tiles_per_req,
                          ctx_tiles=ctx_tiles),
        grid=(rows // tm,),
        in_specs=[
            pl.BlockSpec((tm, kv_lora), ctx_index),
            pl.BlockSpec((tm, LANES), ctx_index),
            pl.BlockSpec((tm, kv_lora), own_index),
            pl.BlockSpec((tm, LANES), own_index),
            _resident(w_ukv_re.shape),
            _resident(kn_rows.shape),
            pl.BlockSpec((tm, LANES), rope_index),
            pl.BlockSpec((tm, LANES), rope_index),
        ],
        out_specs=[pl.BlockSpec((tm, heads * 2 * LANES), row), pl.BlockSpec((tm, heads * LANES), row)],
        out_shape=[
            jax.ShapeDtypeStruct((rows, heads * 2 * LANES), BF16),
            jax.ShapeDtypeStruct((rows, heads * LANES), BF16),
        ],
        compiler_params=_cparams(("parallel",), 40),
        name="mla_kv_expand_latent",
    )(ctx_ckv, ctx_kr, ckv, kr, w_ukv_re, kn_rows, cos_tab, sin_tab)


def _attn_body(q_ref, k_ref, v_ref, o_ref, *, heads_per_step):
    for hd in range(heads_per_step):
        qk = slice(2 * hd * LANES, 2 * (hd + 1) * LANES)
        vo = slice(hd * LANES, (hd + 1) * LANES)
        s = lax.dot_general(q_ref[:, qk], k_ref[:, qk], (((1,), (1,)), ((), ())), preferred_element_type=F32)
        m = jnp.max(s, axis=-1, keepdims=True)
        p = jnp.exp2(s - m)
        l = jnp.sum(p, axis=-1, keepdims=True)
        o = jnp.dot(p.astype(BF16), v_ref[:, vo], preferred_element_type=F32)
        o_ref[:, vo] = (o * (1.0 / l)).astype(BF16)


def _attention(q, k, v, *, row0, batch, seq, kv_len, heads, tq, heads_per_step):
    nq = seq // tq
    qblk0 = row0 // tq
    hps = heads_per_step
    return pl.pallas_call(
        functools.partial(_attn_body, heads_per_step=hps),
        grid=(batch, heads // hps, nq),
        in_specs=[
            pl.BlockSpec((tq, hps * 2 * LANES), lambda b, h, i: (qblk0 + b * nq + i, h)),
            pl.BlockSpec((kv_len, hps * 2 * LANES), lambda b, h, i: (b, h)),
            pl.BlockSpec((kv_len, hps * LANES), lambda b, h, i: (b, h)),
        ],
        out_specs=pl.BlockSpec((tq, hps * LANES), lambda b, h, i: (b * nq + i, h)),
        out_shape=jax.ShapeDtypeStruct((batch * seq, heads * LANES), BF16),
        compiler_params=_cparams(("parallel", "parallel", "parallel"), 48),
        name=f"mla_attention_kv{kv_len}",
    )(q, k, v)


def _self_attn_body(q_ref, ckv_ref, kr_ref, w_ref, kn_ref, cos_ref, sin_ref, o_ref, k_scr, v_scr, *, heads, qk_dim):
    _kv_body(ckv_ref, kr_ref, w_ref, kn_ref, cos_ref, sin_ref, k_scr, v_scr, heads=heads, qk_dim=qk_dim)
    _attn_body(q_ref, k_scr, v_scr, o_ref, heads_per_step=heads)


def _self_attention(q, ckv, kr, w_ukv_re, kn_rows, cos_tab, sin_tab, *, batch, seq, heads, qk_dim):
    kv_lora = ckv.shape[1]
    tok = lambda b: (b, 0)
    return pl.pallas_call(
        functools.partial(_self_attn_body, heads=heads, qk_dim=qk_dim),
        grid=(batch,),
        in_specs=[
            pl.BlockSpec((seq, heads * 2 * LANES), tok),
            pl.BlockSpec((seq, kv_lora), tok),
            pl.BlockSpec((seq, LANES), tok),
            _resident(w_ukv_re.shape),
            _resident(kn_rows.shape),
            pl.BlockSpec((seq, LANES), lambda b: (0, 0)),
            pl.BlockSpec((seq, LANES), lambda b: (0, 0)),
        ],
        out_specs=pl.BlockSpec((seq, heads * LANES), tok),
        out_shape=jax.ShapeDtypeStruct((batch * seq, heads * LANES), BF16),
        scratch_shapes=[pltpu.VMEM((seq, heads * 2 * LANES), BF16), pltpu.VMEM((seq, heads * LANES), BF16)],
        compiler_params=_cparams(("parallel",), 40),
        name="mla_self_attention",
    )(q, ckv, kr, w_ukv_re, kn_rows, cos_tab, sin_tab)


def _oproj_body(x_ref, gate_ref, op_ref, os_ref, w_ref, out_ref, *, prompt_tiles):
    o = _group_select(pl.program_id(0), prompt_tiles, op_ref, os_ref)
    y = jnp.dot(o, w_ref[...], preferred_element_type=F32)
    out_ref[...] = x_ref[...] + gate_ref[...] * y


def _mla_out_proj(x, mod, layer, o_p, o_s, w_o, n_prompt, sample_seq):
    t, d = x.shape
    tm = OUT_PROJ_TILE
    row = lambda i: (i, 0)
    prompt_tiles = n_prompt // tm
    return pl.pallas_call(
        functools.partial(_oproj_body, prompt_tiles=prompt_tiles),
        grid=(t // tm,),
        in_specs=[
            pl.BlockSpec((tm, d), row),
            _mod_spec(layer, 2, tm, n_prompt, sample_seq, d),
            *_group_specs((tm, o_p.shape[1]), prompt_tiles),
            _resident(w_o.shape),
        ],
        out_specs=pl.BlockSpec((tm, d), row),
        out_shape=jax.ShapeDtypeStruct((t, d), F32),
        compiler_params=_cparams(("parallel",), 48),
        name="mla_out_proj",
    )(x, mod, o_p, o_s, w_o.astype(BF16))


def _round_up(v, m):
    return (v + m - 1) // m * m


def _local_rows(tile):
    return TOP_K * tile + RUN_PAD_ROWS


def _router_body(x_ref, g_ref, sh_ref, sc_ref, wst_ref, rb_ref, hb_ref, wt_ref, lp_ref, cnt_ref):
    h = _modulated(x_ref[...], g_ref[...], sh_ref[...], sc_ref[...])
    tm, d = h.shape
    n_experts = rb_ref.shape[0]

    h_hi = h.astype(BF16)
    hb_ref[...] = h_hi
    h_lo = (h - h_hi.astype(F32)).astype(BF16)
    nt = (((1,), (1,)), ((), ()))
    both = lax.dot_general(wst_ref[...], h_hi, nt, preferred_element_type=F32)
    logits = (both[:n_experts] + both[n_experts:]
              + lax.dot_general(wst_ref[:n_experts, :], h_lo, nt, preferred_element_type=F32))
    scores = jax.nn.sigmoid(logits)
    biased = scores + rb_ref[...]
    per_group = n_experts // N_EXPERT_GROUPS
    assert per_group == 4 and TOP_K == 2
    b_rows = [biased[e:e + 1, :] for e in range(n_experts)]
    s_rows = [scores[e:e + 1, :] for e in range(n_experts)]

    best = None
    sel = jnp.zeros((1, tm), I32)
    for g in range(N_EXPERT_GROUPS):
        b0, b1, b2, b3 = b_rows[g * 4:(g + 1) * 4]
        m1, n1 = jnp.maximum(b0, b1), jnp.minimum(b0, b1)
        m2, n2 = jnp.maximum(b2, b3), jnp.minimum(b2, b3)
        top1 = jnp.maximum(m1, m2)
        top2 = jnp.maximum(jnp.minimum(m1, m2), jnp.maximum(n1, n2))
        gsum = top1 + top2
        if best is None:
            best = gsum
        else:
            better = gsum > best
            sel = jnp.where(better, g, sel)
            best = jnp.where(better, gsum, best)

    def pick(rows, j):
        out = rows[j]
        for g in range(1, N_EXPERT_GROUPS):
            out = jnp.where(sel == g, rows[g * 4 + j], out)
        return out

    cand_b = [pick(b_rows, j) for j in range(4)]
    cand_s = [pick(s_rows, j) for j in range(4)]

    def argmax4(vals):
        bv, bi = vals[0], jnp.zeros((1, tm), I32)
        for j in range(1, 4):
            gt = vals[j] > bv
            bi = jnp.where(gt, j, bi)
            bv = jnp.where(gt, vals[j], bv)
        return bi

    i1 = argmax4(cand_b)
    i2 = argmax4([jnp.where(i1 == j, -jnp.inf, cand_b[j]) for j in range(4)])

    def take(vals, idx):
        out = vals[0]
        for j in range(1, 4):
            out = jnp.where(idx == j, vals[j], out)
        return out

    s1 = take(cand_s, i1)
    s2 = take(cand_s, i2)
    tot = s1 + s2
    e1 = sel * 4 + i1
    e2 = sel * 4 + i2
    wt_ref[0:1, :] = s1 / tot
    wt_ref[1:2, :] = s2 / tot

    eid = lax.broadcasted_iota(I32, (n_experts, tm), 0)
    is1 = eid == e1
    is2 = eid == e2
    chosen = jnp.where(is1 | is2, 1.0, 0.0)
    before = (lax.broadcasted_iota(I32, (tm, tm), 0) < lax.broadcasted_iota(I32, (tm, tm), 1))
    rank = jnp.dot(chosen.astype(BF16), jnp.where(before, 1.0, 0.0).astype(BF16), preferred_element_type=F32)
    count = jnp.sum(chosen, axis=1, keepdims=True)
    padded = jnp.floor((count + (RUN_ALIGN - 1)) * (1.0 / RUN_ALIGN)) * RUN_ALIGN
    lower = (lax.broadcasted_iota(I32, (n_experts, n_experts), 1)
             < lax.broadcasted_iota(I32, (n_experts, n_experts), 0))
    run_start = jnp.dot(jnp.where(lower, 1.0, 0.0).astype(BF16),
                        jnp.broadcast_to(padded, (n_experts, LANES)).astype(BF16),
                        preferred_element_type=F32)[:, 0:1]
    row = run_start + rank
    lp_ref[0:1, :] = jnp.sum(jnp.where(is1, row, 0.0), axis=0, keepdims=True).astype(I32)
    lp_ref[1:2, :] = jnp.sum(jnp.where(is2, row, 0.0), axis=0, keepdims=True).astype(I32)
    cnt_ref[...] = jnp.broadcast_to(count, (n_experts, LANES))


def _router(x, mod, layer, gain, w_stack, r_bias, n_prompt, sample_seq):
    t, d = x.shape
    tm = TOKEN_TILE
    n_experts = r_bias.shape[0]
    ms = functools.partial(_mod_spec, layer, tile=tm, n_prompt=n_prompt, sample_seq=sample_seq, d=d)
    col = lambda i: (0, i)
    return pl.pallas_call(
        _router_body,
        grid=(t // tm,),
        in_specs=[
            pl.BlockSpec((tm, d), lambda i: (i, 0)),
            _resident((1, d)),
            ms(chunk=3),
            ms(chunk=4),
            _resident(w_stack.shape),
            _resident((n_experts, 1)),
        ],
        out_specs=[
            pl.BlockSpec((tm, d), lambda i: (i, 0)),
            pl.BlockSpec((TOP_K, tm), col),
            pl.BlockSpec((TOP_K, tm), col),
            pl.BlockSpec((None, n_experts, LANES), lambda i: (i, 0, 0)),
        ],
        out_shape=[
            jax.ShapeDtypeStruct((t, d), BF16),
            jax.ShapeDtypeStruct((TOP_K, t), F32),
            jax.ShapeDtypeStruct((TOP_K, t), I32),
            jax.ShapeDtypeStruct((t // tm, n_experts, LANES), F32),
        ],
        compiler_params=_cparams(("parallel",), 40),
        name="moe_router",
    )(x, gain.reshape(1, d), mod, mod, w_stack, r_bias.reshape(n_experts, 1))


def _run_copies(tile, local_ref, slot, sorted_ref, row_ref, count_ref, sem, to_sorted):
    max_chunks = local_ref.shape[1] // RUN_ALIGN
    min_chunks = TOP_K * TOKEN_TILE // RUN_ALIGN

    def directed(loc, far):
        return pltpu.make_async_copy(loc, far, sem) if to_sorted else pltpu.make_async_copy(far, loc, sem)

    def chunk(c):
        loc = local_ref.at[slot, pl.ds(pl.multiple_of(c * RUN_ALIGN, RUN_ALIGN), RUN_ALIGN)]
        first = row_ref[tile * max_chunks + c]
        return directed(loc, sorted_ref.at[pl.ds(pl.multiple_of(first, RUN_ALIGN), RUN_ALIGN)])

    def apply(op):
        def one(c, carry):
            getattr(chunk(c), op)()
            return carry

        if op == "start":
            lax.fori_loop(0, min_chunks, one, 0, unroll=8)
        else:
            rows = min_chunks * RUN_ALIGN
            directed(local_ref.at[slot, pl.ds(0, rows)], sorted_ref.at[pl.ds(0, rows)]).wait()
        lax.fori_loop(min_chunks, count_ref[tile], one, 0)

    return apply


def _one_hot_rows(lp_ref, rows, tm):
    j = lax.broadcasted_iota(I32, (rows, tm), 0)
    hit = (j == lp_ref[0:1, :]) | (j == lp_ref[1:2, :])
    return jnp.where(hit, 1.0, 0.0).astype(BF16)


def _dispatch_body(row_ref, chunk_ref, last_ref, nv_ref, lp_ref, h_ref, o_ref, zero_ref, loc_ref, sems, zsem):
    tm = h_ref.shape[0]
    zt = zero_ref.shape[0]
    n_tiles = o_ref.shape[0] // zt
    i = pl.program_id(0)
    last = pl.num_programs(0) - 1
    slot = i % 2

    def copies(tile, s):
        return _run_copies(tile, loc_ref, s, o_ref, row_ref, chunk_ref, sems.at[s], True)

    @pl.when(i == 0)
    def _():
        zero_ref[...] = jnp.zeros(zero_ref.shape, zero_ref.dtype)

        def zero_tile(row):
            return pltpu.make_async_copy(zero_ref, o_ref.at[pl.ds(pl.multiple_of(row, zt), zt)], zsem)

        for e in range(last_ref.shape[0]):
            @pl.when(last_ref[e] >= 0)
            def _():
                zero_tile(last_ref[e]).start()

        def start_tail(j, c):
            zero_tile(j * zt).start()
            return c

        lax.fori_loop(nv_ref[0], n_tiles, start_tail, 0)
        for e in range(last_ref.shape[0]):
            @pl.when(last_ref[e] >= 0)
            def _():
                zero_tile(last_ref[e]).wait()

        def wait_tail(j, c):
            zero_tile(j * zt).wait()
            return c

        lax.fori_loop(nv_ref[0], n_tiles, wait_tail, 0)

    @pl.when(i >= 2)
    def _():
        copies(i - 2, slot)("wait")

    loc_ref[slot] = jnp.dot(_one_hot_rows(lp_ref, loc_ref.shape[1], tm), h_ref[...],
                            preferred_element_type=F32).astype(BF16)
    copies(i, slot)("start")

    @pl.when(i == last)
    def _():
        @pl.when(i >= 1)
        def _():
            copies(i - 1, 1 - slot)("wait")
        copies(i, slot)("wait")


def _dispatch(hb, lp, chunk_row, tile_chunks, last_tile_row, n_valid, sorted_rows):
    t, d = hb.shape
    tm = TOKEN_TILE
    grid_spec = pltpu.PrefetchScalarGridSpec(
        num_scalar_prefetch=4,
        grid=(t // tm,),
        in_specs=[
            pl.BlockSpec((TOP_K, tm), lambda i, *_: (0, i)),
            pl.BlockSpec((tm, d), lambda i, *_: (i, 0)),
        ],
        out_specs=pl.BlockSpec(memory_space=pl.ANY),
        scratch_shapes=[pltpu.VMEM((EXPERT_TILE, d), BF16), pltpu.VMEM((2, _local_rows(tm), d), BF16),
                        pltpu.SemaphoreType.DMA((2,)), pltpu.SemaphoreType.DMA],
    )
    return pl.pallas_call(
        _dispatch_body,
        grid_spec=grid_spec,
        out_shape=jax.ShapeDtypeStruct((sorted_rows, d), BF16),
        compiler_params=_cparams(("arbitrary",), 40),
        name="moe_dispatch",
    )(chunk_row, tile_chunks, last_tile_row, n_valid, lp, hb)


def _expert_body(te_ref, nv_ref, nxt_ref, par_ref, x_ref, wg_hbm, wu_hbm, wd_hbm, y_ref,
                 wgf_ref, wuf_ref, wdf_ref, wgb_ref, wub_ref, wdb_ref, sems, *, layer):
    i = pl.program_id(0)
    valid = i < nv_ref[0]
    new_expert = (i == 0) | (te_ref[i] != te_ref[jnp.maximum(i - 1, 0)])

    def weight_copies(e, slot):
        return [pltpu.make_async_copy(hbm.at[layer, e], buf.at[slot], sems.at[slot])
                for hbm, buf in ((wg_hbm, wgf_ref), (wu_hbm, wuf_ref), (wd_hbm, wdf_ref))]

    @pl.when(valid & new_expert)
    def _():
        slot = par_ref[i]

        @pl.when(i == 0)
        def _():
            for c in weight_copies(te_ref[0], slot):
                c.start()

        for c in weight_copies(te_ref[i], slot):
            c.wait()

        @pl.when(nxt_ref[i] >= 0)
        def _():
            for c in weight_copies(nxt_ref[i], 1 - slot):
                c.start()

        wgb_ref[...] = wgf_ref[slot].astype(BF16)
        wub_ref[...] = wuf_ref[slot].astype(BF16)
        wdb_ref[...] = wdf_ref[slot].astype(BF16)

    @pl.when(valid)
    def _():
        xb = x_ref[...]
        g = jnp.dot(xb, wgb_ref[...], preferred_element_type=F32)
        u = jnp.dot(xb, wub_ref[...], preferred_element_type=F32)
        act = (g * jax.nn.sigmoid(g) * u).astype(BF16)
        y_ref[...] = jnp.dot(act, wdb_ref[...], preferred_element_type=F32).astype(BF16)

    @pl.when(i >= nv_ref[0])
    def _():
        y_ref[...] = jnp.zeros(y_ref.shape, BF16)


def _experts(xs, tile_expert, n_valid, tile_next, tile_slot, w_gate, w_up, w_down, layer):
    rows, d = xs.shape
    tm = EXPERT_TILE
    _, n_experts, _, f = w_gate.shape

    def xrow(i, te, nv, *_):
        return (jnp.minimum(i, nv[0] - 1), 0)

    hbm = pl.BlockSpec(memory_space=pl.ANY)
    grid_spec = pltpu.PrefetchScalarGridSpec(
        num_scalar_prefetch=4,
        grid=(rows // tm,),
        in_specs=[pl.BlockSpec((tm, d), xrow), hbm, hbm, hbm],
        out_specs=pl.BlockSpec((tm, d), lambda i, *_: (i, 0)),
        scratch_shapes=[pltpu.VMEM((2, d, f), F32), pltpu.VMEM((2, d, f), F32), pltpu.VMEM((2, f, d), F32),
                        pltpu.VMEM((d, f), BF16), pltpu.VMEM((d, f), BF16), pltpu.VMEM((f, d), BF16),
                        pltpu.SemaphoreType.DMA((2,))],
    )
    return pl.pallas_call(
        functools.partial(_expert_body, layer=layer),
        grid_spec=grid_spec,
        out_shape=jax.ShapeDtypeStruct((rows, d), BF16),
        compiler_params=_cparams(("arbitrary",), 56),
        name="moe_experts",
    )(tile_expert, n_valid, tile_next, tile_slot, xs, w_gate, w_up, w_down)


def _combine_body(row_ref, chunk_ref, lpt_ref, wtt_ref, x_ref, gate_ref, ys_ref, *rest, prompt_tiles):
    o_refs, (loc_ref, sems) = rest[:-2], rest[-2:]
    tm = x_ref.shape[0]
    rows = loc_ref.shape[1]
    i = pl.program_id(0)
    slot = i % 2

    def copies(tile, s):
        return _run_copies(tile, loc_ref, s, ys_ref, row_ref, chunk_ref, sems.at[s], False)

    @pl.when(i == 0)
    def _():
        loc_ref[...] = jnp.zeros(loc_ref.shape, loc_ref.dtype)
        copies(0, 0)("start")

    @pl.when(i + 1 < pl.num_programs(0))
    def _():
        copies(i + 1, 1 - slot)("start")

    copies(i, slot)("wait")

    jt = lax.broadcasted_iota(I32, (tm, rows), 1)
    local = loc_ref[slot]
    moe = jnp.zeros((tm, local.shape[1]), F32)
    for k in range(TOP_K):
        pick = jnp.where(jt == lpt_ref[:, k:k + 1], 1.0, 0.0).astype(BF16)
        moe = moe + wtt_ref[:, k:k + 1] * jnp.dot(pick, local, preferred_element_type=F32)
    out = x_ref[...] + gate_ref[...] * moe
    if len(o_refs) == 1:
        o_refs[0][...] = out
    else:
        for o_ref, own in zip(o_refs, (i < prompt_tiles, i >= prompt_tiles)):
            @pl.when(own)
            def _():
                o_ref[...] = out


def _combine(x, mod, layer, ys, lp, wt, chunk_row, tile_chunks, n_prompt, sample_seq, split):
    t, d = x.shape
    tm = TOKEN_TILE
    prompt_tiles = n_prompt // tm
    tok = lambda i, *_: (i, 0)
    col = lambda i, *_: (0, i)
    if split:
        out_specs = _group_specs((tm, d), prompt_tiles)
        out_shape = [jax.ShapeDtypeStruct((n_prompt, d), F32), jax.ShapeDtypeStruct((t - n_prompt, d), F32)]
    else:
        out_specs = [pl.BlockSpec((tm, d), tok)]
        out_shape = [jax.ShapeDtypeStruct((t, d), F32)]
    grid_spec = pltpu.PrefetchScalarGridSpec(
        num_scalar_prefetch=2,
        grid=(t // tm,),
        in_specs=[
            pl.BlockSpec((tm, TOP_K), tok),
            pl.BlockSpec((tm, TOP_K), tok),
            pl.BlockSpec((tm, d), tok),
            _mod_spec(layer, 5, tm, n_prompt, sample_seq, d),
            pl.BlockSpec(memory_space=pl.ANY),
        ],
        out_specs=out_specs,
        scratch_shapes=[pltpu.VMEM((2, _local_rows(tm), d), BF16), pltpu.SemaphoreType.DMA((2,))],
    )
    return pl.pallas_call(
        functools.partial(_combine_body, prompt_tiles=prompt_tiles),
        grid_spec=grid_spec,
        out_shape=out_shape,
        compiler_params=_cparams(("arbitrary",), 40),
        name="moe_combine",
    )(chunk_row, tile_chunks, lp.T, wt.T, x, mod, ys)


def _moe(x, mod, layer, gain, w_stack, r_bias, w_gate, w_up, w_down, n_prompt, sample_seq, split):
    t, d = x.shape
    n_experts = r_bias.shape[0]
    assert n_experts * (RUN_ALIGN - 1) <= RUN_PAD_ROWS
    hb, wt, lp, cnt = _router(x, mod, layer, gain, w_stack, r_bias, n_prompt, sample_seq)
    n_tok_tiles = t // TOKEN_TILE
    run_len = _round_up(cnt[:, :, 0].astype(I32), RUN_ALIGN)
    run_before = jnp.cumsum(run_len, axis=0) - run_len
    tiles = (jnp.sum(run_len, axis=0) + EXPERT_TILE - 1) // EXPERT_TILE
    tile_end = jnp.cumsum(tiles)
    seg_start = (tile_end - tiles) * EXPERT_TILE
    run_row = seg_start[None, :] + run_before
    run_chunks = run_len // RUN_ALIGN
    chunk_end = jnp.cumsum(run_chunks, axis=1)
    chunk_ids = jnp.arange(_local_rows(TOKEN_TILE) // RUN_ALIGN, dtype=I32)
    owner = jnp.sum((chunk_end[:, None, :] <= chunk_ids[None, :, None]).astype(I32), axis=2)
    owner = jnp.minimum(owner, n_experts - 1)
    is_owner = owner[:, :, None] == jnp.arange(n_experts, dtype=I32)
    pick = lambda a: jnp.sum(jnp.where(is_owner, a[:, None, :], 0), axis=2)
    chunk_row = pick(run_row) + (chunk_ids[None, :] - pick(chunk_end - run_chunks)) * RUN_ALIGN
    chunk_row = chunk_row.reshape(-1).astype(I32)
    tile_chunks = chunk_end[:, -1].astype(I32)
    sorted_rows = (_round_up(TOP_K * t + n_tok_tiles * n_experts * (RUN_ALIGN - 1), EXPERT_TILE)
                   + n_experts * EXPERT_TILE)
    n_tiles = sorted_rows // EXPERT_TILE
    n_valid = tile_end[-1:]
    tile_ids = jnp.arange(n_tiles, dtype=I32)
    ends_before = lambda i: jnp.sum((tile_end[None, :] <= i[:, None]).astype(I32), axis=1)
    tile_expert = jnp.where(tile_ids < n_valid[0], ends_before(tile_ids), ends_before(n_valid - 1))
    tile_expert = jnp.minimum(tile_expert, n_experts - 1).astype(I32)
    last_tile_row = jnp.where(tiles > 0, (tile_end - 1) * EXPERT_TILE, -1).astype(I32)
    n_valid = n_valid.astype(I32)
    e_ids = jnp.arange(n_experts, dtype=I32)
    used = tiles > 0
    later_used = (e_ids[None, :] > e_ids[:, None]) & used[None, :]
    next_used = jnp.min(jnp.where(later_used, e_ids[None, :], n_experts), axis=1)
    next_used = jnp.where(next_used < n_experts, next_used, -1)
    slot_of = (jnp.cumsum(used.astype(I32)) - 1) % 2
    of_tile = lambda a: jnp.sum(jnp.where(tile_expert[:, None] == e_ids[None, :], a[None, :], 0), axis=1).astype(I32)
    xs = _dispatch(hb, lp, chunk_row, tile_chunks, last_tile_row, n_valid, sorted_rows)
    ys = _experts(xs, tile_expert, n_valid, of_tile(next_used), of_tile(slot_of), w_gate, w_up, w_down, layer)
    return _combine(x, mod, layer, ys, lp, wt, chunk_row, tile_chunks, n_prompt, sample_seq, split)


def _rope_tables(sample_seq, rope_dim, tile):
    n_freq = rope_dim // 4
    pos = np.arange(sample_seq)
    inv_freq = ROPE_BASE ** (-np.arange(n_freq, dtype=np.float64) / n_freq)
    ar = (pos // GRID_W)[:, None] * inv_freq
    ac = (pos % GRID_W)[:, None] * inv_freq
    zeros = np.zeros((sample_seq, LANES - rope_dim))
    cos = np.concatenate([np.cos(ar), np.cos(ar), np.cos(ac), np.cos(ac), zeros], axis=1)
    sin = np.concatenate([-np.sin(ar), np.sin(ar), -np.sin(ac), np.sin(ac), zeros], axis=1)
    ident_c = np.concatenate([np.ones((tile, rope_dim)), np.zeros((tile, LANES - rope_dim))], axis=1)
    return (jnp.asarray(np.concatenate([ident_c, cos], axis=0), F32),
            jnp.asarray(np.concatenate([np.zeros((tile, LANES)), sin], axis=0), F32))


def _norm_rows(gain, nope, rope_dim):
    quarter = rope_dim // 4
    gr = gain[nope:]
    grs = jnp.concatenate([gr[quarter:2 * quarter], gr[:quarter], gr[3 * quarter:], gr[2 * quarter:3 * quarter]])
    zpad = jnp.zeros((LANES - rope_dim,), F32)
    return jnp.stack([gain[:nope], jnp.concatenate([gr, zpad]), jnp.concatenate([grs, zpad])])


def kernel(x_prompt, x_sample, state_lru_fwd, state_lru_bwd, cache_mla_ckv, cache_mla_krope, c, c_ctx,
           ada_w, ada_b, norm_mix, norm_ffn, mix0_w_in, mix0_w_out, lru_conv_w, lru_conv_b,
           lru_w_r, lru_b_r, lru_w_i, lru_b_i, lru_lambda, gmlp_v_norm, gmlp_w_s, gmlp_b_s,
           mla_w_down, mla_q_a_norm, mla_kv_a_norm, mla_w_uq, mla_w_ukv, mla_q_norm, mla_k_norm, mla_w_o,
           router_w, router_bias, moe_w_gate, moe_w_up, moe_w_down):
    batch, seq, d = x_prompt.shape
    dec_batch, dec_seq, _ = x_sample.shape
    depth = ada_w.shape[0]
    n_prompt = batch * seq
    n_sample = dec_batch * dec_seq
    assert n_prompt % dec_seq == 0 and seq % TOKEN_TILE == 0 and dec_seq % TOKEN_TILE == 0
    assert 1 + dec_batch <= SUBLANES

    x_parts = (x_prompt.reshape(n_prompt, d), x_sample.reshape(n_sample, d))

    cond = jnp.concatenate([c_ctx[None, :], c, jnp.zeros((SUBLANES - 1 - dec_batch, d), F32)], axis=0)
    mod = _modulation(cond, ada_w, ada_b).reshape(depth, SUBLANES, 6, 1, d)

    rw_t = router_w.T
    rw_hi = rw_t.astype(BF16)
    rw_stack = jnp.concatenate([rw_hi, (rw_t - rw_hi.astype(F32)).astype(BF16)], axis=0)

    fwd_states, bwd_states, ckv_caches, krope_caches = [], [], [], []
    for layer in range(depth):
        j = layer // 2
        if layer % 2 == 0:
            width = lru_conv_w.shape[2]
            heads = lru_w_r.shape[2]
            xb, gg, gu, gv = _even_in_proj(x_parts, mod, layer, norm_mix[layer], mix0_w_in[j], n_prompt, dec_seq)
            wcat = jnp.concatenate([lru_w_r[j, 0], lru_w_i[j, 0], lru_w_r[j, 1], lru_w_i[j, 1]], axis=-1).astype(BF16)
            hb = lambda v: v.reshape(heads, 1, width // heads)
            bcat = jnp.concatenate([hb(lru_b_r[j, 0]), hb(lru_b_i[j, 0]), hb(lru_b_r[j, 1]), hb(lru_b_i[j, 1])], axis=-1)
            lru_args = (lru_conv_w[j], lru_conv_b[j].reshape(1, width), wcat, bcat, lru_lambda[j])
            zero_state = jnp.zeros((batch, 1, width), F32)
            nseq_p = LRU_PROMPT_SEQS_PER_STEP if batch % LRU_PROMPT_SEQS_PER_STEP == 0 else 1
            ya_p, fin_f, fin_b = _lru_mixer(xb, gg, zero_state, zero_state, *lru_args,
                                            row0=0, batch=batch, seq=seq, nseq=nseq_p)
            nseq_s = LRU_LATENT_SEQS_PER_STEP
            if dec_batch % nseq_s or n_prompt % (nseq_s * dec_seq):
                nseq_s = 1
            ya_s, _, _ = _lru_mixer(xb, gg, state_lru_fwd[:, j][:, None, :], state_lru_bwd[:, j][:, None, :],
                                    *lru_args, row0=n_prompt, batch=dec_batch, seq=dec_seq, nseq=nseq_s)
            fwd_states.append(fin_f[:, 0, :])
            bwd_states.append(fin_b[:, 0, :])
            groups, chunk, _ = gmlp_w_s[j].shape
            gd = width // groups
            b_full = jnp.repeat(gmlp_b_s[j].T, gd, axis=1)
            x = _even_out_proj(x_parts, mod, layer, ya_p, ya_s, gu, gv, gmlp_v_norm[j], gmlp_w_s[j], b_full,
                               mix0_w_out[j], n_prompt, dec_seq)
        else:
            q_lora = mla_q_a_norm.shape[1]
            kv_lora = mla_kv_a_norm.shape[1]
            qk_dim = mla_q_norm.shape[1]
            rope_dim = cache_mla_krope.shape[-1]
            nope = qk_dim - rope_dim
            heads = mla_w_uq.shape[2] // qk_dim
            v_dim = mla_w_ukv.shape[2] // heads - nope
            past = cache_mla_ckv.shape[2]
            assert nope == LANES and v_dim == LANES and rope_dim == 4 * ROPE_QUARTER
            wd = mla_w_down[j]
            wd_ext = jnp.concatenate([wd, jnp.zeros((d, LANES - rope_dim), F32)], axis=1).astype(BF16)
            wuq = mla_w_uq[j].reshape(q_lora, heads, qk_dim)
            wuq_pad = jnp.concatenate([wuq, jnp.zeros((q_lora, heads, 2 * LANES - qk_dim), F32)], axis=-1)
            wuq_pad = wuq_pad.reshape(q_lora, heads * 2 * LANES).astype(BF16)
            wukv = mla_w_ukv[j].reshape(kv_lora, heads, nope + v_dim)
            w_ukv_re = jnp.concatenate([wukv[:, :, :nope].reshape(kv_lora, heads * nope),
                                        wukv[:, :, nope:].reshape(kv_lora, heads * v_dim)], axis=1).astype(BF16)
            qn_rows = _norm_rows(mla_q_norm[j], nope, rope_dim)
            kn_rows = _norm_rows(mla_k_norm[j], nope, rope_dim)
            cos_tab, sin_tab = _rope_tables(dec_seq, rope_dim, MLA_TOKEN_TILE)
            x = x_parts[0] if len(x_parts) == 1 else jnp.concatenate(x_parts, axis=0)
            q, ckv_p, ckv_s, kr_p, kr_s = _mla_in_proj(
                x, mod, layer, norm_mix[layer], wd_ext, mla_q_a_norm[j], mla_kv_a_norm[j], wuq_pad, qn_rows,
                cos_tab, sin_tab, n_prompt, dec_seq, heads=heads, q_lora=q_lora, kv_lora=kv_lora, qk_dim=qk_dim)
            ckv_caches.append(ckv_p.reshape(batch, seq, kv_lora))
            krope_caches.append(kr_p[:, :rope_dim].reshape(batch, seq, rope_dim))
            assert seq <= MLA_TOKEN_TILE
            o_p = _self_attention(q, ckv_p, kr_p, w_ukv_re, kn_rows, cos_tab, sin_tab, batch=batch, seq=seq,
                                  heads=heads, qk_dim=qk_dim)
            kv_len = past + dec_seq
            assert past % TOKEN_TILE == 0 and past > 0
            ctx_kr = jnp.pad(cache_mla_krope[:, j], ((0, 0), (0, 0), (0, LANES - rope_dim)))
            k_s, v_s = _kv_expand_latent(
                cache_mla_ckv[:, j].reshape(dec_batch * past, kv_lora), ctx_kr.reshape(dec_batch * past, LANES),
                ckv_s, kr_s, w_ukv_re, kn_rows, cos_tab, sin_tab, row0=0, batch=dec_batch, past=past,
                seq=dec_seq, ident_rows=MLA_TOKEN_TILE, heads=heads, qk_dim=qk_dim)
            o_s = _attention(q, k_s, v_s, row0=n_prompt, batch=dec_batch, seq=dec_seq, kv_len=kv_len,
                             heads=heads, tq=ATTN_Q_TILE, heads_per_step=ATTN_LATENT_HEADS_PER_STEP)
            x = _mla_out_proj(x, mod, layer, o_p, o_s, mla_w_o[j], n_prompt, dec_seq)
        x_parts = _moe(x, mod, layer, norm_ffn[layer], rw_stack, router_bias, moe_w_gate, moe_w_up, moe_w_down,
                       n_prompt, dec_seq, split=layer == depth - 1)

    xp = x_parts[0].reshape(batch, seq, d)
    xs = x_parts[1].reshape(dec_batch, dec_seq, d)
    return (xp, xs,
            jnp.stack(fwd_states, axis=1), jnp.stack(bwd_states, axis=1),
            jnp.stack(ckv_caches, axis=1), jnp.stack(krope_caches, axis=1))
```

```python
import functools

import jax
import jax.numpy as jnp
import numpy as np
from jax import lax
from jax.experimental import pallas as pl
from jax.experimental.pallas import tpu as pltpu

F32 = jnp.float32
BF16 = jnp.bfloat16
I32 = jnp.int32

EPS = 1e-6
LOG2_E = 1.4426950408889634
LRU_C = 8.0
GRID_W = 64
ROPE_BASE = 10000.0
ROPE_QUARTER = 16
N_EXPERT_GROUPS = 4
TOP_K = 2

LANES = 128
SUBLANES = 8
VMEM_BYTES_V7X = 64 * 1024 * 1024

TOKEN_TILE = 256
MLA_TOKEN_TILE = 512
EXPERT_TILE = 512
MOD_COL_TILE = 1024
ATTN_Q_TILE = 512
ATTN_LATENT_HEADS_PER_STEP = 4
OUT_PROJ_TILE = 512
EVEN_IN_TILE = 512
ROUTER_TILE = 512
LRU_GATE_CHUNK = 256
RUN_ALIGN = 16
RUN_PAD_ROWS = 256
LRU_PROMPT_SEQS_PER_STEP = 8
LRU_LATENT_SEQS_PER_STEP = 2


def _cparams(semantics, vmem_mb):
    return pltpu.CompilerParams(dimension_semantics=semantics, vmem_limit_bytes=vmem_mb * 1024 * 1024)


def _resident(shape):
    nd = len(shape)
    return pl.BlockSpec(shape, lambda *_: (0,) * nd, pipeline_mode=pl.Buffered(1))


def _mod_spec(layer, chunk, tile, n_prompt, sample_seq, d):
    def index(i, *_):
        t = i * tile
        row = jnp.where(t < n_prompt, 0, 1 + (t - n_prompt) // sample_seq)
        return (layer, row, chunk, 0, 0)

    return pl.BlockSpec((None, None, None, 1, d), index)


def _modulated(x, gain, shift, scale):
    y = x * lax.rsqrt(jnp.mean(x * x, axis=-1, keepdims=True) + EPS)
    return (y * gain) * (1.0 + scale) + shift


def _mod_body(c_ref, w_ref, b_ref, o_ref):
    c = c_ref[...]
    s = (c * jax.nn.sigmoid(c)).astype(BF16)
    o_ref[...] = jnp.dot(s, w_ref[...].astype(BF16), preferred_element_type=F32) + b_ref[...]


def _modulation(cond, ada_w, ada_b):
    depth, d, n = ada_w.shape
    tn = MOD_COL_TILE
    return pl.pallas_call(
        _mod_body,
        grid=(depth, n // tn),
        in_specs=[
            pl.BlockSpec((SUBLANES, d), lambda l, j: (0, 0)),
            pl.BlockSpec((None, d, tn), lambda l, j: (l, 0, j)),
            pl.BlockSpec((None, 1, tn), lambda l, j: (l, 0, j)),
        ],
        out_specs=pl.BlockSpec((None, SUBLANES, tn), lambda l, j: (l, 0, j)),
        out_shape=jax.ShapeDtypeStruct((depth, SUBLANES, n), F32),
        compiler_params=_cparams(("parallel", "parallel"), 40),
        name="adaln_projection",
    )(cond, ada_w, ada_b.reshape(depth, 1, n))


def _group_select(i, prompt_tiles, p_ref, s_ref):
    return jnp.where(i < prompt_tiles, p_ref[...], s_ref[...])


def _group_specs(block, prompt_tiles):
    return [pl.BlockSpec(block, lambda i, *_: (jnp.minimum(i, prompt_tiles - 1), 0)),
            pl.BlockSpec(block, lambda i, *_: (jnp.maximum(i - prompt_tiles, 0), 0))]


def _token_specs(x_parts, tile, prompt_tiles):
    d = x_parts[0].shape[1]
    if len(x_parts) == 1:
        return [pl.BlockSpec((tile, d), lambda i, *_: (i, 0))]
    return _group_specs((tile, d), prompt_tiles)


def _token_tile(i, prompt_tiles, x_refs):
    return x_refs[0][...] if len(x_refs) == 1 else _group_select(i, prompt_tiles, *x_refs)


def _in0_body(*refs, n_x, prompt_tiles):
    x_refs = refs[:n_x]
    g_ref, sh_ref, sc_ref, w_ref, xb_ref, gg_ref, gu_ref, gv_ref = refs[n_x:]
    x = _token_tile(pl.program_id(0), prompt_tiles, x_refs)
    h = _modulated(x, g_ref[...], sh_ref[...], sc_ref[...])
    z = jnp.dot(h.astype(BF16), w_ref[...], preferred_element_type=F32)
    w = xb_ref.shape[1]
    xb_ref[...] = z[:, :w]
    gg_ref[...] = jax.nn.gelu(z[:, w:2 * w]).astype(BF16)
    gu_ref[...] = jax.nn.gelu(z[:, 2 * w:3 * w]).astype(BF16)
    gv_ref[...] = jax.nn.gelu(z[:, 3 * w:]).astype(BF16)


def _even_in_proj(x_parts, mod, layer, gain, w_in, n_prompt, sample_seq):
    t = sum(p.shape[0] for p in x_parts)
    d = x_parts[0].shape[1]
    tm = EVEN_IN_TILE
    w4 = w_in.shape[1]
    w = w4 // 4
    row = lambda i: (i, 0)
    ms = functools.partial(_mod_spec, layer, tile=tm, n_prompt=n_prompt, sample_seq=sample_seq, d=d)
    return pl.pallas_call(
        functools.partial(_in0_body, n_x=len(x_parts), prompt_tiles=n_prompt // tm),
        grid=(t // tm,),
        in_specs=[
            *_token_specs(x_parts, tm, n_prompt // tm),
            _resident((1, d)),
            ms(chunk=0),
            ms(chunk=1),
            _resident((d, w4)),
        ],
        out_specs=[pl.BlockSpec((tm, w), row)] * 4,
        out_shape=[
            jax.ShapeDtypeStruct((t, w), F32),
            jax.ShapeDtypeStruct((t, w), BF16),
            jax.ShapeDtypeStruct((t, w), BF16),
            jax.ShapeDtypeStruct((t, w), BF16),
        ],
        compiler_params=_cparams(("parallel",), 48),
        name="even_in_proj",
    )(*x_parts, gain.reshape(1, d), mod, mod, w_in.astype(BF16))


def _tile_scan(a, b, row, reverse):
    for d in (1, 2, 4):
        if reverse:
            keep = row < SUBLANES - d
            a_s = jnp.where(keep, pltpu.roll(a, SUBLANES - d, 0), 1.0)
            b_s = jnp.where(keep, pltpu.roll(b, SUBLANES - d, 0), 0.0)
        else:
            keep = row >= d
            a_s = jnp.where(keep, pltpu.roll(a, d, 0), 1.0)
            b_s = jnp.where(keep, pltpu.roll(b, d, 0), 0.0)
        b = b + a * b_s
        a = a * a_s
    return a, b


def _sigmoid(x):
    return 0.5 * jnp.tanh(0.5 * x) + 0.5


def _lru_body(xb_ref, gg_ref, h0f_ref, h0b_ref, cw_ref, cb_ref, w_ref, bias_ref, lam_ref,
              ya_ref, ff_ref, fb_ref,
              xp_ref, af_ref, bf_ref, ab_ref, bb_ref, hf_ref, hb_ref, *, seq, chunk, nseq):
    hw = LANES
    pad = SUBLANES
    pitch = seq + 2 * pad
    for s in range(nseq):
        xp_ref[pl.ds(s * pitch, pad), :] = jnp.zeros((pad, hw), F32)
        xp_ref[pl.ds(s * pitch + pad + seq, pad), :] = jnp.zeros((pad, hw), F32)
        xp_ref[pl.ds(s * pitch + pad, seq), :] = xb_ref[pl.ds(s * seq, seq), :]

    cw = cw_ref[...]
    cb = cb_ref[...]
    lam = lam_ref[...]
    neg = -lam
    softplus = jnp.maximum(neg, 0.0) + jnp.log1p(jnp.exp(-jnp.abs(neg)))
    nsp = -LRU_C * softplus
    w = w_ref[...]
    bias = bias_ref[...]

    for s in range(nseq):
        for c in range(seq // chunk):
            src = s * pitch + pad + c * chunk
            dst = s * seq + c * chunk
            xc = (cw[0:1] * xp_ref[pl.ds(src - 2, chunk), :]
                  + cw[1:2] * xp_ref[pl.ds(src - 1, chunk), :]
                  + cw[2:3] * xp_ref[pl.ds(src, chunk), :]
                  + cw[3:4] * xp_ref[pl.ds(src + 1, chunk), :]) + cb
            g = jnp.dot(xc.astype(BF16), w, preferred_element_type=F32) + bias
            for direction, (a_ref, b_ref) in enumerate(((af_ref, bf_ref), (ab_ref, bb_ref))):
                r = _sigmoid(g[:, (2 * direction) * hw:(2 * direction + 1) * hw])
                gi = _sigmoid(g[:, (2 * direction + 1) * hw:(2 * direction + 2) * hw])
                log_a = r * nsp[direction:direction + 1]
                a = jnp.exp(log_a)
                a_ref[pl.ds(dst, chunk), :] = a
                b_ref[pl.ds(dst, chunk), :] = jnp.sqrt(1.0 - a * a) * gi * xc

    n_tiles = seq // SUBLANES
    row = lax.broadcasted_iota(I32, (SUBLANES, hw), 0)

    def step(i, carry):
        new = []
        for s in range(nseq):
            hf, hb = carry[2 * s], carry[2 * s + 1]
            rf = pl.multiple_of(s * seq + i * SUBLANES, SUBLANES)
            rb = pl.multiple_of(s * seq + (n_tiles - 1 - i) * SUBLANES, SUBLANES)
            a, b = _tile_scan(af_ref[pl.ds(rf, SUBLANES), :], bf_ref[pl.ds(rf, SUBLANES), :], row, False)
            h = b + a * hf
            hf_ref[pl.ds(rf, SUBLANES), :] = h
            new.append(jnp.broadcast_to(h[SUBLANES - 1:SUBLANES, :], (SUBLANES, hw)))
            a, b = _tile_scan(ab_ref[pl.ds(rb, SUBLANES), :], bb_ref[pl.ds(rb, SUBLANES), :], row, True)
            h = b + a * hb
            hb_ref[pl.ds(rb, SUBLANES), :] = h
            new.append(jnp.broadcast_to(h[0:1, :], (SUBLANES, hw)))
        return tuple(new)

    init = []
    for s in range(nseq):
        init.append(jnp.broadcast_to(h0f_ref[s], (SUBLANES, hw)))
        init.append(jnp.broadcast_to(h0b_ref[s], (SUBLANES, hw)))
    final = lax.fori_loop(0, n_tiles, step, tuple(init), unroll=2 if nseq == 1 else 1)
    for s in range(nseq):
        ff_ref[s] = final[2 * s][0:1, :]
        fb_ref[s] = final[2 * s + 1][0:1, :]
    ya_ref[...] = ((hf_ref[...] + hb_ref[...]) * gg_ref[...].astype(F32)).astype(BF16)


def _lru_mixer(xb, gg, h0f, h0b, conv_w, conv_b, wcat, bcat, lam, *, row0, batch, seq, nseq):
    _, width = xb.shape
    heads = width // LANES
    rows = nseq * seq
    assert batch % nseq == 0 and row0 % rows == 0
    blk0 = row0 // rows
    tok = lambda b, h: (blk0 + b, h)
    state = lambda b, h: (b, 0, h)
    seq_buf = pltpu.VMEM((rows, LANES), F32)
    return pl.pallas_call(
        functools.partial(_lru_body, seq=seq, chunk=min(seq, LRU_GATE_CHUNK), nseq=nseq),
        grid=(batch // nseq, heads),
        in_specs=[
            pl.BlockSpec((rows, LANES), tok),
            pl.BlockSpec((rows, LANES), tok),
            pl.BlockSpec((nseq, 1, LANES), state),
            pl.BlockSpec((nseq, 1, LANES), state),
            pl.BlockSpec((conv_w.shape[0], LANES), lambda b, h: (0, h)),
            pl.BlockSpec((1, LANES), lambda b, h: (0, h)),
            pl.BlockSpec((None, LANES, 4 * LANES), lambda b, h: (h, 0, 0)),
            pl.BlockSpec((None, 1, 4 * LANES), lambda b, h: (h, 0, 0)),
            pl.BlockSpec((2, LANES), lambda b, h: (0, h)),
        ],
        out_specs=[
            pl.BlockSpec((rows, LANES), lambda b, h: (b, h)),
            pl.BlockSpec((nseq, 1, LANES), state),
            pl.BlockSpec((nseq, 1, LANES), state),
        ],
        out_shape=[
            jax.ShapeDtypeStruct((batch * seq, width), BF16),
            jax.ShapeDtypeStruct((batch, 1, width), F32),
            jax.ShapeDtypeStruct((batch, 1, width), F32),
        ],
        scratch_shapes=[pltpu.VMEM((nseq * (seq + 2 * SUBLANES), LANES), F32)] + [seq_buf] * 6,
        compiler_params=_cparams(("parallel", "parallel"), 40),
        name=f"rglru_seq{seq}",
    )(xb, gg, h0f, h0b, conv_w, conv_b, wcat, bcat, lam)


def _out0_body(*refs, n_x, chunk, prompt_tiles):
    x_refs = refs[:n_x]
    (gate_ref, yap_ref, yas_ref, gu_ref, gv_ref, vg_ref, ws_ref, bs_ref, wa_ref, wb_ref,
     o_ref, yb_ref) = refs[n_x:]
    tm = o_ref.shape[0]
    x = _token_tile(pl.program_id(0), prompt_tiles, x_refs)
    ya = _group_select(pl.program_id(0), prompt_tiles, yap_ref, yas_ref)
    v = gv_ref[...].astype(F32)
    vn = (v * lax.rsqrt(jnp.mean(v * v, axis=-1, keepdims=True) + EPS) * vg_ref[...]).astype(BF16)
    groups = ws_ref.shape[0]
    gd = vn.shape[1] // groups
    for c in range(tm // chunk):
        rows = slice(c * chunk, (c + 1) * chunk)
        for g in range(groups):
            cols = slice(g * gd, (g + 1) * gd)
            mixed = jnp.dot(ws_ref[g], vn[rows, cols], preferred_element_type=F32) + bs_ref[:, cols]
            yb_ref[rows, cols] = (gu_ref[rows, cols].astype(F32) * mixed).astype(BF16)
    y = (jnp.dot(ya, wa_ref[...], preferred_element_type=F32)
         + jnp.dot(yb_ref[...], wb_ref[...], preferred_element_type=F32))
    o_ref[...] = x + gate_ref[...] * y


def _even_out_proj(x_parts, mod, layer, ya_p, ya_s, gu, gv, v_gain, w_s, b_full, w_out, n_prompt, sample_seq):
    t = sum(p.shape[0] for p in x_parts)
    d = x_parts[0].shape[1]
    tm = OUT_PROJ_TILE
    w = ya_p.shape[1]
    row = lambda i: (i, 0)
    chunk = w_s.shape[1]
    prompt_tiles = n_prompt // tm
    return pl.pallas_call(
        functools.partial(_out0_body, n_x=len(x_parts), chunk=chunk, prompt_tiles=prompt_tiles),
        grid=(t // tm,),
        in_specs=[
            *_token_specs(x_parts, tm, prompt_tiles),
            _mod_spec(layer, 2, tm, n_prompt, sample_seq, d),
            *_group_specs((tm, w), prompt_tiles),
            pl.BlockSpec((tm, w), row),
            pl.BlockSpec((tm, w), row),
            _resident((1, w)),
            _resident(w_s.shape),
            _resident(b_full.shape),
            _resident((w, d)),
            _resident((w, d)),
        ],
        out_specs=pl.BlockSpec((tm, d), row),
        out_shape=jax.ShapeDtypeStruct((t, d), F32),
        scratch_shapes=[pltpu.VMEM((tm, w), BF16)],
        compiler_params=_cparams(("parallel",), 56),
        name="even_out_proj",
    )(*x_parts, mod, ya_p, ya_s, gu, gv, v_gain.reshape(1, w), w_s.astype(BF16), b_full,
      w_out[:w].astype(BF16), w_out[w:].astype(BF16))


def _swap_halves(x, lane):
    quarter = ROPE_QUARTER
    up = pltpu.roll(x, LANES - quarter, 1)
    down = pltpu.roll(x, quarter, 1)
    return jnp.where((lane % (2 * quarter)) < quarter, up, down)


def _mla_in_body(x_ref, g_ref, sh_ref, sc_ref, wd_ref, qan_ref, kvan_ref, wuq_ref, qn_ref, cos_ref, sin_ref,
                 q_ref, ckvp_ref, ckvs_ref, krp_ref, krs_ref,
                 *, heads, q_lora, kv_lora, qk_dim, sm_scale, sub, prompt_tiles):
    gn = qn_ref[0:1, :]
    gr = qn_ref[1:2, :]
    grs = qn_ref[2:3, :]
    lane = lax.broadcasted_iota(I32, (sub, LANES), 1)
    latents = []
    for r0 in range(0, x_ref.shape[0], sub):
        rows = pl.ds(r0, sub)
        h = _modulated(x_ref[rows, :], g_ref[...], sh_ref[...], sc_ref[...])
        z = jnp.dot(h.astype(BF16), wd_ref[...], preferred_element_type=F32)
        cq = z[:, :q_lora]
        cq = cq * lax.rsqrt(jnp.mean(cq * cq, axis=-1, keepdims=True) + EPS) * qan_ref[...]
        ckv = z[:, q_lora:q_lora + kv_lora]
        latents.append((rows, ckv * lax.rsqrt(jnp.mean(ckv * ckv, axis=-1, keepdims=True) + EPS) * kvan_ref[...],
                        z[:, q_lora + kv_lora:]))
        q = jnp.dot(cq.astype(BF16), wuq_ref[...], preferred_element_type=F32)
        cos = gr * cos_ref[rows, :]
        sin = grs * sin_ref[rows, :]
        for hd in range(heads):
            qn = q[:, 2 * hd * LANES:(2 * hd + 1) * LANES]
            qr = q[:, (2 * hd + 1) * LANES:(2 * hd + 2) * LANES]
            ss = jnp.sum(qn * qn + qr * qr, axis=-1, keepdims=True)
            rinv = lax.rsqrt(ss * (1.0 / qk_dim) + EPS) * sm_scale
            q_ref[rows, 2 * hd * LANES:(2 * hd + 1) * LANES] = (qn * gn * rinv).astype(BF16)
            rot = qr * cos + _swap_halves(qr, lane) * sin
            q_ref[rows, (2 * hd + 1) * LANES:(2 * hd + 2) * LANES] = (rot * rinv).astype(BF16)
    i = pl.program_id(0)
    for (c_ref, k_ref), own in zip(((ckvp_ref, krp_ref), (ckvs_ref, krs_ref)), (i < prompt_tiles, i >= prompt_tiles)):
        @pl.when(own)
        def _():
            for rows, ckv_n, kr_v in latents:
                c_ref[rows, :] = ckv_n
                k_ref[rows, :] = kr_v


def _rope_spec(tile, n_prompt, sample_seq):
    def index(i):
        t = i * tile
        return (jnp.where(t < n_prompt, 0, 1 + ((t - n_prompt) % sample_seq) // tile), 0)

    return pl.BlockSpec((tile, LANES), index)


def _mla_in_proj(x, mod, layer, gain, wd_ext, q_a_norm, kv_a_norm, wuq_pad, qn_rows, cos_tab, sin_tab,
                 n_prompt, sample_seq, *, heads, q_lora, kv_lora, qk_dim):
    t, d = x.shape
    tm = MLA_TOKEN_TILE
    row = lambda i: (i, 0)
    ms = functools.partial(_mod_spec, layer, tile=tm, n_prompt=n_prompt, sample_seq=sample_seq, d=d)
    prompt_tiles = n_prompt // tm
    body = functools.partial(_mla_in_body, heads=heads, q_lora=q_lora, kv_lora=kv_lora, qk_dim=qk_dim,
                             sm_scale=float(qk_dim) ** -0.5 * LOG2_E, sub=TOKEN_TILE, prompt_tiles=prompt_tiles)
    return pl.pallas_call(
        body,
        grid=(t // tm,),
        in_specs=[
            pl.BlockSpec((tm, d), row),
            _resident((1, d)),
            ms(chunk=0),
            ms(chunk=1),
            _resident(wd_ext.shape),
            _resident((1, q_lora)),
            _resident((1, kv_lora)),
            _resident(wuq_pad.shape),
            _resident(qn_rows.shape),
            _rope_spec(tm, n_prompt, sample_seq),
            _rope_spec(tm, n_prompt, sample_seq),
        ],
        out_specs=[
            pl.BlockSpec((tm, heads * 2 * LANES), row),
            *_group_specs((tm, kv_lora), prompt_tiles),
            *_group_specs((tm, LANES), prompt_tiles),
        ],
        out_shape=[
            jax.ShapeDtypeStruct((t, heads * 2 * LANES), BF16),
            jax.ShapeDtypeStruct((n_prompt, kv_lora), F32),
            jax.ShapeDtypeStruct((t - n_prompt, kv_lora), F32),
            jax.ShapeDtypeStruct((n_prompt, LANES), F32),
            jax.ShapeDtypeStruct((t - n_prompt, LANES), F32),
        ],
        compiler_params=_cparams(("arbitrary",), 56),
        name="mla_in_proj",
    )(x, gain.reshape(1, d), mod, mod, wd_ext, q_a_norm.reshape(1, q_lora), kv_a_norm.reshape(1, kv_lora),
      wuq_pad, qn_rows, cos_tab, sin_tab)


def _kv_body(ckv_ref, kr_ref, w_ref, kn_ref, cos_ref, sin_ref, k_ref, v_ref, *, heads, qk_dim):
    _kv_compute(ckv_ref[...], kr_ref[...], w_ref, kn_ref, cos_ref, sin_ref, k_ref, v_ref, heads=heads, qk_dim=qk_dim)


def _kv_latent_body(cc_ref, ck_ref, oc_ref, ok_ref, w_ref, kn_ref, cos_ref, sin_ref, k_ref, v_ref,
                    *, heads, qk_dim, tiles_per_req, ctx_tiles):
    from_ctx = (pl.program_id(0) % tiles_per_req) < ctx_tiles
    ckv = jnp.where(from_ctx, cc_ref[...], oc_ref[...])
    kr = jnp.where(from_ctx, ck_ref[...], ok_ref[...])
    _kv_compute(ckv, kr, w_ref, kn_ref, cos_ref, sin_ref, k_ref, v_ref, heads=heads, qk_dim=qk_dim)


def _kv_compute(ckv, kr, w_ref, kn_ref, cos_ref, sin_ref, k_ref, v_ref, *, heads, qk_dim):
    kv = jnp.dot(ckv.astype(BF16), w_ref[...], preferred_element_type=F32)
    tm = kv.shape[0]
    lane = lax.broadcasted_iota(I32, (tm, LANES), 1)
    gn = kn_ref[0:1, :]
    gr = kn_ref[1:2, :]
    grs = kn_ref[2:3, :]
    ssr = jnp.sum(kr * kr, axis=-1, keepdims=True)
    rot = (kr * gr) * cos_ref[...] + (_swap_halves(kr, lane) * grs) * sin_ref[...]
    for hd in range(heads):
        kn = kv[:, hd * LANES:(hd + 1) * LANES]
        rinv = lax.rsqrt((jnp.sum(kn * kn, axis=-1, keepdims=True) + ssr) * (1.0 / qk_dim) + EPS)
        k_ref[:, 2 * hd * LANES:(2 * hd + 1) * LANES] = (kn * gn * rinv).astype(BF16)
        k_ref[:, (2 * hd + 1) * LANES:(2 * hd + 2) * LANES] = (rot * rinv).astype(BF16)
    v_ref[...] = kv[:, heads * LANES:].astype(BF16)


def _kv_expand_latent(ctx_ckv, ctx_kr, ckv, kr, w_ukv_re, kn_rows, cos_tab, sin_tab,
                      *, row0, batch, past, seq, ident_rows, heads, qk_dim):
    kv_lora = ckv.shape[1]
    tm = TOKEN_TILE
    tiles_per_req = (past + seq) // tm
    ctx_tiles = past // tm
    own_tiles = seq // tm
    rows = batch * (past + seq)
    row = lambda i: (i, 0)

    def ctx_index(i):
        return ((i // tiles_per_req) * ctx_tiles + jnp.minimum(i % tiles_per_req, ctx_tiles - 1), 0)

    def own_index(i):
        return (row0 // tm + (i // tiles_per_req) * own_tiles + jnp.maximum(i % tiles_per_req - ctx_tiles, 0), 0)

    def rope_index(i):
        r = i % tiles_per_req
        return (jnp.where(r < ctx_tiles, 0, ident_rows // tm + r - ctx_tiles), 0)

    return pl.pallas_call(
        functools.partial(_kv_latent_body, heads=heads, qk_dim=qk_dim, tiles_per_req=tiles_per_req,
                          ctx_tiles=ctx_tiles),
        grid=(rows // tm,),
        in_specs=[
            pl.BlockSpec((tm, kv_lora), ctx_index),
            pl.BlockSpec((tm, LANES), ctx_index),
            pl.BlockSpec((tm, kv_lora), own_index),
            pl.BlockSpec((tm, LANES), own_index),
            _resident(w_ukv_re.shape),
            _resident(kn_rows.shape),
            pl.BlockSpec((tm, LANES), rope_index),
            pl.BlockSpec((tm, LANES), rope_index),
        ],
        out_specs=[pl.BlockSpec((tm, heads * 2 * LANES), row), pl.BlockSpec((tm, heads * LANES), row)],
        out_shape=[
            jax.ShapeDtypeStruct((rows, heads * 2 * LANES), BF16),
            jax.ShapeDtypeStruct((rows, heads * LANES), BF16),
        ],
        compiler_params=_cparams(("parallel",), 40),
        name="mla_kv_expand_latent",
    )(ctx_ckv, ctx_kr, ckv, kr, w_ukv_re, kn_rows, cos_tab, sin_tab)


def _attn_body(q_ref, k_ref, v_ref, o_ref, *, heads_per_step):
    for hd in range(heads_per_step):
        qk = slice(2 * hd * LANES, 2 * (hd + 1) * LANES)
        vo = slice(hd * LANES, (hd + 1) * LANES)
        s = lax.dot_general(q_ref[:, qk], k_ref[:, qk], (((1,), (1,)), ((), ())), preferred_element_type=F32)
        m = jnp.max(s, axis=-1, keepdims=True)
        p = jnp.exp2(s - m)
        l = jnp.sum(p, axis=-1, keepdims=True)
        o = jnp.dot(p.astype(BF16), v_ref[:, vo], preferred_element_type=F32)
        o_ref[:, vo] = (o * (1.0 / l)).astype(BF16)


def _attention(q, k, v, *, row0, batch, seq, kv_len, heads, tq, heads_per_step):
    nq = seq // tq
    qblk0 = row0 // tq
    hps = heads_per_step
    return pl.pallas_call(
        functools.partial(_attn_body, heads_per_step=hps),
        grid=(batch, heads // hps, nq),
        in_specs=[
            pl.BlockSpec((tq, hps * 2 * LANES), lambda b, h, i: (qblk0 + b * nq + i, h)),
            pl.BlockSpec((kv_len, hps * 2 * LANES), lambda b, h, i: (b, h)),
            pl.BlockSpec((kv_len, hps * LANES), lambda b, h, i: (b, h)),
        ],
        out_specs=pl.BlockSpec((tq, hps * LANES), lambda b, h, i: (b * nq + i, h)),
        out_shape=jax.ShapeDtypeStruct((batch * seq, heads * LANES), BF16),
        compiler_params=_cparams(("parallel", "parallel", "parallel"), 48),
        name=f"mla_attention_kv{kv_len}",
    )(q, k, v)


def _self_attn_body(q_ref, ckv_ref, kr_ref, w_ref, kn_ref, cos_ref, sin_ref, o_ref, k_scr, v_scr, *, heads, qk_dim):
    _kv_body(ckv_ref, kr_ref, w_ref, kn_ref, cos_ref, sin_ref, k_scr, v_scr, heads=heads, qk_dim=qk_dim)
    _attn_body(q_ref, k_scr, v_scr, o_ref, heads_per_step=heads)


def _self_attention(q, ckv, kr, w_ukv_re, kn_rows, cos_tab, sin_tab, *, batch, seq, heads, qk_dim):
    kv_lora = ckv.shape[1]
    tok = lambda b: (b, 0)
    return pl.pallas_call(
        functools.partial(_self_attn_body, heads=heads, qk_dim=qk_dim),
        grid=(batch,),
        in_specs=[
            pl.BlockSpec((seq, heads * 2 * LANES), tok),
            pl.BlockSpec((seq, kv_lora), tok),
            pl.BlockSpec((seq, LANES), tok),
            _resident(w_ukv_re.shape),
            _resident(kn_rows.shape),
            pl.BlockSpec((seq, LANES), lambda b: (0, 0)),
            pl.BlockSpec((seq, LANES), lambda b: (0, 0)),
        ],
        out_specs=pl.BlockSpec((seq, heads * LANES), tok),
        out_shape=jax.ShapeDtypeStruct((batch * seq, heads * LANES), BF16),
        scratch_shapes=[pltpu.VMEM((seq, heads * 2 * LANES), BF16), pltpu.VMEM((seq, heads * LANES), BF16)],
        compiler_params=_cparams(("parallel",), 40),
        name="mla_self_attention",
    )(q, ckv, kr, w_ukv_re, kn_rows, cos_tab, sin_tab)


def _oproj_body(x_ref, gate_ref, op_ref, os_ref, w_ref, out_ref, *, prompt_tiles):
    o = _group_select(pl.program_id(0), prompt_tiles, op_ref, os_ref)
    y = jnp.dot(o, w_ref[...], preferred_element_type=F32)
    out_ref[...] = x_ref[...] + gate_ref[...] * y


def _mla_out_proj(x, mod, layer, o_p, o_s, w_o, n_prompt, sample_seq):
    t, d = x.shape
    tm = OUT_PROJ_TILE
    row = lambda i: (i, 0)
    prompt_tiles = n_prompt // tm
    return pl.pallas_call(
        functools.partial(_oproj_body, prompt_tiles=prompt_tiles),
        grid=(t // tm,),
        in_specs=[
            pl.BlockSpec((tm, d), row),
            _mod_spec(layer, 2, tm, n_prompt, sample_seq, d),
            *_group_specs((tm, o_p.shape[1]), prompt_tiles),
            _resident(w_o.shape),
        ],
        out_specs=pl.BlockSpec((tm, d), row),
        out_shape=jax.ShapeDtypeStruct((t, d), F32),
        compiler_params=_cparams(("parallel",), 48),
        name="mla_out_proj",
    )(x, mod, o_p, o_s, w_o.astype(BF16))


def _round_up(v, m):
    return (v + m - 1) // m * m


def _local_rows(tile):
    return TOP_K * tile + RUN_PAD_ROWS


def _router_body(*refs):
    for r0 in range(0, refs[0].shape[0], TOKEN_TILE):
        _route_tile(*refs, r0=r0)


def _route_tile(x_ref, g_ref, sh_ref, sc_ref, wst_ref, rb_ref, hb_ref, wt_ref, lp_ref, cnt_ref, *, r0):
    rows = pl.ds(r0, TOKEN_TILE)
    h = _modulated(x_ref[rows, :], g_ref[...], sh_ref[...], sc_ref[...])
    tm, d = h.shape
    n_experts = rb_ref.shape[0]

    h_hi = h.astype(BF16)
    hb_ref[rows, :] = h_hi
    h_lo = (h - h_hi.astype(F32)).astype(BF16)
    nt = (((1,), (1,)), ((), ()))
    both = lax.dot_general(wst_ref[...], h_hi, nt, preferred_element_type=F32)
    logits = (both[:n_experts] + both[n_experts:]
              + lax.dot_general(wst_ref[:n_experts, :], h_lo, nt, preferred_element_type=F32))
    scores = jax.nn.sigmoid(logits)
    biased = scores + rb_ref[...]
    per_group = n_experts // N_EXPERT_GROUPS
    assert per_group == 4 and TOP_K == 2
    b_rows = [biased[e:e + 1, :] for e in range(n_experts)]
    s_rows = [scores[e:e + 1, :] for e in range(n_experts)]

    best = None
    sel = jnp.zeros((1, tm), I32)
    for g in range(N_EXPERT_GROUPS):
        b0, b1, b2, b3 = b_rows[g * 4:(g + 1) * 4]
        m1, n1 = jnp.maximum(b0, b1), jnp.minimum(b0, b1)
        m2, n2 = jnp.maximum(b2, b3), jnp.minimum(b2, b3)
        top1 = jnp.maximum(m1, m2)
        top2 = jnp.maximum(jnp.minimum(m1, m2), jnp.maximum(n1, n2))
        gsum = top1 + top2
        if best is None:
            best = gsum
        else:
            better = gsum > best
            sel = jnp.where(better, g, sel)
            best = jnp.where(better, gsum, best)

    def pick(rows, j):
        out = rows[j]
        for g in range(1, N_EXPERT_GROUPS):
            out = jnp.where(sel == g, rows[g * 4 + j], out)
        return out

    cand_b = [pick(b_rows, j) for j in range(4)]
    cand_s = [pick(s_rows, j) for j in range(4)]

    def argmax4(vals):
        bv, bi = vals[0], jnp.zeros((1, tm), I32)
        for j in range(1, 4):
            gt = vals[j] > bv
            bi = jnp.where(gt, j, bi)
            bv = jnp.where(gt, vals[j], bv)
        return bi

    i1 = argmax4(cand_b)
    i2 = argmax4([jnp.where(i1 == j, -jnp.inf, cand_b[j]) for j in range(4)])

    def take(vals, idx):
        out = vals[0]
        for j in range(1, 4):
            out = jnp.where(idx == j, vals[j], out)
        return out

    s1 = take(cand_s, i1)
    s2 = take(cand_s, i2)
    tot = s1 + s2
    e1 = sel * 4 + i1
    e2 = sel * 4 + i2
    wt_ref[0:1, rows] = s1 / tot
    wt_ref[1:2, rows] = s2 / tot

    eid = lax.broadcasted_iota(I32, (n_experts, tm), 0)
    is1 = eid == e1
    is2 = eid == e2
    chosen = jnp.where(is1 | is2, 1.0, 0.0)
    before = (lax.broadcasted_iota(I32, (tm, tm), 0) < lax.broadcasted_iota(I32, (tm, tm), 1))
    rank = jnp.dot(chosen.astype(BF16), jnp.where(before, 1.0, 0.0).astype(BF16), preferred_element_type=F32)
    count = jnp.sum(chosen, axis=1, keepdims=True)
    padded = jnp.floor((count + (RUN_ALIGN - 1)) * (1.0 / RUN_ALIGN)) * RUN_ALIGN
    lower = (lax.broadcasted_iota(I32, (n_experts, n_experts), 1)
             < lax.broadcasted_iota(I32, (n_experts, n_experts), 0))
    run_start = jnp.dot(jnp.where(lower, 1.0, 0.0).astype(BF16),
                        jnp.broadcast_to(padded, (n_experts, LANES)).astype(BF16),
                        preferred_element_type=F32)[:, 0:1]
    row = run_start + rank
    lp_ref[0:1, rows] = jnp.sum(jnp.where(is1, row, 0.0), axis=0, keepdims=True).astype(I32)
    lp_ref[1:2, rows] = jnp.sum(jnp.where(is2, row, 0.0), axis=0, keepdims=True).astype(I32)
    cnt_ref[r0 // TOKEN_TILE] = jnp.broadcast_to(count, (n_experts, LANES))


def _router(x, mod, layer, gain, w_stack, r_bias, n_prompt, sample_seq):
    t, d = x.shape
    tm = ROUTER_TILE
    n_experts = r_bias.shape[0]
    ms = functools.partial(_mod_spec, layer, tile=tm, n_prompt=n_prompt, sample_seq=sample_seq, d=d)
    col = lambda i: (0, i)
    return pl.pallas_call(
        _router_body,
        grid=(t // tm,),
        in_specs=[
            pl.BlockSpec((tm, d), lambda i: (i, 0)),
            _resident((1, d)),
            ms(chunk=3),
            ms(chunk=4),
            _resident(w_stack.shape),
            _resident((n_experts, 1)),
        ],
        out_specs=[
            pl.BlockSpec((tm, d), lambda i: (i, 0)),
            pl.BlockSpec((TOP_K, tm), col),
            pl.BlockSpec((TOP_K, tm), col),
            pl.BlockSpec((tm // TOKEN_TILE, n_experts, LANES), lambda i: (i, 0, 0)),
        ],
        out_shape=[
            jax.ShapeDtypeStruct((t, d), BF16),
            jax.ShapeDtypeStruct((TOP_K, t), F32),
            jax.ShapeDtypeStruct((TOP_K, t), I32),
            jax.ShapeDtypeStruct((t // TOKEN_TILE, n_experts, LANES), F32),
        ],
        compiler_params=_cparams(("parallel",), 40),
        name="moe_router",
    )(x, gain.reshape(1, d), mod, mod, w_stack, r_bias.reshape(n_experts, 1))


def _run_copies(tile, local_ref, slot, sorted_ref, row_ref, count_ref, sem, to_sorted):
    max_chunks = local_ref.shape[1] // RUN_ALIGN
    min_chunks = TOP_K * TOKEN_TILE // RUN_ALIGN

    def directed(loc, far):
        return pltpu.make_async_copy(loc, far, sem) if to_sorted else pltpu.make_async_copy(far, loc, sem)

    def chunk(c):
        loc = local_ref.at[slot, pl.ds(pl.multiple_of(c * RUN_ALIGN, RUN_ALIGN), RUN_ALIGN)]
        first = row_ref[tile * max_chunks + c]
        return directed(loc, sorted_ref.at[pl.ds(pl.multiple_of(first, RUN_ALIGN), RUN_ALIGN)])

    def apply(op):
        def one(c, carry):
            getattr(chunk(c), op)()
            return carry

        if op == "start":
            lax.fori_loop(0, min_chunks, one, 0, unroll=8)
        else:
            rows = min_chunks * RUN_ALIGN
            directed(local_ref.at[slot, pl.ds(0, rows)], sorted_ref.at[pl.ds(0, rows)]).wait()
        lax.fori_loop(min_chunks, count_ref[tile], one, 0)

    return apply


def _one_hot_rows(lp_ref, rows, tm):
    j = lax.broadcasted_iota(I32, (rows, tm), 0)
    hit = (j == lp_ref[0:1, :]) | (j == lp_ref[1:2, :])
    return jnp.where(hit, 1.0, 0.0).astype(BF16)


def _dispatch_body(row_ref, chunk_ref, last_ref, nv_ref, lp_ref, h_ref, o_ref, zero_ref, loc_ref, sems, zsem):
    tm = h_ref.shape[0]
    zt = zero_ref.shape[0]
    n_tiles = o_ref.shape[0] // zt
    i = pl.program_id(0)
    last = pl.num_programs(0) - 1
    slot = i % 2

    def copies(tile, s):
        return _run_copies(tile, loc_ref, s, o_ref, row_ref, chunk_ref, sems.at[s], True)

    @pl.when(i == 0)
    def _():
        zero_ref[...] = jnp.zeros(zero_ref.shape, zero_ref.dtype)

        def zero_tile(row):
            return pltpu.make_async_copy(zero_ref, o_ref.at[pl.ds(pl.multiple_of(row, zt), zt)], zsem)

        for e in range(last_ref.shape[0]):
            @pl.when(last_ref[e] >= 0)
            def _():
                zero_tile(last_ref[e]).start()

        def start_tail(j, c):
            zero_tile(j * zt).start()
            return c

        lax.fori_loop(nv_ref[0], n_tiles, start_tail, 0)
        for e in range(last_ref.shape[0]):
            @pl.when(last_ref[e] >= 0)
            def _():
                zero_tile(last_ref[e]).wait()

        def wait_tail(j, c):
            zero_tile(j * zt).wait()
            return c

        lax.fori_loop(nv_ref[0], n_tiles, wait_tail, 0)

    @pl.when(i >= 2)
    def _():
        copies(i - 2, slot)("wait")

    loc_ref[slot] = jnp.dot(_one_hot_rows(lp_ref, loc_ref.shape[1], tm), h_ref[...],
                            preferred_element_type=F32).astype(BF16)
    copies(i, slot)("start")

    @pl.when(i == last)
    def _():
        @pl.when(i >= 1)
        def _():
            copies(i - 1, 1 - slot)("wait")
        copies(i, slot)("wait")


def _dispatch(hb, lp, chunk_row, tile_chunks, last_tile_row, n_valid, sorted_rows):
    t, d = hb.shape
    tm = TOKEN_TILE
    grid_spec = pltpu.PrefetchScalarGridSpec(
        num_scalar_prefetch=4,
        grid=(t // tm,),
        in_specs=[
            pl.BlockSpec((TOP_K, tm), lambda i, *_: (0, i)),
            pl.BlockSpec((tm, d), lambda i, *_: (i, 0)),
        ],
        out_specs=pl.BlockSpec(memory_space=pl.ANY),
        scratch_shapes=[pltpu.VMEM((EXPERT_TILE, d), BF16), pltpu.VMEM((2, _local_rows(tm), d), BF16),
                        pltpu.SemaphoreType.DMA((2,)), pltpu.SemaphoreType.DMA],
    )
    return pl.pallas_call(
        _dispatch_body,
        grid_spec=grid_spec,
        out_shape=jax.ShapeDtypeStruct((sorted_rows, d), BF16),
        compiler_params=_cparams(("arbitrary",), 40),
        name="moe_dispatch",
    )(chunk_row, tile_chunks, last_tile_row, n_valid, lp, hb)


def _expert_body(te_ref, nv_ref, nxt_ref, par_ref, x_ref, wg_hbm, wu_hbm, wd_hbm, y_ref,
                 wgf_ref, wuf_ref, wdf_ref, wgb_ref, wub_ref, wdb_ref, sems, *, layer):
    i = pl.program_id(0)
    valid = i < nv_ref[0]
    new_expert = (i == 0) | (te_ref[i] != te_ref[jnp.maximum(i - 1, 0)])

    def weight_copies(e, slot):
        return [pltpu.make_async_copy(hbm.at[layer, e], buf.at[slot], sems.at[slot])
                for hbm, buf in ((wg_hbm, wgf_ref), (wu_hbm, wuf_ref), (wd_hbm, wdf_ref))]

    @pl.when(valid & new_expert)
    def _():
        slot = par_ref[i]

        @pl.when(i == 0)
        def _():
            for c in weight_copies(te_ref[0], slot):
                c.start()

        for c in weight_copies(te_ref[i], slot):
            c.wait()

        @pl.when(nxt_ref[i] >= 0)
        def _():
            for c in weight_copies(nxt_ref[i], 1 - slot):
                c.start()

        wgb_ref[...] = wgf_ref[slot].astype(BF16)
        wub_ref[...] = wuf_ref[slot].astype(BF16)
        wdb_ref[...] = wdf_ref[slot].astype(BF16)

    @pl.when(valid)
    def _():
        xb = x_ref[...]
        g = jnp.dot(xb, wgb_ref[...], preferred_element_type=F32)
        u = jnp.dot(xb, wub_ref[...], preferred_element_type=F32)
        act = (g * jax.nn.sigmoid(g) * u).astype(BF16)
        y_ref[...] = jnp.dot(act, wdb_ref[...], preferred_element_type=F32).astype(BF16)

    @pl.when(i >= nv_ref[0])
    def _():
        y_ref[...] = jnp.zeros(y_ref.shape, BF16)


def _experts(xs, tile_expert, n_valid, tile_next, tile_slot, w_gate, w_up, w_down, layer):
    rows, d = xs.shape
    tm = EXPERT_TILE
    _, n_experts, _, f = w_gate.shape

    def xrow(i, te, nv, *_):
        return (jnp.minimum(i, nv[0] - 1), 0)

    hbm = pl.BlockSpec(memory_space=pl.ANY)
    grid_spec = pltpu.PrefetchScalarGridSpec(
        num_scalar_prefetch=4,
        grid=(rows // tm,),
        in_specs=[pl.BlockSpec((tm, d), xrow), hbm, hbm, hbm],
        out_specs=pl.BlockSpec((tm, d), lambda i, *_: (i, 0)),
        scratch_shapes=[pltpu.VMEM((2, d, f), F32), pltpu.VMEM((2, d, f), F32), pltpu.VMEM((2, f, d), F32),
                        pltpu.VMEM((d, f), BF16), pltpu.VMEM((d, f), BF16), pltpu.VMEM((f, d), BF16),
                        pltpu.SemaphoreType.DMA((2,))],
    )
    return pl.pallas_call(
        functools.partial(_expert_body, layer=layer),
        grid_spec=grid_spec,
        out_shape=jax.ShapeDtypeStruct((rows, d), BF16),
        compiler_params=_cparams(("arbitrary",), 56),
        name="moe_experts",
    )(tile_expert, n_valid, tile_next, tile_slot, xs, w_gate, w_up, w_down)


def _combine_body(row_ref, chunk_ref, lpt_ref, wtt_ref, x_ref, gate_ref, ys_ref, *rest, prompt_tiles):
    o_refs, (loc_ref, sems) = rest[:-2], rest[-2:]
    tm = x_ref.shape[0]
    rows = loc_ref.shape[1]
    i = pl.program_id(0)
    slot = i % 2

    def copies(tile, s):
        return _run_copies(tile, loc_ref, s, ys_ref, row_ref, chunk_ref, sems.at[s], False)

    @pl.when(i == 0)
    def _():
        loc_ref[...] = jnp.zeros(loc_ref.shape, loc_ref.dtype)
        copies(0, 0)("start")

    @pl.when(i + 1 < pl.num_programs(0))
    def _():
        copies(i + 1, 1 - slot)("start")

    copies(i, slot)("wait")

    jt = lax.broadcasted_iota(I32, (tm, rows), 1)
    local = loc_ref[slot]
    moe = jnp.zeros((tm, local.shape[1]), F32)
    for k in range(TOP_K):
        pick = jnp.where(jt == lpt_ref[:, k:k + 1], 1.0, 0.0).astype(BF16)
        moe = moe + wtt_ref[:, k:k + 1] * jnp.dot(pick, local, preferred_element_type=F32)
    out = x_ref[...] + gate_ref[...] * moe
    if len(o_refs) == 1:
        o_refs[0][...] = out
    else:
        for o_ref, own in zip(o_refs, (i < prompt_tiles, i >= prompt_tiles)):
            @pl.when(own)
            def _():
                o_ref[...] = out


def _combine(x, mod, layer, ys, lp, wt, chunk_row, tile_chunks, n_prompt, sample_seq, split):
    t, d = x.shape
    tm = TOKEN_TILE
    prompt_tiles = n_prompt // tm
    tok = lambda i, *_: (i, 0)
    col = lambda i, *_: (0, i)
    if split:
        out_specs = _group_specs((tm, d), prompt_tiles)
        out_shape = [jax.ShapeDtypeStruct((n_prompt, d), F32), jax.ShapeDtypeStruct((t - n_prompt, d), F32)]
    else:
        out_specs = [pl.BlockSpec((tm, d), tok)]
        out_shape = [jax.ShapeDtypeStruct((t, d), F32)]
    grid_spec = pltpu.PrefetchScalarGridSpec(
        num_scalar_prefetch=2,
        grid=(t // tm,),
        in_specs=[
            pl.BlockSpec((tm, TOP_K), tok),
            pl.BlockSpec((tm, TOP_K), tok),
            pl.BlockSpec((tm, d), tok),
            _mod_spec(layer, 5, tm, n_prompt, sample_seq, d),
            pl.BlockSpec(memory_space=pl.ANY),
        ],
        out_specs=out_specs,
        scratch_shapes=[pltpu.VMEM((2, _local_rows(tm), d), BF16), pltpu.SemaphoreType.DMA((2,))],
    )
    return pl.pallas_call(
        functools.partial(_combine_body, prompt_tiles=prompt_tiles),
        grid_spec=grid_spec,
        out_shape=out_shape,
        compiler_params=_cparams(("arbitrary",), 40),
        name="moe_combine",
    )(chunk_row, tile_chunks, lp.T, wt.T, x, mod, ys)


def _moe(x, mod, layer, gain, w_stack, r_bias, w_gate, w_up, w_down, n_prompt, sample_seq, split):
    t, d = x.shape
    n_experts = r_bias.shape[0]
    assert n_experts * (RUN_ALIGN - 1) <= RUN_PAD_ROWS
    hb, wt, lp, cnt = _router(x, mod, layer, gain, w_stack, r_bias, n_prompt, sample_seq)
    n_tok_tiles = t // TOKEN_TILE
    run_len = _round_up(cnt[:, :, 0].astype(I32), RUN_ALIGN)
    run_before = jnp.cumsum(run_len, axis=0) - run_len
    tiles = (jnp.sum(run_len, axis=0) + EXPERT_TILE - 1) // EXPERT_TILE
    tile_end = jnp.cumsum(tiles)
    seg_start = (tile_end - tiles) * EXPERT_TILE
    run_row = seg_start[None, :] + run_before
    run_chunks = run_len // RUN_ALIGN
    chunk_end = jnp.cumsum(run_chunks, axis=1)
    chunk_ids = jnp.arange(_local_rows(TOKEN_TILE) // RUN_ALIGN, dtype=I32)
    owner = jnp.sum((chunk_end[:, None, :] <= chunk_ids[None, :, None]).astype(I32), axis=2)
    owner = jnp.minimum(owner, n_experts - 1)
    is_owner = owner[:, :, None] == jnp.arange(n_experts, dtype=I32)
    pick = lambda a: jnp.sum(jnp.where(is_owner, a[:, None, :], 0), axis=2)
    chunk_row = pick(run_row) + (chunk_ids[None, :] - pick(chunk_end - run_chunks)) * RUN_ALIGN
    chunk_row = chunk_row.reshape(-1).astype(I32)
    tile_chunks = chunk_end[:, -1].astype(I32)
    sorted_rows = (_round_up(TOP_K * t + n_tok_tiles * n_experts * (RUN_ALIGN - 1), EXPERT_TILE)
                   + n_experts * EXPERT_TILE)
    n_tiles = sorted_rows // EXPERT_TILE
    n_valid = tile_end[-1:]
    tile_ids = jnp.arange(n_tiles, dtype=I32)
    ends_before = lambda i: jnp.sum((tile_end[None, :] <= i[:, None]).astype(I32), axis=1)
    tile_expert = jnp.where(tile_ids < n_valid[0], ends_before(tile_ids), ends_before(n_valid - 1))
    tile_expert = jnp.minimum(tile_expert, n_experts - 1).astype(I32)
    last_tile_row = jnp.where(tiles > 0, (tile_end - 1) * EXPERT_TILE, -1).astype(I32)
    n_valid = n_valid.astype(I32)
    e_ids = jnp.arange(n_experts, dtype=I32)
    used = tiles > 0
    later_used = (e_ids[None, :] > e_ids[:, None]) & used[None, :]
    next_used = jnp.min(jnp.where(later_used, e_ids[None, :], n_experts), axis=1)
    next_used = jnp.where(next_used < n_experts, next_used, -1)
    slot_of = (jnp.cumsum(used.astype(I32)) - 1) % 2
    of_tile = lambda a: jnp.sum(jnp.where(tile_expert[:, None] == e_ids[None, :], a[None, :], 0), axis=1).astype(I32)
    xs = _dispatch(hb, lp, chunk_row, tile_chunks, last_tile_row, n_valid, sorted_rows)
    ys = _experts(xs, tile_expert, n_valid, of_tile(next_used), of_tile(slot_of), w_gate, w_up, w_down, layer)
    return _combine(x, mod, layer, ys, lp, wt, chunk_row, tile_chunks, n_prompt, sample_seq, split)


def _rope_tables(sample_seq, rope_dim, tile):
    n_freq = rope_dim // 4
    pos = np.arange(sample_seq)
    inv_freq = ROPE_BASE ** (-np.arange(n_freq, dtype=np.float64) / n_freq)
    ar = (pos // GRID_W)[:, None] * inv_freq
    ac = (pos % GRID_W)[:, None] * inv_freq
    zeros = np.zeros((sample_seq, LANES - rope_dim))
    cos = np.concatenate([np.cos(ar), np.cos(ar), np.cos(ac), np.cos(ac), zeros], axis=1)
    sin = np.concatenate([-np.sin(ar), np.sin(ar), -np.sin(ac), np.sin(ac), zeros], axis=1)
    ident_c = np.concatenate([np.ones((tile, rope_dim)), np.zeros((tile, LANES - rope_dim))], axis=1)
    return (jnp.asarray(np.concatenate([ident_c, cos], axis=0), F32),
            jnp.asarray(np.concatenate([np.zeros((tile, LANES)), sin], axis=0), F32))


def _norm_rows(gain, nope, rope_dim):
    quarter = rope_dim // 4
    gr = gain[nope:]
    grs = jnp.concatenate([gr[quarter:2 * quarter], gr[:quarter], gr[3 * quarter:], gr[2 * quarter:3 * quarter]])
    zpad = jnp.zeros((LANES - rope_dim,), F32)
    return jnp.stack([gain[:nope], jnp.concatenate([gr, zpad]), jnp.concatenate([grs, zpad])])


def kernel(x_prompt, x_sample, state_lru_fwd, state_lru_bwd, cache_mla_ckv, cache_mla_krope, c, c_ctx,
           ada_w, ada_b, norm_mix, norm_ffn, mix0_w_in, mix0_w_out, lru_conv_w, lru_conv_b,
           lru_w_r, lru_b_r, lru_w_i, lru_b_i, lru_lambda, gmlp_v_norm, gmlp_w_s, gmlp_b_s,
           mla_w_down, mla_q_a_norm, mla_kv_a_norm, mla_w_uq, mla_w_ukv, mla_q_norm, mla_k_norm, mla_w_o,
           router_w, router_bias, moe_w_gate, moe_w_up, moe_w_down):
    batch, seq, d = x_prompt.shape
    dec_batch, dec_seq, _ = x_sample.shape
    depth = ada_w.shape[0]
    n_prompt = batch * seq
    n_sample = dec_batch * dec_seq
    assert n_prompt % dec_seq == 0 and seq % TOKEN_TILE == 0 and dec_seq % TOKEN_TILE == 0
    assert 1 + dec_batch <= SUBLANES

    x_parts = (x_prompt.reshape(n_prompt, d), x_sample.reshape(n_sample, d))

    cond = jnp.concatenate([c_ctx[None, :], c, jnp.zeros((SUBLANES - 1 - dec_batch, d), F32)], axis=0)
    mod = _modulation(cond, ada_w, ada_b).reshape(depth, SUBLANES, 6, 1, d)

    rw_t = router_w.T
    rw_hi = rw_t.astype(BF16)
    rw_stack = jnp.concatenate([rw_hi, (rw_t - rw_hi.astype(F32)).astype(BF16)], axis=0)

    fwd_states, bwd_states, ckv_caches, krope_caches = [], [], [], []
    for layer in range(depth):
        j = layer // 2
        if layer % 2 == 0:
            width = lru_conv_w.shape[2]
            heads = lru_w_r.shape[2]
            xb, gg, gu, gv = _even_in_proj(x_parts, mod, layer, norm_mix[layer], mix0_w_in[j], n_prompt, dec_seq)
            wcat = jnp.concatenate([lru_w_r[j, 0], lru_w_i[j, 0], lru_w_r[j, 1], lru_w_i[j, 1]], axis=-1).astype(BF16)
            hb = lambda v: v.reshape(heads, 1, width // heads)
            bcat = jnp.concatenate([hb(lru_b_r[j, 0]), hb(lru_b_i[j, 0]), hb(lru_b_r[j, 1]), hb(lru_b_i[j, 1])], axis=-1)
            lru_args = (lru_conv_w[j], lru_conv_b[j].reshape(1, width), wcat, bcat, lru_lambda[j])
            zero_state = jnp.zeros((batch, 1, width), F32)
            nseq_p = LRU_PROMPT_SEQS_PER_STEP if batch % LRU_PROMPT_SEQS_PER_STEP == 0 else 1
            ya_p, fin_f, fin_b = _lru_mixer(xb, gg, zero_state, zero_state, *lru_args,
                                            row0=0, batch=batch, seq=seq, nseq=nseq_p)
            nseq_s = LRU_LATENT_SEQS_PER_STEP
            if dec_batch % nseq_s or n_prompt % (nseq_s * dec_seq):
                nseq_s = 1
            ya_s, _, _ = _lru_mixer(xb, gg, state_lru_fwd[:, j][:, None, :], state_lru_bwd[:, j][:, None, :],
                                    *lru_args, row0=n_prompt, batch=dec_batch, seq=dec_seq, nseq=nseq_s)
            fwd_states.append(fin_f[:, 0, :])
            bwd_states.append(fin_b[:, 0, :])
            groups, chunk, _ = gmlp_w_s[j].shape
            gd = width // groups
            b_full = jnp.repeat(gmlp_b_s[j].T, gd, axis=1)
            x = _even_out_proj(x_parts, mod, layer, ya_p, ya_s, gu, gv, gmlp_v_norm[j], gmlp_w_s[j], b_full,
                               mix0_w_out[j], n_prompt, dec_seq)
        else:
            q_lora = mla_q_a_norm.shape[1]
            kv_lora = mla_kv_a_norm.shape[1]
            qk_dim = mla_q_norm.shape[1]
            rope_dim = cache_mla_krope.shape[-1]
            nope = qk_dim - rope_dim
            heads = mla_w_uq.shape[2] // qk_dim
            v_dim = mla_w_ukv.shape[2] // heads - nope
            past = cache_mla_ckv.shape[2]
            assert nope == LANES and v_dim == LANES and rope_dim == 4 * ROPE_QUARTER
            wd = mla_w_down[j]
            wd_ext = jnp.concatenate([wd, jnp.zeros((d, LANES - rope_dim), F32)], axis=1).astype(BF16)
            wuq = mla_w_uq[j].reshape(q_lora, heads, qk_dim)
            wuq_pad = jnp.concatenate([wuq, jnp.zeros((q_lora, heads, 2 * LANES - qk_dim), F32)], axis=-1)
            wuq_pad = wuq_pad.reshape(q_lora, heads * 2 * LANES).astype(BF16)
            wukv = mla_w_ukv[j].reshape(kv_lora, heads, nope + v_dim)
            w_ukv_re = jnp.concatenate([wukv[:, :, :nope].reshape(kv_lora, heads * nope),
                                        wukv[:, :, nope:].reshape(kv_lora, heads * v_dim)], axis=1).astype(BF16)
            qn_rows = _norm_rows(mla_q_norm[j], nope, rope_dim)
            kn_rows = _norm_rows(mla_k_norm[j], nope, rope_dim)
            cos_tab, sin_tab = _rope_tables(dec_seq, rope_dim, MLA_TOKEN_TILE)
            x = x_parts[0] if len(x_parts) == 1 else jnp.concatenate(x_parts, axis=0)
            q, ckv_p, ckv_s, kr_p, kr_s = _mla_in_proj(
                x, mod, layer, norm_mix[layer], wd_ext, mla_q_a_norm[j], mla_kv_a_norm[j], wuq_pad, qn_rows,
                cos_tab, sin_tab, n_prompt, dec_seq, heads=heads, q_lora=q_lora, kv_lora=kv_lora, qk_dim=qk_dim)
            ckv_caches.append(ckv_p.reshape(batch, seq, kv_lora))
            krope_caches.append(kr_p[:, :rope_dim].reshape(batch, seq, rope_dim))
            assert seq <= MLA_TOKEN_TILE
            o_p = _self_attention(q, ckv_p, kr_p, w_ukv_re, kn_rows, cos_tab, sin_tab, batch=batch, seq=seq,
                                  heads=heads, qk_dim=qk_dim)
            kv_len = past + dec_seq
            assert past % TOKEN_TILE == 0 and past > 0
            ctx_kr = jnp.pad(cache_mla_krope[:, j], ((0, 0), (0, 0), (0, LANES - rope_dim)))
            k_s, v_s = _kv_expand_latent(
                cache_mla_ckv[:, j].reshape(dec_batch * past, kv_lora), ctx_kr.reshape(dec_batch * past, LANES),
                ckv_s, kr_s, w_ukv_re, kn_rows, cos_tab, sin_tab, row0=0, batch=dec_batch, past=past,
                seq=dec_seq, ident_rows=MLA_TOKEN_TILE, heads=heads, qk_dim=qk_dim)
            o_s = _attention(q, k_s, v_s, row0=n_prompt, batch=dec_batch, seq=dec_seq, kv_len=kv_len,
                             heads=heads, tq=ATTN_Q_TILE, heads_per_step=ATTN_LATENT_HEADS_PER_STEP)
            x = _mla_out_proj(x, mod, layer, o_p, o_s, mla_w_o[j], n_prompt, dec_seq)
        x_parts = _moe(x, mod, layer, norm_ffn[layer], rw_stack, router_bias, moe_w_gate, moe_w_up, moe_w_down,
                       n_prompt, dec_seq, split=layer == depth - 1)

    xp = x_parts[0].reshape(batch, seq, d)
    xs = x_parts[1].reshape(dec_batch, dec_seq, d)
    return (xp, xs,
            jnp.stack(fwd_states, axis=1), jnp.stack(bwd_states, axis=1),
            jnp.stack(ckv_caches, axis=1), jnp.stack(krope_caches, axis=1))
```
